```python
import math
import jax, jax.numpy as jnp
from jax import lax
import numpy as np

D_MODEL = 1024
BATCH = 8
SEQ = 4096
DEPTH = 4

N_MIXERS = 3
N_A_LAYERS = (DEPTH + 2) // 3
N_B_LAYERS = (DEPTH + 1) // 3
N_C_LAYERS = DEPTH // 3
HEAD_DIM = 64
EPS = 1e-6

A_GROUPS = ((128, 1), (512, 4), (2048, 16))
A_N_GROUPS = len(A_GROUPS)
A_HEADS = 8
A_BLOCK = 128
A_IN = 3 * A_N_GROUPS * A_HEADS * HEAD_DIM
A_OUT = A_HEADS * HEAD_DIM

B_HEADS = 16
B_KV_HEADS = 4
B_IDX_HEADS = 8
B_IDX_DIM = 64
B_TOPK_MAX = 256
B_BLOCK = 128
B_SPLITS = (B_HEADS * HEAD_DIM, B_KV_HEADS * HEAD_DIM, B_KV_HEADS * HEAD_DIM,
            B_IDX_HEADS * B_IDX_DIM, B_IDX_DIM, B_IDX_HEADS)
B_IN = sum(B_SPLITS)
B_OUT = B_HEADS * HEAD_DIM

C_HEADS = 8
C_KDIM = 128
C_VDIM = 128
C_CONV = 4
C_CHUNK = 64
C_CONV_CH = 2 * C_HEADS * C_KDIM + C_HEADS * C_VDIM
C_SPLITS = (C_CONV_CH, C_HEADS * C_VDIM, C_HEADS, C_HEADS)
C_IN = sum(C_SPLITS)
C_OUT = C_HEADS * C_VDIM

D_FF = 4 * D_MODEL

NUM_BUCKETS = 32
MAX_DISTANCE = 2048
A_BIAS_COLS = A_N_GROUPS * A_HEADS
N_BIAS_COLS = A_BIAS_COLS + B_HEADS

kernel_name = 'hybrid_dilated_dsa_gdn_block'


def rms_norm(x, gain):
    xf = x.astype(jnp.float32)
    y = xf * lax.rsqrt(jnp.mean(xf * xf, axis=-1, keepdims=True) + EPS)
    return (y * gain.astype(jnp.float32)).astype(x.dtype)


def l2_norm(x):
    xf = x.astype(jnp.float32)
    return xf * lax.rsqrt(jnp.sum(xf * xf, axis=-1, keepdims=True) + EPS)


def split_last(t, sizes):
    return jnp.split(t, np.cumsum(sizes)[:-1].tolist(), axis=-1)


def t5_bucket(dist):
    max_exact = NUM_BUCKETS // 2
    d = jnp.maximum(dist, 1).astype(jnp.float32)
    log_part = jnp.log(d / max_exact) / math.log(MAX_DISTANCE / max_exact) * (NUM_BUCKETS - max_exact)
    large = jnp.minimum(max_exact + log_part.astype(jnp.int32), NUM_BUCKETS - 1)
    return jnp.where(dist < max_exact, dist, large)


def squared_relu_mlp(h, w1, w2):
    return jnp.square(jax.nn.relu(h @ w1)) @ w2


def dilated_window_attention(q, k, v, bias_tab, window, dilation):
    B, S, H, Dh = q.shape
    n_back = window // dilation
    sub = S // dilation
    blk = math.gcd(A_BLOCK, sub)
    nb = sub // blk
    width = blk + n_back

    def to_sub(t):
        return t.reshape(B, sub, dilation, H, Dh).transpose(0, 2, 1, 3, 4)

    qs = to_sub(q).reshape(B, dilation, nb, blk, H, Dh)
    pad = ((0, 0), (0, 0), (n_back, 0), (0, 0), (0, 0))
    ks = jnp.pad(to_sub(k), pad)
    vs = jnp.pad(to_sub(v), pad)
    win = jnp.arange(nb)[:, None] * blk + jnp.arange(width)[None, :]
    kb = ks[:, :, win]
    vb = vs[:, :, win]
    logits = jnp.einsum('brnqhd,brnkhd->brnhqk', qs, kb).astype(jnp.float32) * (Dh ** -0.5)
    step = jnp.arange(blk)[:, None] - jnp.arange(width)[None, :] + n_back
    key_sub = win[:, None, :] - n_back
    valid = (step >= 0) & (step <= n_back) & (key_sub >= 0)
    bias = bias_tab[t5_bucket(jnp.maximum(step, 0) * dilation)]
    logits = logits + bias.transpose(2, 0, 1).astype(jnp.float32)
    logits = jnp.where(valid[:, None], logits, -jnp.inf)
    m = jnp.max(logits, axis=-1, keepdims=True)
    p = jnp.exp(logits - m)
    s = jnp.sum(p, axis=-1, keepdims=True)
    o = jnp.einsum('brnhqk,brnkhd->brnqhd', (p / s).astype(v.dtype), vb)
    lse = (m + jnp.log(s))[..., 0]
    o = o.reshape(B, dilation, sub, H, Dh).transpose(0, 2, 1, 3, 4).reshape(B, S, H, Dh)
    lse = lse.transpose(0, 1, 2, 4, 3).reshape(B, dilation, sub, H).transpose(0, 2, 1, 3).reshape(B, S, H)
    return o.astype(jnp.float32), lse


def mixer_a(h, w_in, q_gain, k_gain, w_out, bias_tab):
    B, S, _ = h.shape
    qkv = (h @ w_in).reshape(B, S, 3, A_N_GROUPS, A_HEADS, HEAD_DIM)
    q = rms_norm(qkv[:, :, 0], q_gain)
    k = rms_norm(qkv[:, :, 1], k_gain)
    v = qkv[:, :, 2]
    outs, lses = [], []
    for g, (window, dilation) in enumerate(A_GROUPS):
        o, lse = dilated_window_attention(q[:, :, g], k[:, :, g], v[:, :, g],
                                          bias_tab[:, g * A_HEADS:(g + 1) * A_HEADS], window, dilation)
        outs.append(o)
        lses.append(lse)
    wts = jax.nn.softmax(jnp.stack(lses), axis=0)
    y = jnp.sum(wts[..., None] * jnp.stack(outs), axis=0)
    return y.reshape(B, S, A_OUT).astype(h.dtype) @ w_out


def mixer_b(h, w_in, q_gain, k_gain, w_out, bias_tab):
    B, S, _ = h.shape
    q, k, v, q_idx, k_idx, w_idx = split_last(h @ w_in, B_SPLITS)
    q = rms_norm(q.reshape(B, S, B_HEADS, HEAD_DIM), q_gain)
    k = rms_norm(k.reshape(B, S, B_KV_HEADS, HEAD_DIM), k_gain)
    v = v.reshape(B, S, B_KV_HEADS, HEAD_DIM)
    q_idx = q_idx.reshape(B, S, B_IDX_HEADS, B_IDX_DIM)
    w_idx = w_idx * (B_IDX_HEADS ** -0.5)
    topk = min(B_TOPK_MAX, S // 4)
    nb = S // B_BLOCK
    grp = B_HEADS // B_KV_HEADS
    bidx = jnp.arange(B)[:, None, None]

    def blocks(t):
        return t.reshape(B, nb, B_BLOCK, *t.shape[2:]).swapaxes(0, 1)

    def one_block(args):
        blk_id, qb, qib, wb = args
        t = blk_id * B_BLOCK + jnp.arange(B_BLOCK)
        sc = jnp.einsum('bqhd,bsd->bqhs', qib, k_idx).astype(jnp.float32) * (B_IDX_DIM ** -0.5)
        score = jnp.einsum('bqh,bqhs->bqs', wb.astype(jnp.float32), jax.nn.relu(sc))
        causal = jnp.arange(S)[None, :] <= t[:, None]
        score = jnp.where(causal[None], score, -jnp.inf)
        _, sel = lax.top_k(score, topk)
        valid = sel <= t[None, :, None]
        ks = k[bidx, sel]
        vs = v[bidx, sel]
        qg = qb.reshape(B, B_BLOCK, B_KV_HEADS, grp, HEAD_DIM)
        logits = jnp.einsum('bqhgd,bqkhd->bqhgk', qg, ks).astype(jnp.float32) * (HEAD_DIM ** -0.5)
        dist = jnp.maximum(t[None, :, None] - sel, 0)
        bias = bias_tab[t5_bucket(dist)].astype(jnp.float32)
        bias = bias.reshape(B, B_BLOCK, topk, B_KV_HEADS, grp).transpose(0, 1, 3, 4, 2)
        logits = jnp.where(valid[:, :, None, None, :], logits + bias, -jnp.inf)
        p = jax.nn.softmax(logits, axis=-1)
        o = jnp.einsum('bqhgk,bqkhd->bqhgd', p.astype(vs.dtype), vs)
        return o.reshape(B, B_BLOCK, B_OUT)

    out = lax.map(one_block, (jnp.arange(nb), blocks(q), blocks(q_idx), blocks(w_idx)))
    out = out.swapaxes(0, 1).reshape(B, S, B_OUT)
    return out @ w_out


def chunk_gated_delta_rule(q, k, v, g, beta):
    B, S, H, Dk = q.shape
    Dv = v.shape[-1]
    C = math.gcd(C_CHUNK, S)
    n = S // C

    def chunks(t):
        return t.astype(jnp.float32).reshape(B, n, C, H, -1).transpose(1, 0, 3, 2, 4)

    q, k, v = chunks(q), chunks(k), chunks(v)
    g = g.astype(jnp.float32).reshape(B, n, C, H).transpose(1, 0, 3, 2)
    beta = beta.astype(jnp.float32).reshape(B, n, C, H).transpose(1, 0, 3, 2)
    gc = jnp.cumsum(g, axis=-1)
    diff = gc[..., :, None] - gc[..., None, :]
    lower = jnp.tril(jnp.ones((C, C), dtype=bool))
    strict = jnp.tril(jnp.ones((C, C), dtype=bool), -1)
    decay = jnp.where(lower, jnp.exp(jnp.where(lower, diff, 0.0)), 0.0)
    kb = k * beta[..., None]
    a_mat = jnp.where(strict, jnp.einsum('nbhid,nbhjd->nbhij', kb, k) * decay, 0.0)
    eye = jnp.broadcast_to(jnp.eye(C, dtype=jnp.float32), a_mat.shape)
    t_mat = lax.linalg.triangular_solve(a_mat, eye, left_side=True, lower=True, unit_diagonal=True)
    u = t_mat @ (v * beta[..., None])
    w = t_mat @ (kb * jnp.exp(gc)[..., None])
    qk = jnp.where(lower, jnp.einsum('nbhid,nbhjd->nbhij', q, k) * decay, 0.0)

    def step(state, xs):
        qi, ki, ui, wi, gci, qki = xs
        v_new = ui - wi @ state
        o = (qi * jnp.exp(gci)[..., None]) @ state + qki @ v_new
        g_last = gci[..., -1:]
        state = state * jnp.exp(g_last)[..., None] + jnp.einsum(
            'bhcd,bhce->bhde', ki * jnp.exp(g_last - gci)[..., None], v_new)
        return state, o

    state0 = jnp.zeros((B, H, Dk, Dv), jnp.float32)
    _, o = lax.scan(step, state0, (q, k, u, w, gc, qk))
    return o.transpose(1, 0, 3, 2, 4).reshape(B, S, H, Dv)


def mixer_c(h, w_in, conv_w, a_log, dt_bias, o_gain, w_out):
    B, S, _ = h.shape
    qkv, gate, b, a = split_last(h @ w_in, C_SPLITS)
    xpad = jnp.pad(qkv, ((0, 0), (C_CONV - 1, 0), (0, 0)))
    conv = sum(conv_w[j] * xpad[:, j:j + S] for j in range(C_CONV))
    conv = jax.nn.silu(conv)
    q, k, v = split_last(conv, (C_HEADS * C_KDIM, C_HEADS * C_KDIM, C_HEADS * C_VDIM))
    q = l2_norm(q.reshape(B, S, C_HEADS, C_KDIM)) * (C_KDIM ** -0.5)
    k = l2_norm(k.reshape(B, S, C_HEADS, C_KDIM))
    v = v.reshape(B, S, C_HEADS, C_VDIM)
    beta = jax.nn.sigmoid(b.astype(jnp.float32))
    g = -jnp.exp(a_log.astype(jnp.float32)) * jax.nn.softplus(a.astype(jnp.float32) + dt_bias.astype(jnp.float32))
    o = chunk_gated_delta_rule(q, k, v, g, beta)
    o = rms_norm(o, o_gain) * jax.nn.silu(gate.reshape(B, S, C_HEADS, C_VDIM).astype(jnp.float32))
    return o.reshape(B, S, C_OUT).astype(h.dtype) @ w_out


def setup_inputs(seed: int = 0) -> dict:
    key = jax.random.key(seed)
    ks = jax.random.split(key, 21)

    def nrm(k, shape, scale):
        return jax.random.normal(k, shape, jnp.float32) * scale

    def gain(k, shape):
        return 1.0 + 0.05 * jax.random.normal(k, shape, jnp.float32)

    dt = jnp.exp(jax.random.uniform(ks[20], (N_C_LAYERS, C_HEADS), jnp.float32, math.log(1e-3), math.log(1e-1)))
    return {
        'x': nrm(ks[0], (BATCH, SEQ, D_MODEL), 1.0),
        'rel_bias': nrm(ks[1], (NUM_BUCKETS, N_BIAS_COLS), 0.5),
        'norm_mix': gain(ks[2], (DEPTH, D_MODEL)),
        'norm_mlp': gain(ks[3], (DEPTH, D_MODEL)),
        'mlp_w1': nrm(ks[4], (DEPTH, D_MODEL, D_FF), D_MODEL ** -0.5),
        'mlp_w2': nrm(ks[5], (DEPTH, D_FF, D_MODEL), D_FF ** -0.5),
        'a_w_in': nrm(ks[6], (N_A_LAYERS, D_MODEL, A_IN), D_MODEL ** -0.5),
        'a_q_gain': gain(ks[7], (N_A_LAYERS, HEAD_DIM)),
        'a_k_gain': gain(ks[8], (N_A_LAYERS, HEAD_DIM)),
        'a_w_out': nrm(ks[9], (N_A_LAYERS, A_OUT, D_MODEL), A_OUT ** -0.5),
        'b_w_in': nrm(ks[10], (N_B_LAYERS, D_MODEL, B_IN), D_MODEL ** -0.5),
        'b_q_gain': gain(ks[11], (N_B_LAYERS, HEAD_DIM)),
        'b_k_gain': gain(ks[12], (N_B_LAYERS, HEAD_DIM)),
        'b_w_out': nrm(ks[13], (N_B_LAYERS, B_OUT, D_MODEL), B_OUT ** -0.5),
        'c_w_in': nrm(ks[14], (N_C_LAYERS, D_MODEL, C_IN), D_MODEL ** -0.5),
        'c_conv_w': nrm(ks[15], (N_C_LAYERS, C_CONV, C_CONV_CH), C_CONV ** -0.5),
        'c_a_log': jnp.log(jax.random.uniform(ks[16], (N_C_LAYERS, C_HEADS), jnp.float32, 1.0, 16.0)),
        'c_dt_bias': jnp.log(jnp.expm1(dt)),
        'c_o_gain': gain(ks[17], (N_C_LAYERS, C_VDIM)),
        'c_w_out': nrm(ks[18], (N_C_LAYERS, C_OUT, D_MODEL), C_OUT ** -0.5),
    }


def reference(x, rel_bias, norm_mix, norm_mlp, mlp_w1, mlp_w2,
              a_w_in, a_q_gain, a_k_gain, a_w_out,
              b_w_in, b_q_gain, b_k_gain, b_w_out,
              c_w_in, c_conv_w, c_a_log, c_dt_bias, c_o_gain, c_w_out):
    bias_a = rel_bias[:, :A_BIAS_COLS]
    bias_b = rel_bias[:, A_BIAS_COLS:]
    for i in range(DEPTH):
        kind = i % N_MIXERS
        j = i // N_MIXERS
        h = rms_norm(x, norm_mix[i])
        if kind == 0:
            y = mixer_a(h, a_w_in[j], a_q_gain[j], a_k_gain[j], a_w_out[j], bias_a)
        elif kind == 1:
            y = mixer_b(h, b_w_in[j], b_q_gain[j], b_k_gain[j], b_w_out[j], bias_b)
        else:
            y = mixer_c(h, c_w_in[j], c_conv_w[j], c_a_log[j], c_dt_bias[j], c_o_gain[j], c_w_out[j])
        x = x + y
        x = x + squared_relu_mlp(rms_norm(x, norm_mlp[i]), mlp_w1[i], mlp_w2[i])
    return x
```

```python
import functools
import math

import numpy as np
import jax
import jax.numpy as jnp
from jax import lax
from jax.experimental import pallas as pl
from jax.experimental.pallas import tpu as pltpu

F32 = jnp.float32
BF16 = jnp.bfloat16
I32 = jnp.int32

EPS = 1e-6
HEAD_DIM = 64
NEG = -1e30
INT_MIN = -2 ** 31

V7X_VMEM_BYTES = 64 * 1024 * 1024
VMEM_LIMIT = V7X_VMEM_BYTES - 8 * 1024 * 1024

NUM_BUCKETS = 32
MAX_DISTANCE = 2048

A_GROUPS = ((128, 1), (512, 4), (2048, 16))
A_HEADS = 8
A_BLOCK = 128
A_GW = A_HEADS * HEAD_DIM

B_HEADS = 16
B_KV_HEADS = 4
B_IDX_HEADS = 8
B_IDX_DIM = 64
B_TOPK = 256
B_QT = 128
B_KC = 256
B_NDELTA = 14

C_HEADS = 8
C_DK = 128
C_CONV = 4
C_CHUNK = 64


def _cparams(sem):
    return pltpu.CompilerParams(dimension_semantics=sem, vmem_limit_bytes=VMEM_LIMIT)


def _t5_bucket(dist):
    max_exact = NUM_BUCKETS // 2
    d = jnp.maximum(dist, 1).astype(F32)
    log_part = jnp.log(d / max_exact) / math.log(MAX_DISTANCE / max_exact) * (NUM_BUCKETS - max_exact)
    large = jnp.minimum(max_exact + log_part.astype(I32), NUM_BUCKETS - 1)
    return jnp.where(dist < max_exact, dist, large)


def _rms_rows(x, gain_row):
    ms = jnp.mean(x * x, axis=-1, keepdims=True)
    return x * lax.rsqrt(ms + EPS) * gain_row


def _in_proj_kernel(x_ref, g_ref, w_ref, hg_ref, bd_ref, *rest, n_norm):
    outs, hn_ref = rest[:-1], rest[-1]
    j = pl.program_id(1)

    @pl.when(j == 0)
    def _():
        hn_ref[...] = _rms_rows(x_ref[...], g_ref[...]).astype(BF16)

    y = jnp.dot(hn_ref[...], w_ref[...], preferred_element_type=F32)

    def write(val):
        for o in outs:
            o[...] = val.astype(o.dtype)

    if n_norm == 0:
        write(y)
    else:
        @pl.when(j < n_norm)
        def _():
            ms = jnp.dot((y * y).astype(BF16), bd_ref[...], preferred_element_type=F32) * (1.0 / HEAD_DIM)
            write(y * lax.rsqrt(ms + EPS) * hg_ref[...])

        @pl.when(j >= n_norm)
        def _():
            write(y)


def _in_proj(x2, gain, w, head_gain, *, n_norm_cols, tn, out_dtypes, tm=512):
    n, d = x2.shape
    kout = w.shape[1]
    bd = np.kron(np.eye(tn // HEAD_DIM), np.ones((HEAD_DIM, HEAD_DIM))).astype(np.float32)
    return pl.pallas_call(
        functools.partial(_in_proj_kernel, n_norm=n_norm_cols // tn),
        out_shape=[jax.ShapeDtypeStruct((n, kout), dt) for dt in out_dtypes],
        grid=(n // tm, kout // tn),
        in_specs=[
            pl.BlockSpec((tm, d), lambda i, j: (i, 0)),
            pl.BlockSpec((1, d), lambda i, j: (0, 0)),
            pl.BlockSpec((d, tn), lambda i, j: (0, j)),
            pl.BlockSpec((1, tn), lambda i, j: (0, j)),
            pl.BlockSpec((tn, tn), lambda i, j: (0, 0)),
        ],
        out_specs=[pl.BlockSpec((tm, tn), lambda i, j: (i, j)) for _ in out_dtypes],
        scratch_shapes=[pltpu.VMEM((tm, d), BF16)],
        compiler_params=_cparams(("parallel", "arbitrary")),
        name="in_proj",
    )(x2, gain.reshape(1, d), w, head_gain, jnp.asarray(bd, BF16))


def _out_proj_kernel(y_ref, w_ref, x_ref, o_ref):
    o_ref[...] = x_ref[...] + jnp.dot(y_ref[...].astype(BF16), w_ref[...], preferred_element_type=F32)


def _out_proj(y2, w, x2, tm=512):
    n, kin = y2.shape
    d = w.shape[1]
    return pl.pallas_call(
        _out_proj_kernel,
        out_shape=jax.ShapeDtypeStruct((n, d), F32),
        grid=(n // tm,),
        in_specs=[
            pl.BlockSpec((tm, kin), lambda i: (i, 0)),
            pl.BlockSpec((kin, d), lambda i: (0, 0)),
            pl.BlockSpec((tm, d), lambda i: (i, 0)),
        ],
        out_specs=pl.BlockSpec((tm, d), lambda i: (i, 0)),
        compiler_params=_cparams(("parallel",)),
        name="out_proj",
    )(y2, w, x2)


def _out_proj_a_kernel(o0, o1, o2, l0, l1, l2, w_ref, x_ref, out_ref):
    a, b, c = l0[...], l1[...], l2[...]
    m = jnp.maximum(jnp.maximum(a, b), c)
    ea, eb, ec = jnp.exp(a - m), jnp.exp(b - m), jnp.exp(c - m)
    y = (ea * o0[...] + eb * o1[...] + ec * o2[...]) / (ea + eb + ec)
    out_ref[...] = x_ref[...] + jnp.dot(y.astype(BF16), w_ref[...], preferred_element_type=F32)


def _out_proj_a(os_, ls_, w, x2, tm=512):
    n, d = x2.shape
    row = pl.BlockSpec((tm, A_GW), lambda i: (i, 0))
    return pl.pallas_call(
        _out_proj_a_kernel,
        out_shape=jax.ShapeDtypeStruct((n, d), F32),
        grid=(n // tm,),
        in_specs=[row] * 6 + [
            pl.BlockSpec((A_GW, d), lambda i: (0, 0)),
            pl.BlockSpec((tm, d), lambda i: (i, 0)),
        ],
        out_specs=pl.BlockSpec((tm, d), lambda i: (i, 0)),
        compiler_params=_cparams(("parallel",)),
        name="out_proj_a",
    )(*os_, *ls_, w, x2)


def _mlp_kernel(x_ref, g_ref, w1_ref, w2_ref, o_ref, hn_ref):
    f = pl.program_id(1)

    @pl.when(f == 0)
    def _():
        x = x_ref[...]
        hn_ref[...] = _rms_rows(x, g_ref[...]).astype(BF16)
        o_ref[...] = x

    h = jnp.maximum(jnp.dot(hn_ref[...], w1_ref[...], preferred_element_type=F32), 0.0)
    o_ref[...] += jnp.dot((h * h).astype(BF16), w2_ref[...], preferred_element_type=F32)


def _mlp(x2, gain, w1, w2, tm=1024, tf=1024):
    n, d = x2.shape
    dff = w1.shape[1]
    return pl.pallas_call(
        _mlp_kernel,
        out_shape=jax.ShapeDtypeStruct((n, d), F32),
        grid=(n // tm, dff // tf),
        in_specs=[
            pl.BlockSpec((tm, d), lambda i, f: (i, 0)),
            pl.BlockSpec((1, d), lambda i, f: (0, 0)),
            pl.BlockSpec((d, tf), lambda i, f: (0, f)),
            pl.BlockSpec((tf, d), lambda i, f: (f, 0)),
        ],
        out_specs=pl.BlockSpec((tm, d), lambda i, f: (i, 0)),
        scratch_shapes=[pltpu.VMEM((tm, d), BF16)],
        compiler_params=_cparams(("parallel", "arbitrary")),
        name="mlp",
    )(x2, gain.reshape(1, d), w1, w2)


def _attn_a_kernel(q_ref, kp_ref, kc_ref, vp_ref, vc_ref, bias_ref, o_ref, lse_ref):
    n = pl.program_id(2)
    blk = A_BLOCK
    row = lax.broadcasted_iota(I32, (blk, 2 * blk), 0)
    col = lax.broadcasted_iota(I32, (blk, 2 * blk), 1)
    valid = (col >= jnp.maximum(row, jnp.where(n > 0, 0, blk))) & (col <= row + blk)
    q = q_ref[...]
    outs, lses = [], []
    for h in range(A_HEADS):
        sl = slice(h * HEAD_DIM, (h + 1) * HEAD_DIM)
        kh = jnp.concatenate([kp_ref[:, sl], kc_ref[:, sl]], axis=0)
        vh = jnp.concatenate([vp_ref[:, sl], vc_ref[:, sl]], axis=0)
        s = lax.dot_general(q[:, sl], kh, (((1,), (1,)), ((), ())), preferred_element_type=F32)
        s = jnp.where(valid, s * (HEAD_DIM ** -0.5) + bias_ref[h], NEG)
        m = jnp.max(s, axis=-1, keepdims=True)
        p = jnp.exp(s - m)
        l = jnp.sum(p, axis=-1, keepdims=True)
        o = jnp.dot(p.astype(BF16), vh, preferred_element_type=F32) / l
        outs.append(o)
        lses.append(jnp.broadcast_to(m + jnp.log(l), (blk, HEAD_DIM)))
    o_ref[...] = jnp.concatenate(outs, axis=-1)
    lse_ref[...] = jnp.concatenate(lses, axis=-1)


def _attn_a_group(qkv, bias_tab, g, dilation, batch, seq):
    n_groups = len(A_GROUPS)
    sub = seq // dilation
    nb = sub // A_BLOCK
    ncol = 3 * n_groups
    view = qkv.reshape(batch, sub, dilation * ncol * A_GW)

    def spec(which, prev):
        def imap(b, r, n):
            return (b, jnp.maximum(n - 1, 0) if prev else n, r * ncol + which * n_groups + g)
        return pl.BlockSpec((None, A_BLOCK, A_GW), imap)

    out_spec = pl.BlockSpec((None, A_BLOCK, A_GW), lambda b, r, n: (b, n, r))
    out_sds = jax.ShapeDtypeStruct((batch, sub, dilation * A_GW), F32)
    o, lse = pl.pallas_call(
        _attn_a_kernel,
        out_shape=[out_sds, out_sds],
        grid=(batch, dilation, nb),
        in_specs=[spec(0, False), spec(1, True), spec(1, False), spec(2, True), spec(2, False),
                  pl.BlockSpec((A_HEADS, A_BLOCK, 2 * A_BLOCK), lambda b, r, n: (0, 0, 0))],
        out_specs=[out_spec, out_spec],
        compiler_params=_cparams(("parallel", "parallel", "arbitrary")),
        name=f"attn_a_d{dilation}",
    )(view, view, view, view, view, bias_tab)
    return o.reshape(batch * seq, A_GW), lse.reshape(batch * seq, A_GW)


def _bias_a(tab_g, dilation):
    step = np.arange(A_BLOCK)[:, None] - np.arange(2 * A_BLOCK)[None, :] + A_BLOCK
    bucket = _t5_bucket(jnp.asarray(np.maximum(step, 0) * dilation, I32))
    return tab_g[bucket].transpose(2, 0, 1).astype(F32)


def _mixer_a(x2, gain, w_in, q_gain, k_gain, w_out, bias_a, batch, seq):
    n_groups = len(A_GROUPS)
    reps = n_groups * A_HEADS
    hg = jnp.concatenate([jnp.tile(q_gain, reps), jnp.tile(k_gain, reps),
                          jnp.ones((n_groups * A_GW,), F32)]).reshape(1, -1)
    (qkv,) = _in_proj(x2, gain, w_in.astype(BF16), hg, n_norm_cols=2 * n_groups * A_GW,
                      tn=A_GW, out_dtypes=(BF16,))
    os_, ls_ = [], []
    for g, (window, dilation) in enumerate(A_GROUPS):
        assert window // dilation == A_BLOCK and (seq // dilation) % A_BLOCK == 0
        tab_g = bias_a[:, g * A_HEADS:(g + 1) * A_HEADS]
        o, lse = _attn_a_group(qkv, _bias_a(tab_g, dilation), g, dilation, batch, seq)
        os_.append(o)
        ls_.append(lse)
    return _out_proj_a(os_, ls_, w_out.astype(BF16), x2)


def _float_key(s):
    i = lax.bitcast_convert_type(s, I32)
    k = jnp.where(i < 0, i ^ jnp.int32(0x7FFFFFFF), i)
    return jnp.where(s == 0.0, 0, k)


def _attn_b_kernel(q_ref, k_ref, v_ref, qi_ref, kw_ref, bias_ref, o_ref, key_ref, msk_ref):
    qb = pl.program_id(1)
    nch = qb // 2 + 1
    t_row = qb * B_QT + lax.broadcasted_iota(I32, (B_QT, B_KC), 0)
    lane = lax.broadcasted_iota(I32, (B_QT, B_KC), 1)

    w = kw_ref[pl.ds(pl.multiple_of(qb * B_QT, B_QT), B_QT), B_IDX_DIM:B_IDX_DIM + B_IDX_HEADS]
    w = w * (B_IDX_HEADS ** -0.5 * B_IDX_DIM ** -0.5)
    qi = qi_ref[...]

    def score_chunk(c, carry):
        off = pl.multiple_of(c * B_KC, B_KC)
        ki = kw_ref[pl.ds(off, B_KC), 0:B_IDX_DIM]
        acc = jnp.zeros((B_QT, B_KC), F32)
        for h in range(B_IDX_HEADS):
            sc = lax.dot_general(qi[:, h * B_IDX_DIM:(h + 1) * B_IDX_DIM], ki,
                                 (((1,), (1,)), ((), ())), preferred_element_type=F32,
                                 precision=lax.Precision.HIGHEST)
            acc = acc + w[:, h:h + 1] * jnp.maximum(sc, 0.0)
        key = jnp.where(off + lane <= t_row, _float_key(acc), INT_MIN)
        key_ref[:, pl.ds(off, B_KC)] = key
        return carry

    lax.fori_loop(0, nch, score_chunk, 0)

    def count(pred_fn):
        def body(c, acc):
            off = pl.multiple_of(c * B_KC, B_KC)
            return acc + jnp.where(pred_fn(key_ref[:, pl.ds(off, B_KC)], off + lane), 1, 0)
        acc = lax.fori_loop(0, nch, body, jnp.zeros((B_QT, B_KC), I32))
        return jnp.sum(acc, axis=-1, keepdims=True)

    def thr_bit(it, lo):
        cand = lo + jnp.left_shift(jnp.int32(1), 31 - it)
        cnt = count(lambda key, _: key >= cand)
        return jnp.where(cnt >= B_TOPK, cand, lo)

    thr = lax.fori_loop(0, 32, thr_bit, jnp.full((B_QT, 1), INT_MIN, I32))
    need = B_TOPK - count(lambda key, _: key > thr)
    n_eq = count(lambda key, _: key == thr)

    def tie_search(_):
        def bit(it, j):
            cand = j + jnp.left_shift(jnp.int32(1), 12 - it)
            cnt = count(lambda key, col: (key == thr) & (col < cand))
            return jnp.where(cnt <= need, cand, j)
        return lax.fori_loop(0, 13, bit, jnp.zeros((B_QT, 1), I32))

    any_tie = jnp.max(jnp.where(n_eq > need, 1, 0)) > 0
    j_max = lax.cond(any_tie, tie_search, lambda _: jnp.full((B_QT, 1), 2 ** 13, I32), 0)

    def mask_chunk(c, carry):
        off = pl.multiple_of(c * B_KC, B_KC)
        key = key_ref[:, pl.ds(off, B_KC)]
        col = off + lane
        sel = ((key > thr) | ((key == thr) & (col < j_max))) & (col <= t_row)
        msk_ref[:, pl.ds(off, B_KC)] = jnp.where(sel, 0.0, NEG)
        return carry

    lax.fori_loop(0, nch, mask_chunk, 0)

    grp = B_HEADS // B_KV_HEADS
    q = q_ref[...] * (HEAD_DIM ** -0.5)
    outs = []
    for g in range(B_KV_HEADS):
        qg = jnp.concatenate([q[:, (g * grp + j) * HEAD_DIM:(g * grp + j + 1) * HEAD_DIM]
                              for j in range(grp)], axis=0)
        ksl = slice(g * HEAD_DIM, (g + 1) * HEAD_DIM)

        def flash(c, carry, qg=qg, ksl=ksl, g=g):
            m, l, acc = carry
            off = pl.multiple_of(c * B_KC, B_KC)
            s = lax.dot_general(qg, k_ref[pl.ds(off, B_KC), ksl], (((1,), (1,)), ((), ())),
                                preferred_element_type=F32).reshape(grp, B_QT, B_KC)
            d_lo = jnp.clip(qb - 2 * c, 0, B_NDELTA - 1)
            d_hi = jnp.clip(qb - 2 * c - 1, 0, B_NDELTA - 1)
            bias = jnp.concatenate([bias_ref[d_lo, g * grp:(g + 1) * grp],
                                    bias_ref[d_hi, g * grp:(g + 1) * grp]], axis=-1)
            s = s + bias + msk_ref[:, pl.ds(off, B_KC)][None]
            m_new = jnp.maximum(m, jnp.max(s, axis=-1, keepdims=True))
            alpha = jnp.exp(m - m_new)
            p = jnp.exp(s - m_new)
            l = alpha * l + jnp.sum(p, axis=-1, keepdims=True)
            pv = jnp.dot(p.reshape(grp * B_QT, B_KC).astype(BF16), v_ref[pl.ds(off, B_KC), ksl],
                         preferred_element_type=F32).reshape(grp, B_QT, HEAD_DIM)
            return m_new, l, alpha * acc + pv

        m0 = jnp.full((grp, B_QT, 1), NEG, F32)
        l0 = jnp.zeros((grp, B_QT, 1), F32)
        a0 = jnp.zeros((grp, B_QT, HEAD_DIM), F32)
        _, l, acc = lax.fori_loop(0, nch, flash, (m0, l0, a0))
        o = acc / l
        outs.extend(o[j] for j in range(grp))
    o_ref[...] = jnp.concatenate(outs, axis=-1).astype(o_ref.dtype)


def _bias_b(tab):
    dist = (np.arange(B_NDELTA)[:, None, None] * B_QT + np.arange(B_QT)[None, :, None]
            - np.arange(B_QT)[None, None, :])
    assert (B_NDELTA - 1) * B_QT - (B_QT - 1) >= 16 * 128 ** (15.0 / 16.0) + 1
    bucket = _t5_bucket(jnp.asarray(np.maximum(dist, 0), I32))
    return tab[bucket].transpose(0, 3, 1, 2).astype(F32)


def _mixer_b(x2, gain, w_in, q_gain, k_gain, w_out, bias_b, batch, seq):
    d = x2.shape[1]
    nq, nkv = B_HEADS * HEAD_DIM, B_KV_HEADS * HEAD_DIM
    nidx = B_IDX_HEADS * B_IDX_DIM
    tn = 256
    used = 2 * nkv + nq + nidx + B_IDX_DIM + B_IDX_HEADS
    kout = -(-used // tn) * tn
    w = jnp.concatenate([w_in, jnp.zeros((d, kout - used), F32)], axis=1).astype(BF16)
    hg = jnp.concatenate([jnp.tile(q_gain, B_HEADS), jnp.tile(k_gain, B_KV_HEADS),
                          jnp.ones((kout - nq - nkv,), F32)]).reshape(1, -1)
    p32, p16 = _in_proj(x2, gain, w, hg, n_norm_cols=nq + nkv, tn=tn, out_dtypes=(F32, BF16))
    p32 = p32.reshape(batch, seq, kout)
    p16 = p16.reshape(batch, seq, kout)
    assert seq % (2 * B_QT) == 0 and seq <= 2 ** 12
    kw_blk = (nq + 2 * nkv + nidx) // tn
    y = pl.pallas_call(
        _attn_b_kernel,
        out_shape=jax.ShapeDtypeStruct((batch, seq, nq), BF16),
        grid=(batch, seq // B_QT),
        in_specs=[
            pl.BlockSpec((None, B_QT, nq), lambda b, i: (b, i, 0)),
            pl.BlockSpec((None, seq, nkv), lambda b, i: (b, 0, nq // nkv)),
            pl.BlockSpec((None, seq, nkv), lambda b, i: (b, 0, nq // nkv + 1)),
            pl.BlockSpec((None, B_QT, nidx), lambda b, i: (b, i, (nq + 2 * nkv) // nidx)),
            pl.BlockSpec((None, seq, tn), lambda b, i: (b, 0, kw_blk)),
            pl.BlockSpec((B_NDELTA, B_HEADS, B_QT, B_QT), lambda b, i: (0, 0, 0, 0),
                         pipeline_mode=pl.Buffered(1)),
        ],
        out_specs=pl.BlockSpec((None, B_QT, nq), lambda b, i: (b, i, 0)),
        scratch_shapes=[pltpu.VMEM((B_QT, seq), I32), pltpu.VMEM((B_QT, seq), F32)],
        compiler_params=_cparams(("parallel", "arbitrary")),
        name="attn_b",
    )(p16, p16, p16, p32, p32, _bias_b(bias_b))
    return _out_proj(y.reshape(batch * seq, nq), w_out.astype(BF16), x2)


def _conv_c_kernel(x_ref, halo_ref, cw_ref, o_ref):
    i, j = pl.program_id(1), pl.program_id(2)
    ts = x_ref.shape[0]
    halo = jnp.where(i > 0, halo_ref[...], 0.0)
    xs = jnp.concatenate([halo, x_ref[...]], axis=0)
    cw = cw_ref[...]
    conv = sum(cw[t:t + 1, :] * xs[8 - (C_CONV - 1) + t:8 - (C_CONV - 1) + t + ts, :]
               for t in range(C_CONV))
    y = conv * jax.nn.sigmoid(conv)

    @pl.when(j == 2)
    def _():
        o_ref[...] = y

    @pl.when(j < 2)
    def _():
        scale = jnp.where(j == 0, C_DK ** -0.5, 1.0)
        for h in range(C_HEADS):
            sl = slice(h * C_DK, (h + 1) * C_DK)
            yh = y[:, sl]
            ss = jnp.sum(yh * yh, axis=-1, keepdims=True)
            o_ref[:, sl] = yh * (lax.rsqrt(ss + EPS) * scale)


def _delta_kernel(q_ref, k_ref, v_ref, gate_ref, a_ref, b_ref, alog_ref, dtb_ref, og_ref, y_ref,
                  u_s, w_s, qg_s, qk_s, kgt_s, el_s, o_s):
    cs = C_CHUNK
    n_chunks = q_ref.shape[0] // cs
    hi = lax.Precision.HIGHEST
    r = lax.broadcasted_iota(I32, (cs, cs), 0)
    c_ = lax.broadcasted_iota(I32, (cs, cs), 1)
    lower, strict, eye = r >= c_, r > c_, r == c_
    upper_incl = jnp.where(r <= c_, 1.0, 0.0)

    z = a_ref[...] + dtb_ref[...]
    softplus = jnp.maximum(z, 0.0) + jnp.log(1.0 + jnp.exp(-jnp.abs(z)))
    g_all = -jnp.exp(alog_ref[...]) * softplus
    gc_all = jnp.dot(g_all, upper_incl, preferred_element_type=F32, precision=hi)
    beta_all = jax.nn.sigmoid(b_ref[...])
    row_id = lax.broadcasted_iota(I32, (n_chunks, cs), 0)

    def to_col(row):
        return jnp.sum(jnp.where(eye, jnp.broadcast_to(row, (cs, cs)), 0.0), axis=1, keepdims=True)

    def pick(mat, c):
        return jnp.sum(jnp.where(row_id == c, mat, 0.0), axis=0, keepdims=True)

    def mm(a, b):
        return jnp.dot(a.astype(BF16), b.astype(BF16), preferred_element_type=F32)

    def mm_nt(a, b):
        return lax.dot_general(a.astype(BF16), b.astype(BF16), (((1,), (1,)), ((), ())),
                               preferred_element_type=F32)

    def mm_hi(a, b):
        return jnp.dot(a, b, preferred_element_type=F32, precision=hi)

    def prep(c, carry):
        rows = pl.ds(pl.multiple_of(c * cs, cs), cs)
        q, k, v = q_ref[rows, :], k_ref[rows, :], v_ref[rows, :]
        gc_row, beta_row = pick(gc_all, c), pick(beta_all, c)
        gc_col, beta_col = to_col(gc_row), to_col(beta_row)
        g_last = gc_row[:, cs - 1:cs]
        decay = jnp.where(lower, jnp.exp(jnp.where(lower, gc_col - gc_row, 0.0)), 0.0)
        kb = k * beta_col
        a_mat = jnp.where(strict, mm_nt(kb, k) * decay, 0.0)
        t_mat = jnp.where(eye, 1.0, 0.0) - a_mat
        pw = a_mat
        for _ in range(int(math.log2(cs)) - 1):
            pw = mm_hi(pw, pw)
            t_mat = t_mat + mm_hi(t_mat, pw)
        u_s[rows, :] = mm(t_mat, v * beta_col)
        w_s[rows, :] = mm(t_mat, kb * jnp.exp(gc_col)).astype(BF16)
        qk_s[rows, :] = jnp.where(lower, mm_nt(q, k) * decay, 0.0).astype(BF16)
        qg_s[rows, :] = (q * jnp.exp(gc_col)).astype(BF16)
        kgt_s[c] = (k * jnp.exp(g_last - gc_col)).T.astype(BF16)
        el_s[pl.ds(c, 1), :] = jnp.broadcast_to(jnp.exp(g_last), (1, el_s.shape[1]))
        return carry

    lax.fori_loop(0, n_chunks, prep, 0)

    def scan(c, state):
        rows = pl.ds(pl.multiple_of(c * cs, cs), cs)
        sb = state.astype(BF16)
        v_new = u_s[rows, :] - jnp.dot(w_s[rows, :], sb, preferred_element_type=F32)
        vb = v_new.astype(BF16)
        o_s[rows, :] = (jnp.dot(qg_s[rows, :], sb, preferred_element_type=F32)
                        + jnp.dot(qk_s[rows, :], vb, preferred_element_type=F32))
        return state * el_s[pl.ds(c, 1), :] + jnp.dot(kgt_s[c], vb, preferred_element_type=F32)

    lax.fori_loop(0, n_chunks, scan, jnp.zeros((C_DK, v_ref.shape[1]), F32))

    tr = 512

    def finish(t, carry):
        rows = pl.ds(pl.multiple_of(t * tr, tr), tr)
        gate = gate_ref[rows, :]
        y_ref[rows, :] = (_rms_rows(o_s[rows, :], og_ref[...]) * (gate * jax.nn.sigmoid(gate))).astype(y_ref.dtype)
        return carry

    lax.fori_loop(0, q_ref.shape[0] // tr, finish, 0)


def _mixer_c(x2, gain, w_in, conv_w, a_log, dt_bias, o_gain, w_out, batch, seq):
    d = x2.shape[1]
    hw = C_HEADS * C_DK
    tn = 256
    used = 4 * hw + 2 * C_HEADS
    kout = -(-used // tn) * tn
    w = jnp.concatenate([w_in, jnp.zeros((d, kout - used), F32)], axis=1).astype(BF16)
    (proj,) = _in_proj(x2, gain, w, jnp.ones((1, kout), F32), n_norm_cols=0, tn=tn, out_dtypes=(F32,))
    proj = proj.reshape(batch, seq, kout)

    ts = 512
    qkv = pl.pallas_call(
        _conv_c_kernel,
        out_shape=jax.ShapeDtypeStruct((batch, seq, 3 * hw), F32),
        grid=(batch, seq // ts, 3),
        in_specs=[
            pl.BlockSpec((None, ts, hw), lambda b, i, j: (b, i, j)),
            pl.BlockSpec((None, 8, hw), lambda b, i, j: (b, jnp.maximum(i * (ts // 8) - 1, 0), j)),
            pl.BlockSpec((C_CONV, hw), lambda b, i, j: (0, j)),
        ],
        out_specs=pl.BlockSpec((None, ts, hw), lambda b, i, j: (b, i, j)),
        compiler_params=_cparams(("parallel", "parallel", "arbitrary")),
        name="conv_c",
    )(proj, proj, conv_w)

    n_chunks = seq // C_CHUNK
    ba = proj[:, :, 4 * hw:4 * hw + 2 * C_HEADS].transpose(0, 2, 1).reshape(batch, 2 * C_HEADS, n_chunks, C_CHUNK)
    per_head = lambda v: jnp.broadcast_to(v.reshape(C_HEADS, 1, 1), (C_HEADS, 1, C_CHUNK))
    head_cols = lambda off: pl.BlockSpec((None, seq, C_DK), lambda b, h: (b, 0, off + h))
    small = lambda off: pl.BlockSpec((None, None, n_chunks, C_CHUNK), lambda b, h: (b, off + h, 0, 0))
    scalar_row = pl.BlockSpec((None, 1, C_CHUNK), lambda b, h: (h, 0, 0))
    y = pl.pallas_call(
        _delta_kernel,
        out_shape=jax.ShapeDtypeStruct((batch, seq, hw), BF16),
        grid=(batch, C_HEADS),
        in_specs=[head_cols(0), head_cols(C_HEADS), head_cols(2 * C_HEADS),
                  pl.BlockSpec((None, seq, C_DK), lambda b, h: (b, 0, 3 * C_HEADS + h)),
                  small(C_HEADS), small(0), scalar_row, scalar_row,
                  pl.BlockSpec((1, C_DK), lambda b, h: (0, 0))],
        out_specs=head_cols(0),
        scratch_shapes=[
            pltpu.VMEM((seq, C_DK), F32),
            pltpu.VMEM((seq, C_DK), BF16),
            pltpu.VMEM((seq, C_DK), BF16),
            pltpu.VMEM((seq, C_CHUNK), BF16),
            pltpu.VMEM((n_chunks, C_DK, C_CHUNK), BF16),
            pltpu.VMEM((n_chunks, C_DK), F32),
            pltpu.VMEM((seq, C_DK), F32),
        ],
        compiler_params=_cparams(("parallel", "parallel")),
        name="delta_c",
    )(qkv, qkv, qkv, proj, ba, ba, per_head(a_log), per_head(dt_bias), o_gain.reshape(1, C_DK))
    return _out_proj(y.reshape(batch * seq, hw), w_out.astype(BF16), x2)


def kernel(x, rel_bias, norm_mix, norm_mlp, mlp_w1, mlp_w2, a_w_in, a_q_gain, a_k_gain, a_w_out,
           b_w_in, b_q_gain, b_k_gain, b_w_out, c_w_in, c_conv_w, c_a_log, c_dt_bias, c_o_gain, c_w_out):
    batch, seq, d = x.shape
    depth = norm_mix.shape[0]
    a_cols = len(A_GROUPS) * A_HEADS
    bias_a, bias_b = rel_bias[:, :a_cols], rel_bias[:, a_cols:]
    x2 = x.reshape(batch * seq, d)
    for i in range(depth):
        kind, j = i % 3, i // 3
        if kind == 0:
            x2 = _mixer_a(x2, norm_mix[i], a_w_in[j], a_q_gain[j], a_k_gain[j], a_w_out[j], bias_a, batch, seq)
        elif kind == 1:
            x2 = _mixer_b(x2, norm_mix[i], b_w_in[j], b_q_gain[j], b_k_gain[j], b_w_out[j], bias_b, batch, seq)
        else:
            x2 = _mixer_c(x2, norm_mix[i], c_w_in[j], c_conv_w[j], c_a_log[j], c_dt_bias[j], c_o_gain[j],
                          c_w_out[j], batch, seq)
        x2 = _mlp(x2, norm_mlp[i], mlp_w1[i].astype(BF16), mlp_w2[i].astype(BF16))
    return x2.reshape(batch, seq, d)
```

```python
import functools
import math

import numpy as np
import jax
import jax.numpy as jnp
from jax import lax
from jax.experimental import pallas as pl
from jax.experimental.pallas import tpu as pltpu

F32 = jnp.float32
BF16 = jnp.bfloat16
I32 = jnp.int32

EPS = 1e-6
HEAD_DIM = 64
NEG = -1e30
INT_MIN = -2 ** 31

V7X_VMEM_BYTES = 64 * 1024 * 1024
VMEM_LIMIT = V7X_VMEM_BYTES - 8 * 1024 * 1024

NUM_BUCKETS = 32
MAX_DISTANCE = 2048

A_GROUPS = ((128, 1), (512, 4), (2048, 16))
A_HEADS = 8
A_BLOCK = 128
A_GW = A_HEADS * HEAD_DIM

B_HEADS = 16
B_KV_HEADS = 4
B_IDX_HEADS = 8
B_IDX_DIM = 64
B_TOPK = 256
B_QT = 128
B_KC = 256
B_NDELTA = 14

C_HEADS = 8
C_DK = 128
C_CONV = 4
C_CHUNK = 64
C_GROUP = 4 * C_CHUNK


def _cparams(sem):
    return pltpu.CompilerParams(dimension_semantics=sem, vmem_limit_bytes=VMEM_LIMIT)


def _t5_bucket(dist):
    max_exact = NUM_BUCKETS // 2
    d = jnp.maximum(dist, 1).astype(F32)
    log_part = jnp.log(d / max_exact) / math.log(MAX_DISTANCE / max_exact) * (NUM_BUCKETS - max_exact)
    large = jnp.minimum(max_exact + log_part.astype(I32), NUM_BUCKETS - 1)
    return jnp.where(dist < max_exact, dist, large)


def _rms_rows(x, gain_row):
    ms = jnp.mean(x * x, axis=-1, keepdims=True)
    return x * lax.rsqrt(ms + EPS) * gain_row


def _in_proj_kernel(x_ref, g_ref, w_ref, hg_ref, bd_ref, *rest, n_norm):
    outs, hn_ref = rest[:-1], rest[-1]
    j = pl.program_id(1)

    @pl.when(j == 0)
    def _():
        hn_ref[...] = _rms_rows(x_ref[...], g_ref[...]).astype(BF16)

    y = jnp.dot(hn_ref[...], w_ref[...], preferred_element_type=F32)

    def write(val):
        for o in outs:
            o[...] = val.astype(o.dtype)

    if n_norm == 0:
        write(y)
    else:
        @pl.when(j < n_norm)
        def _():
            ms = jnp.dot((y * y).astype(BF16), bd_ref[...], preferred_element_type=F32) * (1.0 / HEAD_DIM)
            write(y * lax.rsqrt(ms + EPS) * hg_ref[...])

        @pl.when(j >= n_norm)
        def _():
            write(y)


def _in_proj(x2, gain, w, head_gain, *, n_norm_cols, tn, out_dtypes, tm=512):
    n, d = x2.shape
    kout = w.shape[1]
    bd = np.kron(np.eye(tn // HEAD_DIM), np.ones((HEAD_DIM, HEAD_DIM))).astype(np.float32)
    return pl.pallas_call(
        functools.partial(_in_proj_kernel, n_norm=n_norm_cols // tn),
        out_shape=[jax.ShapeDtypeStruct((n, kout), dt) for dt in out_dtypes],
        grid=(n // tm, kout // tn),
        in_specs=[
            pl.BlockSpec((tm, d), lambda i, j: (i, 0)),
            pl.BlockSpec((1, d), lambda i, j: (0, 0)),
            pl.BlockSpec((d, tn), lambda i, j: (0, j)),
            pl.BlockSpec((1, tn), lambda i, j: (0, j)),
            pl.BlockSpec((tn, tn), lambda i, j: (0, 0)),
        ],
        out_specs=[pl.BlockSpec((tm, tn), lambda i, j: (i, j)) for _ in out_dtypes],
        scratch_shapes=[pltpu.VMEM((tm, d), BF16)],
        compiler_params=_cparams(("parallel", "arbitrary")),
        name="in_proj",
    )(x2, gain.reshape(1, d), w, head_gain, jnp.asarray(bd, BF16))


def _out_proj_kernel(y_ref, w_ref, x_ref, o_ref):
    o_ref[...] = x_ref[...] + jnp.dot(y_ref[...].astype(BF16), w_ref[...], preferred_element_type=F32)


def _out_proj(y2, w, x2, tm=512):
    n, kin = y2.shape
    d = w.shape[1]
    return pl.pallas_call(
        _out_proj_kernel,
        out_shape=jax.ShapeDtypeStruct((n, d), F32),
        grid=(n // tm,),
        in_specs=[
            pl.BlockSpec((tm, kin), lambda i: (i, 0)),
            pl.BlockSpec((kin, d), lambda i: (0, 0)),
            pl.BlockSpec((tm, d), lambda i: (i, 0)),
        ],
        out_specs=pl.BlockSpec((tm, d), lambda i: (i, 0)),
        compiler_params=_cparams(("parallel",)),
        name="out_proj",
    )(y2, w, x2)


def _out_proj_a_kernel(o0, o1, o2, l0, l1, l2, w_ref, x_ref, out_ref):
    a, b, c = l0[...], l1[...], l2[...]
    m = jnp.maximum(jnp.maximum(a, b), c)
    ea, eb, ec = jnp.exp(a - m), jnp.exp(b - m), jnp.exp(c - m)
    y = (ea * o0[...] + eb * o1[...] + ec * o2[...]) / (ea + eb + ec)
    out_ref[...] = x_ref[...] + jnp.dot(y.astype(BF16), w_ref[...], preferred_element_type=F32)


def _out_proj_a(os_, ls_, w, x2, tm=512):
    n, d = x2.shape
    row = pl.BlockSpec((tm, A_GW), lambda i: (i, 0))
    return pl.pallas_call(
        _out_proj_a_kernel,
        out_shape=jax.ShapeDtypeStruct((n, d), F32),
        grid=(n // tm,),
        in_specs=[row] * 6 + [
            pl.BlockSpec((A_GW, d), lambda i: (0, 0)),
            pl.BlockSpec((tm, d), lambda i: (i, 0)),
        ],
        out_specs=pl.BlockSpec((tm, d), lambda i: (i, 0)),
        compiler_params=_cparams(("parallel",)),
        name="out_proj_a",
    )(*os_, *ls_, w, x2)


def _mlp_kernel(x_ref, g_ref, w1_ref, w2_ref, o_ref, hn_ref):
    f = pl.program_id(1)

    @pl.when(f == 0)
    def _():
        x = x_ref[...]
        hn_ref[...] = _rms_rows(x, g_ref[...]).astype(BF16)
        o_ref[...] = x

    h = jnp.maximum(jnp.dot(hn_ref[...], w1_ref[...], preferred_element_type=F32), 0.0)
    o_ref[...] += jnp.dot((h * h).astype(BF16), w2_ref[...], preferred_element_type=F32)


def _mlp(x2, gain, w1, w2, tm=1024, tf=1024):
    n, d = x2.shape
    dff = w1.shape[1]
    return pl.pallas_call(
        _mlp_kernel,
        out_shape=jax.ShapeDtypeStruct((n, d), F32),
        grid=(n // tm, dff // tf),
        in_specs=[
            pl.BlockSpec((tm, d), lambda i, f: (i, 0)),
            pl.BlockSpec((1, d), lambda i, f: (0, 0)),
            pl.BlockSpec((d, tf), lambda i, f: (0, f)),
            pl.BlockSpec((tf, d), lambda i, f: (f, 0)),
        ],
        out_specs=pl.BlockSpec((tm, d), lambda i, f: (i, 0)),
        scratch_shapes=[pltpu.VMEM((tm, d), BF16)],
        compiler_params=_cparams(("parallel", "arbitrary")),
        name="mlp",
    )(x2, gain.reshape(1, d), w1, w2)


def _attn_a_kernel(q_ref, kp_ref, kc_ref, vp_ref, vc_ref, bias_ref, o_ref, lse_ref):
    n = pl.program_id(2)
    blk = A_BLOCK
    row = lax.broadcasted_iota(I32, (blk, 2 * blk), 0)
    col = lax.broadcasted_iota(I32, (blk, 2 * blk), 1)
    valid = (col >= jnp.maximum(row, jnp.where(n > 0, 0, blk))) & (col <= row + blk)
    q = q_ref[...]
    outs, lses = [], []
    for h in range(A_HEADS):
        sl = slice(h * HEAD_DIM, (h + 1) * HEAD_DIM)
        kh = jnp.concatenate([kp_ref[:, sl], kc_ref[:, sl]], axis=0)
        vh = jnp.concatenate([vp_ref[:, sl], vc_ref[:, sl]], axis=0)
        s = lax.dot_general(q[:, sl], kh, (((1,), (1,)), ((), ())), preferred_element_type=F32)
        s = jnp.where(valid, s * (HEAD_DIM ** -0.5) + bias_ref[h], NEG)
        m = jnp.max(s, axis=-1, keepdims=True)
        p = jnp.exp(s - m)
        l = jnp.sum(p, axis=-1, keepdims=True)
        o = jnp.dot(p.astype(BF16), vh, preferred_element_type=F32) / l
        outs.append(o)
        lses.append(jnp.broadcast_to(m + jnp.log(l), (blk, HEAD_DIM)))
    o_ref[...] = jnp.concatenate(outs, axis=-1)
    lse_ref[...] = jnp.concatenate(lses, axis=-1)


def _attn_a_group(qkv, bias_tab, g, dilation, batch, seq):
    n_groups = len(A_GROUPS)
    sub = seq // dilation
    nb = sub // A_BLOCK
    ncol = 3 * n_groups
    view = qkv.reshape(batch, sub, dilation * ncol * A_GW)

    def spec(which, prev):
        def imap(b, r, n):
            return (b, jnp.maximum(n - 1, 0) if prev else n, r * ncol + which * n_groups + g)
        return pl.BlockSpec((None, A_BLOCK, A_GW), imap)

    out_spec = pl.BlockSpec((None, A_BLOCK, A_GW), lambda b, r, n: (b, n, r))
    out_sds = jax.ShapeDtypeStruct((batch, sub, dilation * A_GW), F32)
    o, lse = pl.pallas_call(
        _attn_a_kernel,
        out_shape=[out_sds, out_sds],
        grid=(batch, dilation, nb),
        in_specs=[spec(0, False), spec(1, True), spec(1, False), spec(2, True), spec(2, False),
                  pl.BlockSpec((A_HEADS, A_BLOCK, 2 * A_BLOCK), lambda b, r, n: (0, 0, 0))],
        out_specs=[out_spec, out_spec],
        compiler_params=_cparams(("parallel", "parallel", "arbitrary")),
        name=f"attn_a_d{dilation}",
    )(view, view, view, view, view, bias_tab)
    return o.reshape(batch * seq, A_GW), lse.reshape(batch * seq, A_GW)


def _bias_a(tab_g, dilation):
    step = np.arange(A_BLOCK)[:, None] - np.arange(2 * A_BLOCK)[None, :] + A_BLOCK
    bucket = _t5_bucket(jnp.asarray(np.maximum(step, 0) * dilation, I32))
    return tab_g[bucket].transpose(2, 0, 1).astype(F32)


def _mixer_a(x2, gain, w_in, q_gain, k_gain, w_out, bias_a, batch, seq):
    n_groups = len(A_GROUPS)
    reps = n_groups * A_HEADS
    hg = jnp.concatenate([jnp.tile(q_gain, reps), jnp.tile(k_gain, reps),
                          jnp.ones((n_groups * A_GW,), F32)]).reshape(1, -1)
    (qkv,) = _in_proj(x2, gain, w_in.astype(BF16), hg, n_norm_cols=2 * n_groups * A_GW,
                      tn=A_GW, out_dtypes=(BF16,))
    os_, ls_ = [], []
    for g, (window, dilation) in enumerate(A_GROUPS):
        assert window // dilation == A_BLOCK and (seq // dilation) % A_BLOCK == 0
        tab_g = bias_a[:, g * A_HEADS:(g + 1) * A_HEADS]
        o, lse = _attn_a_group(qkv, _bias_a(tab_g, dilation), g, dilation, batch, seq)
        os_.append(o)
        ls_.append(lse)
    return _out_proj_a(os_, ls_, w_out.astype(BF16), x2)


def _float_key(s):
    i = lax.bitcast_convert_type(s, I32)
    k = jnp.where(i < 0, i ^ jnp.int32(0x7FFFFFFF), i)
    return jnp.where(s == 0.0, 0, k)


def _attn_b_kernel(q_ref, k_ref, v_ref, qi_ref, kw_ref, bias_ref, o_ref, key_ref, msk_ref):
    qb = pl.program_id(1)
    nch = qb // 2 + 1
    t_row = qb * B_QT + lax.broadcasted_iota(I32, (B_QT, B_KC), 0)
    lane = lax.broadcasted_iota(I32, (B_QT, B_KC), 1)

    w = kw_ref[pl.ds(pl.multiple_of(qb * B_QT, B_QT), B_QT), B_IDX_DIM:B_IDX_DIM + B_IDX_HEADS]
    w = w * (B_IDX_HEADS ** -0.5 * B_IDX_DIM ** -0.5)
    qi = qi_ref[...]

    def score_chunk(c, carry):
        off = pl.multiple_of(c * B_KC, B_KC)
        ki = kw_ref[pl.ds(off, B_KC), 0:B_IDX_DIM]
        acc = jnp.zeros((B_QT, B_KC), F32)
        for h in range(B_IDX_HEADS):
            sc = lax.dot_general(qi[:, h * B_IDX_DIM:(h + 1) * B_IDX_DIM], ki,
                                 (((1,), (1,)), ((), ())), preferred_element_type=F32,
                                 precision=lax.Precision.HIGHEST)
            acc = acc + w[:, h:h + 1] * jnp.maximum(sc, 0.0)
        key = jnp.where(off + lane <= t_row, _float_key(acc), INT_MIN)
        key_ref[:, pl.ds(off, B_KC)] = key
        return carry

    lax.fori_loop(0, nch, score_chunk, 0)

    def count(pred_fn):
        def body(c, acc):
            off = pl.multiple_of(c * B_KC, B_KC)
            return acc + jnp.where(pred_fn(key_ref[:, pl.ds(off, B_KC)], off + lane), 1, 0)
        acc = lax.fori_loop(0, nch, body, jnp.zeros((B_QT, B_KC), I32))
        return jnp.sum(acc, axis=-1, keepdims=True)

    def thr_bit(it, lo):
        cand = lo + jnp.left_shift(jnp.int32(1), 31 - it)
        cnt = count(lambda key, _: key >= cand)
        return jnp.where(cnt >= B_TOPK, cand, lo)

    thr = lax.fori_loop(0, 32, thr_bit, jnp.full((B_QT, 1), INT_MIN, I32))
    need = B_TOPK - count(lambda key, _: key > thr)
    n_eq = count(lambda key, _: key == thr)

    def tie_search(_):
        def bit(it, j):
            cand = j + jnp.left_shift(jnp.int32(1), 12 - it)
            cnt = count(lambda key, col: (key == thr) & (col < cand))
            return jnp.where(cnt <= need, cand, j)
        return lax.fori_loop(0, 13, bit, jnp.zeros((B_QT, 1), I32))

    any_tie = jnp.max(jnp.where(n_eq > need, 1, 0)) > 0
    j_max = lax.cond(any_tie, tie_search, lambda _: jnp.full((B_QT, 1), 2 ** 13, I32), 0)

    def mask_chunk(c, carry):
        off = pl.multiple_of(c * B_KC, B_KC)
        key = key_ref[:, pl.ds(off, B_KC)]
        col = off + lane
        sel = ((key > thr) | ((key == thr) & (col < j_max))) & (col <= t_row)
        msk_ref[:, pl.ds(off, B_KC)] = jnp.where(sel, 0.0, NEG)
        return carry

    lax.fori_loop(0, nch, mask_chunk, 0)

    grp = B_HEADS // B_KV_HEADS
    q = q_ref[...] * (HEAD_DIM ** -0.5)
    outs = []
    for g in range(B_KV_HEADS):
        qg = jnp.concatenate([q[:, (g * grp + j) * HEAD_DIM:(g * grp + j + 1) * HEAD_DIM]
                              for j in range(grp)], axis=0)
        ksl = slice(g * HEAD_DIM, (g + 1) * HEAD_DIM)

        def flash(c, carry, qg=qg, ksl=ksl, g=g):
            m, l, acc = carry
            off = pl.multiple_of(c * B_KC, B_KC)
            s = lax.dot_general(qg, k_ref[pl.ds(off, B_KC), ksl], (((1,), (1,)), ((), ())),
                                preferred_element_type=F32).reshape(grp, B_QT, B_KC)
            d_lo = jnp.clip(qb - 2 * c, 0, B_NDELTA - 1)
            d_hi = jnp.clip(qb - 2 * c - 1, 0, B_NDELTA - 1)
            bias = jnp.concatenate([bias_ref[d_lo, g * grp:(g + 1) * grp],
                                    bias_ref[d_hi, g * grp:(g + 1) * grp]], axis=-1)
            s = s + bias + msk_ref[:, pl.ds(off, B_KC)][None]
            m_new = jnp.maximum(m, jnp.max(s, axis=-1, keepdims=True))
            alpha = jnp.exp(m - m_new)
            p = jnp.exp(s - m_new)
            l = alpha * l + jnp.sum(p, axis=-1, keepdims=True)
            pv = jnp.dot(p.reshape(grp * B_QT, B_KC).astype(BF16), v_ref[pl.ds(off, B_KC), ksl],
                         preferred_element_type=F32).reshape(grp, B_QT, HEAD_DIM)
            return m_new, l, alpha * acc + pv

        m0 = jnp.full((grp, B_QT, 1), NEG, F32)
        l0 = jnp.zeros((grp, B_QT, 1), F32)
        a0 = jnp.zeros((grp, B_QT, HEAD_DIM), F32)
        _, l, acc = lax.fori_loop(0, nch, flash, (m0, l0, a0))
        o = acc / l
        outs.extend(o[j] for j in range(grp))
    o_ref[...] = jnp.concatenate(outs, axis=-1).astype(o_ref.dtype)


def _bias_b(tab):
    dist = (np.arange(B_NDELTA)[:, None, None] * B_QT + np.arange(B_QT)[None, :, None]
            - np.arange(B_QT)[None, None, :])
    assert (B_NDELTA - 1) * B_QT - (B_QT - 1) >= 16 * 128 ** (15.0 / 16.0) + 1
    bucket = _t5_bucket(jnp.asarray(np.maximum(dist, 0), I32))
    return tab[bucket].transpose(0, 3, 1, 2).astype(F32)


def _mixer_b(x2, gain, w_in, q_gain, k_gain, w_out, bias_b, batch, seq):
    d = x2.shape[1]
    nq, nkv = B_HEADS * HEAD_DIM, B_KV_HEADS * HEAD_DIM
    nidx = B_IDX_HEADS * B_IDX_DIM
    tn = 256
    used = 2 * nkv + nq + nidx + B_IDX_DIM + B_IDX_HEADS
    kout = -(-used // tn) * tn
    w = jnp.concatenate([w_in, jnp.zeros((d, kout - used), F32)], axis=1).astype(BF16)
    hg = jnp.concatenate([jnp.tile(q_gain, B_HEADS), jnp.tile(k_gain, B_KV_HEADS),
                          jnp.ones((kout - nq - nkv,), F32)]).reshape(1, -1)
    p32, p16 = _in_proj(x2, gain, w, hg, n_norm_cols=nq + nkv, tn=tn, out_dtypes=(F32, BF16))
    p32 = p32.reshape(batch, seq, kout)
    p16 = p16.reshape(batch, seq, kout)
    assert seq % (2 * B_QT) == 0 and seq <= 2 ** 12
    kw_blk = (nq + 2 * nkv + nidx) // tn
    y = pl.pallas_call(
        _attn_b_kernel,
        out_shape=jax.ShapeDtypeStruct((batch, seq, nq), BF16),
        grid=(batch, seq // B_QT),
        in_specs=[
            pl.BlockSpec((None, B_QT, nq), lambda b, i: (b, i, 0)),
            pl.BlockSpec((None, seq, nkv), lambda b, i: (b, 0, nq // nkv)),
            pl.BlockSpec((None, seq, nkv), lambda b, i: (b, 0, nq // nkv + 1)),
            pl.BlockSpec((None, B_QT, nidx), lambda b, i: (b, i, (nq + 2 * nkv) // nidx)),
            pl.BlockSpec((None, seq, tn), lambda b, i: (b, 0, kw_blk)),
            pl.BlockSpec((B_NDELTA, B_HEADS, B_QT, B_QT), lambda b, i: (0, 0, 0, 0),
                         pipeline_mode=pl.Buffered(1)),
        ],
        out_specs=pl.BlockSpec((None, B_QT, nq), lambda b, i: (b, i, 0)),
        scratch_shapes=[pltpu.VMEM((B_QT, seq), I32), pltpu.VMEM((B_QT, seq), F32)],
        compiler_params=_cparams(("parallel", "arbitrary")),
        name="attn_b",
    )(p16, p16, p16, p32, p32, _bias_b(bias_b))
    return _out_proj(y.reshape(batch * seq, nq), w_out.astype(BF16), x2)


def _conv_c_kernel(x_ref, halo_ref, cw_ref, o_ref):
    i, j = pl.program_id(1), pl.program_id(2)
    ts = x_ref.shape[0]
    halo = jnp.where(i > 0, halo_ref[...], 0.0)
    xs = jnp.concatenate([halo, x_ref[...]], axis=0)
    cw = cw_ref[...]
    conv = sum(cw[t:t + 1, :] * xs[8 - (C_CONV - 1) + t:8 - (C_CONV - 1) + t + ts, :]
               for t in range(C_CONV))
    y = conv * jax.nn.sigmoid(conv)

    @pl.when(j == 2)
    def _():
        o_ref[...] = y

    @pl.when(j < 2)
    def _():
        scale = jnp.where(j == 0, C_DK ** -0.5, 1.0)
        for h in range(C_HEADS):
            sl = slice(h * C_DK, (h + 1) * C_DK)
            yh = y[:, sl]
            ss = jnp.sum(yh * yh, axis=-1, keepdims=True)
            o_ref[:, sl] = yh * (lax.rsqrt(ss + EPS) * scale)


def _delta_kernel(q_ref, k_ref, v_ref, gate_ref, a_ref, b_ref, alog_ref, dtb_ref, og_ref, y_ref,
                  rows_s, u_s, w_s, qg_s, qk_s, kgt_s, el_s, o_s):
    cs, gs = C_CHUNK, C_GROUP
    per = gs // cs
    seq, dv = v_ref.shape
    n_groups = seq // gs
    hi = lax.Precision.HIGHEST
    r = lax.broadcasted_iota(I32, (gs, gs), 0)
    c_ = lax.broadcasted_iota(I32, (gs, gs), 1)

    def same_block(size):
        sh = int(math.log2(size))
        return (r >> sh) == (c_ >> sh)

    chunk = same_block(cs)
    lower, strict, eye = chunk & (r >= c_), chunk & (r > c_), r == c_

    z = a_ref[...] + dtb_ref[...]
    softplus = jnp.maximum(z, 0.0) + jnp.log(1.0 + jnp.exp(-jnp.abs(z)))
    g_all = -jnp.exp(alog_ref[...]) * softplus
    gc_all = jnp.dot(g_all, jnp.where(chunk & (r <= c_), 1.0, 0.0), preferred_element_type=F32, precision=hi)
    gl_all = jnp.dot(g_all, jnp.where(chunk, 1.0, 0.0), preferred_element_type=F32, precision=hi)
    rows_s[0] = jax.nn.sigmoid(b_ref[...])
    rows_s[1] = gc_all
    rows_s[2] = jnp.exp(gc_all)
    rows_s[3] = jnp.exp(gl_all - gc_all)
    first = (lax.broadcasted_iota(I32, (gs, per * dv), 0)
             == (lax.broadcasted_iota(I32, (gs, per * dv), 1) // dv) * cs)
    el_s[...] = jnp.exp(jnp.dot(gl_all, jnp.where(first, 1.0, 0.0), preferred_element_type=F32, precision=hi))

    def mm(a, b):
        return jnp.dot(a.astype(BF16), b.astype(BF16), preferred_element_type=F32)

    def mm_nt(a, b):
        return lax.dot_general(a.astype(BF16), b.astype(BF16), (((1,), (1,)), ((), ())),
                               preferred_element_type=F32)

    ways = 2

    def prep(m, carry):
        ids = [m * ways + a for a in range(ways)]
        rows = [pl.ds(pl.multiple_of(i * gs, gs), gs) for i in ids]
        rw = [jnp.concatenate([rows_s[j, pl.ds(i, 1), :] for j in range(4)] + [jnp.zeros((4, gs), F32)], axis=0)
              for i in ids]
        cl = [x.T for x in rw]
        beta_col, gc_col, eg_col, ekg_col = ([x[:, j:j + 1] for x in cl] for j in range(4))
        q, k, v = ([ref[rw_, :] for rw_ in rows] for ref in (q_ref, k_ref, v_ref))
        decay = [jnp.where(lower, jnp.exp(jnp.where(lower, gcc - x[1:2, :], 0.0)), 0.0)
                 for gcc, x in zip(gc_col, rw)]
        kb = [k_ * b_ for k_, b_ in zip(k, beta_col)]
        a_mat = [jnp.where(strict, mm_nt(kb_, k_) * d_, 0.0) for kb_, k_, d_ in zip(kb, k, decay)]
        a8 = [jnp.where(same_block(8), a_, 0.0) for a_ in a_mat]
        t = [jnp.where(eye, 1.0, 0.0) - a_ for a_ in a8]
        pw = [mm(a_, a_) for a_ in a8]
        t = [t_ + mm(t_, p_) for t_, p_ in zip(t, pw)]
        pw = [mm(p_, p_) for p_ in pw]
        t = [t_ + mm(t_, p_) for t_, p_ in zip(t, pw)]
        size = 8
        while size < cs:
            sel = same_block(2 * size) & jnp.logical_not(same_block(size))
            nt = [mm(jnp.where(sel, a_, 0.0), t_) for a_, t_ in zip(a_mat, t)]
            t = [t_ - mm(t_, n_) for t_, n_ in zip(t, nt)]
            size *= 2
        uw = [mm(t_, jnp.concatenate([v_ * b_, kb_ * e_], axis=1))
              for t_, v_, b_, kb_, e_ in zip(t, v, beta_col, kb, eg_col)]
        qk = [jnp.where(lower, mm_nt(q_, k_) * d_, 0.0).astype(BF16) for q_, k_, d_ in zip(q, k, decay)]
        for a, i in enumerate(ids):
            u_s[rows[a], :] = uw[a][:, :dv]
            w_s[rows[a], :] = uw[a][:, dv:].astype(BF16)
            for j in range(per):
                qk_s[pl.ds(pl.multiple_of(i * gs + j * cs, cs), cs), :] = qk[a][j * cs:(j + 1) * cs, j * cs:(j + 1) * cs]
            qg_s[rows[a], :] = (q[a] * eg_col[a]).astype(BF16)
            kgt_s[i] = (k[a] * ekg_col[a]).T.astype(BF16)
        return carry

    lax.fori_loop(0, n_groups // ways, prep, 0)

    def scan(i, state):
        el = el_s[pl.ds(i, 1), :]
        for j in range(per):
            rows = pl.ds(pl.multiple_of(i * gs + j * cs, cs), cs)
            sb = state.astype(BF16)
            v_new = u_s[rows, :] - jnp.dot(w_s[rows, :], sb, preferred_element_type=F32)
            vb = v_new.astype(BF16)
            o_s[rows, :] = (jnp.dot(qg_s[rows, :], sb, preferred_element_type=F32)
                            + jnp.dot(qk_s[rows, :], vb, preferred_element_type=F32))
            state = (state * el[:, j * dv:(j + 1) * dv]
                     + jnp.dot(kgt_s[i, :, j * cs:(j + 1) * cs], vb, preferred_element_type=F32))
        return state

    lax.fori_loop(0, n_groups, scan, jnp.zeros((C_DK, dv), F32))

    tr = 512

    def finish(t, carry):
        rows = pl.ds(pl.multiple_of(t * tr, tr), tr)
        gate = gate_ref[rows, :]
        y_ref[rows, :] = (_rms_rows(o_s[rows, :], og_ref[...]) * (gate * jax.nn.sigmoid(gate))).astype(y_ref.dtype)
        return carry

    lax.fori_loop(0, seq // tr, finish, 0)


def _mixer_c(x2, gain, w_in, conv_w, a_log, dt_bias, o_gain, w_out, batch, seq):
    d = x2.shape[1]
    hw = C_HEADS * C_DK
    tn = 256
    used = 4 * hw + 2 * C_HEADS
    kout = -(-used // tn) * tn
    w = jnp.concatenate([w_in, jnp.zeros((d, kout - used), F32)], axis=1).astype(BF16)
    (proj,) = _in_proj(x2, gain, w, jnp.ones((1, kout), F32), n_norm_cols=0, tn=tn, out_dtypes=(F32,))
    proj = proj.reshape(batch, seq, kout)

    ts = 512
    qkv = pl.pallas_call(
        _conv_c_kernel,
        out_shape=jax.ShapeDtypeStruct((batch, seq, 3 * hw), F32),
        grid=(batch, seq // ts, 3),
        in_specs=[
            pl.BlockSpec((None, ts, hw), lambda b, i, j: (b, i, j)),
            pl.BlockSpec((None, 8, hw), lambda b, i, j: (b, jnp.maximum(i * (ts // 8) - 1, 0), j)),
            pl.BlockSpec((C_CONV, hw), lambda b, i, j: (0, j)),
        ],
        out_specs=pl.BlockSpec((None, ts, hw), lambda b, i, j: (b, i, j)),
        compiler_params=_cparams(("parallel", "parallel", "arbitrary")),
        name="conv_c",
    )(proj, proj, conv_w)

    n_groups = seq // C_GROUP
    ba = proj[:, :, 4 * hw:4 * hw + 2 * C_HEADS].transpose(0, 2, 1).reshape(batch, 2 * C_HEADS, n_groups, C_GROUP)
    per_head = lambda v: jnp.broadcast_to(v.reshape(C_HEADS, 1, 1), (C_HEADS, 1, C_GROUP))
    head_cols = lambda off: pl.BlockSpec((None, seq, C_DK), lambda b, h: (b, 0, off + h))
    small = lambda off: pl.BlockSpec((None, None, n_groups, C_GROUP), lambda b, h: (b, off + h, 0, 0))
    scalar_row = pl.BlockSpec((None, 1, C_GROUP), lambda b, h: (h, 0, 0))
    y = pl.pallas_call(
        _delta_kernel,
        out_shape=jax.ShapeDtypeStruct((batch, seq, hw), BF16),
        grid=(batch, C_HEADS),
        in_specs=[head_cols(0), head_cols(C_HEADS), head_cols(2 * C_HEADS),
                  pl.BlockSpec((None, seq, C_DK), lambda b, h: (b, 0, 3 * C_HEADS + h)),
                  small(C_HEADS), small(0), scalar_row, scalar_row,
                  pl.BlockSpec((1, C_DK), lambda b, h: (0, 0))],
        out_specs=head_cols(0),
        scratch_shapes=[
            pltpu.VMEM((4, n_groups, C_GROUP), F32),
            pltpu.VMEM((seq, C_DK), F32),
            pltpu.VMEM((seq, C_DK), BF16),
            pltpu.VMEM((seq, C_DK), BF16),
            pltpu.VMEM((seq, C_CHUNK), BF16),
            pltpu.VMEM((n_groups, C_DK, C_GROUP), BF16),
            pltpu.VMEM((n_groups, (C_GROUP // C_CHUNK) * C_DK), F32),
            pltpu.VMEM((seq, C_DK), F32),
        ],
        compiler_params=_cparams(("parallel", "parallel")),
        name="delta_c",
    )(qkv, qkv, qkv, proj, ba, ba, per_head(a_log), per_head(dt_bias), o_gain.reshape(1, C_DK))
    return _out_proj(y.reshape(batch * seq, hw), w_out.astype(BF16), x2)


def kernel(x, rel_bias, norm_mix, norm_mlp, mlp_w1, mlp_w2, a_w_in, a_q_gain, a_k_gain, a_w_out,
           b_w_in, b_q_gain, b_k_gain, b_w_out, c_w_in, c_conv_w, c_a_log, c_dt_bias, c_o_gain, c_w_out):
    batch, seq, d = x.shape
    depth = norm_mix.shape[0]
    a_cols = len(A_GROUPS) * A_HEADS
    bias_a, bias_b = rel_bias[:, :a_cols], rel_bias[:, a_cols:]
    x2 = x.reshape(batch * seq, d)
    for i in range(depth):
        kind, j = i % 3, i // 3
        if kind == 0:
            x2 = _mixer_a(x2, norm_mix[i], a_w_in[j], a_q_gain[j], a_k_gain[j], a_w_out[j], bias_a, batch, seq)
        elif kind == 1:
            x2 = _mixer_b(x2, norm_mix[i], b_w_in[j], b_q_gain[j], b_k_gain[j], b_w_out[j], bias_b, batch, seq)
        else:
            x2 = _mixer_c(x2, norm_mix[i], c_w_in[j], c_conv_w[j], c_a_log[j], c_dt_bias[j], c_o_gain[j],
                          c_w_out[j], batch, seq)
        x2 = _mlp(x2, norm_mlp[i], mlp_w1[i].astype(BF16), mlp_w2[i].astype(BF16))
    return x2.reshape(batch, seq, d)
```

```python
import functools
import math

import numpy as np
import jax
import jax.numpy as jnp
from jax import lax
from jax.experimental import pallas as pl
from jax.experimental.pallas import tpu as pltpu

F32 = jnp.float32
BF16 = jnp.bfloat16
I32 = jnp.int32

EPS = 1e-6
HEAD_DIM = 64
NEG = -1e30
INT_MIN = -2 ** 31
LOG2E = math.log2(math.e)

V7X_VMEM_BYTES = 64 * 1024 * 1024
VMEM_LIMIT = V7X_VMEM_BYTES - 8 * 1024 * 1024

NUM_BUCKETS = 32
MAX_DISTANCE = 2048

A_GROUPS = ((128, 1), (512, 4), (2048, 16))
A_HEADS = 8
A_BLOCK = 128
A_GW = A_HEADS * HEAD_DIM

B_HEADS = 16
B_KV_HEADS = 4
B_IDX_HEADS = 8
B_IDX_DIM = 64
B_TOPK = 256
B_QT = 128
B_KC = 256
B_NDELTA = 14

C_HEADS = 8
C_DK = 128
C_CONV = 4
C_CHUNK = 64
C_GROUP = 4 * C_CHUNK


def _cparams(sem):
    return pltpu.CompilerParams(dimension_semantics=sem, vmem_limit_bytes=VMEM_LIMIT)


def _t5_bucket(dist):
    max_exact = NUM_BUCKETS // 2
    d = jnp.maximum(dist, 1).astype(F32)
    log_part = jnp.log(d / max_exact) / math.log(MAX_DISTANCE / max_exact) * (NUM_BUCKETS - max_exact)
    large = jnp.minimum(max_exact + log_part.astype(I32), NUM_BUCKETS - 1)
    return jnp.where(dist < max_exact, dist, large)


def _rms_rows(x, gain_row):
    ms = jnp.mean(x * x, axis=-1, keepdims=True)
    return x * lax.rsqrt(ms + EPS) * gain_row


def _in_proj_kernel(x_ref, g_ref, w_ref, hg_ref, bd_ref, *rest, n_norm):
    outs, hn_ref = rest[:-1], rest[-1]
    j = pl.program_id(1)

    @pl.when(j == 0)
    def _():
        hn_ref[...] = _rms_rows(x_ref[...], g_ref[...]).astype(BF16)

    y = jnp.dot(hn_ref[...], w_ref[...], preferred_element_type=F32)

    def write(val):
        for o in outs:
            o[...] = val.astype(o.dtype)

    if n_norm == 0:
        write(y)
    else:
        @pl.when(j < n_norm)
        def _():
            ms = jnp.dot((y * y).astype(BF16), bd_ref[...], preferred_element_type=F32) * (1.0 / HEAD_DIM)
            write(y * lax.rsqrt(ms + EPS) * hg_ref[...])

        @pl.when(j >= n_norm)
        def _():
            write(y)


def _in_proj(x2, gain, w, head_gain, *, n_norm_cols, tn, out_dtypes, tm=512):
    n, d = x2.shape
    kout = w.shape[1]
    bd = np.kron(np.eye(tn // HEAD_DIM), np.ones((HEAD_DIM, HEAD_DIM))).astype(np.float32)
    return pl.pallas_call(
        functools.partial(_in_proj_kernel, n_norm=n_norm_cols // tn),
        out_shape=[jax.ShapeDtypeStruct((n, kout), dt) for dt in out_dtypes],
        grid=(n // tm, kout // tn),
        in_specs=[
            pl.BlockSpec((tm, d), lambda i, j: (i, 0)),
            pl.BlockSpec((1, d), lambda i, j: (0, 0)),
            pl.BlockSpec((d, tn), lambda i, j: (0, j)),
            pl.BlockSpec((1, tn), lambda i, j: (0, j)),
            pl.BlockSpec((tn, tn), lambda i, j: (0, 0)),
        ],
        out_specs=[pl.BlockSpec((tm, tn), lambda i, j: (i, j)) for _ in out_dtypes],
        scratch_shapes=[pltpu.VMEM((tm, d), BF16)],
        compiler_params=_cparams(("parallel", "arbitrary")),
        name="in_proj",
    )(x2, gain.reshape(1, d), w, head_gain, jnp.asarray(bd, BF16))


def _out_proj_kernel(y_ref, w_ref, x_ref, o_ref):
    o_ref[...] = x_ref[...] + jnp.dot(y_ref[...].astype(BF16), w_ref[...], preferred_element_type=F32)


def _out_proj(y2, w, x2, tm=512):
    n, kin = y2.shape
    d = w.shape[1]
    return pl.pallas_call(
        _out_proj_kernel,
        out_shape=jax.ShapeDtypeStruct((n, d), F32),
        grid=(n // tm,),
        in_specs=[
            pl.BlockSpec((tm, kin), lambda i: (i, 0)),
            pl.BlockSpec((kin, d), lambda i: (0, 0)),
            pl.BlockSpec((tm, d), lambda i: (i, 0)),
        ],
        out_specs=pl.BlockSpec((tm, d), lambda i: (i, 0)),
        compiler_params=_cparams(("parallel",)),
        name="out_proj",
    )(y2, w, x2)


def _out_proj_a_kernel(o0, o1, o2, l0, l1, l2, w_ref, x_ref, out_ref):
    a, b, c = l0[...], l1[...], l2[...]
    m = jnp.maximum(jnp.maximum(a, b), c)
    ea, eb, ec = jnp.exp(a - m), jnp.exp(b - m), jnp.exp(c - m)
    y = (ea * o0[...] + eb * o1[...] + ec * o2[...]) / (ea + eb + ec)
    out_ref[...] = x_ref[...] + jnp.dot(y.astype(BF16), w_ref[...], preferred_element_type=F32)


def _out_proj_a(os_, ls_, w, x2, tm=512):
    n, d = x2.shape
    row = pl.BlockSpec((tm, A_GW), lambda i: (i, 0))
    return pl.pallas_call(
        _out_proj_a_kernel,
        out_shape=jax.ShapeDtypeStruct((n, d), F32),
        grid=(n // tm,),
        in_specs=[row] * 6 + [
            pl.BlockSpec((A_GW, d), lambda i: (0, 0)),
            pl.BlockSpec((tm, d), lambda i: (i, 0)),
        ],
        out_specs=pl.BlockSpec((tm, d), lambda i: (i, 0)),
        compiler_params=_cparams(("parallel",)),
        name="out_proj_a",
    )(*os_, *ls_, w, x2)


def _mlp_kernel(x_ref, g_ref, w1_ref, w2_ref, o_ref, hn_ref):
    f = pl.program_id(1)

    @pl.when(f == 0)
    def _():
        x = x_ref[...]
        hn_ref[...] = _rms_rows(x, g_ref[...]).astype(BF16)
        o_ref[...] = x

    h = jnp.maximum(jnp.dot(hn_ref[...], w1_ref[...], preferred_element_type=F32), 0.0)
    o_ref[...] += jnp.dot((h * h).astype(BF16), w2_ref[...], preferred_element_type=F32)


def _mlp(x2, gain, w1, w2, tm=1024, tf=1024):
    n, d = x2.shape
    dff = w1.shape[1]
    return pl.pallas_call(
        _mlp_kernel,
        out_shape=jax.ShapeDtypeStruct((n, d), F32),
        grid=(n // tm, dff // tf),
        in_specs=[
            pl.BlockSpec((tm, d), lambda i, f: (i, 0)),
            pl.BlockSpec((1, d), lambda i, f: (0, 0)),
            pl.BlockSpec((d, tf), lambda i, f: (0, f)),
            pl.BlockSpec((tf, d), lambda i, f: (f, 0)),
        ],
        out_specs=pl.BlockSpec((tm, d), lambda i, f: (i, 0)),
        scratch_shapes=[pltpu.VMEM((tm, d), BF16)],
        compiler_params=_cparams(("parallel", "arbitrary")),
        name="mlp",
    )(x2, gain.reshape(1, d), w1, w2)


def _attn_a_kernel(q_ref, kp_ref, kc_ref, vp_ref, vc_ref, bias_ref, o_ref, lse_ref):
    n = pl.program_id(2)
    blk = A_BLOCK
    row = lax.broadcasted_iota(I32, (blk, 2 * blk), 0)
    col = lax.broadcasted_iota(I32, (blk, 2 * blk), 1)
    valid = (col >= jnp.maximum(row, jnp.where(n > 0, 0, blk))) & (col <= row + blk)
    q = q_ref[...]
    outs, lses = [], []
    for h in range(A_HEADS):
        sl = slice(h * HEAD_DIM, (h + 1) * HEAD_DIM)
        kh = jnp.concatenate([kp_ref[:, sl], kc_ref[:, sl]], axis=0)
        vh = jnp.concatenate([vp_ref[:, sl], vc_ref[:, sl]], axis=0)
        s = lax.dot_general(q[:, sl], kh, (((1,), (1,)), ((), ())), preferred_element_type=F32)
        s = jnp.where(valid, s * (HEAD_DIM ** -0.5) + bias_ref[h], NEG)
        m = jnp.max(s, axis=-1, keepdims=True)
        p = jnp.exp(s - m)
        l = jnp.sum(p, axis=-1, keepdims=True)
        o = jnp.dot(p.astype(BF16), vh, preferred_element_type=F32) / l
        outs.append(o)
        lses.append(jnp.broadcast_to(m + jnp.log(l), (blk, HEAD_DIM)))
    o_ref[...] = jnp.concatenate(outs, axis=-1)
    lse_ref[...] = jnp.concatenate(lses, axis=-1)


def _attn_a_group(qkv, bias_tab, g, dilation, batch, seq):
    n_groups = len(A_GROUPS)
    sub = seq // dilation
    nb = sub // A_BLOCK
    ncol = 3 * n_groups
    view = qkv.reshape(batch, sub, dilation * ncol * A_GW)

    def spec(which, prev):
        def imap(b, r, n):
            return (b, jnp.maximum(n - 1, 0) if prev else n, r * ncol + which * n_groups + g)
        return pl.BlockSpec((None, A_BLOCK, A_GW), imap)

    out_spec = pl.BlockSpec((None, A_BLOCK, A_GW), lambda b, r, n: (b, n, r))
    out_sds = jax.ShapeDtypeStruct((batch, sub, dilation * A_GW), F32)
    o, lse = pl.pallas_call(
        _attn_a_kernel,
        out_shape=[out_sds, out_sds],
        grid=(batch, dilation, nb),
        in_specs=[spec(0, False), spec(1, True), spec(1, False), spec(2, True), spec(2, False),
                  pl.BlockSpec((A_HEADS, A_BLOCK, 2 * A_BLOCK), lambda b, r, n: (0, 0, 0))],
        out_specs=[out_spec, out_spec],
        compiler_params=_cparams(("parallel", "parallel", "arbitrary")),
        name=f"attn_a_d{dilation}",
    )(view, view, view, view, view, bias_tab)
    return o.reshape(batch * seq, A_GW), lse.reshape(batch * seq, A_GW)


def _bias_a(tab_g, dilation):
    step = np.arange(A_BLOCK)[:, None] - np.arange(2 * A_BLOCK)[None, :] + A_BLOCK
    bucket = _t5_bucket(jnp.asarray(np.maximum(step, 0) * dilation, I32))
    return tab_g[bucket].transpose(2, 0, 1).astype(F32)


def _mixer_a(x2, gain, w_in, q_gain, k_gain, w_out, bias_a, batch, seq):
    n_groups = len(A_GROUPS)
    reps = n_groups * A_HEADS
    hg = jnp.concatenate([jnp.tile(q_gain, reps), jnp.tile(k_gain, reps),
                          jnp.ones((n_groups * A_GW,), F32)]).reshape(1, -1)
    (qkv,) = _in_proj(x2, gain, w_in.astype(BF16), hg, n_norm_cols=2 * n_groups * A_GW,
                      tn=A_GW, out_dtypes=(BF16,))
    os_, ls_ = [], []
    for g, (window, dilation) in enumerate(A_GROUPS):
        assert window // dilation == A_BLOCK and (seq // dilation) % A_BLOCK == 0
        tab_g = bias_a[:, g * A_HEADS:(g + 1) * A_HEADS]
        o, lse = _attn_a_group(qkv, _bias_a(tab_g, dilation), g, dilation, batch, seq)
        os_.append(o)
        ls_.append(lse)
    return _out_proj_a(os_, ls_, w_out.astype(BF16), x2)


def _float_key(s):
    i = lax.bitcast_convert_type(s, I32)
    k = jnp.where(i < 0, i ^ jnp.int32(0x7FFFFFFF), i)
    return jnp.where(s == 0.0, 0, k)


def _split_bf16(x):
    hi = x.astype(BF16)
    return hi, (x - hi.astype(F32)).astype(BF16)


def _attn_b_kernel(q_ref, k_ref, vt_ref, qi_ref, kw_ref, bias_ref, o_ref, key_ref, msk_ref):
    qb = pl.program_id(1)
    kc, qt = B_KC, B_QT
    nch = qb // 2 + 1
    t_q = qb * qt + lax.broadcasted_iota(I32, (kc, qt), 1)
    sub = lax.broadcasted_iota(I32, (kc, qt), 0)
    nt = (((1,), (1,)), ((), ()))

    qh, ql = _split_bf16(qi_ref[...])
    qi3 = jnp.concatenate(
        [jnp.concatenate([x[:, h * B_IDX_DIM:(h + 1) * B_IDX_DIM] for x in (qh, qh, ql)], axis=1)
         for h in range(B_IDX_HEADS)], axis=0)
    wt = kw_ref[pl.ds(pl.multiple_of(qb * qt, qt), qt), 0:qt].T
    wt = wt * (B_IDX_HEADS ** -0.5 * B_IDX_DIM ** -0.5)

    def score_chunk(c, carry):
        off = pl.multiple_of(c * kc, kc)
        kh, kl = _split_bf16(kw_ref[pl.ds(off, kc), 0:B_IDX_DIM])
        sc = lax.dot_general(jnp.concatenate([kh, kl, kh], axis=1), qi3, nt, preferred_element_type=F32)
        acc = jnp.zeros((kc, qt), F32)
        for h in range(B_IDX_HEADS):
            acc = acc + wt[B_IDX_DIM + h:B_IDX_DIM + h + 1, :] * jnp.maximum(sc[:, h * qt:(h + 1) * qt], 0.0)
        key_ref[pl.ds(off, kc), :] = jnp.where(off + sub <= t_q, _float_key(acc), INT_MIN)
        return carry

    lax.fori_loop(0, nch, score_chunk, 0)

    def count(pred_fn):
        def body(c, acc):
            off = pl.multiple_of(c * kc, kc)
            hit = jnp.where(pred_fn(key_ref[pl.ds(off, kc), :], off + sub), 1, 0)
            return acc + jnp.sum(hit.reshape(kc // 8, 8, qt), axis=0)
        acc = lax.fori_loop(0, nch, body, jnp.zeros((8, qt), I32))
        return jnp.sum(acc, axis=0, keepdims=True)

    def thr_bit(it, lo):
        cand = lo + jnp.left_shift(jnp.int32(1), 31 - it)
        cnt = count(lambda key, _: key >= cand)
        return jnp.where(cnt >= B_TOPK, cand, lo)

    thr = lax.fori_loop(0, 32, thr_bit, jnp.full((1, qt), INT_MIN, I32))
    need = B_TOPK - count(lambda key, _: key > thr)
    n_eq = count(lambda key, _: key == thr)

    def tie_search(_):
        def bit(it, j):
            cand = j + jnp.left_shift(jnp.int32(1), 12 - it)
            cnt = count(lambda key, idx: (key == thr) & (idx < cand))
            return jnp.where(cnt <= need, cand, j)
        return lax.fori_loop(0, 13, bit, jnp.zeros((1, qt), I32))

    any_tie = jnp.max(jnp.where(n_eq > need, 1, 0)) > 0
    j_max = lax.cond(any_tie, tie_search, lambda _: jnp.full((1, qt), 2 ** 13, I32), 0)

    def mask_chunk(c, carry):
        off = pl.multiple_of(c * kc, kc)
        key = key_ref[pl.ds(off, kc), :]
        idx = off + sub
        sel = ((key > thr) | ((key == thr) & (idx < j_max))) & (idx <= t_q)
        msk_ref[pl.ds(off, kc), :] = jnp.where(sel, 0.0, NEG)
        return carry

    lax.fori_loop(0, nch, mask_chunk, 0)

    grp = B_HEADS // B_KV_HEADS
    q = q_ref[...]
    groups = range(B_KV_HEADS)
    qg = [jnp.concatenate([q[:, (g * grp + j) * HEAD_DIM:(g * grp + j + 1) * HEAD_DIM]
                           for j in range(grp)], axis=0) for g in groups]
    ksl = [slice(g * HEAD_DIM, (g + 1) * HEAD_DIM) for g in groups]

    def flash(c, carry):
        m, l, acc = carry
        off = pl.multiple_of(c * kc, kc)
        d_lo = jnp.clip(qb - 2 * c, 0, B_NDELTA - 1)
        d_hi = jnp.clip(qb - 2 * c - 1, 0, B_NDELTA - 1)
        mk = msk_ref[pl.ds(off, kc), :]
        mk = jnp.concatenate([mk] * grp, axis=1)
        s = [lax.dot_general(k_ref[pl.ds(off, kc), ksl[g]], qg[g], nt, preferred_element_type=F32)
             for g in groups]
        s = [s[g] + jnp.concatenate([bias_ref[d_lo, g], bias_ref[d_hi, g]], axis=0) + mk for g in groups]
        m_new = [jnp.maximum(m[g], jnp.max(s[g], axis=0, keepdims=True)) for g in groups]
        p = [jnp.exp2(s[g] - m_new[g]) for g in groups]
        pv = [jnp.dot(vt_ref[ksl[g], pl.ds(off, kc)], p[g].astype(BF16), preferred_element_type=F32)
              for g in groups]
        alpha = [jnp.exp2(m[g] - m_new[g]) for g in groups]
        l = [alpha[g] * l[g] + jnp.sum(p[g], axis=0, keepdims=True) for g in groups]
        acc = [alpha[g] * acc[g] + pv[g] for g in groups]
        return m_new, l, acc

    m0 = [jnp.full((1, grp * qt), NEG, F32) for _ in groups]
    l0 = [jnp.zeros((1, grp * qt), F32) for _ in groups]
    a0 = [jnp.zeros((HEAD_DIM, grp * qt), F32) for _ in groups]
    _, l, acc = lax.fori_loop(0, nch, flash, (m0, l0, a0))
    for g in groups:
        o = acc[g] / l[g]
        for j in range(grp):
            h = g * grp + j
            o_ref[:, h * HEAD_DIM:(h + 1) * HEAD_DIM] = o[:, j * qt:(j + 1) * qt].T.astype(o_ref.dtype)


def _toeplitz(vec, n_rows, n_cols):
    span = n_rows + n_cols - 1
    assert vec.shape[-1] == span
    lead = vec.shape[:-1]
    padded = jnp.concatenate([vec, jnp.zeros(lead + (1,), vec.dtype)], axis=-1)
    flat = jnp.tile(padded, n_rows)[..., :n_rows * span]
    return flat.reshape(lead + (n_rows, span))[..., n_rows - 1:]


def _bias_b(tab):
    grp = B_HEADS // B_KV_HEADS
    span = 2 * B_QT - 1
    dist = np.arange(B_NDELTA)[:, None] * B_QT + np.arange(span)[None, :] - (B_QT - 1)
    assert (B_NDELTA - 1) * B_QT - (B_QT - 1) >= 16 * 128 ** (15.0 / 16.0) + 1
    vec = tab[_t5_bucket(jnp.asarray(np.maximum(dist, 0), I32))].astype(F32) * LOG2E
    t = _toeplitz(vec.transpose(0, 2, 1), B_QT, B_QT)
    t = t.reshape(B_NDELTA, B_KV_HEADS, grp, B_QT, B_QT).transpose(0, 1, 3, 2, 4)
    return t.reshape(B_NDELTA, B_KV_HEADS, B_QT, grp * B_QT)


def _mixer_b(x2, gain, w_in, q_gain, k_gain, w_out, bias_b, batch, seq):
    d = x2.shape[1]
    nq, nkv = B_HEADS * HEAD_DIM, B_KV_HEADS * HEAD_DIM
    nidx = B_IDX_HEADS * B_IDX_DIM
    tn = 256
    used = 2 * nkv + nq + nidx + B_IDX_DIM + B_IDX_HEADS
    kout = -(-used // tn) * tn
    w = jnp.concatenate([w_in, jnp.zeros((d, kout - used), F32)], axis=1).astype(BF16)
    hg = jnp.concatenate([jnp.tile(q_gain * (HEAD_DIM ** -0.5 * LOG2E), B_HEADS), jnp.tile(k_gain, B_KV_HEADS),
                          jnp.ones((kout - nq - nkv,), F32)]).reshape(1, -1)
    p32, p16 = _in_proj(x2, gain, w, hg, n_norm_cols=nq + nkv, tn=tn, out_dtypes=(F32, BF16))
    p32 = p32.reshape(batch, seq, kout)
    p16 = p16.reshape(batch, seq, kout)
    assert seq % (2 * B_QT) == 0 and seq <= 2 ** 12
    kw_blk = (nq + 2 * nkv + nidx) // tn
    vt = p16[:, :, nq + nkv:nq + 2 * nkv].transpose(0, 2, 1)
    y = pl.pallas_call(
        _attn_b_kernel,
        out_shape=jax.ShapeDtypeStruct((batch, seq, nq), BF16),
        grid=(batch, seq // B_QT),
        in_specs=[
            pl.BlockSpec((None, B_QT, nq), lambda b, i: (b, i, 0)),
            pl.BlockSpec((None, seq, nkv), lambda b, i: (b, 0, nq // nkv)),
            pl.BlockSpec((None, nkv, seq), lambda b, i: (b, 0, 0)),
            pl.BlockSpec((None, B_QT, nidx), lambda b, i: (b, i, (nq + 2 * nkv) // nidx)),
            pl.BlockSpec((None, seq, tn), lambda b, i: (b, 0, kw_blk)),
            pl.BlockSpec((B_NDELTA, B_KV_HEADS, B_QT, (B_HEADS // B_KV_HEADS) * B_QT),
                         lambda b, i: (0, 0, 0, 0), pipeline_mode=pl.Buffered(1)),
        ],
        out_specs=pl.BlockSpec((None, B_QT, nq), lambda b, i: (b, i, 0)),
        scratch_shapes=[pltpu.VMEM((seq, B_QT), I32), pltpu.VMEM((seq, B_QT), F32)],
        compiler_params=_cparams(("parallel", "arbitrary")),
        name="attn_b",
    )(p16, p16, vt, p32, p32, _bias_b(bias_b))
    return _out_proj(y.reshape(batch * seq, nq), w_out.astype(BF16), x2)


def _conv_c_kernel(x_ref, halo_ref, cw_ref, o_ref):
    i, j = pl.program_id(1), pl.program_id(2)
    ts = x_ref.shape[0]
    halo = jnp.where(i > 0, halo_ref[...], 0.0)
    xs = jnp.concatenate([halo, x_ref[...]], axis=0)
    cw = cw_ref[...]
    conv = sum(cw[t:t + 1, :] * xs[8 - (C_CONV - 1) + t:8 - (C_CONV - 1) + t + ts, :]
               for t in range(C_CONV))
    y = conv * jax.nn.sigmoid(conv)

    @pl.when(j == 2)
    def _():
        o_ref[...] = y

    @pl.when(j < 2)
    def _():
        scale = jnp.where(j == 0, C_DK ** -0.5, 1.0)
        for h in range(C_HEADS):
            sl = slice(h * C_DK, (h + 1) * C_DK)
            yh = y[:, sl]
            ss = jnp.sum(yh * yh, axis=-1, keepdims=True)
            o_ref[:, sl] = yh * (lax.rsqrt(ss + EPS) * scale)


def _delta_kernel(q_ref, k_ref, v_ref, gate_ref, a_ref, b_ref, alog_ref, dtb_ref, og_ref, y_ref,
                  rows_s, u_s, w_s, qg_s, qk_s, kgt_s, el_s, o_s):
    cs, gs = C_CHUNK, C_GROUP
    per = gs // cs
    seq, dv = v_ref.shape
    n_groups = seq // gs
    hi = lax.Precision.HIGHEST
    r = lax.broadcasted_iota(I32, (gs, gs), 0)
    c_ = lax.broadcasted_iota(I32, (gs, gs), 1)

    def same_block(size):
        sh = int(math.log2(size))
        return (r >> sh) == (c_ >> sh)

    chunk = same_block(cs)
    lower, strict, eye = chunk & (r >= c_), chunk & (r > c_), r == c_

    z = a_ref[...] + dtb_ref[...]
    softplus = jnp.maximum(z, 0.0) + jnp.log(1.0 + jnp.exp(-jnp.abs(z)))
    g_all = -jnp.exp(alog_ref[...]) * softplus
    gc_all = jnp.dot(g_all, jnp.where(chunk & (r <= c_), 1.0, 0.0), preferred_element_type=F32, precision=hi)
    gl_all = jnp.dot(g_all, jnp.where(chunk, 1.0, 0.0), preferred_element_type=F32, precision=hi)
    rows_s[0] = jax.nn.sigmoid(b_ref[...])
    rows_s[1] = gc_all
    rows_s[2] = jnp.exp(gc_all)
    rows_s[3] = jnp.exp(gl_all - gc_all)
    first = (lax.broadcasted_iota(I32, (gs, per * dv), 0)
             == (lax.broadcasted_iota(I32, (gs, per * dv), 1) // dv) * cs)
    el_s[...] = jnp.exp(jnp.dot(gl_all, jnp.where(first, 1.0, 0.0), preferred_element_type=F32, precision=hi))

    def mm(a, b):
        return jnp.dot(a.astype(BF16), b.astype(BF16), preferred_element_type=F32)

    def mm_nt(a, b):
        return lax.dot_general(a.astype(BF16), b.astype(BF16), (((1,), (1,)), ((), ())),
                               preferred_element_type=F32)

    ways = 2

    def prep(m, carry):
        ids = [m * ways + a for a in range(ways)]
        rows = [pl.ds(pl.multiple_of(i * gs, gs), gs) for i in ids]
        rw = [jnp.concatenate([rows_s[j, pl.ds(i, 1), :] for j in range(4)] + [jnp.zeros((4, gs), F32)], axis=0)
              for i in ids]
        cl = [x.T for x in rw]
        beta_col, gc_col, eg_col, ekg_col = ([x[:, j:j + 1] for x in cl] for j in range(4))
        q, k, v = ([ref[rw_, :] for rw_ in rows] for ref in (q_ref, k_ref, v_ref))
        decay = [jnp.where(lower, jnp.exp(jnp.where(lower, gcc - x[1:2, :], 0.0)), 0.0)
                 for gcc, x in zip(gc_col, rw)]
        kb = [k_ * b_ for k_, b_ in zip(k, beta_col)]
        a_mat = [jnp.where(strict, mm_nt(kb_, k_) * d_, 0.0) for kb_, k_, d_ in zip(kb, k, decay)]
        a8 = [jnp.where(same_block(8), a_, 0.0) for a_ in a_mat]
        t = [jnp.where(eye, 1.0, 0.0) - a_ for a_ in a8]
        pw = [mm(a_, a_) for a_ in a8]
        t = [t_ + mm(t_, p_) for t_, p_ in zip(t, pw)]
        pw = [mm(p_, p_) for p_ in pw]
        t = [t_ + mm(t_, p_) for t_, p_ in zip(t, pw)]
        size = 8
        while size < cs:
            sel = same_block(2 * size) & jnp.logical_not(same_block(size))
            nt = [mm(jnp.where(sel, a_, 0.0), t_) for a_, t_ in zip(a_mat, t)]
            t = [t_ - mm(t_, n_) for t_, n_ in zip(t, nt)]
            size *= 2
        uw = [mm(t_, jnp.concatenate([v_ * b_, kb_ * e_], axis=1))
              for t_, v_, b_, kb_, e_ in zip(t, v, beta_col, kb, eg_col)]
        qk = [jnp.where(lower, mm_nt(q_, k_) * d_, 0.0).astype(BF16) for q_, k_, d_ in zip(q, k, decay)]
        for a, i in enumerate(ids):
            u_s[rows[a], :] = uw[a][:, :dv]
            w_s[rows[a], :] = uw[a][:, dv:].astype(BF16)
            for j in range(per):
                qk_s[pl.ds(pl.multiple_of(i * gs + j * cs, cs), cs), :] = qk[a][j * cs:(j + 1) * cs, j * cs:(j + 1) * cs]
            qg_s[rows[a], :] = (q[a] * eg_col[a]).astype(BF16)
            kgt_s[i] = (k[a] * ekg_col[a]).T.astype(BF16)
        return carry

    lax.fori_loop(0, n_groups // ways, prep, 0)

    def scan(i, state):
        el = el_s[pl.ds(i, 1), :]
        for j in range(per):
            rows = pl.ds(pl.multiple_of(i * gs + j * cs, cs), cs)
            sb = state.astype(BF16)
            v_new = u_s[rows, :] - jnp.dot(w_s[rows, :], sb, preferred_element_type=F32)
            vb = v_new.astype(BF16)
            o_s[rows, :] = (jnp.dot(qg_s[rows, :], sb, preferred_element_type=F32)
                            + jnp.dot(qk_s[rows, :], vb, preferred_element_type=F32))
            state = (state * el[:, j * dv:(j + 1) * dv]
                     + jnp.dot(kgt_s[i, :, j * cs:(j + 1) * cs], vb, preferred_element_type=F32))
        return state

    lax.fori_loop(0, n_groups, scan, jnp.zeros((C_DK, dv), F32))

    tr = 512

    def finish(t, carry):
        rows = pl.ds(pl.multiple_of(t * tr, tr), tr)
        gate = gate_ref[rows, :]
        y_ref[rows, :] = (_rms_rows(o_s[rows, :], og_ref[...]) * (gate * jax.nn.sigmoid(gate))).astype(y_ref.dtype)
        return carry

    lax.fori_loop(0, seq // tr, finish, 0)


def _mixer_c(x2, gain, w_in, conv_w, a_log, dt_bias, o_gain, w_out, batch, seq):
    d = x2.shape[1]
    hw = C_HEADS * C_DK
    tn = 256
    used = 4 * hw + 2 * C_HEADS
    kout = -(-used // tn) * tn
    w = jnp.concatenate([w_in, jnp.zeros((d, kout - used), F32)], axis=1).astype(BF16)
    (proj,) = _in_proj(x2, gain, w, jnp.ones((1, kout), F32), n_norm_cols=0, tn=tn, out_dtypes=(F32,))
    proj = proj.reshape(batch, seq, kout)

    ts = 512
    qkv = pl.pallas_call(
        _conv_c_kernel,
        out_shape=jax.ShapeDtypeStruct((batch, seq, 3 * hw), F32),
        grid=(batch, seq // ts, 3),
        in_specs=[
            pl.BlockSpec((None, ts, hw), lambda b, i, j: (b, i, j)),
            pl.BlockSpec((None, 8, hw), lambda b, i, j: (b, jnp.maximum(i * (ts // 8) - 1, 0), j)),
            pl.BlockSpec((C_CONV, hw), lambda b, i, j: (0, j)),
        ],
        out_specs=pl.BlockSpec((None, ts, hw), lambda b, i, j: (b, i, j)),
        compiler_params=_cparams(("parallel", "parallel", "arbitrary")),
        name="conv_c",
    )(proj, proj, conv_w)

    n_groups = seq // C_GROUP
    ba = proj[:, :, 4 * hw:4 * hw + 2 * C_HEADS].transpose(0, 2, 1).reshape(batch, 2 * C_HEADS, n_groups, C_GROUP)
    per_head = lambda v: jnp.broadcast_to(v.reshape(C_HEADS, 1, 1), (C_HEADS, 1, C_GROUP))
    head_cols = lambda off: pl.BlockSpec((None, seq, C_DK), lambda b, h: (b, 0, off + h))
    small = lambda off: pl.BlockSpec((None, None, n_groups, C_GROUP), lambda b, h: (b, off + h, 0, 0))
    scalar_row = pl.BlockSpec((None, 1, C_GROUP), lambda b, h: (h, 0, 0))
    y = pl.pallas_call(
        _delta_kernel,
        out_shape=jax.ShapeDtypeStruct((batch, seq, hw), BF16),
        grid=(batch, C_HEADS),
        in_specs=[head_cols(0), head_cols(C_HEADS), head_cols(2 * C_HEADS),
                  pl.BlockSpec((None, seq, C_DK), lambda b, h: (b, 0, 3 * C_HEADS + h)),
                  small(C_HEADS), small(0), scalar_row, scalar_row,
                  pl.BlockSpec((1, C_DK), lambda b, h: (0, 0))],
        out_specs=head_cols(0),
        scratch_shapes=[
            pltpu.VMEM((4, n_groups, C_GROUP), F32),
            pltpu.VMEM((seq, C_DK), F32),
            pltpu.VMEM((seq, C_DK), BF16),
            pltpu.VMEM((seq, C_DK), BF16),
            pltpu.VMEM((seq, C_CHUNK), BF16),
            pltpu.VMEM((n_groups, C_DK, C_GROUP), BF16),
            pltpu.VMEM((n_groups, (C_GROUP // C_CHUNK) * C_DK), F32),
            pltpu.VMEM((seq, C_DK), F32),
        ],
        compiler_params=_cparams(("parallel", "parallel")),
        name="delta_c",
    )(qkv, qkv, qkv, proj, ba, ba, per_head(a_log), per_head(dt_bias), o_gain.reshape(1, C_DK))
    return _out_proj(y.reshape(batch * seq, hw), w_out.astype(BF16), x2)


def kernel(x, rel_bias, norm_mix, norm_mlp, mlp_w1, mlp_w2, a_w_in, a_q_gain, a_k_gain, a_w_out,
           b_w_in, b_q_gain, b_k_gain, b_w_out, c_w_in, c_conv_w, c_a_log, c_dt_bias, c_o_gain, c_w_out):
    batch, seq, d = x.shape
    depth = norm_mix.shape[0]
    a_cols = len(A_GROUPS) * A_HEADS
    bias_a, bias_b = rel_bias[:, :a_cols], rel_bias[:, a_cols:]
    x2 = x.reshape(batch * seq, d)
    for i in range(depth):
        kind, j = i % 3, i // 3
        if kind == 0:
            x2 = _mixer_a(x2, norm_mix[i], a_w_in[j], a_q_gain[j], a_k_gain[j], a_w_out[j], bias_a, batch, seq)
        elif kind == 1:
            x2 = _mixer_b(x2, norm_mix[i], b_w_in[j], b_q_gain[j], b_k_gain[j], b_w_out[j], bias_b, batch, seq)
        else:
            x2 = _mixer_c(x2, norm_mix[i], c_w_in[j], c_conv_w[j], c_a_log[j], c_dt_bias[j], c_o_gain[j],
                          c_w_out[j], batch, seq)
        x2 = _mlp(x2, norm_mlp[i], mlp_w1[i].astype(BF16), mlp_w2[i].astype(BF16))
    return x2.reshape(batch, seq, d)
```

```python
import functools
import math

import numpy as np
import jax
import jax.numpy as jnp
from jax import lax
from jax.experimental import pallas as pl
from jax.experimental.pallas import tpu as pltpu

F32 = jnp.float32
BF16 = jnp.bfloat16
I32 = jnp.int32

EPS = 1e-6
HEAD_DIM = 64
NEG = -1e30
INT_MIN = -2 ** 31
LOG2E = math.log2(math.e)

V7X_VMEM_BYTES = 64 * 1024 * 1024
VMEM_LIMIT = V7X_VMEM_BYTES - 8 * 1024 * 1024

NUM_BUCKETS = 32
MAX_DISTANCE = 2048

A_GROUPS = ((128, 1), (512, 4), (2048, 16))
A_HEADS = 8
A_BLOCK = 128
A_GW = A_HEADS * HEAD_DIM

B_HEADS = 16
B_KV_HEADS = 4
B_IDX_HEADS = 8
B_IDX_DIM = 64
B_TOPK = 256
B_QT = 128
B_KC = 256
B_NDELTA = 14

C_HEADS = 8
C_DK = 128
C_CONV = 4
C_CHUNK = 64
C_GROUP = 4 * C_CHUNK


def _cparams(sem):
    return pltpu.CompilerParams(dimension_semantics=sem, vmem_limit_bytes=VMEM_LIMIT)


def _t5_bucket(dist):
    max_exact = NUM_BUCKETS // 2
    d = jnp.maximum(dist, 1).astype(F32)
    log_part = jnp.log(d / max_exact) / math.log(MAX_DISTANCE / max_exact) * (NUM_BUCKETS - max_exact)
    large = jnp.minimum(max_exact + log_part.astype(I32), NUM_BUCKETS - 1)
    return jnp.where(dist < max_exact, dist, large)


def _rms_rows(x, gain_row):
    ms = jnp.mean(x * x, axis=-1, keepdims=True)
    return x * lax.rsqrt(ms + EPS) * gain_row


def _in_proj_kernel(x_ref, g_ref, w_ref, hg_ref, bd_ref, *outs, n_norm, tn):
    hn = _rms_rows(x_ref[...], g_ref[...]).astype(BF16)
    for j in range(w_ref.shape[1] // tn):
        cols = slice(j * tn, (j + 1) * tn)
        y = jnp.dot(hn, w_ref[:, cols], preferred_element_type=F32)
        if j < n_norm:
            ms = jnp.dot((y * y).astype(BF16), bd_ref[...], preferred_element_type=F32) * (1.0 / HEAD_DIM)
            y = y * lax.rsqrt(ms + EPS) * hg_ref[:, cols]
        for o in outs:
            o[:, cols] = y.astype(o.dtype)


def _in_proj(x2, gain, w, head_gain, *, n_norm_cols, tn, out_dtypes, tm=512):
    n, d = x2.shape
    kout = w.shape[1]
    bd = np.kron(np.eye(tn // HEAD_DIM), np.ones((HEAD_DIM, HEAD_DIM))).astype(np.float32)
    const = lambda shape: pl.BlockSpec(shape, lambda i: (0, 0), pipeline_mode=pl.Buffered(1))
    return pl.pallas_call(
        functools.partial(_in_proj_kernel, n_norm=n_norm_cols // tn, tn=tn),
        out_shape=[jax.ShapeDtypeStruct((n, kout), dt) for dt in out_dtypes],
        grid=(n // tm,),
        in_specs=[pl.BlockSpec((tm, d), lambda i: (i, 0)), const((1, d)), const((d, kout)),
                  const((1, kout)), const((tn, tn))],
        out_specs=[pl.BlockSpec((tm, kout), lambda i: (i, 0)) for _ in out_dtypes],
        compiler_params=_cparams(("parallel",)),
        name="in_proj",
    )(x2, gain.reshape(1, d), w, head_gain, jnp.asarray(bd, BF16))


def _out_proj_kernel(y_ref, w_ref, x_ref, o_ref):
    o_ref[...] = x_ref[...] + jnp.dot(y_ref[...].astype(BF16), w_ref[...], preferred_element_type=F32)


def _out_proj(y2, w, x2, tm=512):
    n, kin = y2.shape
    d = w.shape[1]
    return pl.pallas_call(
        _out_proj_kernel,
        out_shape=jax.ShapeDtypeStruct((n, d), F32),
        grid=(n // tm,),
        in_specs=[
            pl.BlockSpec((tm, kin), lambda i: (i, 0)),
            pl.BlockSpec((kin, d), lambda i: (0, 0)),
            pl.BlockSpec((tm, d), lambda i: (i, 0)),
        ],
        out_specs=pl.BlockSpec((tm, d), lambda i: (i, 0)),
        compiler_params=_cparams(("parallel",)),
        name="out_proj",
    )(y2, w, x2)


def _out_proj_a_kernel(o0, o1, o2, l0, l1, l2, pt4_ref, pt16_ref, w_ref, x_ref, out_ref):
    tm = x_ref.shape[0]

    def token_order(ref, pt_ref):
        v = ref[...].reshape(tm, A_GW)
        if pt_ref is None:
            return v
        hi, lo = _split_bf16(v)
        return (jnp.dot(pt_ref[...], hi, preferred_element_type=F32)
                + jnp.dot(pt_ref[...], lo, preferred_element_type=F32))

    pts = (None, pt4_ref, pt16_ref)
    a, b, c = (token_order(r, pt) for r, pt in zip((l0, l1, l2), pts))
    m = jnp.maximum(jnp.maximum(a, b), c)
    ea, eb, ec = jnp.exp2(a - m), jnp.exp2(b - m), jnp.exp2(c - m)
    oa, ob, oc = (token_order(r, pt) for r, pt in zip((o0, o1, o2), pts))
    y = (ea * oa + eb * ob + ec * oc) / (ea + eb + ec)
    out_ref[...] = x_ref[...] + jnp.dot(y.astype(BF16), w_ref[...], preferred_element_type=F32)


def _mlp_kernel(x_ref, g_ref, w1_ref, w2_ref, o_ref, hn_ref):
    f = pl.program_id(1)

    @pl.when(f == 0)
    def _():
        x = x_ref[...]
        hn_ref[...] = _rms_rows(x, g_ref[...]).astype(BF16)
        o_ref[...] = x

    h = jnp.maximum(jnp.dot(hn_ref[...], w1_ref[...], preferred_element_type=F32), 0.0)
    o_ref[...] += jnp.dot((h * h).astype(BF16), w2_ref[...], preferred_element_type=F32)


def _mlp(x2, gain, w1, w2, tm=1024, tf=1024):
    n, d = x2.shape
    dff = w1.shape[1]
    return pl.pallas_call(
        _mlp_kernel,
        out_shape=jax.ShapeDtypeStruct((n, d), F32),
        grid=(n // tm, dff // tf),
        in_specs=[
            pl.BlockSpec((tm, d), lambda i, f: (i, 0)),
            pl.BlockSpec((1, d), lambda i, f: (0, 0)),
            pl.BlockSpec((d, tf), lambda i, f: (0, f)),
            pl.BlockSpec((tf, d), lambda i, f: (f, 0)),
        ],
        out_specs=pl.BlockSpec((tm, d), lambda i, f: (i, 0)),
        scratch_shapes=[pltpu.VMEM((tm, d), BF16)],
        compiler_params=_cparams(("parallel", "arbitrary")),
        name="mlp",
    )(x2, gain.reshape(1, d), w1, w2)


def _residue_major(tm, dilation):
    p = np.zeros((tm, tm), np.float32)
    j, r = np.meshgrid(np.arange(tm // dilation), np.arange(dilation), indexing="ij")
    p[(r * (tm // dilation) + j).ravel(), (j * dilation + r).ravel()] = 1.0
    return p


def _in_proj_a_kernel(x_ref, g_ref, w_ref, hg_ref, bd_ref, *rest):
    n_groups = len(A_GROUPS)
    perms, outs = rest[:n_groups - 1], rest[n_groups - 1:]
    tm = x_ref.shape[0]
    hn = _rms_rows(x_ref[...], g_ref[...]).astype(BF16)
    for which in range(3):
        for g, (_, dilation) in enumerate(A_GROUPS):
            j = which * n_groups + g
            cols = slice(j * A_GW, (j + 1) * A_GW)
            y = jnp.dot(hn, w_ref[:, cols], preferred_element_type=F32)
            if which < 2:
                ms = jnp.dot((y * y).astype(BF16), bd_ref[...], preferred_element_type=F32) * (1.0 / HEAD_DIM)
                y = y * lax.rsqrt(ms + EPS) * hg_ref[:, cols]
            y = y.astype(BF16)
            if dilation > 1:
                y = jnp.dot(perms[g - 1][...], y, preferred_element_type=F32).astype(BF16)
            outs[g][:, :, which * A_GW:(which + 1) * A_GW] = y.reshape(dilation, tm // dilation, A_GW)


def _attn_a_kernel(q_ref, kp_ref, kc_ref, vp_ref, vc_ref, bias_ref, o_ref, lse_ref):
    n = pl.program_id(2)
    blk = A_BLOCK
    pair = 2 * HEAD_DIM
    lane = lax.broadcasted_iota(I32, (2 * blk, pair), 1)
    row = lax.broadcasted_iota(I32, (2 * blk, blk), 0)
    pen = jnp.where((row < blk) & (n == 0), NEG, 0.0)
    nt = (((1,), (1,)), ((), ()))
    tn = (((0,), (0,)), ((), ()))
    for hp in range(A_HEADS // 2):
        sl = slice(hp * pair, (hp + 1) * pair)
        kpair = jnp.concatenate([kp_ref[:, sl], kc_ref[:, sl]], axis=0)
        vpair = jnp.concatenate([vp_ref[:, sl], vc_ref[:, sl]], axis=0)
        qpair = q_ref[:, sl]
        o_t, l_t = [], []
        for a in range(2):
            keep = (lane < HEAD_DIM) if a == 0 else (lane >= HEAD_DIM)
            ka = jnp.where(keep, kpair, jnp.zeros_like(kpair))
            s = lax.dot_general(ka, qpair, nt, preferred_element_type=F32) + bias_ref[2 * hp + a] + pen
            m = jnp.max(s, axis=0, keepdims=True)
            p = jnp.exp2(s - m)
            l = jnp.sum(p, axis=0, keepdims=True)
            pv = lax.dot_general(vpair, p.astype(BF16), tn, preferred_element_type=F32)
            rows = slice(a * HEAD_DIM, (a + 1) * HEAD_DIM)
            o_t.append((pv / l)[rows])
            l_t.append(jnp.broadcast_to(m + jnp.log2(l), (HEAD_DIM, blk)))
        o_ref[:, sl] = jnp.concatenate(o_t, axis=0).T
        lse_ref[:, sl] = jnp.concatenate(l_t, axis=0).T


def _attn_a_group(arr, bias_t, dilation, batch, seq):
    sub = seq // dilation
    nb = sub // A_BLOCK

    def spec(which, prev):
        def imap(b, r, n):
            return (b, r, jnp.maximum(n - 1, 0) if prev else n, which)
        return pl.BlockSpec((None, None, A_BLOCK, A_GW), imap)

    out_spec = pl.BlockSpec((None, None, A_BLOCK, A_GW), lambda b, r, n: (b, r, n, 0))
    out_sds = jax.ShapeDtypeStruct((batch, dilation, sub, A_GW), F32)
    return pl.pallas_call(
        _attn_a_kernel,
        out_shape=[out_sds, out_sds],
        grid=(batch, dilation, nb),
        in_specs=[spec(0, False), spec(1, True), spec(1, False), spec(2, True), spec(2, False),
                  pl.BlockSpec((A_HEADS, 2 * A_BLOCK, A_BLOCK), lambda b, r, n: (0, 0, 0))],
        out_specs=[out_spec, out_spec],
        compiler_params=_cparams(("parallel", "parallel", "arbitrary")),
        name=f"attn_a_d{dilation}",
    )(arr, arr, arr, arr, arr, bias_t)


def _bias_a(tab_g, dilation):
    step = np.arange(3 * A_BLOCK - 1) - (A_BLOCK - 1)
    vec = tab_g[_t5_bucket(jnp.asarray(np.maximum(step, 0) * dilation, I32))].astype(F32) * LOG2E
    vec = jnp.where(jnp.asarray((step >= 0) & (step <= A_BLOCK))[:, None], vec, NEG)
    return _toeplitz(vec.T, 2 * A_BLOCK, A_BLOCK)


def _mixer_a(x2, gain, w_in, q_gain, k_gain, w_out, bias_a, batch, seq, tm=512):
    n, d = x2.shape
    n_groups = len(A_GROUPS)
    reps = n_groups * A_HEADS
    hg = jnp.concatenate([jnp.tile(q_gain * (HEAD_DIM ** -0.5 * LOG2E), reps), jnp.tile(k_gain, reps),
                          jnp.ones((n_groups * A_GW,), F32)]).reshape(1, -1)
    kout = 3 * n_groups * A_GW
    bd = np.kron(np.eye(A_GW // HEAD_DIM), np.ones((HEAD_DIM, HEAD_DIM))).astype(np.float32)
    perms = [_residue_major(tm, dil) for _, dil in A_GROUPS[1:]]
    tiles = seq // tm
    const = lambda shape: pl.BlockSpec(shape, lambda i: (0, 0), pipeline_mode=pl.Buffered(1))
    grouped = lambda dil, width: pl.BlockSpec((None, dil, tm // dil, width),
                                              lambda i: (i // tiles, 0, i % tiles, 0))
    arrs = pl.pallas_call(
        _in_proj_a_kernel,
        out_shape=[jax.ShapeDtypeStruct((batch, dil, seq // dil, 3 * A_GW), BF16) for _, dil in A_GROUPS],
        grid=(n // tm,),
        in_specs=[pl.BlockSpec((tm, d), lambda i: (i, 0)), const((1, d)), const((d, kout)),
                  const((1, kout)), const((A_GW, A_GW))] + [const((tm, tm)) for _ in perms],
        out_specs=[grouped(dil, 3 * A_GW) for _, dil in A_GROUPS],
        compiler_params=_cparams(("parallel",)),
        name="in_proj_a",
    )(x2, gain.reshape(1, d), w_in.astype(BF16), hg, jnp.asarray(bd, BF16),
      *[jnp.asarray(p, BF16) for p in perms])

    os_, ls_ = [], []
    for g, (window, dilation) in enumerate(A_GROUPS):
        assert window // dilation == A_BLOCK and (seq // dilation) % A_BLOCK == 0 and dilation <= tm // 16
        tab_g = bias_a[:, g * A_HEADS:(g + 1) * A_HEADS]
        o, lse = _attn_a_group(arrs[g], _bias_a(tab_g, dilation), dilation, batch, seq)
        os_.append(o)
        ls_.append(lse)

    return pl.pallas_call(
        _out_proj_a_kernel,
        out_shape=jax.ShapeDtypeStruct((n, d), F32),
        grid=(n // tm,),
        in_specs=[grouped(dil, A_GW) for _, dil in A_GROUPS] * 2 + [const((tm, tm)) for _ in perms]
        + [const((A_GW, d)), pl.BlockSpec((tm, d), lambda i: (i, 0))],
        out_specs=pl.BlockSpec((tm, d), lambda i: (i, 0)),
        compiler_params=_cparams(("parallel",)),
        name="out_proj_a",
    )(*os_, *ls_, *[jnp.asarray(p.T, BF16) for p in perms], w_out.astype(BF16), x2)


def _float_key(s):
    i = lax.bitcast_convert_type(s, I32)
    k = jnp.where(i < 0, i ^ jnp.int32(0x7FFFFFFF), i)
    return jnp.where(s == 0.0, 0, k)


def _split_bf16(x):
    hi = x.astype(BF16)
    return hi, (x - hi.astype(F32)).astype(BF16)


def _attn_b_kernel(q_ref, k_ref, vt_ref, qi_ref, kw_ref, bias_ref, o_ref, key_ref, msk_ref):
    qb = pl.program_id(1)
    kc, qt = B_KC, B_QT
    nch = qb // 2 + 1
    t_q = qb * qt + lax.broadcasted_iota(I32, (kc, qt), 1)
    sub = lax.broadcasted_iota(I32, (kc, qt), 0)
    nt = (((1,), (1,)), ((), ()))

    qh, ql = _split_bf16(qi_ref[...])
    qi3 = jnp.concatenate(
        [jnp.concatenate([x[:, h * B_IDX_DIM:(h + 1) * B_IDX_DIM] for x in (qh, qh, ql)], axis=1)
         for h in range(B_IDX_HEADS)], axis=0)
    wt = kw_ref[pl.ds(pl.multiple_of(qb * qt, qt), qt), 0:qt].T
    wt = wt * (B_IDX_HEADS ** -0.5 * B_IDX_DIM ** -0.5)

    def score_chunk(c, carry):
        off = pl.multiple_of(c * kc, kc)
        kh, kl = _split_bf16(kw_ref[pl.ds(off, kc), 0:B_IDX_DIM])
        sc = lax.dot_general(jnp.concatenate([kh, kl, kh], axis=1), qi3, nt, preferred_element_type=F32)
        acc = jnp.zeros((kc, qt), F32)
        for h in range(B_IDX_HEADS):
            acc = acc + wt[B_IDX_DIM + h:B_IDX_DIM + h + 1, :] * jnp.maximum(sc[:, h * qt:(h + 1) * qt], 0.0)
        key_ref[pl.ds(off, kc), :] = jnp.where(off + sub <= t_q, _float_key(acc), INT_MIN)
        return carry

    lax.fori_loop(0, nch, score_chunk, 0)

    def count(pred_fn):
        def body(c, acc):
            off = pl.multiple_of(c * kc, kc)
            hit = jnp.where(pred_fn(key_ref[pl.ds(off, kc), :], off + sub), 1, 0)
            return acc + jnp.sum(hit.reshape(kc // 8, 8, qt), axis=0)
        acc = lax.fori_loop(0, nch, body, jnp.zeros((8, qt), I32))
        return jnp.sum(acc, axis=0, keepdims=True)

    def thr_bit(it, lo):
        cand = lo + jnp.left_shift(jnp.int32(1), 31 - it)
        cnt = count(lambda key, _: key >= cand)
        return jnp.where(cnt >= B_TOPK, cand, lo)

    thr = lax.fori_loop(0, 32, thr_bit, jnp.full((1, qt), INT_MIN, I32))
    need = B_TOPK - count(lambda key, _: key > thr)
    n_eq = count(lambda key, _: key == thr)

    def tie_search(_):
        def bit(it, j):
            cand = j + jnp.left_shift(jnp.int32(1), 12 - it)
            cnt = count(lambda key, idx: (key == thr) & (idx < cand))
            return jnp.where(cnt <= need, cand, j)
        return lax.fori_loop(0, 13, bit, jnp.zeros((1, qt), I32))

    any_tie = jnp.max(jnp.where(n_eq > need, 1, 0)) > 0
    j_max = lax.cond(any_tie, tie_search, lambda _: jnp.full((1, qt), 2 ** 13, I32), 0)

    def mask_chunk(c, carry):
        off = pl.multiple_of(c * kc, kc)
        key = key_ref[pl.ds(off, kc), :]
        idx = off + sub
        sel = ((key > thr) | ((key == thr) & (idx < j_max))) & (idx <= t_q)
        msk_ref[pl.ds(off, kc), :] = jnp.where(sel, 0.0, NEG)
        return carry

    lax.fori_loop(0, nch, mask_chunk, 0)

    grp = B_HEADS // B_KV_HEADS
    q = q_ref[...]
    groups = range(B_KV_HEADS)
    qg = [jnp.concatenate([q[:, (g * grp + j) * HEAD_DIM:(g * grp + j + 1) * HEAD_DIM]
                           for j in range(grp)], axis=0) for g in groups]
    ksl = [slice(g * HEAD_DIM, (g + 1) * HEAD_DIM) for g in groups]

    def flash(c, carry):
        m, l, acc = carry
        off = pl.multiple_of(c * kc, kc)
        d_lo = jnp.clip(qb - 2 * c, 0, B_NDELTA - 1)
        d_hi = jnp.clip(qb - 2 * c - 1, 0, B_NDELTA - 1)
        mk = msk_ref[pl.ds(off, kc), :]
        mk = jnp.concatenate([mk] * grp, axis=1)
        s = [lax.dot_general(k_ref[pl.ds(off, kc), ksl[g]], qg[g], nt, preferred_element_type=F32)
             for g in groups]
        s = [s[g] + jnp.concatenate([bias_ref[d_lo, g], bias_ref[d_hi, g]], axis=0) + mk for g in groups]
        m_new = [jnp.maximum(m[g], jnp.max(s[g], axis=0, keepdims=True)) for g in groups]
        p = [jnp.exp2(s[g] - m_new[g]) for g in groups]
        pv = [jnp.dot(vt_ref[ksl[g], pl.ds(off, kc)], p[g].astype(BF16), preferred_element_type=F32)
              for g in groups]
        alpha = [jnp.exp2(m[g] - m_new[g]) for g in groups]
        l = [alpha[g] * l[g] + jnp.sum(p[g], axis=0, keepdims=True) for g in groups]
        acc = [alpha[g] * acc[g] + pv[g] for g in groups]
        return m_new, l, acc

    m0 = [jnp.full((1, grp * qt), NEG, F32) for _ in groups]
    l0 = [jnp.zeros((1, grp * qt), F32) for _ in groups]
    a0 = [jnp.zeros((HEAD_DIM, grp * qt), F32) for _ in groups]
    _, l, acc = lax.fori_loop(0, nch, flash, (m0, l0, a0))
    for g in groups:
        o = acc[g] / l[g]
        for j in range(grp):
            h = g * grp + j
            o_ref[:, h * HEAD_DIM:(h + 1) * HEAD_DIM] = o[:, j * qt:(j + 1) * qt].T.astype(o_ref.dtype)


def _toeplitz(vec, n_rows, n_cols):
    span = n_rows + n_cols - 1
    assert vec.shape[-1] == span
    lead = vec.shape[:-1]
    padded = jnp.concatenate([vec, jnp.zeros(lead + (1,), vec.dtype)], axis=-1)
    flat = jnp.tile(padded, n_rows)[..., :n_rows * span]
    return flat.reshape(lead + (n_rows, span))[..., n_rows - 1:]


def _bias_b(tab):
    grp = B_HEADS // B_KV_HEADS
    span = 2 * B_QT - 1
    dist = np.arange(B_NDELTA)[:, None] * B_QT + np.arange(span)[None, :] - (B_QT - 1)
    assert (B_NDELTA - 1) * B_QT - (B_QT - 1) >= 16 * 128 ** (15.0 / 16.0) + 1
    vec = tab[_t5_bucket(jnp.asarray(np.maximum(dist, 0), I32))].astype(F32) * LOG2E
    t = _toeplitz(vec.transpose(0, 2, 1), B_QT, B_QT)
    t = t.reshape(B_NDELTA, B_KV_HEADS, grp, B_QT, B_QT).transpose(0, 1, 3, 2, 4)
    return t.reshape(B_NDELTA, B_KV_HEADS, B_QT, grp * B_QT)


def _mixer_b(x2, gain, w_in, q_gain, k_gain, w_out, bias_b, batch, seq):
    d = x2.shape[1]
    nq, nkv = B_HEADS * HEAD_DIM, B_KV_HEADS * HEAD_DIM
    nidx = B_IDX_HEADS * B_IDX_DIM
    tn = 256
    used = 2 * nkv + nq + nidx + B_IDX_DIM + B_IDX_HEADS
    kout = -(-used // tn) * tn
    w = jnp.concatenate([w_in, jnp.zeros((d, kout - used), F32)], axis=1).astype(BF16)
    hg = jnp.concatenate([jnp.tile(q_gain * (HEAD_DIM ** -0.5 * LOG2E), B_HEADS), jnp.tile(k_gain, B_KV_HEADS),
                          jnp.ones((kout - nq - nkv,), F32)]).reshape(1, -1)
    p32, p16 = _in_proj(x2, gain, w, hg, n_norm_cols=nq + nkv, tn=tn, out_dtypes=(F32, BF16))
    p32 = p32.reshape(batch, seq, kout)
    p16 = p16.reshape(batch, seq, kout)
    assert seq % (2 * B_QT) == 0 and seq <= 2 ** 12
    kw_blk = (nq + 2 * nkv + nidx) // tn
    vt = p16[:, :, nq + nkv:nq + 2 * nkv].transpose(0, 2, 1)
    y = pl.pallas_call(
        _attn_b_kernel,
        out_shape=jax.ShapeDtypeStruct((batch, seq, nq), BF16),
        grid=(batch, seq // B_QT),
        in_specs=[
            pl.BlockSpec((None, B_QT, nq), lambda b, i: (b, i, 0)),
            pl.BlockSpec((None, seq, nkv), lambda b, i: (b, 0, nq // nkv)),
            pl.BlockSpec((None, nkv, seq), lambda b, i: (b, 0, 0)),
            pl.BlockSpec((None, B_QT, nidx), lambda b, i: (b, i, (nq + 2 * nkv) // nidx)),
            pl.BlockSpec((None, seq, tn), lambda b, i: (b, 0, kw_blk)),
            pl.BlockSpec((B_NDELTA, B_KV_HEADS, B_QT, (B_HEADS // B_KV_HEADS) * B_QT),
                         lambda b, i: (0, 0, 0, 0), pipeline_mode=pl.Buffered(1)),
        ],
        out_specs=pl.BlockSpec((None, B_QT, nq), lambda b, i: (b, i, 0)),
        scratch_shapes=[pltpu.VMEM((seq, B_QT), I32), pltpu.VMEM((seq, B_QT), F32)],
        compiler_params=_cparams(("parallel", "arbitrary")),
        name="attn_b",
    )(p16, p16, vt, p32, p32, _bias_b(bias_b))
    return _out_proj(y.reshape(batch * seq, nq), w_out.astype(BF16), x2)


def _conv_c_kernel(x_ref, halo_ref, cw_ref, o_ref):
    i, j = pl.program_id(1), pl.program_id(2)
    ts = x_ref.shape[0]
    halo = jnp.where(i > 0, halo_ref[...], 0.0)
    xs = jnp.concatenate([halo, x_ref[...]], axis=0)
    cw = cw_ref[...]
    conv = sum(cw[t:t + 1, :] * xs[8 - (C_CONV - 1) + t:8 - (C_CONV - 1) + t + ts, :]
               for t in range(C_CONV))
    y = conv * jax.nn.sigmoid(conv)

    @pl.when(j == 2)
    def _():
        o_ref[...] = y

    @pl.when(j < 2)
    def _():
        scale = jnp.where(j == 0, C_DK ** -0.5, 1.0)
        for h in range(C_HEADS):
            sl = slice(h * C_DK, (h + 1) * C_DK)
            yh = y[:, sl]
            ss = jnp.sum(yh * yh, axis=-1, keepdims=True)
            o_ref[:, sl] = yh * (lax.rsqrt(ss + EPS) * scale)


def _delta_kernel(q_ref, k_ref, v_ref, gate_ref, a_ref, b_ref, alog_ref, dtb_ref, og_ref, y_ref,
                  rows_s, u_s, w_s, qg_s, qk_s, kgt_s, el_s, o_s):
    cs, gs = C_CHUNK, C_GROUP
    per = gs // cs
    seq, dv = v_ref.shape
    n_groups = seq // gs
    hi = lax.Precision.HIGHEST
    r = lax.broadcasted_iota(I32, (gs, gs), 0)
    c_ = lax.broadcasted_iota(I32, (gs, gs), 1)

    def same_block(size):
        sh = int(math.log2(size))
        return (r >> sh) == (c_ >> sh)

    chunk = same_block(cs)
    lower, strict, eye = chunk & (r >= c_), chunk & (r > c_), r == c_

    z = a_ref[...] + dtb_ref[...]
    softplus = jnp.maximum(z, 0.0) + jnp.log(1.0 + jnp.exp(-jnp.abs(z)))
    g_all = -jnp.exp(alog_ref[...]) * softplus
    gc_all = jnp.dot(g_all, jnp.where(chunk & (r <= c_), 1.0, 0.0), preferred_element_type=F32, precision=hi)
    gl_all = jnp.dot(g_all, jnp.where(chunk, 1.0, 0.0), preferred_element_type=F32, precision=hi)
    rows_s[0] = jax.nn.sigmoid(b_ref[...])
    rows_s[1] = gc_all
    rows_s[2] = jnp.exp(gc_all)
    rows_s[3] = jnp.exp(gl_all - gc_all)
    first = (lax.broadcasted_iota(I32, (gs, per * dv), 0)
             == (lax.broadcasted_iota(I32, (gs, per * dv), 1) // dv) * cs)
    el_s[...] = jnp.exp(jnp.dot(gl_all, jnp.where(first, 1.0, 0.0), preferred_element_type=F32, precision=hi))

    def mm(a, b):
        return jnp.dot(a.astype(BF16), b.astype(BF16), preferred_element_type=F32)

    def mm_nt(a, b):
        return lax.dot_general(a.astype(BF16), b.astype(BF16), (((1,), (1,)), ((), ())),
                               preferred_element_type=F32)

    ways = 2

    def prep(m, carry):
        ids = [m * ways + a for a in range(ways)]
        rows = [pl.ds(pl.multiple_of(i * gs, gs), gs) for i in ids]
        rw = [jnp.concatenate([rows_s[j, pl.ds(i, 1), :] for j in range(4)] + [jnp.zeros((4, gs), F32)], axis=0)
              for i in ids]
        cl = [x.T for x in rw]
        beta_col, gc_col, eg_col, ekg_col = ([x[:, j:j + 1] for x in cl] for j in range(4))
        q, k, v = ([ref[rw_, :] for rw_ in rows] for ref in (q_ref, k_ref, v_ref))
        decay = [jnp.where(lower, jnp.exp(jnp.where(lower, gcc - x[1:2, :], 0.0)), 0.0)
                 for gcc, x in zip(gc_col, rw)]
        kb = [k_ * b_ for k_, b_ in zip(k, beta_col)]
        a_mat = [jnp.where(strict, mm_nt(kb_, k_) * d_, 0.0) for kb_, k_, d_ in zip(kb, k, decay)]
        a8 = [jnp.where(same_block(8), a_, 0.0) for a_ in a_mat]
        t = [jnp.where(eye, 1.0, 0.0) - a_ for a_ in a8]
        pw = [mm(a_, a_) for a_ in a8]
        t = [t_ + mm(t_, p_) for t_, p_ in zip(t, pw)]
        pw = [mm(p_, p_) for p_ in pw]
        t = [t_ + mm(t_, p_) for t_, p_ in zip(t, pw)]
        size = 8
        while size < cs:
            sel = same_block(2 * size) & jnp.logical_not(same_block(size))
            nt = [mm(jnp.where(sel, a_, 0.0), t_) for a_, t_ in zip(a_mat, t)]
            t = [t_ - mm(t_, n_) for t_, n_ in zip(t, nt)]
            size *= 2
        uw = [mm(t_, jnp.concatenate([v_ * b_, kb_ * e_], axis=1))
              for t_, v_, b_, kb_, e_ in zip(t, v, beta_col, kb, eg_col)]
        qk = [jnp.where(lower, mm_nt(q_, k_) * d_, 0.0).astype(BF16) for q_, k_, d_ in zip(q, k, decay)]
        for a, i in enumerate(ids):
            u_s[rows[a], :] = uw[a][:, :dv]
            w_s[rows[a], :] = uw[a][:, dv:].astype(BF16)
            for j in range(per):
                qk_s[pl.ds(pl.multiple_of(i * gs + j * cs, cs), cs), :] = qk[a][j * cs:(j + 1) * cs, j * cs:(j + 1) * cs]
            qg_s[rows[a], :] = (q[a] * eg_col[a]).astype(BF16)
            kgt_s[i] = (k[a] * ekg_col[a]).T.astype(BF16)
        return carry

    lax.fori_loop(0, n_groups // ways, prep, 0)

    def scan(i, state):
        el = el_s[pl.ds(i, 1), :]
        for j in range(per):
            rows = pl.ds(pl.multiple_of(i * gs + j * cs, cs), cs)
            sb = state.astype(BF16)
            v_new = u_s[rows, :] - jnp.dot(w_s[rows, :], sb, preferred_element_type=F32)
            vb = v_new.astype(BF16)
            o_s[rows, :] = (jnp.dot(qg_s[rows, :], sb, preferred_element_type=F32)
                            + jnp.dot(qk_s[rows, :], vb, preferred_element_type=F32))
            state = (state * el[:, j * dv:(j + 1) * dv]
                     + jnp.dot(kgt_s[i, :, j * cs:(j + 1) * cs], vb, preferred_element_type=F32))
        return state

    lax.fori_loop(0, n_groups, scan, jnp.zeros((C_DK, dv), F32))

    tr = 512

    def finish(t, carry):
        rows = pl.ds(pl.multiple_of(t * tr, tr), tr)
        gate = gate_ref[rows, :]
        y_ref[rows, :] = (_rms_rows(o_s[rows, :], og_ref[...]) * (gate * jax.nn.sigmoid(gate))).astype(y_ref.dtype)
        return carry

    lax.fori_loop(0, seq // tr, finish, 0)


def _mixer_c(x2, gain, w_in, conv_w, a_log, dt_bias, o_gain, w_out, batch, seq):
    d = x2.shape[1]
    hw = C_HEADS * C_DK
    tn = 256
    used = 4 * hw + 2 * C_HEADS
    kout = -(-used // tn) * tn
    w = jnp.concatenate([w_in, jnp.zeros((d, kout - used), F32)], axis=1).astype(BF16)
    (proj,) = _in_proj(x2, gain, w, jnp.ones((1, kout), F32), n_norm_cols=0, tn=tn, out_dtypes=(F32,))
    proj = proj.reshape(batch, seq, kout)

    ts = 512
    qkv = pl.pallas_call(
        _conv_c_kernel,
        out_shape=jax.ShapeDtypeStruct((batch, seq, 3 * hw), F32),
        grid=(batch, seq // ts, 3),
        in_specs=[
            pl.BlockSpec((None, ts, hw), lambda b, i, j: (b, i, j)),
            pl.BlockSpec((None, 8, hw), lambda b, i, j: (b, jnp.maximum(i * (ts // 8) - 1, 0), j)),
            pl.BlockSpec((C_CONV, hw), lambda b, i, j: (0, j)),
        ],
        out_specs=pl.BlockSpec((None, ts, hw), lambda b, i, j: (b, i, j)),
        compiler_params=_cparams(("parallel", "parallel", "arbitrary")),
        name="conv_c",
    )(proj, proj, conv_w)

    n_groups = seq // C_GROUP
    ba = proj[:, :, 4 * hw:4 * hw + 2 * C_HEADS].transpose(0, 2, 1).reshape(batch, 2 * C_HEADS, n_groups, C_GROUP)
    per_head = lambda v: jnp.broadcast_to(v.reshape(C_HEADS, 1, 1), (C_HEADS, 1, C_GROUP))
    head_cols = lambda off: pl.BlockSpec((None, seq, C_DK), lambda b, h: (b, 0, off + h))
    small = lambda off: pl.BlockSpec((None, None, n_groups, C_GROUP), lambda b, h: (b, off + h, 0, 0))
    scalar_row = pl.BlockSpec((None, 1, C_GROUP), lambda b, h: (h, 0, 0))
    y = pl.pallas_call(
        _delta_kernel,
        out_shape=jax.ShapeDtypeStruct((batch, seq, hw), BF16),
        grid=(batch, C_HEADS),
        in_specs=[head_cols(0), head_cols(C_HEADS), head_cols(2 * C_HEADS),
                  pl.BlockSpec((None, seq, C_DK), lambda b, h: (b, 0, 3 * C_HEADS + h)),
                  small(C_HEADS), small(0), scalar_row, scalar_row,
                  pl.BlockSpec((1, C_DK), lambda b, h: (0, 0))],
        out_specs=head_cols(0),
        scratch_shapes=[
            pltpu.VMEM((4, n_groups, C_GROUP), F32),
            pltpu.VMEM((seq, C_DK), F32),
            pltpu.VMEM((seq, C_DK), BF16),
            pltpu.VMEM((seq, C_DK), BF16),
            pltpu.VMEM((seq, C_CHUNK), BF16),
            pltpu.VMEM((n_groups, C_DK, C_GROUP), BF16),
            pltpu.VMEM((n_groups, (C_GROUP // C_CHUNK) * C_DK), F32),
            pltpu.VMEM((seq, C_DK), F32),
        ],
        compiler_params=_cparams(("parallel", "parallel")),
        name="delta_c",
    )(qkv, qkv, qkv, proj, ba, ba, per_head(a_log), per_head(dt_bias), o_gain.reshape(1, C_DK))
    return _out_proj(y.reshape(batch * seq, hw), w_out.astype(BF16), x2)


def kernel(x, rel_bias, norm_mix, norm_mlp, mlp_w1, mlp_w2, a_w_in, a_q_gain, a_k_gain, a_w_out,
           b_w_in, b_q_gain, b_k_gain, b_w_out, c_w_in, c_conv_w, c_a_log, c_dt_bias, c_o_gain, c_w_out):
    batch, seq, d = x.shape
    depth = norm_mix.shape[0]
    a_cols = len(A_GROUPS) * A_HEADS
    bias_a, bias_b = rel_bias[:, :a_cols], rel_bias[:, a_cols:]
    x2 = x.reshape(batch * seq, d)
    for i in range(depth):
        kind, j = i % 3, i // 3
        if kind == 0:
            x2 = _mixer_a(x2, norm_mix[i], a_w_in[j], a_q_gain[j], a_k_gain[j], a_w_out[j], bias_a, batch, seq)
        elif kind == 1:
            x2 = _mixer_b(x2, norm_mix[i], b_w_in[j], b_q_gain[j], b_k_gain[j], b_w_out[j], bias_b, batch, seq)
        else:
            x2 = _mixer_c(x2, norm_mix[i], c_w_in[j], c_conv_w[j], c_a_log[j], c_dt_bias[j], c_o_gain[j],
                          c_w_out[j], batch, seq)
        x2 = _mlp(x2, norm_mlp[i], mlp_w1[i].astype(BF16), mlp_w2[i].astype(BF16))
    return x2.reshape(batch, seq, d)
```

```python
import functools
import math

import numpy as np
import jax
import jax.numpy as jnp
from jax import lax
from jax.experimental import pallas as pl
from jax.experimental.pallas import tpu as pltpu

F32 = jnp.float32
BF16 = jnp.bfloat16
I32 = jnp.int32

EPS = 1e-6
HEAD_DIM = 64
NEG = -1e30
INT_MIN = -2 ** 31
LOG2E = math.log2(math.e)

V7X_VMEM_BYTES = 64 * 1024 * 1024
VMEM_LIMIT = V7X_VMEM_BYTES - 8 * 1024 * 1024

NUM_BUCKETS = 32
MAX_DISTANCE = 2048

A_GROUPS = ((128, 1), (512, 4), (2048, 16))
A_HEADS = 8
A_BLOCK = 128
A_GW = A_HEADS * HEAD_DIM

B_HEADS = 16
B_KV_HEADS = 4
B_IDX_HEADS = 8
B_IDX_DIM = 64
B_TOPK = 256
B_QT = 128
B_KC = 256
B_NDELTA = 14

C_HEADS = 8
C_DK = 128
C_CONV = 4
C_CHUNK = 64
C_GROUP = 4 * C_CHUNK


def _cparams(sem):
    return pltpu.CompilerParams(dimension_semantics=sem, vmem_limit_bytes=VMEM_LIMIT)


def _t5_bucket(dist):
    max_exact = NUM_BUCKETS // 2
    d = jnp.maximum(dist, 1).astype(F32)
    log_part = jnp.log(d / max_exact) / math.log(MAX_DISTANCE / max_exact) * (NUM_BUCKETS - max_exact)
    large = jnp.minimum(max_exact + log_part.astype(I32), NUM_BUCKETS - 1)
    return jnp.where(dist < max_exact, dist, large)


def _rms_rows(x, gain_row):
    ms = jnp.mean(x * x, axis=-1, keepdims=True)
    return x * lax.rsqrt(ms + EPS) * gain_row


def _in_proj_kernel(x_ref, g_ref, w_ref, hg_ref, bd_ref, *outs, n_norm, tn):
    hn = _rms_rows(x_ref[...], g_ref[...]).astype(BF16)
    for j in range(w_ref.shape[1] // tn):
        cols = slice(j * tn, (j + 1) * tn)
        y = jnp.dot(hn, w_ref[:, cols], preferred_element_type=F32)
        if j < n_norm:
            ms = jnp.dot((y * y).astype(BF16), bd_ref[...], preferred_element_type=F32) * (1.0 / HEAD_DIM)
            y = y * lax.rsqrt(ms + EPS) * hg_ref[:, cols]
        for o in outs:
            o[:, cols] = y.astype(o.dtype)


def _in_proj(x2, gain, w, head_gain, *, n_norm_cols, tn, out_dtypes, tm=512):
    n, d = x2.shape
    kout = w.shape[1]
    bd = np.kron(np.eye(tn // HEAD_DIM), np.ones((HEAD_DIM, HEAD_DIM))).astype(np.float32)
    const = lambda shape: pl.BlockSpec(shape, lambda i: (0, 0), pipeline_mode=pl.Buffered(1))
    return pl.pallas_call(
        functools.partial(_in_proj_kernel, n_norm=n_norm_cols // tn, tn=tn),
        out_shape=[jax.ShapeDtypeStruct((n, kout), dt) for dt in out_dtypes],
        grid=(n // tm,),
        in_specs=[pl.BlockSpec((tm, d), lambda i: (i, 0)), const((1, d)), const((d, kout)),
                  const((1, kout)), const((tn, tn))],
        out_specs=[pl.BlockSpec((tm, kout), lambda i: (i, 0)) for _ in out_dtypes],
        compiler_params=_cparams(("parallel",)),
        name="in_proj",
    )(x2, gain.reshape(1, d), w, head_gain, jnp.asarray(bd, BF16))


def _out_proj_kernel(y_ref, w_ref, x_ref, o_ref):
    o_ref[...] = x_ref[...] + jnp.dot(y_ref[...].astype(BF16), w_ref[...], preferred_element_type=F32)


def _out_proj(y2, w, x2, tm=512):
    n, kin = y2.shape
    d = w.shape[1]
    return pl.pallas_call(
        _out_proj_kernel,
        out_shape=jax.ShapeDtypeStruct((n, d), F32),
        grid=(n // tm,),
        in_specs=[
            pl.BlockSpec((tm, kin), lambda i: (i, 0)),
            pl.BlockSpec((kin, d), lambda i: (0, 0)),
            pl.BlockSpec((tm, d), lambda i: (i, 0)),
        ],
        out_specs=pl.BlockSpec((tm, d), lambda i: (i, 0)),
        compiler_params=_cparams(("parallel",)),
        name="out_proj",
    )(y2, w, x2)


def _out_proj_a_kernel(o0, o1, o2, l0, l1, l2, pt4_ref, pt16_ref, w_ref, x_ref, out_ref):
    tm = x_ref.shape[0]

    def token_order(ref, pt_ref):
        v = ref[...].reshape(tm, A_GW)
        if pt_ref is None:
            return v
        hi, lo = _split_bf16(v)
        return (jnp.dot(pt_ref[...], hi, preferred_element_type=F32)
                + jnp.dot(pt_ref[...], lo, preferred_element_type=F32))

    pts = (None, pt4_ref, pt16_ref)
    a, b, c = (token_order(r, pt) for r, pt in zip((l0, l1, l2), pts))
    m = jnp.maximum(jnp.maximum(a, b), c)
    ea, eb, ec = jnp.exp2(a - m), jnp.exp2(b - m), jnp.exp2(c - m)
    oa, ob, oc = (token_order(r, pt) for r, pt in zip((o0, o1, o2), pts))
    y = (ea * oa + eb * ob + ec * oc) / (ea + eb + ec)
    out_ref[...] = x_ref[...] + jnp.dot(y.astype(BF16), w_ref[...], preferred_element_type=F32)


def _mlp_kernel(x_ref, g_ref, w1_ref, w2_ref, o_ref, hn_ref):
    f = pl.program_id(1)

    @pl.when(f == 0)
    def _():
        x = x_ref[...]
        hn_ref[...] = _rms_rows(x, g_ref[...]).astype(BF16)
        o_ref[...] = x

    h = jnp.maximum(jnp.dot(hn_ref[...], w1_ref[...], preferred_element_type=F32), 0.0)
    o_ref[...] += jnp.dot((h * h).astype(BF16), w2_ref[...], preferred_element_type=F32)


def _mlp(x2, gain, w1, w2, tm=1024, tf=1024):
    n, d = x2.shape
    dff = w1.shape[1]
    return pl.pallas_call(
        _mlp_kernel,
        out_shape=jax.ShapeDtypeStruct((n, d), F32),
        grid=(n // tm, dff // tf),
        in_specs=[
            pl.BlockSpec((tm, d), lambda i, f: (i, 0)),
            pl.BlockSpec((1, d), lambda i, f: (0, 0)),
            pl.BlockSpec((d, tf), lambda i, f: (0, f)),
            pl.BlockSpec((tf, d), lambda i, f: (f, 0)),
        ],
        out_specs=pl.BlockSpec((tm, d), lambda i, f: (i, 0)),
        scratch_shapes=[pltpu.VMEM((tm, d), BF16)],
        compiler_params=_cparams(("parallel", "arbitrary")),
        name="mlp",
    )(x2, gain.reshape(1, d), w1, w2)


def _residue_major(tm, dilation):
    p = np.zeros((tm, tm), np.float32)
    j, r = np.meshgrid(np.arange(tm // dilation), np.arange(dilation), indexing="ij")
    p[(r * (tm // dilation) + j).ravel(), (j * dilation + r).ravel()] = 1.0
    return p


def _in_proj_a_kernel(x_ref, g_ref, w_ref, hg_ref, bd_ref, *rest):
    n_groups = len(A_GROUPS)
    perms, outs = rest[:n_groups - 1], rest[n_groups - 1:]
    tm = x_ref.shape[0]
    hn = _rms_rows(x_ref[...], g_ref[...]).astype(BF16)
    for which in range(3):
        for g, (_, dilation) in enumerate(A_GROUPS):
            j = which * n_groups + g
            cols = slice(j * A_GW, (j + 1) * A_GW)
            y = jnp.dot(hn, w_ref[:, cols], preferred_element_type=F32)
            if which < 2:
                ms = jnp.dot((y * y).astype(BF16), bd_ref[...], preferred_element_type=F32) * (1.0 / HEAD_DIM)
                y = y * lax.rsqrt(ms + EPS) * hg_ref[:, cols]
            y = y.astype(BF16)
            if dilation > 1:
                y = jnp.dot(perms[g - 1][...], y, preferred_element_type=F32).astype(BF16)
            outs[g][:, :, which * A_GW:(which + 1) * A_GW] = y.reshape(dilation, tm // dilation, A_GW)


def _attn_a_kernel(q_ref, kp_ref, kc_ref, vp_ref, vc_ref, bias_ref, o_ref, lse_ref):
    n = pl.program_id(2)
    blk = A_BLOCK
    pair = 2 * HEAD_DIM
    lane = lax.broadcasted_iota(I32, (2 * blk, pair), 1)
    row = lax.broadcasted_iota(I32, (2 * blk, blk), 0)
    pen = jnp.where((row < blk) & (n == 0), NEG, 0.0)
    nt = (((1,), (1,)), ((), ()))
    tn = (((0,), (0,)), ((), ()))
    for hp in range(A_HEADS // 2):
        sl = slice(hp * pair, (hp + 1) * pair)
        kpair = jnp.concatenate([kp_ref[:, sl], kc_ref[:, sl]], axis=0)
        vpair = jnp.concatenate([vp_ref[:, sl], vc_ref[:, sl]], axis=0)
        qpair = q_ref[:, sl]
        o_t, l_t = [], []
        for a in range(2):
            keep = (lane < HEAD_DIM) if a == 0 else (lane >= HEAD_DIM)
            ka = jnp.where(keep, kpair, jnp.zeros_like(kpair))
            s = lax.dot_general(ka, qpair, nt, preferred_element_type=F32) + bias_ref[2 * hp + a] + pen
            m = jnp.max(s, axis=0, keepdims=True)
            p = jnp.exp2(s - m)
            l = jnp.sum(p, axis=0, keepdims=True)
            pv = lax.dot_general(vpair, p.astype(BF16), tn, preferred_element_type=F32)
            rows = slice(a * HEAD_DIM, (a + 1) * HEAD_DIM)
            o_t.append((pv / l)[rows])
            l_t.append(jnp.broadcast_to(m + jnp.log2(l), (HEAD_DIM, blk)))
        o_ref[:, sl] = jnp.concatenate(o_t, axis=0).T
        lse_ref[:, sl] = jnp.concatenate(l_t, axis=0).T


def _attn_a_group(arr, bias_t, dilation, batch, seq):
    sub = seq // dilation
    nb = sub // A_BLOCK

    def spec(which, prev):
        def imap(b, r, n):
            return (b, r, jnp.maximum(n - 1, 0) if prev else n, which)
        return pl.BlockSpec((None, None, A_BLOCK, A_GW), imap)

    out_spec = pl.BlockSpec((None, None, A_BLOCK, A_GW), lambda b, r, n: (b, r, n, 0))
    out_sds = jax.ShapeDtypeStruct((batch, dilation, sub, A_GW), F32)
    return pl.pallas_call(
        _attn_a_kernel,
        out_shape=[out_sds, out_sds],
        grid=(batch, dilation, nb),
        in_specs=[spec(0, False), spec(1, True), spec(1, False), spec(2, True), spec(2, False),
                  pl.BlockSpec((A_HEADS, 2 * A_BLOCK, A_BLOCK), lambda b, r, n: (0, 0, 0))],
        out_specs=[out_spec, out_spec],
        compiler_params=_cparams(("parallel", "parallel", "arbitrary")),
        name=f"attn_a_d{dilation}",
    )(arr, arr, arr, arr, arr, bias_t)


def _bias_a(tab_g, dilation):
    step = np.arange(3 * A_BLOCK - 1) - (A_BLOCK - 1)
    vec = tab_g[_t5_bucket(jnp.asarray(np.maximum(step, 0) * dilation, I32))].astype(F32) * LOG2E
    vec = jnp.where(jnp.asarray((step >= 0) & (step <= A_BLOCK))[:, None], vec, NEG)
    return _toeplitz(vec.T, 2 * A_BLOCK, A_BLOCK)


def _mixer_a(x2, gain, w_in, q_gain, k_gain, w_out, bias_a, batch, seq, tm=512):
    n, d = x2.shape
    n_groups = len(A_GROUPS)
    reps = n_groups * A_HEADS
    hg = jnp.concatenate([jnp.tile(q_gain * (HEAD_DIM ** -0.5 * LOG2E), reps), jnp.tile(k_gain, reps),
                          jnp.ones((n_groups * A_GW,), F32)]).reshape(1, -1)
    kout = 3 * n_groups * A_GW
    bd = np.kron(np.eye(A_GW // HEAD_DIM), np.ones((HEAD_DIM, HEAD_DIM))).astype(np.float32)
    perms = [_residue_major(tm, dil) for _, dil in A_GROUPS[1:]]
    tiles = seq // tm
    const = lambda shape: pl.BlockSpec(shape, lambda i: (0, 0), pipeline_mode=pl.Buffered(1))
    grouped = lambda dil, width: pl.BlockSpec((None, dil, tm // dil, width),
                                              lambda i: (i // tiles, 0, i % tiles, 0))
    arrs = pl.pallas_call(
        _in_proj_a_kernel,
        out_shape=[jax.ShapeDtypeStruct((batch, dil, seq // dil, 3 * A_GW), BF16) for _, dil in A_GROUPS],
        grid=(n // tm,),
        in_specs=[pl.BlockSpec((tm, d), lambda i: (i, 0)), const((1, d)), const((d, kout)),
                  const((1, kout)), const((A_GW, A_GW))] + [const((tm, tm)) for _ in perms],
        out_specs=[grouped(dil, 3 * A_GW) for _, dil in A_GROUPS],
        compiler_params=_cparams(("parallel",)),
        name="in_proj_a",
    )(x2, gain.reshape(1, d), w_in.astype(BF16), hg, jnp.asarray(bd, BF16),
      *[jnp.asarray(p, BF16) for p in perms])

    os_, ls_ = [], []
    for g, (window, dilation) in enumerate(A_GROUPS):
        assert window // dilation == A_BLOCK and (seq // dilation) % A_BLOCK == 0 and dilation <= tm // 16
        tab_g = bias_a[:, g * A_HEADS:(g + 1) * A_HEADS]
        o, lse = _attn_a_group(arrs[g], _bias_a(tab_g, dilation), dilation, batch, seq)
        os_.append(o)
        ls_.append(lse)

    return pl.pallas_call(
        _out_proj_a_kernel,
        out_shape=jax.ShapeDtypeStruct((n, d), F32),
        grid=(n // tm,),
        in_specs=[grouped(dil, A_GW) for _, dil in A_GROUPS] * 2 + [const((tm, tm)) for _ in perms]
        + [const((A_GW, d)), pl.BlockSpec((tm, d), lambda i: (i, 0))],
        out_specs=pl.BlockSpec((tm, d), lambda i: (i, 0)),
        compiler_params=_cparams(("parallel",)),
        name="out_proj_a",
    )(*os_, *ls_, *[jnp.asarray(p.T, BF16) for p in perms], w_out.astype(BF16), x2)


def _float_key(s):
    i = lax.bitcast_convert_type(s, I32)
    k = jnp.where(i < 0, i ^ jnp.int32(0x7FFFFFFF), i)
    return jnp.where(s == 0.0, 0, k)


def _split_bf16(x):
    hi = x.astype(BF16)
    return hi, (x - hi.astype(F32)).astype(BF16)


def _attn_b_kernel(q_ref, k_ref, vt_ref, qi_ref, kw_ref, bias_ref, o_ref, key_ref, msk_ref):
    qb = pl.program_id(1)
    kc, qt = B_KC, B_QT
    nch = qb // 2 + 1
    t_q = qb * qt + lax.broadcasted_iota(I32, (kc, qt), 1)
    sub = lax.broadcasted_iota(I32, (kc, qt), 0)
    nt = (((1,), (1,)), ((), ()))

    qh, ql = _split_bf16(qi_ref[...])
    qi3 = jnp.concatenate(
        [jnp.concatenate([x[:, h * B_IDX_DIM:(h + 1) * B_IDX_DIM] for x in (qh, qh, ql)], axis=1)
         for h in range(B_IDX_HEADS)], axis=0)
    wt = kw_ref[pl.ds(pl.multiple_of(qb * qt, qt), qt), 0:qt].T
    wt = wt * (B_IDX_HEADS ** -0.5 * B_IDX_DIM ** -0.5)

    npair = (nch + 1) // 2

    def score_pair(c2, carry):
        offs = [pl.multiple_of((2 * c2 + a) * kc, kc) for a in range(2)]
        ks = [_split_bf16(kw_ref[pl.ds(off, kc), 0:B_IDX_DIM]) for off in offs]
        sc = [lax.dot_general(jnp.concatenate([kh, kl, kh], axis=1), qi3, nt, preferred_element_type=F32)
              for kh, kl in ks]
        for off, s in zip(offs, sc):
            acc = jnp.zeros((kc, qt), F32)
            for h in range(B_IDX_HEADS):
                acc = acc + wt[B_IDX_DIM + h:B_IDX_DIM + h + 1, :] * jnp.maximum(s[:, h * qt:(h + 1) * qt], 0.0)
            key_ref[pl.ds(off, kc), :] = jnp.where(off + sub <= t_q, _float_key(acc), INT_MIN)
        return carry

    lax.fori_loop(0, npair, score_pair, 0)

    def count(pred_fn):
        def body(c2, acc):
            for a in range(2):
                off = pl.multiple_of((2 * c2 + a) * kc, kc)
                hit = jnp.where(pred_fn(key_ref[pl.ds(off, kc), :], off + sub), 1, 0)
                acc = acc + jnp.sum(hit.reshape(kc // 8, 8, qt), axis=0)
            return acc
        acc = lax.fori_loop(0, npair, body, jnp.zeros((8, qt), I32))
        return jnp.sum(acc, axis=0, keepdims=True)

    def thr_bit(it, lo):
        cand = lo + jnp.left_shift(jnp.int32(1), 31 - it)
        cnt = count(lambda key, _: key >= cand)
        return jnp.where(cnt >= B_TOPK, cand, lo)

    thr = lax.fori_loop(0, 32, thr_bit, jnp.full((1, qt), INT_MIN, I32))
    need = B_TOPK - count(lambda key, _: key > thr)
    n_eq = count(lambda key, _: key == thr)

    def tie_search(_):
        def bit(it, j):
            cand = j + jnp.left_shift(jnp.int32(1), 12 - it)
            cnt = count(lambda key, idx: (key == thr) & (idx < cand))
            return jnp.where(cnt <= need, cand, j)
        return lax.fori_loop(0, 13, bit, jnp.zeros((1, qt), I32))

    any_tie = jnp.max(jnp.where(n_eq > need, 1, 0)) > 0
    j_max = lax.cond(any_tie, tie_search, lambda _: jnp.full((1, qt), 2 ** 13, I32), 0)

    def mask_chunk(c, carry):
        off = pl.multiple_of(c * kc, kc)
        key = key_ref[pl.ds(off, kc), :]
        idx = off + sub
        sel = ((key > thr) | ((key == thr) & (idx < j_max))) & (idx <= t_q)
        msk_ref[pl.ds(off, kc), :] = jnp.where(sel, 0.0, NEG)
        return carry

    lax.fori_loop(0, nch, mask_chunk, 0)

    grp = B_HEADS // B_KV_HEADS
    q = q_ref[...]
    groups = range(B_KV_HEADS)
    qg = [jnp.concatenate([q[:, (g * grp + j) * HEAD_DIM:(g * grp + j + 1) * HEAD_DIM]
                           for j in range(grp)], axis=0) for g in groups]
    ksl = [slice(g * HEAD_DIM, (g + 1) * HEAD_DIM) for g in groups]

    def flash(c, carry):
        m, l, acc = carry
        off = pl.multiple_of(c * kc, kc)
        d_lo = jnp.clip(qb - 2 * c, 0, B_NDELTA - 1)
        d_hi = jnp.clip(qb - 2 * c - 1, 0, B_NDELTA - 1)
        mk = msk_ref[pl.ds(off, kc), :]
        mk = jnp.concatenate([mk] * grp, axis=1)
        s = [lax.dot_general(k_ref[pl.ds(off, kc), ksl[g]], qg[g], nt, preferred_element_type=F32)
             for g in groups]
        s = [s[g] + jnp.concatenate([bias_ref[d_lo, g], bias_ref[d_hi, g]], axis=0) + mk for g in groups]
        m_new = [jnp.maximum(m[g], jnp.max(s[g], axis=0, keepdims=True)) for g in groups]
        p = [jnp.exp2(s[g] - m_new[g]) for g in groups]
        pv = [jnp.dot(vt_ref[ksl[g], pl.ds(off, kc)], p[g].astype(BF16), preferred_element_type=F32)
              for g in groups]
        alpha = [jnp.exp2(m[g] - m_new[g]) for g in groups]
        l = [alpha[g] * l[g] + jnp.sum(p[g], axis=0, keepdims=True) for g in groups]
        acc = [alpha[g] * acc[g] + pv[g] for g in groups]
        return m_new, l, acc

    m0 = [jnp.full((1, grp * qt), NEG, F32) for _ in groups]
    l0 = [jnp.zeros((1, grp * qt), F32) for _ in groups]
    a0 = [jnp.zeros((HEAD_DIM, grp * qt), F32) for _ in groups]
    _, l, acc = lax.fori_loop(0, nch, flash, (m0, l0, a0))
    for g in groups:
        o = acc[g] / l[g]
        for j in range(grp):
            h = g * grp + j
            o_ref[:, h * HEAD_DIM:(h + 1) * HEAD_DIM] = o[:, j * qt:(j + 1) * qt].T.astype(o_ref.dtype)


def _toeplitz(vec, n_rows, n_cols):
    span = n_rows + n_cols - 1
    assert vec.shape[-1] == span
    lead = vec.shape[:-1]
    padded = jnp.concatenate([vec, jnp.zeros(lead + (1,), vec.dtype)], axis=-1)
    flat = jnp.tile(padded, n_rows)[..., :n_rows * span]
    return flat.reshape(lead + (n_rows, span))[..., n_rows - 1:]


def _bias_b(tab):
    grp = B_HEADS // B_KV_HEADS
    span = 2 * B_QT - 1
    dist = np.arange(B_NDELTA)[:, None] * B_QT + np.arange(span)[None, :] - (B_QT - 1)
    assert (B_NDELTA - 1) * B_QT - (B_QT - 1) >= 16 * 128 ** (15.0 / 16.0) + 1
    vec = tab[_t5_bucket(jnp.asarray(np.maximum(dist, 0), I32))].astype(F32) * LOG2E
    t = _toeplitz(vec.transpose(0, 2, 1), B_QT, B_QT)
    t = t.reshape(B_NDELTA, B_KV_HEADS, grp, B_QT, B_QT).transpose(0, 1, 3, 2, 4)
    return t.reshape(B_NDELTA, B_KV_HEADS, B_QT, grp * B_QT)


def _mixer_b(x2, gain, w_in, q_gain, k_gain, w_out, bias_b, batch, seq):
    d = x2.shape[1]
    nq, nkv = B_HEADS * HEAD_DIM, B_KV_HEADS * HEAD_DIM
    nidx = B_IDX_HEADS * B_IDX_DIM
    tn = 256
    used = 2 * nkv + nq + nidx + B_IDX_DIM + B_IDX_HEADS
    kout = -(-used // tn) * tn
    w = jnp.concatenate([w_in, jnp.zeros((d, kout - used), F32)], axis=1).astype(BF16)
    hg = jnp.concatenate([jnp.tile(q_gain * (HEAD_DIM ** -0.5 * LOG2E), B_HEADS), jnp.tile(k_gain, B_KV_HEADS),
                          jnp.ones((kout - nq - nkv,), F32)]).reshape(1, -1)
    p32, p16 = _in_proj(x2, gain, w, hg, n_norm_cols=nq + nkv, tn=tn, out_dtypes=(F32, BF16))
    p32 = p32.reshape(batch, seq, kout)
    p16 = p16.reshape(batch, seq, kout)
    assert seq % (2 * B_QT) == 0 and seq <= 2 ** 12
    kw_blk = (nq + 2 * nkv + nidx) // tn
    vt = p16[:, :, nq + nkv:nq + 2 * nkv].transpose(0, 2, 1)
    y = pl.pallas_call(
        _attn_b_kernel,
        out_shape=jax.ShapeDtypeStruct((batch, seq, nq), BF16),
        grid=(batch, seq // B_QT),
        in_specs=[
            pl.BlockSpec((None, B_QT, nq), lambda b, i: (b, i, 0)),
            pl.BlockSpec((None, seq, nkv), lambda b, i: (b, 0, nq // nkv)),
            pl.BlockSpec((None, nkv, seq), lambda b, i: (b, 0, 0)),
            pl.BlockSpec((None, B_QT, nidx), lambda b, i: (b, i, (nq + 2 * nkv) // nidx)),
            pl.BlockSpec((None, seq, tn), lambda b, i: (b, 0, kw_blk)),
            pl.BlockSpec((B_NDELTA, B_KV_HEADS, B_QT, (B_HEADS // B_KV_HEADS) * B_QT),
                         lambda b, i: (0, 0, 0, 0), pipeline_mode=pl.Buffered(1)),
        ],
        out_specs=pl.BlockSpec((None, B_QT, nq), lambda b, i: (b, i, 0)),
        scratch_shapes=[pltpu.VMEM((seq, B_QT), I32), pltpu.VMEM((seq, B_QT), F32)],
        compiler_params=_cparams(("parallel", "arbitrary")),
        name="attn_b",
    )(p16, p16, vt, p32, p32, _bias_b(bias_b))
    return _out_proj(y.reshape(batch * seq, nq), w_out.astype(BF16), x2)


def _conv_c_kernel(x_ref, halo_ref, cw_ref, o_ref):
    i, j = pl.program_id(1), pl.program_id(2)
    ts = x_ref.shape[0]
    halo = jnp.where(i > 0, halo_ref[...], 0.0)
    xs = jnp.concatenate([halo, x_ref[...]], axis=0)
    cw = cw_ref[...]
    conv = sum(cw[t:t + 1, :] * xs[8 - (C_CONV - 1) + t:8 - (C_CONV - 1) + t + ts, :]
               for t in range(C_CONV))
    y = conv * jax.nn.sigmoid(conv)

    @pl.when(j == 2)
    def _():
        o_ref[...] = y

    @pl.when(j < 2)
    def _():
        scale = jnp.where(j == 0, C_DK ** -0.5, 1.0)
        for h in range(C_HEADS):
            sl = slice(h * C_DK, (h + 1) * C_DK)
            yh = y[:, sl]
            ss = jnp.sum(yh * yh, axis=-1, keepdims=True)
            o_ref[:, sl] = yh * (lax.rsqrt(ss + EPS) * scale)


def _delta_kernel(q_ref, k_ref, v_ref, gate_ref, a_ref, b_ref, alog_ref, dtb_ref, og_ref, y_ref,
                  rows_s, p_s, r_s, m_s, n_s, el_s, o_s):
    cs, gs = C_CHUNK, C_GROUP
    per = gs // cs
    seq, dv = v_ref.shape
    n_groups = seq // gs
    hi = lax.Precision.HIGHEST
    r = lax.broadcasted_iota(I32, (gs, gs), 0)
    c_ = lax.broadcasted_iota(I32, (gs, gs), 1)

    def same_block(size):
        sh = int(math.log2(size))
        return (r >> sh) == (c_ >> sh)

    chunk = same_block(cs)
    lower, strict, eye = chunk & (r >= c_), chunk & (r > c_), r == c_

    z = a_ref[...] + dtb_ref[...]
    softplus = jnp.maximum(z, 0.0) + jnp.log(1.0 + jnp.exp(-jnp.abs(z)))
    g_all = -jnp.exp(alog_ref[...]) * softplus
    gc_all = jnp.dot(g_all, jnp.where(chunk & (r <= c_), 1.0, 0.0), preferred_element_type=F32, precision=hi)
    gl_all = jnp.dot(g_all, jnp.where(chunk, 1.0, 0.0), preferred_element_type=F32, precision=hi)
    rows_s[0] = jax.nn.sigmoid(b_ref[...])
    rows_s[1] = gc_all
    rows_s[2] = jnp.exp(gc_all)
    rows_s[3] = jnp.exp(gl_all - gc_all)
    first = (lax.broadcasted_iota(I32, (gs, per * dv), 0)
             == (lax.broadcasted_iota(I32, (gs, per * dv), 1) // dv) * cs)
    el_s[...] = jnp.exp(jnp.dot(gl_all, jnp.where(first, 1.0, 0.0), preferred_element_type=F32, precision=hi))

    def mm(a, b):
        return jnp.dot(a.astype(BF16), b.astype(BF16), preferred_element_type=F32)

    def mm_nt(a, b):
        return lax.dot_general(a.astype(BF16), b.astype(BF16), (((1,), (1,)), ((), ())),
                               preferred_element_type=F32)

    ways = 4

    def prep(m):
        ids = [m * ways + a for a in range(ways)]
        rows = [pl.ds(pl.multiple_of(i * gs, gs), gs) for i in ids]
        rw = [jnp.concatenate([rows_s[j, pl.ds(i, 1), :] for j in range(4)] + [jnp.zeros((4, gs), F32)], axis=0)
              for i in ids]
        cl = [x.T for x in rw]
        beta_col, gc_col, eg_col, ekg_col = ([x[:, j:j + 1] for x in cl] for j in range(4))
        q, k, v = ([ref[rw_, :] for rw_ in rows] for ref in (q_ref, k_ref, v_ref))
        decay = [jnp.where(lower, jnp.exp(jnp.where(lower, gcc - x[1:2, :], 0.0)), 0.0)
                 for gcc, x in zip(gc_col, rw)]
        kb = [k_ * b_ for k_, b_ in zip(k, beta_col)]
        yield
        a_mat = [jnp.where(strict, mm_nt(kb_, k_) * d_, 0.0) for kb_, k_, d_ in zip(kb, k, decay)]
        yield
        a8 = [jnp.where(same_block(8), a_, 0.0) for a_ in a_mat]
        t = [jnp.where(eye, 1.0, 0.0) - a_ for a_ in a8]
        pw = [mm(a_, a_) for a_ in a8]
        yield
        t = [t_ + mm(t_, p_) for t_, p_ in zip(t, pw)]
        yield
        pw = [mm(p_, p_) for p_ in pw]
        yield
        t = [t_ + mm(t_, p_) for t_, p_ in zip(t, pw)]
        yield
        size = 8
        while size < cs:
            sel = same_block(2 * size) & jnp.logical_not(same_block(size))
            nt = [mm(jnp.where(sel, a_, 0.0), t_) for a_, t_ in zip(a_mat, t)]
            yield
            t = [t_ - mm(t_, n_) for t_, n_ in zip(t, nt)]
            yield
            size *= 2
        uw = [mm(t_, jnp.concatenate([v_ * b_, kb_ * e_], axis=1))
              for t_, v_, b_, kb_, e_ in zip(t, v, beta_col, kb, eg_col)]
        yield
        qk = [jnp.where(lower, mm_nt(q_, k_) * d_, 0.0) for q_, k_, d_ in zip(q, k, decay)]
        yield
        qkwu = [mm(qk_, uw_) for qk_, uw_ in zip(qk, uw)]
        kgt = [(k_ * e_).T for k_, e_ in zip(k, ekg_col)]
        yield
        nm = [[mm(kgt[a][:, j * cs:(j + 1) * cs], uw[a][j * cs:(j + 1) * cs, :]) for j in range(per)]
              for a in range(ways)]

        def write():
            for a, i in enumerate(ids):
                r_s[rows[a], :] = qkwu[a][:, :dv]
                p_s[rows[a], :] = (q[a] * eg_col[a] - qkwu[a][:, dv:]).astype(BF16)
                for j in range(per):
                    n_s[i * per + j] = nm[a][j][:, :dv]
                    m_s[i * per + j] = nm[a][j][:, dv:].astype(BF16)
        return write

    def scan(m, state):
        for a in range(ways):
            i = m * ways + a
            el = el_s[pl.ds(i, 1), :]
            for j in range(per):
                rows = pl.ds(pl.multiple_of(i * gs + j * cs, cs), cs)
                c = i * per + j
                sb = state[0].astype(BF16)
                o_s[rows, :] = jnp.dot(p_s[rows, :], sb, preferred_element_type=F32) + r_s[rows, :]
                state[0] = (state[0] * el[:, j * dv:(j + 1) * dv] + n_s[c]
                            - jnp.dot(m_s[c], sb, preferred_element_type=F32))
                yield

    def weave(*gens):
        gens, results = list(gens), [None] * len(gens)
        live = list(range(len(gens)))
        while live:
            for idx in list(live):
                try:
                    next(gens[idx])
                except StopIteration as stop:
                    results[idx] = stop.value
                    live.remove(idx)
        return results

    n_blocks = n_groups // ways
    weave(prep(0))[0]()

    def step(m, state):
        holder = [state]
        _, write = weave(scan(m - 1, holder), prep(m))
        write()
        return holder[0]

    state = lax.fori_loop(1, n_blocks, step, jnp.zeros((C_DK, dv), F32))

    def finish(first, last):
        for t in range(first, last):
            rows = pl.ds(t * gs, gs)
            gate = gate_ref[rows, :]
            y_ref[rows, :] = (_rms_rows(o_s[rows, :], og_ref[...])
                              * (gate * jax.nn.sigmoid(gate))).astype(y_ref.dtype)
            yield

    done = (n_blocks - 1) * ways
    weave(scan(n_blocks - 1, [state]), finish(0, done))
    weave(finish(done, n_groups))


def _mixer_c(x2, gain, w_in, conv_w, a_log, dt_bias, o_gain, w_out, batch, seq):
    d = x2.shape[1]
    hw = C_HEADS * C_DK
    tn = 256
    used = 4 * hw + 2 * C_HEADS
    kout = -(-used // tn) * tn
    w = jnp.concatenate([w_in, jnp.zeros((d, kout - used), F32)], axis=1).astype(BF16)
    (proj,) = _in_proj(x2, gain, w, jnp.ones((1, kout), F32), n_norm_cols=0, tn=tn, out_dtypes=(F32,))
    proj = proj.reshape(batch, seq, kout)

    ts = 512
    qkv = pl.pallas_call(
        _conv_c_kernel,
        out_shape=jax.ShapeDtypeStruct((batch, seq, 3 * hw), F32),
        grid=(batch, seq // ts, 3),
        in_specs=[
            pl.BlockSpec((None, ts, hw), lambda b, i, j: (b, i, j)),
            pl.BlockSpec((None, 8, hw), lambda b, i, j: (b, jnp.maximum(i * (ts // 8) - 1, 0), j)),
            pl.BlockSpec((C_CONV, hw), lambda b, i, j: (0, j)),
        ],
        out_specs=pl.BlockSpec((None, ts, hw), lambda b, i, j: (b, i, j)),
        compiler_params=_cparams(("parallel", "parallel", "arbitrary")),
        name="conv_c",
    )(proj, proj, conv_w)

    n_groups = seq // C_GROUP
    ba = proj[:, :, 4 * hw:4 * hw + 2 * C_HEADS].transpose(0, 2, 1).reshape(batch, 2 * C_HEADS, n_groups, C_GROUP)
    per_head = lambda v: jnp.broadcast_to(v.reshape(C_HEADS, 1, 1), (C_HEADS, 1, C_GROUP))
    head_cols = lambda off: pl.BlockSpec((None, seq, C_DK), lambda b, h: (b, 0, off + h))
    small = lambda off: pl.BlockSpec((None, None, n_groups, C_GROUP), lambda b, h: (b, off + h, 0, 0))
    scalar_row = pl.BlockSpec((None, 1, C_GROUP), lambda b, h: (h, 0, 0))
    y = pl.pallas_call(
        _delta_kernel,
        out_shape=jax.ShapeDtypeStruct((batch, seq, hw), BF16),
        grid=(batch, C_HEADS),
        in_specs=[head_cols(0), head_cols(C_HEADS), head_cols(2 * C_HEADS),
                  pl.BlockSpec((None, seq, C_DK), lambda b, h: (b, 0, 3 * C_HEADS + h)),
                  small(C_HEADS), small(0), scalar_row, scalar_row,
                  pl.BlockSpec((1, C_DK), lambda b, h: (0, 0))],
        out_specs=head_cols(0),
        scratch_shapes=[
            pltpu.VMEM((4, n_groups, C_GROUP), F32),
            pltpu.VMEM((seq, C_DK), BF16),
            pltpu.VMEM((seq, C_DK), F32),
            pltpu.VMEM((seq // C_CHUNK, C_DK, C_DK), BF16),
            pltpu.VMEM((seq // C_CHUNK, C_DK, C_DK), F32),
            pltpu.VMEM((n_groups, (C_GROUP // C_CHUNK) * C_DK), F32),
            pltpu.VMEM((seq, C_DK), F32),
        ],
        compiler_params=_cparams(("parallel", "parallel")),
        name="delta_c",
    )(qkv, qkv, qkv, proj, ba, ba, per_head(a_log), per_head(dt_bias), o_gain.reshape(1, C_DK))
    return _out_proj(y.reshape(batch * seq, hw), w_out.astype(BF16), x2)


def kernel(x, rel_bias, norm_mix, norm_mlp, mlp_w1, mlp_w2, a_w_in, a_q_gain, a_k_gain, a_w_out,
           b_w_in, b_q_gain, b_k_gain, b_w_out, c_w_in, c_conv_w, c_a_log, c_dt_bias, c_o_gain, c_w_out):
    batch, seq, d = x.shape
    depth = norm_mix.shape[0]
    a_cols = len(A_GROUPS) * A_HEADS
    bias_a, bias_b = rel_bias[:, :a_cols], rel_bias[:, a_cols:]
    x2 = x.reshape(batch * seq, d)
    for i in range(depth):
        kind, j = i % 3, i // 3
        if kind == 0:
            x2 = _mixer_a(x2, norm_mix[i], a_w_in[j], a_q_gain[j], a_k_gain[j], a_w_out[j], bias_a, batch, seq)
        elif kind == 1:
            x2 = _mixer_b(x2, norm_mix[i], b_w_in[j], b_q_gain[j], b_k_gain[j], b_w_out[j], bias_b, batch, seq)
        else:
            x2 = _mixer_c(x2, norm_mix[i], c_w_in[j], c_conv_w[j], c_a_log[j], c_dt_bias[j], c_o_gain[j],
                          c_w_out[j], batch, seq)
        x2 = _mlp(x2, norm_mlp[i], mlp_w1[i].astype(BF16), mlp_w2[i].astype(BF16))
    return x2.reshape(batch, seq, d)
```

```python
import functools
import math

import numpy as np
import jax
import jax.numpy as jnp
from jax import lax
from jax.experimental import pallas as pl
from jax.experimental.pallas import tpu as pltpu

F32 = jnp.float32
BF16 = jnp.bfloat16
I32 = jnp.int32

EPS = 1e-6
HEAD_DIM = 64
NEG = -1e30
INT_MIN = -2 ** 31
LOG2E = math.log2(math.e)

V7X_VMEM_BYTES = 64 * 1024 * 1024
VMEM_LIMIT = V7X_VMEM_BYTES - 8 * 1024 * 1024

NUM_BUCKETS = 32
MAX_DISTANCE = 2048

A_GROUPS = ((128, 1), (512, 4), (2048, 16))
A_HEADS = 8
A_BLOCK = 128
A_GW = A_HEADS * HEAD_DIM
A_STEP = 2

B_HEADS = 16
B_KV_HEADS = 4
B_IDX_HEADS = 8
B_IDX_DIM = 64
B_TOPK = 256
B_QT = 128
B_KC = 256
B_NDELTA = 14

C_HEADS = 8
C_DK = 128
C_CONV = 4
C_CHUNK = 64
C_GROUP = 4 * C_CHUNK


def _cparams(sem):
    return pltpu.CompilerParams(dimension_semantics=sem, vmem_limit_bytes=VMEM_LIMIT)


def _t5_bucket(dist):
    max_exact = NUM_BUCKETS // 2
    d = jnp.maximum(dist, 1).astype(F32)
    log_part = jnp.log(d / max_exact) / math.log(MAX_DISTANCE / max_exact) * (NUM_BUCKETS - max_exact)
    large = jnp.minimum(max_exact + log_part.astype(I32), NUM_BUCKETS - 1)
    return jnp.where(dist < max_exact, dist, large)


def _rms_rows(x, gain_row):
    ms = jnp.mean(x * x, axis=-1, keepdims=True)
    return x * lax.rsqrt(ms + EPS) * gain_row


def _in_proj_kernel(x_ref, g_ref, w_ref, hg_ref, bd_ref, *outs, n_norm, tn):
    hn = _rms_rows(x_ref[...], g_ref[...]).astype(BF16)
    for j in range(w_ref.shape[1] // tn):
        cols = slice(j * tn, (j + 1) * tn)
        y = jnp.dot(hn, w_ref[:, cols], preferred_element_type=F32)
        if j < n_norm:
            ms = jnp.dot((y * y).astype(BF16), bd_ref[...], preferred_element_type=F32) * (1.0 / HEAD_DIM)
            y = y * lax.rsqrt(ms + EPS) * hg_ref[:, cols]
        for o in outs:
            o[:, cols] = y.astype(o.dtype)


def _in_proj(x2, gain, w, head_gain, *, n_norm_cols, tn, out_dtypes, tm=512):
    n, d = x2.shape
    kout = w.shape[1]
    bd = np.kron(np.eye(tn // HEAD_DIM), np.ones((HEAD_DIM, HEAD_DIM))).astype(np.float32)
    const = lambda shape: pl.BlockSpec(shape, lambda i: (0, 0), pipeline_mode=pl.Buffered(1))
    return pl.pallas_call(
        functools.partial(_in_proj_kernel, n_norm=n_norm_cols // tn, tn=tn),
        out_shape=[jax.ShapeDtypeStruct((n, kout), dt) for dt in out_dtypes],
        grid=(n // tm,),
        in_specs=[pl.BlockSpec((tm, d), lambda i: (i, 0)), const((1, d)), const((d, kout)),
                  const((1, kout)), const((tn, tn))],
        out_specs=[pl.BlockSpec((tm, kout), lambda i: (i, 0)) for _ in out_dtypes],
        compiler_params=_cparams(("parallel",)),
        name="in_proj",
    )(x2, gain.reshape(1, d), w, head_gain, jnp.asarray(bd, BF16))


def _out_proj_kernel(y_ref, w_ref, x_ref, o_ref):
    o_ref[...] = x_ref[...] + jnp.dot(y_ref[...].astype(BF16), w_ref[...], preferred_element_type=F32)


def _out_proj(y2, w, x2, tm=512):
    n, kin = y2.shape
    d = w.shape[1]
    return pl.pallas_call(
        _out_proj_kernel,
        out_shape=jax.ShapeDtypeStruct((n, d), F32),
        grid=(n // tm,),
        in_specs=[
            pl.BlockSpec((tm, kin), lambda i: (i, 0)),
            pl.BlockSpec((kin, d), lambda i: (0, 0)),
            pl.BlockSpec((tm, d), lambda i: (i, 0)),
        ],
        out_specs=pl.BlockSpec((tm, d), lambda i: (i, 0)),
        compiler_params=_cparams(("parallel",)),
        name="out_proj",
    )(y2, w, x2)


def _out_proj_a_kernel(o0, o1, o2, l0, l1, l2, pt4_ref, pt16_ref, w_ref, x_ref, out_ref):
    tm = x_ref.shape[0]

    def token_order(ref, pt_ref):
        v = ref[...].reshape(tm, A_GW)
        if pt_ref is None:
            return v
        hi, lo = _split_bf16(v)
        return (jnp.dot(pt_ref[...], hi, preferred_element_type=F32)
                + jnp.dot(pt_ref[...], lo, preferred_element_type=F32))

    pts = (None, pt4_ref, pt16_ref)
    a, b, c = (token_order(r, pt) for r, pt in zip((l0, l1, l2), pts))
    m = jnp.maximum(jnp.maximum(a, b), c)
    ea, eb, ec = jnp.exp2(a - m), jnp.exp2(b - m), jnp.exp2(c - m)
    oa, ob, oc = (token_order(r, pt) for r, pt in zip((o0, o1, o2), pts))
    y = (ea * oa + eb * ob + ec * oc) / (ea + eb + ec)
    out_ref[...] = x_ref[...] + jnp.dot(y.astype(BF16), w_ref[...], preferred_element_type=F32)


def _mlp_kernel(x_ref, g_ref, w1_ref, w2_ref, o_ref, *, tf):
    x = x_ref[...]
    hn = _rms_rows(x, g_ref[...]).astype(BF16)
    acc = x
    for f in range(w1_ref.shape[1] // tf):
        cols = slice(f * tf, (f + 1) * tf)
        h = jnp.maximum(jnp.dot(hn, w1_ref[:, cols], preferred_element_type=F32), 0.0)
        acc = acc + jnp.dot((h * h).astype(BF16), w2_ref[cols, :], preferred_element_type=F32)
    o_ref[...] = acc


def _mlp(x2, gain, w1, w2, tm=512, tf=1024):
    n, d = x2.shape
    dff = w1.shape[1]
    const = lambda shape: pl.BlockSpec(shape, lambda i: (0, 0), pipeline_mode=pl.Buffered(1))
    return pl.pallas_call(
        functools.partial(_mlp_kernel, tf=tf),
        out_shape=jax.ShapeDtypeStruct((n, d), F32),
        grid=(n // tm,),
        in_specs=[pl.BlockSpec((tm, d), lambda i: (i, 0)), const((1, d)), const((d, dff)), const((dff, d))],
        out_specs=pl.BlockSpec((tm, d), lambda i: (i, 0)),
        compiler_params=_cparams(("parallel",)),
        name="mlp",
    )(x2, gain.reshape(1, d), w1, w2)


def _residue_major(tm, dilation):
    p = np.zeros((tm, tm), np.float32)
    j, r = np.meshgrid(np.arange(tm // dilation), np.arange(dilation), indexing="ij")
    p[(r * (tm // dilation) + j).ravel(), (j * dilation + r).ravel()] = 1.0
    return p


def _in_proj_a_kernel(x_ref, g_ref, w_ref, hg_ref, bd_ref, *rest):
    n_groups = len(A_GROUPS)
    perms, outs = rest[:n_groups - 1], rest[n_groups - 1:]
    tm = x_ref.shape[0]
    hn = _rms_rows(x_ref[...], g_ref[...]).astype(BF16)
    for which in range(3):
        for g, (_, dilation) in enumerate(A_GROUPS):
            j = which * n_groups + g
            cols = slice(j * A_GW, (j + 1) * A_GW)
            y = jnp.dot(hn, w_ref[:, cols], preferred_element_type=F32)
            if which < 2:
                ms = jnp.dot((y * y).astype(BF16), bd_ref[...], preferred_element_type=F32) * (1.0 / HEAD_DIM)
                y = y * lax.rsqrt(ms + EPS) * hg_ref[:, cols]
            y = y.astype(BF16)
            if dilation > 1:
                y = jnp.dot(perms[g - 1][...], y, preferred_element_type=F32).astype(BF16)
            outs[g][:, :, which * A_GW:(which + 1) * A_GW] = y.reshape(dilation, tm // dilation, A_GW)


def _attn_a_kernel(q_ref, kp_ref, kc_ref, vp_ref, vc_ref, bias_ref, o_ref, lse_ref):
    n = pl.program_id(2)
    blk = A_BLOCK
    pair = 2 * HEAD_DIM
    lane = lax.broadcasted_iota(I32, (2 * blk, pair), 1)
    row = lax.broadcasted_iota(I32, (2 * blk, blk), 0)
    pen = jnp.where((row < blk) & (n == 0), NEG, 0.0)
    nt = (((1,), (1,)), ((), ()))
    tn = (((0,), (0,)), ((), ()))
    k_all = jnp.concatenate([kp_ref[...], kc_ref[...]], axis=0)
    v_all = jnp.concatenate([vp_ref[...], vc_ref[...]], axis=0)
    q_all = q_ref[...]
    chains = [(sb, hp, a) for sb in range(A_STEP) for hp in range(A_HEADS // 2) for a in range(2)]

    def window(x, sb, hp):
        return x[sb * blk:(sb + 2) * blk, hp * pair:(hp + 1) * pair]

    keep = [lane < HEAD_DIM, lane >= HEAD_DIM]
    ka = [jnp.where(keep[a], window(k_all, sb, hp), jnp.zeros((2 * blk, pair), BF16))
          for sb, hp, a in chains]
    s = [lax.dot_general(k_, q_all[sb * blk:(sb + 1) * blk, hp * pair:(hp + 1) * pair], nt,
                         preferred_element_type=F32) + bias_ref[2 * hp + a]
         for k_, (sb, hp, a) in zip(ka, chains)]
    s = [s_ + pen if sb == 0 else s_ for s_, (sb, _, _) in zip(s, chains)]
    m = [jnp.max(s_, axis=0, keepdims=True) for s_ in s]
    p = [jnp.exp2(s_ - m_) for s_, m_ in zip(s, m)]
    l = [jnp.sum(p_, axis=0, keepdims=True) for p_ in p]
    pv = [lax.dot_general(window(v_all, sb, hp), p_.astype(BF16), tn, preferred_element_type=F32)
          for p_, (sb, hp, _) in zip(p, chains)]
    for idx in range(0, len(chains), 2):
        sb, hp, _ = chains[idx]
        o_t = jnp.concatenate([(pv[idx + a] / l[idx + a])[a * HEAD_DIM:(a + 1) * HEAD_DIM] for a in range(2)],
                              axis=0)
        l_t = jnp.concatenate([jnp.broadcast_to(m[idx + a] + jnp.log2(l[idx + a]), (HEAD_DIM, blk))
                               for a in range(2)], axis=0)
        o_ref[sb * blk:(sb + 1) * blk, hp * pair:(hp + 1) * pair] = o_t.T
        lse_ref[sb * blk:(sb + 1) * blk, hp * pair:(hp + 1) * pair] = l_t.T


def _attn_a_group(arr, bias_t, dilation, batch, seq):
    sub = seq // dilation
    rows = A_STEP * A_BLOCK
    nb = sub // rows

    def spec(which, prev):
        if prev:
            return pl.BlockSpec((None, None, A_BLOCK, A_GW),
                                lambda b, r, n: (b, r, jnp.maximum(n * A_STEP - 1, 0), which))
        return pl.BlockSpec((None, None, rows, A_GW), lambda b, r, n: (b, r, n, which))

    out_spec = pl.BlockSpec((None, None, rows, A_GW), lambda b, r, n: (b, r, n, 0))
    out_sds = jax.ShapeDtypeStruct((batch, dilation, sub, A_GW), F32)
    return pl.pallas_call(
        _attn_a_kernel,
        out_shape=[out_sds, out_sds],
        grid=(batch, dilation, nb),
        in_specs=[spec(0, False), spec(1, True), spec(1, False), spec(2, True), spec(2, False),
                  pl.BlockSpec((A_HEADS, 2 * A_BLOCK, A_BLOCK), lambda b, r, n: (0, 0, 0))],
        out_specs=[out_spec, out_spec],
        compiler_params=_cparams(("parallel", "parallel", "arbitrary")),
        name=f"attn_a_d{dilation}",
    )(arr, arr, arr, arr, arr, bias_t)


def _bias_a(tab_g, dilation):
    step = np.arange(3 * A_BLOCK - 1) - (A_BLOCK - 1)
    vec = tab_g[_t5_bucket(jnp.asarray(np.maximum(step, 0) * dilation, I32))].astype(F32) * LOG2E
    vec = jnp.where(jnp.asarray((step >= 0) & (step <= A_BLOCK))[:, None], vec, NEG)
    return _toeplitz(vec.T, 2 * A_BLOCK, A_BLOCK)


def _mixer_a(x2, gain, w_in, q_gain, k_gain, w_out, bias_a, batch, seq, tm=512):
    n, d = x2.shape
    n_groups = len(A_GROUPS)
    reps = n_groups * A_HEADS
    hg = jnp.concatenate([jnp.tile(q_gain * (HEAD_DIM ** -0.5 * LOG2E), reps), jnp.tile(k_gain, reps),
                          jnp.ones((n_groups * A_GW,), F32)]).reshape(1, -1)
    kout = 3 * n_groups * A_GW
    bd = np.kron(np.eye(A_GW // HEAD_DIM), np.ones((HEAD_DIM, HEAD_DIM))).astype(np.float32)
    perms = [_residue_major(tm, dil) for _, dil in A_GROUPS[1:]]
    tiles = seq // tm
    const = lambda shape: pl.BlockSpec(shape, lambda i: (0, 0), pipeline_mode=pl.Buffered(1))
    grouped = lambda dil, width: pl.BlockSpec((None, dil, tm // dil, width),
                                              lambda i: (i // tiles, 0, i % tiles, 0))
    arrs = pl.pallas_call(
        _in_proj_a_kernel,
        out_shape=[jax.ShapeDtypeStruct((batch, dil, seq // dil, 3 * A_GW), BF16) for _, dil in A_GROUPS],
        grid=(n // tm,),
        in_specs=[pl.BlockSpec((tm, d), lambda i: (i, 0)), const((1, d)), const((d, kout)),
                  const((1, kout)), const((A_GW, A_GW))] + [const((tm, tm)) for _ in perms],
        out_specs=[grouped(dil, 3 * A_GW) for _, dil in A_GROUPS],
        compiler_params=_cparams(("parallel",)),
        name="in_proj_a",
    )(x2, gain.reshape(1, d), w_in.astype(BF16), hg, jnp.asarray(bd, BF16),
      *[jnp.asarray(p, BF16) for p in perms])

    os_, ls_ = [], []
    for g, (window, dilation) in enumerate(A_GROUPS):
        assert window // dilation == A_BLOCK and (seq // dilation) % (A_STEP * A_BLOCK) == 0 and dilation <= tm // 16
        tab_g = bias_a[:, g * A_HEADS:(g + 1) * A_HEADS]
        o, lse = _attn_a_group(arrs[g], _bias_a(tab_g, dilation), dilation, batch, seq)
        os_.append(o)
        ls_.append(lse)

    return pl.pallas_call(
        _out_proj_a_kernel,
        out_shape=jax.ShapeDtypeStruct((n, d), F32),
        grid=(n // tm,),
        in_specs=[grouped(dil, A_GW) for _, dil in A_GROUPS] * 2 + [const((tm, tm)) for _ in perms]
        + [const((A_GW, d)), pl.BlockSpec((tm, d), lambda i: (i, 0))],
        out_specs=pl.BlockSpec((tm, d), lambda i: (i, 0)),
        compiler_params=_cparams(("parallel",)),
        name="out_proj_a",
    )(*os_, *ls_, *[jnp.asarray(p.T, BF16) for p in perms], w_out.astype(BF16), x2)


def _float_key(s):
    i = lax.bitcast_convert_type(s, I32)
    k = jnp.where(i < 0, i ^ jnp.int32(0x7FFFFFFF), i)
    return jnp.where(s == 0.0, 0, k)


def _split_bf16(x):
    hi = x.astype(BF16)
    return hi, (x - hi.astype(F32)).astype(BF16)


def _attn_b_kernel(q_ref, k_ref, vt_ref, qi_ref, kw_ref, bias_ref, o_ref, key_ref, msk_ref):
    qb = pl.program_id(1)
    kc, qt = B_KC, B_QT
    nch = qb // 2 + 1
    t_q = qb * qt + lax.broadcasted_iota(I32, (kc, qt), 1)
    sub = lax.broadcasted_iota(I32, (kc, qt), 0)
    nt = (((1,), (1,)), ((), ()))

    qh, ql = _split_bf16(qi_ref[...])
    qi3 = jnp.concatenate(
        [jnp.concatenate([x[:, h * B_IDX_DIM:(h + 1) * B_IDX_DIM] for x in (qh, qh, ql)], axis=1)
         for h in range(B_IDX_HEADS)], axis=0)
    wt = kw_ref[pl.ds(pl.multiple_of(qb * qt, qt), qt), 0:qt].T
    wt = wt * (B_IDX_HEADS ** -0.5 * B_IDX_DIM ** -0.5)

    npair = (nch + 1) // 2

    def score_pair(c2, carry):
        offs = [pl.multiple_of((2 * c2 + a) * kc, kc) for a in range(2)]
        ks = [_split_bf16(kw_ref[pl.ds(off, kc), 0:B_IDX_DIM]) for off in offs]
        sc = [lax.dot_general(jnp.concatenate([kh, kl, kh], axis=1), qi3, nt, preferred_element_type=F32)
              for kh, kl in ks]
        for off, s in zip(offs, sc):
            acc = jnp.zeros((kc, qt), F32)
            for h in range(B_IDX_HEADS):
                acc = acc + wt[B_IDX_DIM + h:B_IDX_DIM + h + 1, :] * jnp.maximum(s[:, h * qt:(h + 1) * qt], 0.0)
            key_ref[pl.ds(off, kc), :] = jnp.where(off + sub <= t_q, _float_key(acc), INT_MIN)
        return carry

    lax.fori_loop(0, npair, score_pair, 0)

    def count(pred_fn):
        def body(c2, acc):
            for a in range(2):
                off = pl.multiple_of((2 * c2 + a) * kc, kc)
                hit = jnp.where(pred_fn(key_ref[pl.ds(off, kc), :], off + sub), 1, 0)
                acc = acc + jnp.sum(hit.reshape(kc // 8, 8, qt), axis=0)
            return acc
        acc = lax.fori_loop(0, npair, body, jnp.zeros((8, qt), I32))
        return jnp.sum(acc, axis=0, keepdims=True)

    def thr_bit(it, lo):
        cand = lo + jnp.left_shift(jnp.int32(1), 31 - it)
        cnt = count(lambda key, _: key >= cand)
        return jnp.where(cnt >= B_TOPK, cand, lo)

    thr = lax.fori_loop(0, 32, thr_bit, jnp.full((1, qt), INT_MIN, I32))
    need = B_TOPK - count(lambda key, _: key > thr)
    n_eq = count(lambda key, _: key == thr)

    def tie_search(_):
        def bit(it, j):
            cand = j + jnp.left_shift(jnp.int32(1), 12 - it)
            cnt = count(lambda key, idx: (key == thr) & (idx < cand))
            return jnp.where(cnt <= need, cand, j)
        return lax.fori_loop(0, 13, bit, jnp.zeros((1, qt), I32))

    any_tie = jnp.max(jnp.where(n_eq > need, 1, 0)) > 0
    j_max = lax.cond(any_tie, tie_search, lambda _: jnp.full((1, qt), 2 ** 13, I32), 0)

    def mask_chunk(c, carry):
        off = pl.multiple_of(c * kc, kc)
        key = key_ref[pl.ds(off, kc), :]
        idx = off + sub
        sel = ((key > thr) | ((key == thr) & (idx < j_max))) & (idx <= t_q)
        msk_ref[pl.ds(off, kc), :] = jnp.where(sel, 0.0, NEG)
        return carry

    lax.fori_loop(0, nch, mask_chunk, 0)

    grp = B_HEADS // B_KV_HEADS
    q = q_ref[...]
    groups = range(B_KV_HEADS)
    qg = [jnp.concatenate([q[:, (g * grp + j) * HEAD_DIM:(g * grp + j + 1) * HEAD_DIM]
                           for j in range(grp)], axis=0) for g in groups]
    ksl = [slice(g * HEAD_DIM, (g + 1) * HEAD_DIM) for g in groups]

    def flash(c, carry):
        m, l, acc = carry
        off = pl.multiple_of(c * kc, kc)
        d_lo = jnp.clip(qb - 2 * c, 0, B_NDELTA - 1)
        d_hi = jnp.clip(qb - 2 * c - 1, 0, B_NDELTA - 1)
        mk = msk_ref[pl.ds(off, kc), :]
        mk = jnp.concatenate([mk] * grp, axis=1)
        s = [lax.dot_general(k_ref[pl.ds(off, kc), ksl[g]], qg[g], nt, preferred_element_type=F32)
             for g in groups]
        s = [s[g] + jnp.concatenate([bias_ref[d_lo, g], bias_ref[d_hi, g]], axis=0) + mk for g in groups]
        m_new = [jnp.maximum(m[g], jnp.max(s[g], axis=0, keepdims=True)) for g in groups]
        p = [jnp.exp2(s[g] - m_new[g]) for g in groups]
        pv = [jnp.dot(vt_ref[ksl[g], pl.ds(off, kc)], p[g].astype(BF16), preferred_element_type=F32)
              for g in groups]
        alpha = [jnp.exp2(m[g] - m_new[g]) for g in groups]
        l = [alpha[g] * l[g] + jnp.sum(p[g], axis=0, keepdims=True) for g in groups]
        acc = [alpha[g] * acc[g] + pv[g] for g in groups]
        return m_new, l, acc

    m0 = [jnp.full((1, grp * qt), NEG, F32) for _ in groups]
    l0 = [jnp.zeros((1, grp * qt), F32) for _ in groups]
    a0 = [jnp.zeros((HEAD_DIM, grp * qt), F32) for _ in groups]
    _, l, acc = lax.fori_loop(0, nch, flash, (m0, l0, a0))
    for g in groups:
        o = acc[g] / l[g]
        for j in range(grp):
            h = g * grp + j
            o_ref[:, h * HEAD_DIM:(h + 1) * HEAD_DIM] = o[:, j * qt:(j + 1) * qt].T.astype(o_ref.dtype)


def _toeplitz(vec, n_rows, n_cols):
    span = n_rows + n_cols - 1
    assert vec.shape[-1] == span
    lead = vec.shape[:-1]
    padded = jnp.concatenate([vec, jnp.zeros(lead + (1,), vec.dtype)], axis=-1)
    flat = jnp.tile(padded, n_rows)[..., :n_rows * span]
    return flat.reshape(lead + (n_rows, span))[..., n_rows - 1:]


def _bias_b(tab):
    grp = B_HEADS // B_KV_HEADS
    span = 2 * B_QT - 1
    dist = np.arange(B_NDELTA)[:, None] * B_QT + np.arange(span)[None, :] - (B_QT - 1)
    assert (B_NDELTA - 1) * B_QT - (B_QT - 1) >= 16 * 128 ** (15.0 / 16.0) + 1
    vec = tab[_t5_bucket(jnp.asarray(np.maximum(dist, 0), I32))].astype(F32) * LOG2E
    t = _toeplitz(vec.transpose(0, 2, 1), B_QT, B_QT)
    t = t.reshape(B_NDELTA, B_KV_HEADS, grp, B_QT, B_QT).transpose(0, 1, 3, 2, 4)
    return t.reshape(B_NDELTA, B_KV_HEADS, B_QT, grp * B_QT)


def _mixer_b(x2, gain, w_in, q_gain, k_gain, w_out, bias_b, batch, seq):
    d = x2.shape[1]
    nq, nkv = B_HEADS * HEAD_DIM, B_KV_HEADS * HEAD_DIM
    nidx = B_IDX_HEADS * B_IDX_DIM
    tn = 256
    used = 2 * nkv + nq + nidx + B_IDX_DIM + B_IDX_HEADS
    kout = -(-used // tn) * tn
    w = jnp.concatenate([w_in, jnp.zeros((d, kout - used), F32)], axis=1).astype(BF16)
    hg = jnp.concatenate([jnp.tile(q_gain * (HEAD_DIM ** -0.5 * LOG2E), B_HEADS), jnp.tile(k_gain, B_KV_HEADS),
                          jnp.ones((kout - nq - nkv,), F32)]).reshape(1, -1)
    p32, p16 = _in_proj(x2, gain, w, hg, n_norm_cols=nq + nkv, tn=tn, out_dtypes=(F32, BF16))
    p32 = p32.reshape(batch, seq, kout)
    p16 = p16.reshape(batch, seq, kout)
    assert seq % (2 * B_QT) == 0 and seq <= 2 ** 12
    kw_blk = (nq + 2 * nkv + nidx) // tn
    vt = p16[:, :, nq + nkv:nq + 2 * nkv].transpose(0, 2, 1)
    y = pl.pallas_call(
        _attn_b_kernel,
        out_shape=jax.ShapeDtypeStruct((batch, seq, nq), BF16),
        grid=(batch, seq // B_QT),
        in_specs=[
            pl.BlockSpec((None, B_QT, nq), lambda b, i: (b, i, 0)),
            pl.BlockSpec((None, seq, nkv), lambda b, i: (b, 0, nq // nkv)),
            pl.BlockSpec((None, nkv, seq), lambda b, i: (b, 0, 0)),
            pl.BlockSpec((None, B_QT, nidx), lambda b, i: (b, i, (nq + 2 * nkv) // nidx)),
            pl.BlockSpec((None, seq, tn), lambda b, i: (b, 0, kw_blk)),
            pl.BlockSpec((B_NDELTA, B_KV_HEADS, B_QT, (B_HEADS // B_KV_HEADS) * B_QT),
                         lambda b, i: (0, 0, 0, 0), pipeline_mode=pl.Buffered(1)),
        ],
        out_specs=pl.BlockSpec((None, B_QT, nq), lambda b, i: (b, i, 0)),
        scratch_shapes=[pltpu.VMEM((seq, B_QT), I32), pltpu.VMEM((seq, B_QT), F32)],
        compiler_params=_cparams(("parallel", "arbitrary")),
        name="attn_b",
    )(p16, p16, vt, p32, p32, _bias_b(bias_b))
    return _out_proj(y.reshape(batch * seq, nq), w_out.astype(BF16), x2)


def _conv_c_kernel(x_ref, halo_ref, cw_ref, o_ref, xs_ref):
    i, j = pl.program_id(1), pl.program_id(2)
    ts = x_ref.shape[0]
    xs_ref[0:8, :] = jnp.where(i > 0, halo_ref[...], 0.0)
    xs_ref[8:, :] = x_ref[...]
    cw = cw_ref[...]
    conv = sum(cw[t:t + 1, :] * xs_ref[8 - (C_CONV - 1) + t:8 - (C_CONV - 1) + t + ts, :]
               for t in range(C_CONV))
    y = conv * jax.nn.sigmoid(conv)

    @pl.when(j == 2)
    def _():
        o_ref[...] = y

    @pl.when(j < 2)
    def _():
        scale = jnp.where(j == 0, C_DK ** -0.5, 1.0)
        for h in range(C_HEADS):
            sl = slice(h * C_DK, (h + 1) * C_DK)
            yh = y[:, sl]
            ss = jnp.sum(yh * yh, axis=-1, keepdims=True)
            o_ref[:, sl] = yh * (lax.rsqrt(ss + EPS) * scale)


def _delta_kernel(q_ref, k_ref, v_ref, gate_ref, a_ref, b_ref, alog_ref, dtb_ref, og_ref, y_ref,
                  rows_s, p_s, r_s, m_s, n_s, el_s, o_s):
    cs, gs = C_CHUNK, C_GROUP
    per = gs // cs
    seq, dv = v_ref.shape
    n_groups = seq // gs
    hi = lax.Precision.HIGHEST
    r = lax.broadcasted_iota(I32, (gs, gs), 0)
    c_ = lax.broadcasted_iota(I32, (gs, gs), 1)

    def same_block(size):
        sh = int(math.log2(size))
        return (r >> sh) == (c_ >> sh)

    chunk = same_block(cs)
    lower, strict, eye = chunk & (r >= c_), chunk & (r > c_), r == c_

    z = a_ref[...] + dtb_ref[...]
    softplus = jnp.maximum(z, 0.0) + jnp.log(1.0 + jnp.exp(-jnp.abs(z)))
    g_all = -jnp.exp(alog_ref[...]) * softplus
    gc_all = jnp.dot(g_all, jnp.where(chunk & (r <= c_), 1.0, 0.0), preferred_element_type=F32, precision=hi)
    gl_all = jnp.dot(g_all, jnp.where(chunk, 1.0, 0.0), preferred_element_type=F32, precision=hi)
    rows_s[0] = jax.nn.sigmoid(b_ref[...])
    rows_s[1] = gc_all
    rows_s[2] = jnp.exp(gc_all)
    rows_s[3] = jnp.exp(gl_all - gc_all)
    first = (lax.broadcasted_iota(I32, (gs, per * dv), 0)
             == (lax.broadcasted_iota(I32, (gs, per * dv), 1) // dv) * cs)
    el_s[...] = jnp.exp(jnp.dot(gl_all, jnp.where(first, 1.0, 0.0), preferred_element_type=F32, precision=hi))

    def mm(a, b):
        return jnp.dot(a.astype(BF16), b.astype(BF16), preferred_element_type=F32)

    def mm_nt(a, b):
        return lax.dot_general(a.astype(BF16), b.astype(BF16), (((1,), (1,)), ((), ())),
                               preferred_element_type=F32)

    ways = 4

    def prep(m):
        ids = [m * ways + a for a in range(ways)]
        rows = [pl.ds(pl.multiple_of(i * gs, gs), gs) for i in ids]
        rw = [jnp.concatenate([rows_s[j, pl.ds(i, 1), :] for j in range(4)] + [jnp.zeros((4, gs), F32)], axis=0)
              for i in ids]
        cl = [x.T for x in rw]
        beta_col, gc_col, eg_col, ekg_col = ([x[:, j:j + 1] for x in cl] for j in range(4))
        q, k, v = ([ref[rw_, :] for rw_ in rows] for ref in (q_ref, k_ref, v_ref))
        decay = [jnp.where(lower, jnp.exp(jnp.where(lower, gcc - x[1:2, :], 0.0)), 0.0)
                 for gcc, x in zip(gc_col, rw)]
        kb = [k_ * b_ for k_, b_ in zip(k, beta_col)]
        yield
        a_mat = [jnp.where(strict, mm_nt(kb_, k_) * d_, 0.0) for kb_, k_, d_ in zip(kb, k, decay)]
        yield
        a8 = [jnp.where(same_block(8), a_, 0.0) for a_ in a_mat]
        t = [jnp.where(eye, 1.0, 0.0) - a_ for a_ in a8]
        pw = [mm(a_, a_) for a_ in a8]
        yield
        t = [t_ + mm(t_, p_) for t_, p_ in zip(t, pw)]
        yield
        pw = [mm(p_, p_) for p_ in pw]
        yield
        t = [t_ + mm(t_, p_) for t_, p_ in zip(t, pw)]
        yield
        size = 8
        while size < cs:
            sel = same_block(2 * size) & jnp.logical_not(same_block(size))
            nt = [mm(jnp.where(sel, a_, 0.0), t_) for a_, t_ in zip(a_mat, t)]
            yield
            t = [t_ - mm(t_, n_) for t_, n_ in zip(t, nt)]
            yield
            size *= 2
        uw = [mm(t_, jnp.concatenate([v_ * b_, kb_ * e_], axis=1))
              for t_, v_, b_, kb_, e_ in zip(t, v, beta_col, kb, eg_col)]
        yield
        qk = [jnp.where(lower, mm_nt(q_, k_) * d_, 0.0) for q_, k_, d_ in zip(q, k, decay)]
        yield
        qkwu = [mm(qk_, uw_) for qk_, uw_ in zip(qk, uw)]
        kgt = [(k_ * e_).T for k_, e_ in zip(k, ekg_col)]
        yield
        nm = [[mm(kgt[a][:, j * cs:(j + 1) * cs], uw[a][j * cs:(j + 1) * cs, :]) for j in range(per)]
              for a in range(ways)]

        def write():
            for a, i in enumerate(ids):
                r_s[rows[a], :] = qkwu[a][:, :dv]
                p_s[rows[a], :] = (q[a] * eg_col[a] - qkwu[a][:, dv:]).astype(BF16)
                for j in range(per):
                    n_s[i * per + j] = nm[a][j][:, :dv]
                    m_s[i * per + j] = nm[a][j][:, dv:].astype(BF16)
        return write

    def scan(m, state):
        for a in range(ways):
            i = m * ways + a
            el = el_s[pl.ds(i, 1), :]
            for j in range(per):
                rows = pl.ds(pl.multiple_of(i * gs + j * cs, cs), cs)
                c = i * per + j
                sb = state[0].astype(BF16)
                o_s[rows, :] = jnp.dot(p_s[rows, :], sb, preferred_element_type=F32) + r_s[rows, :]
                state[0] = (state[0] * el[:, j * dv:(j + 1) * dv] + n_s[c]
                            - jnp.dot(m_s[c], sb, preferred_element_type=F32))
                yield

    def weave(*gens):
        gens, results = list(gens), [None] * len(gens)
        live = list(range(len(gens)))
        while live:
            for idx in list(live):
                try:
                    next(gens[idx])
                except StopIteration as stop:
                    results[idx] = stop.value
                    live.remove(idx)
        return results

    n_blocks = n_groups // ways
    weave(prep(0))[0]()

    def step(m, state):
        holder = [state]
        _, write = weave(scan(m - 1, holder), prep(m))
        write()
        return holder[0]

    state = lax.fori_loop(1, n_blocks, step, jnp.zeros((C_DK, dv), F32))

    def finish(first, last):
        for t in range(first, last):
            rows = pl.ds(t * gs, gs)
            gate = gate_ref[rows, :]
            y_ref[rows, :] = (_rms_rows(o_s[rows, :], og_ref[...])
                              * (gate * jax.nn.sigmoid(gate))).astype(y_ref.dtype)
            yield

    done = (n_blocks - 1) * ways
    weave(scan(n_blocks - 1, [state]), finish(0, done))
    weave(finish(done, n_groups))


def _mixer_c(x2, gain, w_in, conv_w, a_log, dt_bias, o_gain, w_out, batch, seq):
    d = x2.shape[1]
    hw = C_HEADS * C_DK
    tn = 256
    used = 4 * hw + 2 * C_HEADS
    kout = -(-used // tn) * tn
    w = jnp.concatenate([w_in, jnp.zeros((d, kout - used), F32)], axis=1).astype(BF16)
    (proj,) = _in_proj(x2, gain, w, jnp.ones((1, kout), F32), n_norm_cols=0, tn=tn, out_dtypes=(F32,))
    proj = proj.reshape(batch, seq, kout)

    ts = 512
    qkv = pl.pallas_call(
        _conv_c_kernel,
        out_shape=jax.ShapeDtypeStruct((batch, seq, 3 * hw), F32),
        grid=(batch, seq // ts, 3),
        in_specs=[
            pl.BlockSpec((None, ts, hw), lambda b, i, j: (b, i, j)),
            pl.BlockSpec((None, 8, hw), lambda b, i, j: (b, jnp.maximum(i * (ts // 8) - 1, 0), j)),
            pl.BlockSpec((C_CONV, hw), lambda b, i, j: (0, j)),
        ],
        out_specs=pl.BlockSpec((None, ts, hw), lambda b, i, j: (b, i, j)),
        scratch_shapes=[pltpu.VMEM((ts + 8, hw), F32)],
        compiler_params=_cparams(("parallel", "parallel", "arbitrary")),
        name="conv_c",
    )(proj, proj, conv_w)

    n_groups = seq // C_GROUP
    ba = proj[:, :, 4 * hw:4 * hw + 2 * C_HEADS].transpose(0, 2, 1).reshape(batch, 2 * C_HEADS, n_groups, C_GROUP)
    per_head = lambda v: jnp.broadcast_to(v.reshape(C_HEADS, 1, 1), (C_HEADS, 1, C_GROUP))
    head_cols = lambda off: pl.BlockSpec((None, seq, C_DK), lambda b, h: (b, 0, off + h))
    small = lambda off: pl.BlockSpec((None, None, n_groups, C_GROUP), lambda b, h: (b, off + h, 0, 0))
    scalar_row = pl.BlockSpec((None, 1, C_GROUP), lambda b, h: (h, 0, 0))
    y = pl.pallas_call(
        _delta_kernel,
        out_shape=jax.ShapeDtypeStruct((batch, seq, hw), BF16),
        grid=(batch, C_HEADS),
        in_specs=[head_cols(0), head_cols(C_HEADS), head_cols(2 * C_HEADS),
                  pl.BlockSpec((None, seq, C_DK), lambda b, h: (b, 0, 3 * C_HEADS + h)),
                  small(C_HEADS), small(0), scalar_row, scalar_row,
                  pl.BlockSpec((1, C_DK), lambda b, h: (0, 0))],
        out_specs=head_cols(0),
        scratch_shapes=[
            pltpu.VMEM((4, n_groups, C_GROUP), F32),
            pltpu.VMEM((seq, C_DK), BF16),
            pltpu.VMEM((seq, C_DK), F32),
            pltpu.VMEM((seq // C_CHUNK, C_DK, C_DK), BF16),
            pltpu.VMEM((seq // C_CHUNK, C_DK, C_DK), F32),
            pltpu.VMEM((n_groups, (C_GROUP // C_CHUNK) * C_DK), F32),
            pltpu.VMEM((seq, C_DK), F32),
        ],
        compiler_params=_cparams(("parallel", "parallel")),
        name="delta_c",
    )(qkv, qkv, qkv, proj, ba, ba, per_head(a_log), per_head(dt_bias), o_gain.reshape(1, C_DK))
    return _out_proj(y.reshape(batch * seq, hw), w_out.astype(BF16), x2)


def kernel(x, rel_bias, norm_mix, norm_mlp, mlp_w1, mlp_w2, a_w_in, a_q_gain, a_k_gain, a_w_out,
           b_w_in, b_q_gain, b_k_gain, b_w_out, c_w_in, c_conv_w, c_a_log, c_dt_bias, c_o_gain, c_w_out):
    batch, seq, d = x.shape
    depth = norm_mix.shape[0]
    a_cols = len(A_GROUPS) * A_HEADS
    bias_a, bias_b = rel_bias[:, :a_cols], rel_bias[:, a_cols:]
    x2 = x.reshape(batch * seq, d)
    for i in range(depth):
        kind, j = i % 3, i // 3
        if kind == 0:
            x2 = _mixer_a(x2, norm_mix[i], a_w_in[j], a_q_gain[j], a_k_gain[j], a_w_out[j], bias_a, batch, seq)
        elif kind == 1:
            x2 = _mixer_b(x2, norm_mix[i], b_w_in[j], b_q_gain[j], b_k_gain[j], b_w_out[j], bias_b, batch, seq)
        else:
            x2 = _mixer_c(x2, norm_mix[i], c_w_in[j], c_conv_w[j], c_a_log[j], c_dt_bias[j], c_o_gain[j],
                          c_w_out[j], batch, seq)
        x2 = _mlp(x2, norm_mlp[i], mlp_w1[i].astype(BF16), mlp_w2[i].astype(BF16))
    return x2.reshape(batch, seq, d)
```

```python
import functools
import math

import numpy as np
import jax
import jax.numpy as jnp
from jax import lax
from jax.experimental import pallas as pl
from jax.experimental.pallas import tpu as pltpu

F32 = jnp.float32
BF16 = jnp.bfloat16
I32 = jnp.int32

EPS = 1e-6
HEAD_DIM = 64
NEG = -1e30
INT_MIN = -2 ** 31
LOG2E = math.log2(math.e)

V7X_VMEM_BYTES = 64 * 1024 * 1024
VMEM_LIMIT = V7X_VMEM_BYTES - 8 * 1024 * 1024

NUM_BUCKETS = 32
MAX_DISTANCE = 2048

A_GROUPS = ((128, 1), (512, 4), (2048, 16))
A_HEADS = 8
A_BLOCK = 128
A_GW = A_HEADS * HEAD_DIM
A_STEP = 2

B_HEADS = 16
B_KV_HEADS = 4
B_IDX_HEADS = 8
B_IDX_DIM = 64
B_TOPK = 256
B_QT = 128
B_KC = 256
B_NDELTA = 14
B_SAFE_SPREAD = 100.0

C_HEADS = 8
C_DK = 128
C_CONV = 4
C_CHUNK = 64
C_GROUP = 4 * C_CHUNK


def _cparams(sem):
    return pltpu.CompilerParams(dimension_semantics=sem, vmem_limit_bytes=VMEM_LIMIT)


def _t5_bucket(dist):
    max_exact = NUM_BUCKETS // 2
    d = jnp.maximum(dist, 1).astype(F32)
    log_part = jnp.log(d / max_exact) / math.log(MAX_DISTANCE / max_exact) * (NUM_BUCKETS - max_exact)
    large = jnp.minimum(max_exact + log_part.astype(I32), NUM_BUCKETS - 1)
    return jnp.where(dist < max_exact, dist, large)


def _rms_rows(x, gain_row):
    ms = jnp.mean(x * x, axis=-1, keepdims=True)
    return x * lax.rsqrt(ms + EPS) * gain_row


def _in_proj_kernel(x_ref, g_ref, w_ref, hg_ref, bd_ref, *outs, n_norm, tn):
    hn = _rms_rows(x_ref[...], g_ref[...]).astype(BF16)
    for j in range(w_ref.shape[1] // tn):
        cols = slice(j * tn, (j + 1) * tn)
        y = jnp.dot(hn, w_ref[:, cols], preferred_element_type=F32)
        if j < n_norm:
            ms = jnp.dot((y * y).astype(BF16), bd_ref[...], preferred_element_type=F32) * (1.0 / HEAD_DIM)
            y = y * lax.rsqrt(ms + EPS) * hg_ref[:, cols]
        for o in outs:
            o[:, cols] = y.astype(o.dtype)


def _in_proj(x2, gain, w, head_gain, *, n_norm_cols, tn, out_dtypes, tm=512):
    n, d = x2.shape
    kout = w.shape[1]
    bd = np.kron(np.eye(tn // HEAD_DIM), np.ones((HEAD_DIM, HEAD_DIM))).astype(np.float32)
    const = lambda shape: pl.BlockSpec(shape, lambda i: (0, 0), pipeline_mode=pl.Buffered(1))
    return pl.pallas_call(
        functools.partial(_in_proj_kernel, n_norm=n_norm_cols // tn, tn=tn),
        out_shape=[jax.ShapeDtypeStruct((n, kout), dt) for dt in out_dtypes],
        grid=(n // tm,),
        in_specs=[pl.BlockSpec((tm, d), lambda i: (i, 0)), const((1, d)), const((d, kout)),
                  const((1, kout)), const((tn, tn))],
        out_specs=[pl.BlockSpec((tm, kout), lambda i: (i, 0)) for _ in out_dtypes],
        compiler_params=_cparams(("parallel",)),
        name="in_proj",
    )(x2, gain.reshape(1, d), w, head_gain, jnp.asarray(bd, BF16))


def _out_proj_kernel(y_ref, w_ref, x_ref, o_ref):
    o_ref[...] = x_ref[...] + jnp.dot(y_ref[...].astype(BF16), w_ref[...], preferred_element_type=F32)


def _out_proj(y2, w, x2, tm=512):
    n, kin = y2.shape
    d = w.shape[1]
    return pl.pallas_call(
        _out_proj_kernel,
        out_shape=jax.ShapeDtypeStruct((n, d), F32),
        grid=(n // tm,),
        in_specs=[
            pl.BlockSpec((tm, kin), lambda i: (i, 0)),
            pl.BlockSpec((kin, d), lambda i: (0, 0)),
            pl.BlockSpec((tm, d), lambda i: (i, 0)),
        ],
        out_specs=pl.BlockSpec((tm, d), lambda i: (i, 0)),
        compiler_params=_cparams(("parallel",)),
        name="out_proj",
    )(y2, w, x2)


def _out_proj_a_kernel(o0, o1, o2, l0, l1, l2, pt4_ref, pt16_ref, w_ref, x_ref, out_ref):
    tm = x_ref.shape[0]

    def token_order(ref, pt_ref):
        v = ref[...].reshape(tm, A_GW)
        if pt_ref is None:
            return v
        hi, lo = _split_bf16(v)
        return (jnp.dot(pt_ref[...], hi, preferred_element_type=F32)
                + jnp.dot(pt_ref[...], lo, preferred_element_type=F32))

    pts = (None, pt4_ref, pt16_ref)
    a, b, c = (token_order(r, pt) for r, pt in zip((l0, l1, l2), pts))
    m = jnp.maximum(jnp.maximum(a, b), c)
    ea, eb, ec = jnp.exp2(a - m), jnp.exp2(b - m), jnp.exp2(c - m)
    oa, ob, oc = (token_order(r, pt) for r, pt in zip((o0, o1, o2), pts))
    y = (ea * oa + eb * ob + ec * oc) / (ea + eb + ec)
    out_ref[...] = x_ref[...] + jnp.dot(y.astype(BF16), w_ref[...], preferred_element_type=F32)


def _mlp_kernel(x_ref, g_ref, w1_ref, w2_ref, o_ref, *, tf):
    x = x_ref[...]
    hn = _rms_rows(x, g_ref[...]).astype(BF16)
    acc = x
    for f in range(w1_ref.shape[1] // tf):
        cols = slice(f * tf, (f + 1) * tf)
        h = jnp.maximum(jnp.dot(hn, w1_ref[:, cols], preferred_element_type=F32), 0.0)
        acc = acc + jnp.dot((h * h).astype(BF16), w2_ref[cols, :], preferred_element_type=F32)
    o_ref[...] = acc


def _mlp(x2, gain, w1, w2, tm=512, tf=1024):
    n, d = x2.shape
    dff = w1.shape[1]
    const = lambda shape: pl.BlockSpec(shape, lambda i: (0, 0), pipeline_mode=pl.Buffered(1))
    return pl.pallas_call(
        functools.partial(_mlp_kernel, tf=tf),
        out_shape=jax.ShapeDtypeStruct((n, d), F32),
        grid=(n // tm,),
        in_specs=[pl.BlockSpec((tm, d), lambda i: (i, 0)), const((1, d)), const((d, dff)), const((dff, d))],
        out_specs=pl.BlockSpec((tm, d), lambda i: (i, 0)),
        compiler_params=_cparams(("parallel",)),
        name="mlp",
    )(x2, gain.reshape(1, d), w1, w2)


def _residue_major(tm, dilation):
    p = np.zeros((tm, tm), np.float32)
    j, r = np.meshgrid(np.arange(tm // dilation), np.arange(dilation), indexing="ij")
    p[(r * (tm // dilation) + j).ravel(), (j * dilation + r).ravel()] = 1.0
    return p


def _in_proj_a_kernel(x_ref, g_ref, w_ref, hg_ref, bd_ref, *rest):
    n_groups = len(A_GROUPS)
    perms, outs = rest[:n_groups - 1], rest[n_groups - 1:]
    tm = x_ref.shape[0]
    hn = _rms_rows(x_ref[...], g_ref[...]).astype(BF16)
    for which in range(3):
        for g, (_, dilation) in enumerate(A_GROUPS):
            j = which * n_groups + g
            cols = slice(j * A_GW, (j + 1) * A_GW)
            y = jnp.dot(hn, w_ref[:, cols], preferred_element_type=F32)
            if which < 2:
                ms = jnp.dot((y * y).astype(BF16), bd_ref[...], preferred_element_type=F32) * (1.0 / HEAD_DIM)
                y = y * lax.rsqrt(ms + EPS) * hg_ref[:, cols]
            y = y.astype(BF16)
            if dilation > 1:
                y = jnp.dot(perms[g - 1][...], y, preferred_element_type=F32).astype(BF16)
            outs[g][:, :, which * A_GW:(which + 1) * A_GW] = y.reshape(dilation, tm // dilation, A_GW)


def _attn_a_kernel(q_ref, kp_ref, kc_ref, vp_ref, vc_ref, bias_ref, o_ref, lse_ref):
    n = pl.program_id(2)
    blk = A_BLOCK
    pair = 2 * HEAD_DIM
    lane = lax.broadcasted_iota(I32, (2 * blk, pair), 1)
    row = lax.broadcasted_iota(I32, (2 * blk, blk), 0)
    pen = jnp.where((row < blk) & (n == 0), NEG, 0.0)
    nt = (((1,), (1,)), ((), ()))
    tn = (((0,), (0,)), ((), ()))
    k_all = jnp.concatenate([kp_ref[...], kc_ref[...]], axis=0)
    v_all = jnp.concatenate([vp_ref[...], vc_ref[...]], axis=0)
    q_all = q_ref[...]
    chains = [(sb, hp, a) for sb in range(A_STEP) for hp in range(A_HEADS // 2) for a in range(2)]

    def window(x, sb, hp):
        return x[sb * blk:(sb + 2) * blk, hp * pair:(hp + 1) * pair]

    keep = [lane < HEAD_DIM, lane >= HEAD_DIM]
    ka = [jnp.where(keep[a], window(k_all, sb, hp), jnp.zeros((2 * blk, pair), BF16))
          for sb, hp, a in chains]
    s = [lax.dot_general(k_, q_all[sb * blk:(sb + 1) * blk, hp * pair:(hp + 1) * pair], nt,
                         preferred_element_type=F32) + bias_ref[2 * hp + a]
         for k_, (sb, hp, a) in zip(ka, chains)]
    s = [s_ + pen if sb == 0 else s_ for s_, (sb, _, _) in zip(s, chains)]
    m = [jnp.max(s_, axis=0, keepdims=True) for s_ in s]
    p = [jnp.exp2(s_ - m_) for s_, m_ in zip(s, m)]
    l = [jnp.sum(p_, axis=0, keepdims=True) for p_ in p]
    pv = [lax.dot_general(window(v_all, sb, hp), p_.astype(BF16), tn, preferred_element_type=F32)
          for p_, (sb, hp, _) in zip(p, chains)]
    for idx in range(0, len(chains), 2):
        sb, hp, _ = chains[idx]
        o_t = jnp.concatenate([(pv[idx + a] / l[idx + a])[a * HEAD_DIM:(a + 1) * HEAD_DIM] for a in range(2)],
                              axis=0)
        l_t = jnp.concatenate([jnp.broadcast_to(m[idx + a] + jnp.log2(l[idx + a]), (HEAD_DIM, blk))
                               for a in range(2)], axis=0)
        o_ref[sb * blk:(sb + 1) * blk, hp * pair:(hp + 1) * pair] = o_t.T
        lse_ref[sb * blk:(sb + 1) * blk, hp * pair:(hp + 1) * pair] = l_t.T


def _attn_a_group(arr, bias_t, dilation, batch, seq):
    sub = seq // dilation
    rows = A_STEP * A_BLOCK
    nb = sub // rows

    def spec(which, prev):
        if prev:
            return pl.BlockSpec((None, None, A_BLOCK, A_GW),
                                lambda b, r, n: (b, r, jnp.maximum(n * A_STEP - 1, 0), which))
        return pl.BlockSpec((None, None, rows, A_GW), lambda b, r, n: (b, r, n, which))

    out_spec = pl.BlockSpec((None, None, rows, A_GW), lambda b, r, n: (b, r, n, 0))
    out_sds = jax.ShapeDtypeStruct((batch, dilation, sub, A_GW), F32)
    return pl.pallas_call(
        _attn_a_kernel,
        out_shape=[out_sds, out_sds],
        grid=(batch, dilation, nb),
        in_specs=[spec(0, False), spec(1, True), spec(1, False), spec(2, True), spec(2, False),
                  pl.BlockSpec((A_HEADS, 2 * A_BLOCK, A_BLOCK), lambda b, r, n: (0, 0, 0))],
        out_specs=[out_spec, out_spec],
        compiler_params=_cparams(("parallel", "parallel", "arbitrary")),
        name=f"attn_a_d{dilation}",
    )(arr, arr, arr, arr, arr, bias_t)


def _bias_a(tab_g, dilation):
    step = np.arange(3 * A_BLOCK - 1) - (A_BLOCK - 1)
    vec = tab_g[_t5_bucket(jnp.asarray(np.maximum(step, 0) * dilation, I32))].astype(F32) * LOG2E
    vec = jnp.where(jnp.asarray((step >= 0) & (step <= A_BLOCK))[:, None], vec, NEG)
    return _toeplitz(vec.T, 2 * A_BLOCK, A_BLOCK)


def _mixer_a(x2, gain, w_in, q_gain, k_gain, w_out, bias_a, batch, seq, tm=512):
    n, d = x2.shape
    n_groups = len(A_GROUPS)
    reps = n_groups * A_HEADS
    hg = jnp.concatenate([jnp.tile(q_gain * (HEAD_DIM ** -0.5 * LOG2E), reps), jnp.tile(k_gain, reps),
                          jnp.ones((n_groups * A_GW,), F32)]).reshape(1, -1)
    kout = 3 * n_groups * A_GW
    bd = np.kron(np.eye(A_GW // HEAD_DIM), np.ones((HEAD_DIM, HEAD_DIM))).astype(np.float32)
    perms = [_residue_major(tm, dil) for _, dil in A_GROUPS[1:]]
    tiles = seq // tm
    const = lambda shape: pl.BlockSpec(shape, lambda i: (0, 0), pipeline_mode=pl.Buffered(1))
    grouped = lambda dil, width: pl.BlockSpec((None, dil, tm // dil, width),
                                              lambda i: (i // tiles, 0, i % tiles, 0))
    arrs = pl.pallas_call(
        _in_proj_a_kernel,
        out_shape=[jax.ShapeDtypeStruct((batch, dil, seq // dil, 3 * A_GW), BF16) for _, dil in A_GROUPS],
        grid=(n // tm,),
        in_specs=[pl.BlockSpec((tm, d), lambda i: (i, 0)), const((1, d)), const((d, kout)),
                  const((1, kout)), const((A_GW, A_GW))] + [const((tm, tm)) for _ in perms],
        out_specs=[grouped(dil, 3 * A_GW) for _, dil in A_GROUPS],
        compiler_params=_cparams(("parallel",)),
        name="in_proj_a",
    )(x2, gain.reshape(1, d), w_in.astype(BF16), hg, jnp.asarray(bd, BF16),
      *[jnp.asarray(p, BF16) for p in perms])

    os_, ls_ = [], []
    for g, (window, dilation) in enumerate(A_GROUPS):
        assert window // dilation == A_BLOCK and (seq // dilation) % (A_STEP * A_BLOCK) == 0 and dilation <= tm // 16
        tab_g = bias_a[:, g * A_HEADS:(g + 1) * A_HEADS]
        o, lse = _attn_a_group(arrs[g], _bias_a(tab_g, dilation), dilation, batch, seq)
        os_.append(o)
        ls_.append(lse)

    return pl.pallas_call(
        _out_proj_a_kernel,
        out_shape=jax.ShapeDtypeStruct((n, d), F32),
        grid=(n // tm,),
        in_specs=[grouped(dil, A_GW) for _, dil in A_GROUPS] * 2 + [const((tm, tm)) for _ in perms]
        + [const((A_GW, d)), pl.BlockSpec((tm, d), lambda i: (i, 0))],
        out_specs=pl.BlockSpec((tm, d), lambda i: (i, 0)),
        compiler_params=_cparams(("parallel",)),
        name="out_proj_a",
    )(*os_, *ls_, *[jnp.asarray(p.T, BF16) for p in perms], w_out.astype(BF16), x2)


def _float_key(s):
    i = lax.bitcast_convert_type(s, I32)
    k = jnp.where(i < 0, i ^ jnp.int32(0x7FFFFFFF), i)
    return jnp.where(s == 0.0, 0, k)


def _split_bf16(x):
    hi = x.astype(BF16)
    return hi, (x - hi.astype(F32)).astype(BF16)


def _attn_b_kernel(q_ref, k_ref, vt_ref, qi_ref, kw_ref, bias_ref, o_ref, key_ref, msk_ref, *, bounded):
    qb = pl.program_id(1)
    kc, qt = B_KC, B_QT
    nch = qb // 2 + 1
    t_q = qb * qt + lax.broadcasted_iota(I32, (kc, qt), 1)
    sub = lax.broadcasted_iota(I32, (kc, qt), 0)
    nt = (((1,), (1,)), ((), ()))

    qh, ql = _split_bf16(qi_ref[...])
    qi3 = jnp.concatenate(
        [jnp.concatenate([x[:, h * B_IDX_DIM:(h + 1) * B_IDX_DIM] for x in (qh, qh, ql)], axis=1)
         for h in range(B_IDX_HEADS)], axis=0)
    wt = kw_ref[pl.ds(pl.multiple_of(qb * qt, qt), qt), 0:qt].T
    wt = wt * (B_IDX_HEADS ** -0.5 * B_IDX_DIM ** -0.5)

    npair = (nch + 1) // 2

    def score_pair(c2, carry):
        offs = [pl.multiple_of((2 * c2 + a) * kc, kc) for a in range(2)]
        ks = [_split_bf16(kw_ref[pl.ds(off, kc), 0:B_IDX_DIM]) for off in offs]
        sc = [lax.dot_general(jnp.concatenate([kh, kl, kh], axis=1), qi3, nt, preferred_element_type=F32)
              for kh, kl in ks]
        for off, s in zip(offs, sc):
            acc = jnp.zeros((kc, qt), F32)
            for h in range(B_IDX_HEADS):
                acc = acc + wt[B_IDX_DIM + h:B_IDX_DIM + h + 1, :] * jnp.maximum(s[:, h * qt:(h + 1) * qt], 0.0)
            key_ref[pl.ds(off, kc), :] = jnp.where(off + sub <= t_q, _float_key(acc), INT_MIN)
        return carry

    lax.fori_loop(0, npair, score_pair, 0)

    def count(pred_fn):
        def body(c2, acc):
            for a in range(2):
                off = pl.multiple_of((2 * c2 + a) * kc, kc)
                hit = jnp.where(pred_fn(key_ref[pl.ds(off, kc), :], off + sub), 1, 0)
                acc = acc + jnp.sum(hit.reshape(kc // 8, 8, qt), axis=0)
            return acc
        acc = lax.fori_loop(0, npair, body, jnp.zeros((8, qt), I32))
        return jnp.sum(acc, axis=0, keepdims=True)

    def thr_bit(it, lo):
        cand = lo + jnp.left_shift(jnp.int32(1), 31 - it)
        cnt = count(lambda key, _: key >= cand)
        return jnp.where(cnt >= B_TOPK, cand, lo)

    thr = lax.fori_loop(0, 32, thr_bit, jnp.full((1, qt), INT_MIN, I32))
    need = B_TOPK - count(lambda key, _: key > thr)
    n_eq = count(lambda key, _: key == thr)

    def tie_search(_):
        def bit(it, j):
            cand = j + jnp.left_shift(jnp.int32(1), 12 - it)
            cnt = count(lambda key, idx: (key == thr) & (idx < cand))
            return jnp.where(cnt <= need, cand, j)
        return lax.fori_loop(0, 13, bit, jnp.zeros((1, qt), I32))

    any_tie = jnp.max(jnp.where(n_eq > need, 1, 0)) > 0
    j_max = lax.cond(any_tie, tie_search, lambda _: jnp.full((1, qt), 2 ** 13, I32), 0)

    def mask_chunk(c, carry):
        off = pl.multiple_of(c * kc, kc)
        key = key_ref[pl.ds(off, kc), :]
        idx = off + sub
        sel = ((key > thr) | ((key == thr) & (idx < j_max))) & (idx <= t_q)
        msk_ref[pl.ds(off, kc), :] = jnp.where(sel, 0.0, NEG)
        return carry

    lax.fori_loop(0, nch, mask_chunk, 0)

    grp = B_HEADS // B_KV_HEADS
    q = q_ref[...]
    groups = range(B_KV_HEADS)
    qg = [jnp.concatenate([q[:, (g * grp + j) * HEAD_DIM:(g * grp + j + 1) * HEAD_DIM]
                           for j in range(grp)], axis=0) for g in groups]
    ksl = [slice(g * HEAD_DIM, (g + 1) * HEAD_DIM) for g in groups]

    def flash(c, carry):
        m, l, acc = carry
        off = pl.multiple_of(c * kc, kc)
        d_lo = jnp.clip(qb - 2 * c, 0, B_NDELTA - 1)
        d_hi = jnp.clip(qb - 2 * c - 1, 0, B_NDELTA - 1)
        mk = msk_ref[pl.ds(off, kc), :]
        mk = jnp.concatenate([mk] * grp, axis=1)
        s = [lax.dot_general(k_ref[pl.ds(off, kc), ksl[g]], qg[g], nt, preferred_element_type=F32)
             for g in groups]
        s = [s[g] + jnp.concatenate([bias_ref[d_lo, g], bias_ref[d_hi, g]], axis=0) + mk for g in groups]
        if bounded:
            p = [jnp.exp2(s[g]) for g in groups]
        else:
            m_new = [jnp.maximum(m[g], jnp.max(s[g], axis=0, keepdims=True)) for g in groups]
            p = [jnp.exp2(s[g] - m_new[g]) for g in groups]
        pv = [jnp.dot(vt_ref[ksl[g], pl.ds(off, kc)], p[g].astype(BF16), preferred_element_type=F32)
              for g in groups]
        if bounded:
            l = [l[g] + jnp.sum(p[g], axis=0, keepdims=True) for g in groups]
            acc = [acc[g] + pv[g] for g in groups]
            return m, l, acc
        alpha = [jnp.exp2(m[g] - m_new[g]) for g in groups]
        l = [alpha[g] * l[g] + jnp.sum(p[g], axis=0, keepdims=True) for g in groups]
        acc = [alpha[g] * acc[g] + pv[g] for g in groups]
        return m_new, l, acc

    m0 = [jnp.full((1, grp * qt), NEG, F32) for _ in groups]
    l0 = [jnp.zeros((1, grp * qt), F32) for _ in groups]
    a0 = [jnp.zeros((HEAD_DIM, grp * qt), F32) for _ in groups]
    _, l, acc = lax.fori_loop(0, nch, flash, (m0, l0, a0))
    for g in groups:
        o = acc[g] / l[g]
        for j in range(grp):
            h = g * grp + j
            o_ref[:, h * HEAD_DIM:(h + 1) * HEAD_DIM] = o[:, j * qt:(j + 1) * qt].T.astype(o_ref.dtype)


def _toeplitz(vec, n_rows, n_cols):
    span = n_rows + n_cols - 1
    assert vec.shape[-1] == span
    lead = vec.shape[:-1]
    padded = jnp.concatenate([vec, jnp.zeros(lead + (1,), vec.dtype)], axis=-1)
    flat = jnp.tile(padded, n_rows)[..., :n_rows * span]
    return flat.reshape(lead + (n_rows, span))[..., n_rows - 1:]


def _bias_b(tab):
    grp = B_HEADS // B_KV_HEADS
    span = 2 * B_QT - 1
    dist = np.arange(B_NDELTA)[:, None] * B_QT + np.arange(span)[None, :] - (B_QT - 1)
    assert (B_NDELTA - 1) * B_QT - (B_QT - 1) >= 16 * 128 ** (15.0 / 16.0) + 1
    vec = tab[_t5_bucket(jnp.asarray(np.maximum(dist, 0), I32))].astype(F32) * LOG2E
    t = _toeplitz(vec.transpose(0, 2, 1), B_QT, B_QT)
    t = t.reshape(B_NDELTA, B_KV_HEADS, grp, B_QT, B_QT).transpose(0, 1, 3, 2, 4)
    return t.reshape(B_NDELTA, B_KV_HEADS, B_QT, grp * B_QT)


def _mixer_b(x2, gain, w_in, q_gain, k_gain, w_out, bias_b, batch, seq):
    d = x2.shape[1]
    nq, nkv = B_HEADS * HEAD_DIM, B_KV_HEADS * HEAD_DIM
    nidx = B_IDX_HEADS * B_IDX_DIM
    tn = 256
    used = 2 * nkv + nq + nidx + B_IDX_DIM + B_IDX_HEADS
    kout = -(-used // tn) * tn
    w = jnp.concatenate([w_in, jnp.zeros((d, kout - used), F32)], axis=1).astype(BF16)
    hg = jnp.concatenate([jnp.tile(q_gain * (HEAD_DIM ** -0.5 * LOG2E), B_HEADS), jnp.tile(k_gain, B_KV_HEADS),
                          jnp.ones((kout - nq - nkv,), F32)]).reshape(1, -1)
    p32, p16 = _in_proj(x2, gain, w, hg, n_norm_cols=nq + nkv, tn=tn, out_dtypes=(F32, BF16))
    p32 = p32.reshape(batch, seq, kout)
    p16 = p16.reshape(batch, seq, kout)
    assert seq % (2 * B_QT) == 0 and seq <= 2 ** 12
    kw_blk = (nq + 2 * nkv + nidx) // tn
    vt = p16[:, :, nq + nkv:nq + 2 * nkv].transpose(0, 2, 1)
    def attend(bias_t, bounded):
        return pl.pallas_call(
            functools.partial(_attn_b_kernel, bounded=bounded),
            out_shape=jax.ShapeDtypeStruct((batch, seq, nq), BF16),
            grid=(batch, seq // B_QT),
            in_specs=[
                pl.BlockSpec((None, B_QT, nq), lambda b, i: (b, i, 0)),
                pl.BlockSpec((None, seq, nkv), lambda b, i: (b, 0, nq // nkv)),
                pl.BlockSpec((None, nkv, seq), lambda b, i: (b, 0, 0)),
                pl.BlockSpec((None, B_QT, nidx), lambda b, i: (b, i, (nq + 2 * nkv) // nidx)),
                pl.BlockSpec((None, seq, tn), lambda b, i: (b, 0, kw_blk)),
                pl.BlockSpec((B_NDELTA, B_KV_HEADS, B_QT, (B_HEADS // B_KV_HEADS) * B_QT),
                             lambda b, i: (0, 0, 0, 0), pipeline_mode=pl.Buffered(1)),
            ],
            out_specs=pl.BlockSpec((None, B_QT, nq), lambda b, i: (b, i, 0)),
            scratch_shapes=[pltpu.VMEM((seq, B_QT), I32), pltpu.VMEM((seq, B_QT), F32)],
            compiler_params=_cparams(("parallel", "arbitrary")),
            name="attn_b_bounded" if bounded else "attn_b",
        )(p16, p16, vt, p32, p32, bias_t)

    bias_t = _bias_b(bias_b)
    qk_bound = 1.02 * HEAD_DIM * (HEAD_DIM ** -0.5 * LOG2E) * jnp.max(jnp.abs(q_gain)) * jnp.max(jnp.abs(k_gain))
    bound = qk_bound + jnp.max(bias_b) * LOG2E
    spread = bound + qk_bound - jnp.min(bias_b) * LOG2E
    y = lax.cond(spread <= B_SAFE_SPREAD,
                 lambda: attend(bias_t - bound, True),
                 lambda: attend(bias_t, False))
    return _out_proj(y.reshape(batch * seq, nq), w_out.astype(BF16), x2)


def _conv_c_kernel(x_ref, halo_ref, cw_ref, o_ref, xs_ref):
    i, j = pl.program_id(1), pl.program_id(2)
    ts = x_ref.shape[0]
    xs_ref[0:8, :] = jnp.where(i > 0, halo_ref[...], 0.0)
    xs_ref[8:, :] = x_ref[...]
    cw = cw_ref[...]
    conv = sum(cw[t:t + 1, :] * xs_ref[8 - (C_CONV - 1) + t:8 - (C_CONV - 1) + t + ts, :]
               for t in range(C_CONV))
    y = conv * jax.nn.sigmoid(conv)

    @pl.when(j == 2)
    def _():
        o_ref[...] = y

    @pl.when(j < 2)
    def _():
        scale = jnp.where(j == 0, C_DK ** -0.5, 1.0)
        for h in range(C_HEADS):
            sl = slice(h * C_DK, (h + 1) * C_DK)
            yh = y[:, sl]
            ss = jnp.sum(yh * yh, axis=-1, keepdims=True)
            o_ref[:, sl] = yh * (lax.rsqrt(ss + EPS) * scale)


def _delta_kernel(q_ref, k_ref, v_ref, gate_ref, a_ref, b_ref, alog_ref, dtb_ref, og_ref, y_ref,
                  rows_s, p_s, r_s, m_s, n_s, el_s, o_s):
    cs, gs = C_CHUNK, C_GROUP
    per = gs // cs
    seq, dv = v_ref.shape
    n_groups = seq // gs
    hi = lax.Precision.HIGHEST
    r = lax.broadcasted_iota(I32, (gs, gs), 0)
    c_ = lax.broadcasted_iota(I32, (gs, gs), 1)

    def same_block(size):
        sh = int(math.log2(size))
        return (r >> sh) == (c_ >> sh)

    chunk = same_block(cs)
    lower, strict, eye = chunk & (r >= c_), chunk & (r > c_), r == c_

    z = a_ref[...] + dtb_ref[...]
    softplus = jnp.maximum(z, 0.0) + jnp.log(1.0 + jnp.exp(-jnp.abs(z)))
    g_all = -jnp.exp(alog_ref[...]) * softplus
    gc_all = jnp.dot(g_all, jnp.where(chunk & (r <= c_), 1.0, 0.0), preferred_element_type=F32, precision=hi)
    gl_all = jnp.dot(g_all, jnp.where(chunk, 1.0, 0.0), preferred_element_type=F32, precision=hi)
    rows_s[0] = jax.nn.sigmoid(b_ref[...])
    rows_s[1] = gc_all
    rows_s[2] = jnp.exp(gc_all)
    rows_s[3] = jnp.exp(gl_all - gc_all)
    first = (lax.broadcasted_iota(I32, (gs, per * dv), 0)
             == (lax.broadcasted_iota(I32, (gs, per * dv), 1) // dv) * cs)
    el_s[...] = jnp.exp(jnp.dot(gl_all, jnp.where(first, 1.0, 0.0), preferred_element_type=F32, precision=hi))

    def mm(a, b):
        return jnp.dot(a.astype(BF16), b.astype(BF16), preferred_element_type=F32)

    def mm_nt(a, b):
        return lax.dot_general(a.astype(BF16), b.astype(BF16), (((1,), (1,)), ((), ())),
                               preferred_element_type=F32)

    ways = 4

    def prep(m):
        ids = [m * ways + a for a in range(ways)]
        rows = [pl.ds(pl.multiple_of(i * gs, gs), gs) for i in ids]
        rw = [jnp.concatenate([rows_s[j, pl.ds(i, 1), :] for j in range(4)] + [jnp.zeros((4, gs), F32)], axis=0)
              for i in ids]
        cl = [x.T for x in rw]
        beta_col, gc_col, eg_col, ekg_col = ([x[:, j:j + 1] for x in cl] for j in range(4))
        q, k, v = ([ref[rw_, :] for rw_ in rows] for ref in (q_ref, k_ref, v_ref))
        decay = [jnp.where(lower, jnp.exp(jnp.where(lower, gcc - x[1:2, :], 0.0)), 0.0)
                 for gcc, x in zip(gc_col, rw)]
        kb = [k_ * b_ for k_, b_ in zip(k, beta_col)]
        yield
        a_mat = [jnp.where(strict, mm_nt(kb_, k_) * d_, 0.0) for kb_, k_, d_ in zip(kb, k, decay)]
        yield
        a8 = [jnp.where(same_block(8), a_, 0.0) for a_ in a_mat]
        t = [jnp.where(eye, 1.0, 0.0) - a_ for a_ in a8]
        pw = [mm(a_, a_) for a_ in a8]
        yield
        t = [t_ + mm(t_, p_) for t_, p_ in zip(t, pw)]
        yield
        pw = [mm(p_, p_) for p_ in pw]
        yield
        t = [t_ + mm(t_, p_) for t_, p_ in zip(t, pw)]
        yield
        size = 8
        while size < cs:
            sel = same_block(2 * size) & jnp.logical_not(same_block(size))
            nt = [mm(jnp.where(sel, a_, 0.0), t_) for a_, t_ in zip(a_mat, t)]
            yield
            t = [t_ - mm(t_, n_) for t_, n_ in zip(t, nt)]
            yield
            size *= 2
        uw = [mm(t_, jnp.concatenate([v_ * b_, kb_ * e_], axis=1))
              for t_, v_, b_, kb_, e_ in zip(t, v, beta_col, kb, eg_col)]
        yield
        qk = [jnp.where(lower, mm_nt(q_, k_) * d_, 0.0) for q_, k_, d_ in zip(q, k, decay)]
        yield
        qkwu = [mm(qk_, uw_) for qk_, uw_ in zip(qk, uw)]
        kgt = [(k_ * e_).T for k_, e_ in zip(k, ekg_col)]
        yield
        nm = [[mm(kgt[a][:, j * cs:(j + 1) * cs], uw[a][j * cs:(j + 1) * cs, :]) for j in range(per)]
              for a in range(ways)]

        def write():
            for a, i in enumerate(ids):
                r_s[rows[a], :] = qkwu[a][:, :dv]
                p_s[rows[a], :] = (q[a] * eg_col[a] - qkwu[a][:, dv:]).astype(BF16)
                for j in range(per):
                    n_s[i * per + j] = nm[a][j][:, :dv]
                    m_s[i * per + j] = nm[a][j][:, dv:].astype(BF16)
        return write

    def scan(m, state):
        for a in range(ways):
            i = m * ways + a
            el = el_s[pl.ds(i, 1), :]
            for j in range(per):
                rows = pl.ds(pl.multiple_of(i * gs + j * cs, cs), cs)
                c = i * per + j
                sb = state[0].astype(BF16)
                o_s[rows, :] = jnp.dot(p_s[rows, :], sb, preferred_element_type=F32) + r_s[rows, :]
                state[0] = (state[0] * el[:, j * dv:(j + 1) * dv] + n_s[c]
                            - jnp.dot(m_s[c], sb, preferred_element_type=F32))
                yield

    def weave(*gens):
        gens, results = list(gens), [None] * len(gens)
        live = list(range(len(gens)))
        while live:
            for idx in list(live):
                try:
                    next(gens[idx])
                except StopIteration as stop:
                    results[idx] = stop.value
                    live.remove(idx)
        return results

    n_blocks = n_groups // ways
    weave(prep(0))[0]()

    def step(m, state):
        holder = [state]
        _, write = weave(scan(m - 1, holder), prep(m))
        write()
        return holder[0]

    state = lax.fori_loop(1, n_blocks, step, jnp.zeros((C_DK, dv), F32))

    def finish(first, last):
        for t in range(first, last):
            rows = pl.ds(t * gs, gs)
            gate = gate_ref[rows, :]
            y_ref[rows, :] = (_rms_rows(o_s[rows, :], og_ref[...])
                              * (gate * jax.nn.sigmoid(gate))).astype(y_ref.dtype)
            yield

    done = (n_blocks - 1) * ways
    weave(scan(n_blocks - 1, [state]), finish(0, done))
    weave(finish(done, n_groups))


def _mixer_c(x2, gain, w_in, conv_w, a_log, dt_bias, o_gain, w_out, batch, seq):
    d = x2.shape[1]
    hw = C_HEADS * C_DK
    tn = 256
    used = 4 * hw + 2 * C_HEADS
    kout = -(-used // tn) * tn
    w = jnp.concatenate([w_in, jnp.zeros((d, kout - used), F32)], axis=1).astype(BF16)
    (proj,) = _in_proj(x2, gain, w, jnp.ones((1, kout), F32), n_norm_cols=0, tn=tn, out_dtypes=(F32,))
    proj = proj.reshape(batch, seq, kout)

    ts = 512
    qkv = pl.pallas_call(
        _conv_c_kernel,
        out_shape=jax.ShapeDtypeStruct((batch, seq, 3 * hw), F32),
        grid=(batch, seq // ts, 3),
        in_specs=[
            pl.BlockSpec((None, ts, hw), lambda b, i, j: (b, i, j)),
            pl.BlockSpec((None, 8, hw), lambda b, i, j: (b, jnp.maximum(i * (ts // 8) - 1, 0), j)),
            pl.BlockSpec((C_CONV, hw), lambda b, i, j: (0, j)),
        ],
        out_specs=pl.BlockSpec((None, ts, hw), lambda b, i, j: (b, i, j)),
        scratch_shapes=[pltpu.VMEM((ts + 8, hw), F32)],
        compiler_params=_cparams(("parallel", "parallel", "arbitrary")),
        name="conv_c",
    )(proj, proj, conv_w)

    n_groups = seq // C_GROUP
    ba = proj[:, :, 4 * hw:4 * hw + 2 * C_HEADS].transpose(0, 2, 1).reshape(batch, 2 * C_HEADS, n_groups, C_GROUP)
    per_head = lambda v: jnp.broadcast_to(v.reshape(C_HEADS, 1, 1), (C_HEADS, 1, C_GROUP))
    head_cols = lambda off: pl.BlockSpec((None, seq, C_DK), lambda b, h: (b, 0, off + h))
    small = lambda off: pl.BlockSpec((None, None, n_groups, C_GROUP), lambda b, h: (b, off + h, 0, 0))
    scalar_row = pl.BlockSpec((None, 1, C_GROUP), lambda b, h: (h, 0, 0))
    y = pl.pallas_call(
        _delta_kernel,
        out_shape=jax.ShapeDtypeStruct((batch, seq, hw), BF16),
        grid=(batch, C_HEADS),
        in_specs=[head_cols(0), head_cols(C_HEADS), head_cols(2 * C_HEADS),
                  pl.BlockSpec((None, seq, C_DK), lambda b, h: (b, 0, 3 * C_HEADS + h)),
                  small(C_HEADS), small(0), scalar_row, scalar_row,
                  pl.BlockSpec((1, C_DK), lambda b, h: (0, 0))],
        out_specs=head_cols(0),
        scratch_shapes=[
            pltpu.VMEM((4, n_groups, C_GROUP), F32),
            pltpu.VMEM((seq, C_DK), BF16),
            pltpu.VMEM((seq, C_DK), F32),
            pltpu.VMEM((seq // C_CHUNK, C_DK, C_DK), BF16),
            pltpu.VMEM((seq // C_CHUNK, C_DK, C_DK), F32),
            pltpu.VMEM((n_groups, (C_GROUP // C_CHUNK) * C_DK), F32),
            pltpu.VMEM((seq, C_DK), F32),
        ],
        compiler_params=_cparams(("parallel", "parallel")),
        name="delta_c",
    )(qkv, qkv, qkv, proj, ba, ba, per_head(a_log), per_head(dt_bias), o_gain.reshape(1, C_DK))
    return _out_proj(y.reshape(batch * seq, hw), w_out.astype(BF16), x2)


def kernel(x, rel_bias, norm_mix, norm_mlp, mlp_w1, mlp_w2, a_w_in, a_q_gain, a_k_gain, a_w_out,
           b_w_in, b_q_gain, b_k_gain, b_w_out, c_w_in, c_conv_w, c_a_log, c_dt_bias, c_o_gain, c_w_out):
    batch, seq, d = x.shape
    depth = norm_mix.shape[0]
    a_cols = len(A_GROUPS) * A_HEADS
    bias_a, bias_b = rel_bias[:, :a_cols], rel_bias[:, a_cols:]
    x2 = x.reshape(batch * seq, d)
    for i in range(depth):
        kind, j = i % 3, i // 3
        if kind == 0:
            x2 = _mixer_a(x2, norm_mix[i], a_w_in[j], a_q_gain[j], a_k_gain[j], a_w_out[j], bias_a, batch, seq)
        elif kind == 1:
            x2 = _mixer_b(x2, norm_mix[i], b_w_in[j], b_q_gain[j], b_k_gain[j], b_w_out[j], bias_b, batch, seq)
        else:
            x2 = _mixer_c(x2, norm_mix[i], c_w_in[j], c_conv_w[j], c_a_log[j], c_dt_bias[j], c_o_gain[j],
                          c_w_out[j], batch, seq)
        x2 = _mlp(x2, norm_mlp[i], mlp_w1[i].astype(BF16), mlp_w2[i].astype(BF16))
    return x2.reshape(batch, seq, d)
```

```python
import functools
import math

import numpy as np
import jax
import jax.numpy as jnp
from jax import lax
from jax.experimental import pallas as pl
from jax.experimental.pallas import tpu as pltpu

F32 = jnp.float32
BF16 = jnp.bfloat16
I32 = jnp.int32
I16 = jnp.int16

EPS = 1e-6
HEAD_DIM = 64
NEG = -1e30
INT_MIN = -2 ** 31
LOG2E = math.log2(math.e)

V7X_VMEM_BYTES = 64 * 1024 * 1024
VMEM_LIMIT = V7X_VMEM_BYTES - 8 * 1024 * 1024
V7X_MXU_DIM = 256

NUM_BUCKETS = 32
MAX_DISTANCE = 2048

A_GROUPS = ((128, 1), (512, 4), (2048, 16))
A_HEADS = 8
A_BLOCK = 128
A_GW = A_HEADS * HEAD_DIM
A_STEP = 2

B_HEADS = 16
B_KV_HEADS = 4
B_IDX_HEADS = 8
B_IDX_DIM = 64
B_TOPK = 256
B_QT = 128
B_KC = 256
B_NDELTA = 14
B_SAFE_SPREAD = 100.0

C_HEADS = 8
C_DK = 128
C_CONV = 4
C_CHUNK = 64
C_GROUP = 4 * C_CHUNK


def _cparams(sem):
    return pltpu.CompilerParams(dimension_semantics=sem, vmem_limit_bytes=VMEM_LIMIT)


def _t5_bucket(dist):
    max_exact = NUM_BUCKETS // 2
    d = jnp.maximum(dist, 1).astype(F32)
    log_part = jnp.log(d / max_exact) / math.log(MAX_DISTANCE / max_exact) * (NUM_BUCKETS - max_exact)
    large = jnp.minimum(max_exact + log_part.astype(I32), NUM_BUCKETS - 1)
    return jnp.where(dist < max_exact, dist, large)


def _rms_rows(x, gain_row):
    ms = jnp.mean(x * x, axis=-1, keepdims=True)
    return x * lax.rsqrt(ms + EPS) * gain_row


def _in_proj_kernel(x_ref, g_ref, w_ref, hg_ref, bd_ref, *outs, n_norm, tn):
    hn = _rms_rows(x_ref[...], g_ref[...]).astype(BF16)
    for j in range(w_ref.shape[1] // tn):
        cols = slice(j * tn, (j + 1) * tn)
        y = jnp.dot(hn, w_ref[:, cols], preferred_element_type=F32)
        if j < n_norm:
            ms = jnp.dot((y * y).astype(BF16), bd_ref[...], preferred_element_type=F32) * (1.0 / HEAD_DIM)
            y = y * lax.rsqrt(ms + EPS) * hg_ref[:, cols]
        for o in outs:
            o[:, cols] = y.astype(o.dtype)


def _in_proj(x2, gain, w, head_gain, *, n_norm_cols, tn, out_dtypes, tm=512):
    n, d = x2.shape
    kout = w.shape[1]
    bd = np.kron(np.eye(tn // HEAD_DIM), np.ones((HEAD_DIM, HEAD_DIM))).astype(np.float32)
    const = lambda shape: pl.BlockSpec(shape, lambda i: (0, 0), pipeline_mode=pl.Buffered(1))
    return pl.pallas_call(
        functools.partial(_in_proj_kernel, n_norm=n_norm_cols // tn, tn=tn),
        out_shape=[jax.ShapeDtypeStruct((n, kout), dt) for dt in out_dtypes],
        grid=(n // tm,),
        in_specs=[pl.BlockSpec((tm, d), lambda i: (i, 0)), const((1, d)), const((d, kout)),
                  const((1, kout)), const((tn, tn))],
        out_specs=[pl.BlockSpec((tm, kout), lambda i: (i, 0)) for _ in out_dtypes],
        compiler_params=_cparams(("parallel",)),
        name="in_proj",
    )(x2, gain.reshape(1, d), w, head_gain, jnp.asarray(bd, BF16))


def _out_proj_kernel(y_ref, w_ref, x_ref, o_ref):
    o_ref[...] = x_ref[...] + jnp.dot(y_ref[...].astype(BF16), w_ref[...], preferred_element_type=F32)


def _out_proj(y2, w, x2, tm=512):
    n, kin = y2.shape
    d = w.shape[1]
    return pl.pallas_call(
        _out_proj_kernel,
        out_shape=jax.ShapeDtypeStruct((n, d), F32),
        grid=(n // tm,),
        in_specs=[
            pl.BlockSpec((tm, kin), lambda i: (i, 0)),
            pl.BlockSpec((kin, d), lambda i: (0, 0)),
            pl.BlockSpec((tm, d), lambda i: (i, 0)),
        ],
        out_specs=pl.BlockSpec((tm, d), lambda i: (i, 0)),
        compiler_params=_cparams(("parallel",)),
        name="out_proj",
    )(y2, w, x2)


def _out_proj_a_kernel(o0, o1, o2, l0, l1, l2, pt4_ref, pt16_ref, w_ref, x_ref, out_ref):
    tm = x_ref.shape[0]

    def token_order(ref, pt_ref):
        v = ref[...]
        if pt_ref is None:
            return v.reshape(tm, A_GW)
        pb = pt_ref.shape[0]
        per = pb // v.shape[0]
        pieces = []
        for h in range(tm // pb):
            hi, lo = _split_bf16(v[:, h * per:(h + 1) * per, :].reshape(pb, A_GW))
            pieces.append(jnp.dot(pt_ref[...], hi, preferred_element_type=F32)
                          + jnp.dot(pt_ref[...], lo, preferred_element_type=F32))
        return jnp.concatenate(pieces, axis=0)

    pts = (None, pt4_ref, pt16_ref)
    a, b, c = (token_order(r, pt) for r, pt in zip((l0, l1, l2), pts))
    m = jnp.maximum(jnp.maximum(a, b), c)
    ea, eb, ec = jnp.exp2(a - m), jnp.exp2(b - m), jnp.exp2(c - m)
    oa, ob, oc = (token_order(r, pt) for r, pt in zip((o0, o1, o2), pts))
    y = (ea * oa + eb * ob + ec * oc) / (ea + eb + ec)
    out_ref[...] = x_ref[...] + jnp.dot(y.astype(BF16), w_ref[...], preferred_element_type=F32)


def _mlp_kernel(x_ref, g_ref, w1_ref, w2_ref, o_ref, *, tf):
    x = x_ref[...]
    hn = _rms_rows(x, g_ref[...]).astype(BF16)
    acc = x
    for f in range(w1_ref.shape[1] // tf):
        cols = slice(f * tf, (f + 1) * tf)
        h = jnp.maximum(jnp.dot(hn, w1_ref[:, cols], preferred_element_type=F32), 0.0)
        acc = acc + jnp.dot((h * h).astype(BF16), w2_ref[cols, :], preferred_element_type=F32)
    o_ref[...] = acc


def _mlp(x2, gain, w1, w2, tm=512, tf=1024):
    n, d = x2.shape
    dff = w1.shape[1]
    const = lambda shape: pl.BlockSpec(shape, lambda i: (0, 0), pipeline_mode=pl.Buffered(1))
    return pl.pallas_call(
        functools.partial(_mlp_kernel, tf=tf),
        out_shape=jax.ShapeDtypeStruct((n, d), F32),
        grid=(n // tm,),
        in_specs=[pl.BlockSpec((tm, d), lambda i: (i, 0)), const((1, d)), const((d, dff)), const((dff, d))],
        out_specs=pl.BlockSpec((tm, d), lambda i: (i, 0)),
        compiler_params=_cparams(("parallel",)),
        name="mlp",
    )(x2, gain.reshape(1, d), w1, w2)


def _residue_major(tm, dilation):
    p = np.zeros((tm, tm), np.float32)
    j, r = np.meshgrid(np.arange(tm // dilation), np.arange(dilation), indexing="ij")
    p[(r * (tm // dilation) + j).ravel(), (j * dilation + r).ravel()] = 1.0
    return p


def _in_proj_a_kernel(x_ref, g_ref, w_ref, hg_ref, bd_ref, *rest):
    n_groups = len(A_GROUPS)
    perms, outs = rest[:n_groups - 1], rest[n_groups - 1:]
    tm = x_ref.shape[0]
    hn = _rms_rows(x_ref[...], g_ref[...]).astype(BF16)
    for which in range(3):
        for g, (_, dilation) in enumerate(A_GROUPS):
            j = which * n_groups + g
            cols = slice(j * A_GW, (j + 1) * A_GW)
            y = jnp.dot(hn, w_ref[:, cols], preferred_element_type=F32)
            if which < 2:
                nb = bd_ref.shape[0]
                y2 = (y * y).astype(BF16)
                ms = jnp.concatenate([jnp.dot(y2[:, c:c + nb], bd_ref[...], preferred_element_type=F32)
                                      for c in range(0, A_GW, nb)], axis=1) * (1.0 / HEAD_DIM)
                y = y * lax.rsqrt(ms + EPS) * hg_ref[:, cols]
            y = y.astype(BF16)
            if dilation == 1:
                y = y.reshape(1, tm, A_GW)
            else:
                perm = perms[g - 1]
                pb = perm.shape[0]
                y = jnp.concatenate(
                    [jnp.dot(perm[...], y[h:h + pb], preferred_element_type=F32).astype(BF16)
                     .reshape(dilation, pb // dilation, A_GW) for h in range(0, tm, pb)], axis=1)
            outs[g][:, :, which * A_GW:(which + 1) * A_GW] = y


def _attn_a_kernel(q_ref, kp_ref, kc_ref, vp_ref, vc_ref, bias_ref, o_ref, lse_ref):
    n = pl.program_id(2)
    blk = A_BLOCK
    pair = 2 * HEAD_DIM
    lane = lax.broadcasted_iota(I32, (2 * blk, pair), 1)
    row = lax.broadcasted_iota(I32, (2 * blk, blk), 0)
    pen = jnp.where((row < blk) & (n == 0), NEG, 0.0)
    nt = (((1,), (1,)), ((), ()))
    tn = (((0,), (0,)), ((), ()))
    k_all = jnp.concatenate([kp_ref[...], kc_ref[...]], axis=0)
    v_all = jnp.concatenate([vp_ref[...], vc_ref[...]], axis=0)
    q_all = q_ref[...]
    chains = [(sb, hp, a) for sb in range(A_STEP) for hp in range(A_HEADS // 2) for a in range(2)]

    def window(x, sb, hp):
        return x[sb * blk:(sb + 2) * blk, hp * pair:(hp + 1) * pair]

    keep = [lane < HEAD_DIM, lane >= HEAD_DIM]
    ka = [jnp.where(keep[a], window(k_all, sb, hp), jnp.zeros((2 * blk, pair), BF16))
          for sb, hp, a in chains]
    s = [lax.dot_general(k_, q_all[sb * blk:(sb + 1) * blk, hp * pair:(hp + 1) * pair], nt,
                         preferred_element_type=F32) + bias_ref[2 * hp + a]
         for k_, (sb, hp, a) in zip(ka, chains)]
    s = [s_ + pen if sb == 0 else s_ for s_, (sb, _, _) in zip(s, chains)]
    m = [jnp.max(s_, axis=0, keepdims=True) for s_ in s]
    p = [jnp.exp2(s_ - m_) for s_, m_ in zip(s, m)]
    l = [jnp.sum(p_, axis=0, keepdims=True) for p_ in p]
    pv = [lax.dot_general(window(v_all, sb, hp), p_.astype(BF16), tn, preferred_element_type=F32)
          for p_, (sb, hp, _) in zip(p, chains)]
    for idx in range(0, len(chains), 2):
        sb, hp, _ = chains[idx]
        o_t = jnp.concatenate([(pv[idx + a] / l[idx + a])[a * HEAD_DIM:(a + 1) * HEAD_DIM] for a in range(2)],
                              axis=0)
        l_t = jnp.concatenate([jnp.broadcast_to(m[idx + a] + jnp.log2(l[idx + a]), (HEAD_DIM, blk))
                               for a in range(2)], axis=0)
        o_ref[sb * blk:(sb + 1) * blk, hp * pair:(hp + 1) * pair] = o_t.T
        lse_ref[sb * blk:(sb + 1) * blk, hp * pair:(hp + 1) * pair] = l_t.T


def _attn_a_group(arr, bias_t, dilation, batch, seq):
    sub = seq // dilation
    rows = A_STEP * A_BLOCK
    nb = sub // rows

    def spec(which, prev):
        if prev:
            return pl.BlockSpec((None, None, A_BLOCK, A_GW),
                                lambda b, r, n: (b, r, jnp.maximum(n * A_STEP - 1, 0), which))
        return pl.BlockSpec((None, None, rows, A_GW), lambda b, r, n: (b, r, n, which))

    out_spec = pl.BlockSpec((None, None, rows, A_GW), lambda b, r, n: (b, r, n, 0))
    out_sds = jax.ShapeDtypeStruct((batch, dilation, sub, A_GW), F32)
    return pl.pallas_call(
        _attn_a_kernel,
        out_shape=[out_sds, out_sds],
        grid=(batch, dilation, nb),
        in_specs=[spec(0, False), spec(1, True), spec(1, False), spec(2, True), spec(2, False),
                  pl.BlockSpec((A_HEADS, 2 * A_BLOCK, A_BLOCK), lambda b, r, n: (0, 0, 0))],
        out_specs=[out_spec, out_spec],
        compiler_params=_cparams(("parallel", "parallel", "arbitrary")),
        name=f"attn_a_d{dilation}",
    )(arr, arr, arr, arr, arr, bias_t)


def _bias_a(tab_g, dilation):
    step = np.arange(3 * A_BLOCK - 1) - (A_BLOCK - 1)
    vec = tab_g[_t5_bucket(jnp.asarray(np.maximum(step, 0) * dilation, I32))].astype(F32) * LOG2E
    vec = jnp.where(jnp.asarray((step >= 0) & (step <= A_BLOCK))[:, None], vec, NEG)
    return _toeplitz(vec.T, 2 * A_BLOCK, A_BLOCK)


def _mixer_a(x2, gain, w_in, q_gain, k_gain, w_out, bias_a, batch, seq, tm=512):
    n, d = x2.shape
    n_groups = len(A_GROUPS)
    reps = n_groups * A_HEADS
    hg = jnp.concatenate([jnp.tile(q_gain * (HEAD_DIM ** -0.5 * LOG2E), reps), jnp.tile(k_gain, reps),
                          jnp.ones((n_groups * A_GW,), F32)]).reshape(1, -1)
    kout = 3 * n_groups * A_GW
    pb = V7X_MXU_DIM
    bd = np.kron(np.eye(pb // HEAD_DIM), np.ones((HEAD_DIM, HEAD_DIM))).astype(np.float32)
    perms = [_residue_major(pb, dil) for _, dil in A_GROUPS[1:]]
    tiles = seq // tm
    const = lambda shape: pl.BlockSpec(shape, lambda i: (0, 0), pipeline_mode=pl.Buffered(1))
    grouped = lambda dil, width: pl.BlockSpec((None, dil, tm // dil, width),
                                              lambda i: (i // tiles, 0, i % tiles, 0))
    arrs = pl.pallas_call(
        _in_proj_a_kernel,
        out_shape=[jax.ShapeDtypeStruct((batch, dil, seq // dil, 3 * A_GW), BF16) for _, dil in A_GROUPS],
        grid=(n // tm,),
        in_specs=[pl.BlockSpec((tm, d), lambda i: (i, 0)), const((1, d)), const((d, kout)),
                  const((1, kout)), const((pb, pb))] + [const((pb, pb)) for _ in perms],
        out_specs=[grouped(dil, 3 * A_GW) for _, dil in A_GROUPS],
        compiler_params=_cparams(("parallel",)),
        name="in_proj_a",
    )(x2, gain.reshape(1, d), w_in.astype(BF16), hg, jnp.asarray(bd, BF16),
      *[jnp.asarray(p, BF16) for p in perms])

    os_, ls_ = [], []
    for g, (window, dilation) in enumerate(A_GROUPS):
        assert window // dilation == A_BLOCK and (seq // dilation) % (A_STEP * A_BLOCK) == 0 and dilation <= pb // 16
        tab_g = bias_a[:, g * A_HEADS:(g + 1) * A_HEADS]
        o, lse = _attn_a_group(arrs[g], _bias_a(tab_g, dilation), dilation, batch, seq)
        os_.append(o)
        ls_.append(lse)

    return pl.pallas_call(
        _out_proj_a_kernel,
        out_shape=jax.ShapeDtypeStruct((n, d), F32),
        grid=(n // tm,),
        in_specs=[grouped(dil, A_GW) for _, dil in A_GROUPS] * 2 + [const((pb, pb)) for _ in perms]
        + [const((A_GW, d)), pl.BlockSpec((tm, d), lambda i: (i, 0))],
        out_specs=pl.BlockSpec((tm, d), lambda i: (i, 0)),
        compiler_params=_cparams(("parallel",)),
        name="out_proj_a",
    )(*os_, *ls_, *[jnp.asarray(p.T, BF16) for p in perms], w_out.astype(BF16), x2)


def _float_key(s):
    i = lax.bitcast_convert_type(s, I32)
    k = jnp.where(i < 0, i ^ jnp.int32(0x7FFFFFFF), i)
    return jnp.where(s == 0.0, 0, k)


def _split_bf16(x):
    hi = x.astype(BF16)
    return hi, (x - hi.astype(F32)).astype(BF16)


def _attn_b_kernel(q_ref, k_ref, vt_ref, qi_ref, kw_ref, bias_ref, o_ref, key_ref, msk_ref, k16_ref, *, bounded):
    qb = pl.program_id(1)
    kc, qt = B_KC, B_QT
    nch = qb // 2 + 1
    t_q = qb * qt + lax.broadcasted_iota(I32, (kc, qt), 1)
    sub = lax.broadcasted_iota(I32, (kc, qt), 0)
    nt = (((1,), (1,)), ((), ()))

    qh, ql = _split_bf16(qi_ref[...])
    qi3 = jnp.concatenate(
        [jnp.concatenate([x[:, h * B_IDX_DIM:(h + 1) * B_IDX_DIM] for x in (qh, qh, ql)], axis=1)
         for h in range(B_IDX_HEADS)], axis=0)
    wt = kw_ref[pl.ds(pl.multiple_of(qb * qt, qt), qt), 0:qt].T
    wt = wt * (B_IDX_HEADS ** -0.5 * B_IDX_DIM ** -0.5)

    npair = (nch + 1) // 2

    def score_pair(c2, carry):
        offs = [pl.multiple_of((2 * c2 + a) * kc, kc) for a in range(2)]
        ks = [_split_bf16(kw_ref[pl.ds(off, kc), 0:B_IDX_DIM]) for off in offs]
        sc = [lax.dot_general(jnp.concatenate([kh, kl, kh], axis=1), qi3, nt, preferred_element_type=F32)
              for kh, kl in ks]
        for off, s in zip(offs, sc):
            acc = jnp.zeros((kc, qt), F32)
            for h in range(B_IDX_HEADS):
                acc = acc + wt[B_IDX_DIM + h:B_IDX_DIM + h + 1, :] * jnp.maximum(s[:, h * qt:(h + 1) * qt], 0.0)
            key = jnp.where(off + sub <= t_q, _float_key(acc), INT_MIN)
            key_ref[pl.ds(off, kc), :] = key
            k16_ref[pl.ds(off, kc), :] = (key >> 16).astype(I16)
        return carry

    lax.fori_loop(0, npair, score_pair, 0)

    def search16():
        def count16(cand):
            def body(c2, acc):
                for a in range(2):
                    off = pl.multiple_of((2 * c2 + a) * kc, kc)
                    hit = jnp.where(k16_ref[pl.ds(off, kc), :] >= cand, jnp.int16(1), jnp.int16(0))
                    hit = hit.reshape(kc // 16, 16, qt)
                    parts = [hit[r] for r in range(kc // 16)]
                    while len(parts) > 1:
                        parts = [parts[i] + parts[i + 1] for i in range(0, len(parts), 2)]
                    acc = acc + parts[0]
                return acc
            acc = lax.fori_loop(0, npair, body, jnp.zeros((16, qt), I16))
            return jnp.sum(acc.astype(I32), axis=0, keepdims=True)

        def bit(it, lo):
            cand = lo + jnp.left_shift(jnp.int32(1), 15 - it)
            return jnp.where(count16(cand.astype(I16)) >= B_TOPK, cand, lo)

        return lax.fori_loop(0, 16, bit, jnp.full((1, qt), -2 ** 15, I32))

    def count(pred_fn):
        def body(c2, acc):
            for a in range(2):
                off = pl.multiple_of((2 * c2 + a) * kc, kc)
                hit = jnp.where(pred_fn(key_ref[pl.ds(off, kc), :], off + sub), 1, 0)
                acc = acc + jnp.sum(hit.reshape(kc // 8, 8, qt), axis=0)
            return acc
        acc = lax.fori_loop(0, npair, body, jnp.zeros((8, qt), I32))
        return jnp.sum(acc, axis=0, keepdims=True)

    thr_hi = search16()

    def low_halves(c2, carry):
        for a in range(2):
            off = pl.multiple_of((2 * c2 + a) * kc, kc)
            key = key_ref[pl.ds(off, kc), :]
            hi = key >> 16
            lo = (key & 0xFFFF) - 2 ** 15
            lo = jnp.where(hi == thr_hi, lo, jnp.where(hi > thr_hi, 2 ** 15 - 1, -2 ** 15))
            k16_ref[pl.ds(off, kc), :] = lo.astype(I16)
        return carry

    lax.fori_loop(0, npair, low_halves, 0)
    thr = jnp.left_shift(thr_hi, 16) | (search16() + 2 ** 15)
    need = B_TOPK - count(lambda key, _: key > thr)
    n_eq = count(lambda key, _: key == thr)

    def tie_search(_):
        def bit(it, j):
            cand = j + jnp.left_shift(jnp.int32(1), 12 - it)
            cnt = count(lambda key, idx: (key == thr) & (idx < cand))
            return jnp.where(cnt <= need, cand, j)
        return lax.fori_loop(0, 13, bit, jnp.zeros((1, qt), I32))

    any_tie = jnp.max(jnp.where(n_eq > need, 1, 0)) > 0
    j_max = lax.cond(any_tie, tie_search, lambda _: jnp.full((1, qt), 2 ** 13, I32), 0)

    def mask_chunk(c, carry):
        off = pl.multiple_of(c * kc, kc)
        key = key_ref[pl.ds(off, kc), :]
        idx = off + sub
        sel = ((key > thr) | ((key == thr) & (idx < j_max))) & (idx <= t_q)
        msk_ref[pl.ds(off, kc), :] = jnp.where(sel, 0.0, NEG)
        return carry

    lax.fori_loop(0, nch, mask_chunk, 0)

    grp = B_HEADS // B_KV_HEADS
    q = q_ref[...]
    groups = range(B_KV_HEADS)
    qg = [jnp.concatenate([q[:, (g * grp + j) * HEAD_DIM:(g * grp + j + 1) * HEAD_DIM]
                           for j in range(grp)], axis=0) for g in groups]
    ksl = [slice(g * HEAD_DIM, (g + 1) * HEAD_DIM) for g in groups]

    def flash(c, carry):
        m, l, acc = carry
        off = pl.multiple_of(c * kc, kc)
        d_lo = jnp.clip(qb - 2 * c, 0, B_NDELTA - 1)
        d_hi = jnp.clip(qb - 2 * c - 1, 0, B_NDELTA - 1)
        mk = msk_ref[pl.ds(off, kc), :]
        mk = jnp.concatenate([mk] * grp, axis=1)
        s = [lax.dot_general(k_ref[pl.ds(off, kc), ksl[g]], qg[g], nt, preferred_element_type=F32)
             for g in groups]
        s = [s[g] + jnp.concatenate([bias_ref[d_lo, g], bias_ref[d_hi, g]], axis=0) + mk for g in groups]
        if bounded:
            p = [jnp.exp2(s[g]) for g in groups]
        else:
            m_new = [jnp.maximum(m[g], jnp.max(s[g], axis=0, keepdims=True)) for g in groups]
            p = [jnp.exp2(s[g] - m_new[g]) for g in groups]
        pv = [jnp.dot(vt_ref[ksl[g], pl.ds(off, kc)], p[g].astype(BF16), preferred_element_type=F32)
              for g in groups]
        if bounded:
            l = [l[g] + jnp.sum(p[g], axis=0, keepdims=True) for g in groups]
            acc = [acc[g] + pv[g] for g in groups]
            return m, l, acc
        alpha = [jnp.exp2(m[g] - m_new[g]) for g in groups]
        l = [alpha[g] * l[g] + jnp.sum(p[g], axis=0, keepdims=True) for g in groups]
        acc = [alpha[g] * acc[g] + pv[g] for g in groups]
        return m_new, l, acc

    m0 = [jnp.full((1, grp * qt), NEG, F32) for _ in groups]
    l0 = [jnp.zeros((1, grp * qt), F32) for _ in groups]
    a0 = [jnp.zeros((HEAD_DIM, grp * qt), F32) for _ in groups]
    _, l, acc = lax.fori_loop(0, nch, flash, (m0, l0, a0))
    for g in groups:
        o = acc[g] / l[g]
        for j in range(grp):
            h = g * grp + j
            o_ref[:, h * HEAD_DIM:(h + 1) * HEAD_DIM] = o[:, j * qt:(j + 1) * qt].T.astype(o_ref.dtype)


def _toeplitz(vec, n_rows, n_cols):
    span = n_rows + n_cols - 1
    assert vec.shape[-1] == span
    lead = vec.shape[:-1]
    padded = jnp.concatenate([vec, jnp.zeros(lead + (1,), vec.dtype)], axis=-1)
    flat = jnp.tile(padded, n_rows)[..., :n_rows * span]
    return flat.reshape(lead + (n_rows, span))[..., n_rows - 1:]


def _bias_b(tab):
    grp = B_HEADS // B_KV_HEADS
    span = 2 * B_QT - 1
    dist = np.arange(B_NDELTA)[:, None] * B_QT + np.arange(span)[None, :] - (B_QT - 1)
    assert (B_NDELTA - 1) * B_QT - (B_QT - 1) >= 16 * 128 ** (15.0 / 16.0) + 1
    vec = tab[_t5_bucket(jnp.asarray(np.maximum(dist, 0), I32))].astype(F32) * LOG2E
    t = _toeplitz(vec.transpose(0, 2, 1), B_QT, B_QT)
    t = t.reshape(B_NDELTA, B_KV_HEADS, grp, B_QT, B_QT).transpose(0, 1, 3, 2, 4)
    return t.reshape(B_NDELTA, B_KV_HEADS, B_QT, grp * B_QT)


def _mixer_b(x2, gain, w_in, q_gain, k_gain, w_out, bias_b, batch, seq):
    d = x2.shape[1]
    nq, nkv = B_HEADS * HEAD_DIM, B_KV_HEADS * HEAD_DIM
    nidx = B_IDX_HEADS * B_IDX_DIM
    tn = 256
    used = 2 * nkv + nq + nidx + B_IDX_DIM + B_IDX_HEADS
    kout = -(-used // tn) * tn
    w = jnp.concatenate([w_in, jnp.zeros((d, kout - used), F32)], axis=1).astype(BF16)
    hg = jnp.concatenate([jnp.tile(q_gain * (HEAD_DIM ** -0.5 * LOG2E), B_HEADS), jnp.tile(k_gain, B_KV_HEADS),
                          jnp.ones((kout - nq - nkv,), F32)]).reshape(1, -1)
    p32, p16 = _in_proj(x2, gain, w, hg, n_norm_cols=nq + nkv, tn=tn, out_dtypes=(F32, BF16))
    p32 = p32.reshape(batch, seq, kout)
    p16 = p16.reshape(batch, seq, kout)
    assert seq % (2 * B_QT) == 0 and seq <= 2 ** 12
    kw_blk = (nq + 2 * nkv + nidx) // tn
    vt = p16[:, :, nq + nkv:nq + 2 * nkv].transpose(0, 2, 1)
    def attend(bias_t, bounded):
        return pl.pallas_call(
            functools.partial(_attn_b_kernel, bounded=bounded),
            out_shape=jax.ShapeDtypeStruct((batch, seq, nq), BF16),
            grid=(batch, seq // B_QT),
            in_specs=[
                pl.BlockSpec((None, B_QT, nq), lambda b, i: (b, i, 0)),
                pl.BlockSpec((None, seq, nkv), lambda b, i: (b, 0, nq // nkv)),
                pl.BlockSpec((None, nkv, seq), lambda b, i: (b, 0, 0)),
                pl.BlockSpec((None, B_QT, nidx), lambda b, i: (b, i, (nq + 2 * nkv) // nidx)),
                pl.BlockSpec((None, seq, tn), lambda b, i: (b, 0, kw_blk)),
                pl.BlockSpec((B_NDELTA, B_KV_HEADS, B_QT, (B_HEADS // B_KV_HEADS) * B_QT),
                             lambda b, i: (0, 0, 0, 0), pipeline_mode=pl.Buffered(1)),
            ],
            out_specs=pl.BlockSpec((None, B_QT, nq), lambda b, i: (b, i, 0)),
            scratch_shapes=[pltpu.VMEM((seq, B_QT), I32), pltpu.VMEM((seq, B_QT), F32),
                            pltpu.VMEM((seq, B_QT), I16)],
            compiler_params=_cparams(("parallel", "arbitrary")),
            name="attn_b_bounded" if bounded else "attn_b",
        )(p16, p16, vt, p32, p32, bias_t)

    bias_t = _bias_b(bias_b)
    qk_bound = 1.02 * HEAD_DIM * (HEAD_DIM ** -0.5 * LOG2E) * jnp.max(jnp.abs(q_gain)) * jnp.max(jnp.abs(k_gain))
    bound = qk_bound + jnp.max(bias_b) * LOG2E
    spread = bound + qk_bound - jnp.min(bias_b) * LOG2E
    y = lax.cond(spread <= B_SAFE_SPREAD,
                 lambda: attend(bias_t - bound, True),
                 lambda: attend(bias_t, False))
    return _out_proj(y.reshape(batch * seq, nq), w_out.astype(BF16), x2)


def _conv_c_kernel(x_ref, halo_ref, cw_ref, o_ref, xs_ref):
    i, j = pl.program_id(1), pl.program_id(2)
    ts = x_ref.shape[0]
    xs_ref[0:8, :] = jnp.where(i > 0, halo_ref[...], 0.0)
    xs_ref[8:, :] = x_ref[...]
    cw = cw_ref[...]
    conv = sum(cw[t:t + 1, :] * xs_ref[8 - (C_CONV - 1) + t:8 - (C_CONV - 1) + t + ts, :]
               for t in range(C_CONV))
    y = conv * jax.nn.sigmoid(conv)

    @pl.when(j == 2)
    def _():
        o_ref[...] = y

    @pl.when(j < 2)
    def _():
        scale = jnp.where(j == 0, C_DK ** -0.5, 1.0)
        for h in range(C_HEADS):
            sl = slice(h * C_DK, (h + 1) * C_DK)
            yh = y[:, sl]
            ss = jnp.sum(yh * yh, axis=-1, keepdims=True)
            o_ref[:, sl] = yh * (lax.rsqrt(ss + EPS) * scale)


def _delta_kernel(q_ref, k_ref, v_ref, gate_ref, a_ref, b_ref, alog_ref, dtb_ref, og_ref, y_ref,
                  rows_s, p_s, r_s, m_s, n_s, el_s, o_s):
    cs, gs = C_CHUNK, C_GROUP
    per = gs // cs
    seq, dv = v_ref.shape
    n_groups = seq // gs
    hi = lax.Precision.HIGHEST
    r = lax.broadcasted_iota(I32, (gs, gs), 0)
    c_ = lax.broadcasted_iota(I32, (gs, gs), 1)

    def same_block(size):
        sh = int(math.log2(size))
        return (r >> sh) == (c_ >> sh)

    chunk = same_block(cs)
    lower, strict, eye = chunk & (r >= c_), chunk & (r > c_), r == c_

    z = a_ref[...] + dtb_ref[...]
    softplus = jnp.maximum(z, 0.0) + jnp.log(1.0 + jnp.exp(-jnp.abs(z)))
    g_all = -jnp.exp(alog_ref[...]) * softplus
    gc_all = jnp.dot(g_all, jnp.where(chunk & (r <= c_), 1.0, 0.0), preferred_element_type=F32, precision=hi)
    gl_all = jnp.dot(g_all, jnp.where(chunk, 1.0, 0.0), preferred_element_type=F32, precision=hi)
    rows_s[0] = jax.nn.sigmoid(b_ref[...])
    rows_s[1] = gc_all
    rows_s[2] = jnp.exp(gc_all)
    rows_s[3] = jnp.exp(gl_all - gc_all)
    first = (lax.broadcasted_iota(I32, (gs, per * dv), 0)
             == (lax.broadcasted_iota(I32, (gs, per * dv), 1) // dv) * cs)
    el_s[...] = jnp.exp(jnp.dot(gl_all, jnp.where(first, 1.0, 0.0), preferred_element_type=F32, precision=hi))

    def mm(a, b):
        return jnp.dot(a.astype(BF16), b.astype(BF16), preferred_element_type=F32)

    def mm_nt(a, b):
        return lax.dot_general(a.astype(BF16), b.astype(BF16), (((1,), (1,)), ((), ())),
                               preferred_element_type=F32)

    ways = 4

    def prep(m):
        ids = [m * ways + a for a in range(ways)]
        rows = [pl.ds(pl.multiple_of(i * gs, gs), gs) for i in ids]
        rw = [jnp.concatenate([rows_s[j, pl.ds(i, 1), :] for j in range(4)] + [jnp.zeros((4, gs), F32)], axis=0)
              for i in ids]
        cl = [x.T for x in rw]
        beta_col, gc_col, eg_col, ekg_col = ([x[:, j:j + 1] for x in cl] for j in range(4))
        q, k, v = ([ref[rw_, :] for rw_ in rows] for ref in (q_ref, k_ref, v_ref))
        decay = [jnp.where(lower, jnp.exp(jnp.where(lower, gcc - x[1:2, :], 0.0)), 0.0)
                 for gcc, x in zip(gc_col, rw)]
        kb = [k_ * b_ for k_, b_ in zip(k, beta_col)]
        yield
        a_mat = [jnp.where(strict, mm_nt(kb_, k_) * d_, 0.0) for kb_, k_, d_ in zip(kb, k, decay)]
        yield
        a8 = [jnp.where(same_block(8), a_, 0.0) for a_ in a_mat]
        t = [jnp.where(eye, 1.0, 0.0) - a_ for a_ in a8]
        pw = [mm(a_, a_) for a_ in a8]
        yield
        t = [t_ + mm(t_, p_) for t_, p_ in zip(t, pw)]
        yield
        pw = [mm(p_, p_) for p_ in pw]
        yield
        t = [t_ + mm(t_, p_) for t_, p_ in zip(t, pw)]
        yield
        size = 8
        while size < cs:
            sel = same_block(2 * size) & jnp.logical_not(same_block(size))
            nt = [mm(jnp.where(sel, a_, 0.0), t_) for a_, t_ in zip(a_mat, t)]
            yield
            t = [t_ - mm(t_, n_) for t_, n_ in zip(t, nt)]
            yield
            size *= 2
        uw = [mm(t_, jnp.concatenate([v_ * b_, kb_ * e_], axis=1))
              for t_, v_, b_, kb_, e_ in zip(t, v, beta_col, kb, eg_col)]
        yield
        qk = [jnp.where(lower, mm_nt(q_, k_) * d_, 0.0) for q_, k_, d_ in zip(q, k, decay)]
        yield
        qkwu = [mm(qk_, uw_) for qk_, uw_ in zip(qk, uw)]
        kgt = [(k_ * e_).T for k_, e_ in zip(k, ekg_col)]
        yield
        nm = [[mm(kgt[a][:, j * cs:(j + 1) * cs], uw[a][j * cs:(j + 1) * cs, :]) for j in range(per)]
              for a in range(ways)]

        def write():
            for a, i in enumerate(ids):
                r_s[rows[a], :] = qkwu[a][:, :dv]
                p_s[rows[a], :] = (q[a] * eg_col[a] - qkwu[a][:, dv:]).astype(BF16)
                for j in range(per):
                    n_s[i * per + j] = nm[a][j][:, :dv]
                    m_s[i * per + j] = nm[a][j][:, dv:].astype(BF16)
        return write

    def scan(m, state):
        for a in range(ways):
            i = m * ways + a
            el = el_s[pl.ds(i, 1), :]
            for j in range(per):
                rows = pl.ds(pl.multiple_of(i * gs + j * cs, cs), cs)
                c = i * per + j
                sb = state[0].astype(BF16)
                o_s[rows, :] = jnp.dot(p_s[rows, :], sb, preferred_element_type=F32) + r_s[rows, :]
                state[0] = (state[0] * el[:, j * dv:(j + 1) * dv] + n_s[c]
                            - jnp.dot(m_s[c], sb, preferred_element_type=F32))
                yield

    def weave(*gens):
        gens, results = list(gens), [None] * len(gens)
        live = list(range(len(gens)))
        while live:
            for idx in list(live):
                try:
                    next(gens[idx])
                except StopIteration as stop:
                    results[idx] = stop.value
                    live.remove(idx)
        return results

    n_blocks = n_groups // ways
    weave(prep(0))[0]()

    def step(m, state):
        holder = [state]
        _, write = weave(scan(m - 1, holder), prep(m))
        write()
        return holder[0]

    state = lax.fori_loop(1, n_blocks, step, jnp.zeros((C_DK, dv), F32))

    def finish(first, last):
        for t in range(first, last):
            rows = pl.ds(t * gs, gs)
            gate = gate_ref[rows, :]
            y_ref[rows, :] = (_rms_rows(o_s[rows, :], og_ref[...])
                              * (gate * jax.nn.sigmoid(gate))).astype(y_ref.dtype)
            yield

    done = (n_blocks - 1) * ways
    weave(scan(n_blocks - 1, [state]), finish(0, done))
    weave(finish(done, n_groups))


def _mixer_c(x2, gain, w_in, conv_w, a_log, dt_bias, o_gain, w_out, batch, seq):
    d = x2.shape[1]
    hw = C_HEADS * C_DK
    tn = 256
    used = 4 * hw + 2 * C_HEADS
    kout = -(-used // tn) * tn
    w = jnp.concatenate([w_in, jnp.zeros((d, kout - used), F32)], axis=1).astype(BF16)
    (proj,) = _in_proj(x2, gain, w, jnp.ones((1, kout), F32), n_norm_cols=0, tn=tn, out_dtypes=(F32,))
    proj = proj.reshape(batch, seq, kout)

    ts = 512
    qkv = pl.pallas_call(
        _conv_c_kernel,
        out_shape=jax.ShapeDtypeStruct((batch, seq, 3 * hw), F32),
        grid=(batch, seq // ts, 3),
        in_specs=[
            pl.BlockSpec((None, ts, hw), lambda b, i, j: (b, i, j)),
            pl.BlockSpec((None, 8, hw), lambda b, i, j: (b, jnp.maximum(i * (ts // 8) - 1, 0), j)),
            pl.BlockSpec((C_CONV, hw), lambda b, i, j: (0, j)),
        ],
        out_specs=pl.BlockSpec((None, ts, hw), lambda b, i, j: (b, i, j)),
        scratch_shapes=[pltpu.VMEM((ts + 8, hw), F32)],
        compiler_params=_cparams(("parallel", "parallel", "arbitrary")),
        name="conv_c",
    )(proj, proj, conv_w)

    n_groups = seq // C_GROUP
    ba = proj[:, :, 4 * hw:4 * hw + 2 * C_HEADS].transpose(0, 2, 1).reshape(batch, 2 * C_HEADS, n_groups, C_GROUP)
    per_head = lambda v: jnp.broadcast_to(v.reshape(C_HEADS, 1, 1), (C_HEADS, 1, C_GROUP))
    head_cols = lambda off: pl.BlockSpec((None, seq, C_DK), lambda b, h: (b, 0, off + h))
    small = lambda off: pl.BlockSpec((None, None, n_groups, C_GROUP), lambda b, h: (b, off + h, 0, 0))
    scalar_row = pl.BlockSpec((None, 1, C_GROUP), lambda b, h: (h, 0, 0))
    y = pl.pallas_call(
        _delta_kernel,
        out_shape=jax.ShapeDtypeStruct((batch, seq, hw), BF16),
        grid=(batch, C_HEADS),
        in_specs=[head_cols(0), head_cols(C_HEADS), head_cols(2 * C_HEADS),
                  pl.BlockSpec((None, seq, C_DK), lambda b, h: (b, 0, 3 * C_HEADS + h)),
                  small(C_HEADS), small(0), scalar_row, scalar_row,
                  pl.BlockSpec((1, C_DK), lambda b, h: (0, 0))],
        out_specs=head_cols(0),
        scratch_shapes=[
            pltpu.VMEM((4, n_groups, C_GROUP), F32),
            pltpu.VMEM((seq, C_DK), BF16),
            pltpu.VMEM((seq, C_DK), F32),
            pltpu.VMEM((seq // C_CHUNK, C_DK, C_DK), BF16),
            pltpu.VMEM((seq // C_CHUNK, C_DK, C_DK), F32),
            pltpu.VMEM((n_groups, (C_GROUP // C_CHUNK) * C_DK), F32),
            pltpu.VMEM((seq, C_DK), F32),
        ],
        compiler_params=_cparams(("parallel", "parallel")),
        name="delta_c",
    )(qkv, qkv, qkv, proj, ba, ba, per_head(a_log), per_head(dt_bias), o_gain.reshape(1, C_DK))
    return _out_proj(y.reshape(batch * seq, hw), w_out.astype(BF16), x2)


def kernel(x, rel_bias, norm_mix, norm_mlp, mlp_w1, mlp_w2, a_w_in, a_q_gain, a_k_gain, a_w_out,
           b_w_in, b_q_gain, b_k_gain, b_w_out, c_w_in, c_conv_w, c_a_log, c_dt_bias, c_o_gain, c_w_out):
    batch, seq, d = x.shape
    depth = norm_mix.shape[0]
    a_cols = len(A_GROUPS) * A_HEADS
    bias_a, bias_b = rel_bias[:, :a_cols], rel_bias[:, a_cols:]
    x2 = x.reshape(batch * seq, d)
    for i in range(depth):
        kind, j = i % 3, i // 3
        if kind == 0:
            x2 = _mixer_a(x2, norm_mix[i], a_w_in[j], a_q_gain[j], a_k_gain[j], a_w_out[j], bias_a, batch, seq)
        elif kind == 1:
            x2 = _mixer_b(x2, norm_mix[i], b_w_in[j], b_q_gain[j], b_k_gain[j], b_w_out[j], bias_b, batch, seq)
        else:
            x2 = _mixer_c(x2, norm_mix[i], c_w_in[j], c_conv_w[j], c_a_log[j], c_dt_bias[j], c_o_gain[j],
                          c_w_out[j], batch, seq)
        x2 = _mlp(x2, norm_mlp[i], mlp_w1[i].astype(BF16), mlp_w2[i].astype(BF16))
    return x2.reshape(batch, seq, d)
```

```python
import functools
import math

import numpy as np
import jax
import jax.numpy as jnp
from jax import lax
from jax.experimental import pallas as pl
from jax.experimental.pallas import tpu as pltpu

F32 = jnp.float32
BF16 = jnp.bfloat16
I32 = jnp.int32

EPS = 1e-6
HEAD_DIM = 64
NEG = -1e30
INT_MIN = -2 ** 31
LOG2E = math.log2(math.e)

V7X_VMEM_BYTES = 64 * 1024 * 1024
VMEM_LIMIT = V7X_VMEM_BYTES - 8 * 1024 * 1024
V7X_MXU_DIM = 256

NUM_BUCKETS = 32
MAX_DISTANCE = 2048

A_GROUPS = ((128, 1), (512, 4), (2048, 16))
A_HEADS = 8
A_BLOCK = 128
A_GW = A_HEADS * HEAD_DIM
A_STEP = 2

B_HEADS = 16
B_KV_HEADS = 4
B_IDX_HEADS = 8
B_IDX_DIM = 64
B_TOPK = 256
B_QT = 128
B_KC = 256
B_NDELTA = 14
B_SAFE_SPREAD = 100.0

C_HEADS = 8
C_DK = 128
C_CONV = 4
C_CHUNK = 64
C_GROUP = 4 * C_CHUNK


def _cparams(sem):
    return pltpu.CompilerParams(dimension_semantics=sem, vmem_limit_bytes=VMEM_LIMIT)


def _t5_bucket(dist):
    max_exact = NUM_BUCKETS // 2
    d = jnp.maximum(dist, 1).astype(F32)
    log_part = jnp.log(d / max_exact) / math.log(MAX_DISTANCE / max_exact) * (NUM_BUCKETS - max_exact)
    large = jnp.minimum(max_exact + log_part.astype(I32), NUM_BUCKETS - 1)
    return jnp.where(dist < max_exact, dist, large)


def _rms_rows(x, gain_row):
    ms = jnp.mean(x * x, axis=-1, keepdims=True)
    return x * lax.rsqrt(ms + EPS) * gain_row


def _in_proj_kernel(x_ref, g_ref, w_ref, hg_ref, bd_ref, *outs, n_norm, tn):
    hn = _rms_rows(x_ref[...], g_ref[...]).astype(BF16)
    for j in range(w_ref.shape[1] // tn):
        cols = slice(j * tn, (j + 1) * tn)
        y = jnp.dot(hn, w_ref[:, cols], preferred_element_type=F32)
        if j < n_norm:
            ms = jnp.dot((y * y).astype(BF16), bd_ref[...], preferred_element_type=F32) * (1.0 / HEAD_DIM)
            y = y * lax.rsqrt(ms + EPS) * hg_ref[:, cols]
        for o in outs:
            o[:, cols] = y.astype(o.dtype)


def _in_proj(x2, gain, w, head_gain, *, n_norm_cols, tn, out_dtypes, tm=512):
    n, d = x2.shape
    kout = w.shape[1]
    bd = np.kron(np.eye(tn // HEAD_DIM), np.ones((HEAD_DIM, HEAD_DIM))).astype(np.float32)
    const = lambda shape: pl.BlockSpec(shape, lambda i: (0, 0), pipeline_mode=pl.Buffered(1))
    return pl.pallas_call(
        functools.partial(_in_proj_kernel, n_norm=n_norm_cols // tn, tn=tn),
        out_shape=[jax.ShapeDtypeStruct((n, kout), dt) for dt in out_dtypes],
        grid=(n // tm,),
        in_specs=[pl.BlockSpec((tm, d), lambda i: (i, 0)), const((1, d)), const((d, kout)),
                  const((1, kout)), const((tn, tn))],
        out_specs=[pl.BlockSpec((tm, kout), lambda i: (i, 0)) for _ in out_dtypes],
        compiler_params=_cparams(("parallel",)),
        name="in_proj",
    )(x2, gain.reshape(1, d), w, head_gain, jnp.asarray(bd, BF16))


def _out_proj_kernel(y_ref, w_ref, x_ref, o_ref):
    o_ref[...] = x_ref[...] + jnp.dot(y_ref[...].astype(BF16), w_ref[...], preferred_element_type=F32)


def _out_proj(y2, w, x2, tm=512):
    n, kin = y2.shape
    d = w.shape[1]
    return pl.pallas_call(
        _out_proj_kernel,
        out_shape=jax.ShapeDtypeStruct((n, d), F32),
        grid=(n // tm,),
        in_specs=[
            pl.BlockSpec((tm, kin), lambda i: (i, 0)),
            pl.BlockSpec((kin, d), lambda i: (0, 0)),
            pl.BlockSpec((tm, d), lambda i: (i, 0)),
        ],
        out_specs=pl.BlockSpec((tm, d), lambda i: (i, 0)),
        compiler_params=_cparams(("parallel",)),
        name="out_proj",
    )(y2, w, x2)


def _out_proj_a_kernel(o0, o1, o2, l0, l1, l2, pt4_ref, pt16_ref, w_ref, x_ref, out_ref):
    tm = x_ref.shape[0]

    def token_order(ref, pt_ref):
        v = ref[...]
        if pt_ref is None:
            return v.reshape(tm, A_GW)
        pb = pt_ref.shape[0]
        per = pb // v.shape[0]
        pieces = []
        for h in range(tm // pb):
            hi, lo = _split_bf16(v[:, h * per:(h + 1) * per, :].reshape(pb, A_GW))
            pieces.append(jnp.dot(pt_ref[...], hi, preferred_element_type=F32)
                          + jnp.dot(pt_ref[...], lo, preferred_element_type=F32))
        return jnp.concatenate(pieces, axis=0)

    pts = (None, pt4_ref, pt16_ref)
    a, b, c = (token_order(r, pt) for r, pt in zip((l0, l1, l2), pts))
    m = jnp.maximum(jnp.maximum(a, b), c)
    ea, eb, ec = jnp.exp2(a - m), jnp.exp2(b - m), jnp.exp2(c - m)
    oa, ob, oc = (token_order(r, pt) for r, pt in zip((o0, o1, o2), pts))
    y = (ea * oa + eb * ob + ec * oc) / (ea + eb + ec)
    out_ref[...] = x_ref[...] + jnp.dot(y.astype(BF16), w_ref[...], preferred_element_type=F32)


def _mlp_kernel(x_ref, g_ref, w1_ref, w2_ref, o_ref, *, tf):
    x = x_ref[...]
    hn = _rms_rows(x, g_ref[...]).astype(BF16)
    acc = x
    for f in range(w1_ref.shape[1] // tf):
        cols = slice(f * tf, (f + 1) * tf)
        h = jnp.maximum(jnp.dot(hn, w1_ref[:, cols], preferred_element_type=F32), 0.0)
        acc = acc + jnp.dot((h * h).astype(BF16), w2_ref[cols, :], preferred_element_type=F32)
    o_ref[...] = acc


def _mlp(x2, gain, w1, w2, tm=512, tf=1024):
    n, d = x2.shape
    dff = w1.shape[1]
    const = lambda shape: pl.BlockSpec(shape, lambda i: (0, 0), pipeline_mode=pl.Buffered(1))
    return pl.pallas_call(
        functools.partial(_mlp_kernel, tf=tf),
        out_shape=jax.ShapeDtypeStruct((n, d), F32),
        grid=(n // tm,),
        in_specs=[pl.BlockSpec((tm, d), lambda i: (i, 0)), const((1, d)), const((d, dff)), const((dff, d))],
        out_specs=pl.BlockSpec((tm, d), lambda i: (i, 0)),
        compiler_params=_cparams(("parallel",)),
        name="mlp",
    )(x2, gain.reshape(1, d), w1, w2)


def _residue_major(tm, dilation):
    p = np.zeros((tm, tm), np.float32)
    j, r = np.meshgrid(np.arange(tm // dilation), np.arange(dilation), indexing="ij")
    p[(r * (tm // dilation) + j).ravel(), (j * dilation + r).ravel()] = 1.0
    return p


def _in_proj_a_kernel(x_ref, g_ref, w_ref, hg_ref, bd_ref, *rest):
    n_groups = len(A_GROUPS)
    perms, outs = rest[:n_groups - 1], rest[n_groups - 1:]
    tm = x_ref.shape[0]
    hn = _rms_rows(x_ref[...], g_ref[...]).astype(BF16)
    for which in range(3):
        for g, (_, dilation) in enumerate(A_GROUPS):
            j = which * n_groups + g
            cols = slice(j * A_GW, (j + 1) * A_GW)
            y = jnp.dot(hn, w_ref[:, cols], preferred_element_type=F32)
            if which < 2:
                nb = bd_ref.shape[0]
                y2 = (y * y).astype(BF16)
                ms = jnp.concatenate([jnp.dot(y2[:, c:c + nb], bd_ref[...], preferred_element_type=F32)
                                      for c in range(0, A_GW, nb)], axis=1) * (1.0 / HEAD_DIM)
                y = y * lax.rsqrt(ms + EPS) * hg_ref[:, cols]
            y = y.astype(BF16)
            if dilation == 1:
                y = y.reshape(1, tm, A_GW)
            else:
                perm = perms[g - 1]
                pb = perm.shape[0]
                y = jnp.concatenate(
                    [jnp.dot(perm[...], y[h:h + pb], preferred_element_type=F32).astype(BF16)
                     .reshape(dilation, pb // dilation, A_GW) for h in range(0, tm, pb)], axis=1)
            outs[g][:, :, which * A_GW:(which + 1) * A_GW] = y


def _attn_a_kernel(q_ref, kp_ref, kc_ref, vp_ref, vc_ref, bias_ref, o_ref, lse_ref):
    n = pl.program_id(2)
    blk = A_BLOCK
    pair = 2 * HEAD_DIM
    lane = lax.broadcasted_iota(I32, (2 * blk, pair), 1)
    row = lax.broadcasted_iota(I32, (2 * blk, blk), 0)
    pen = jnp.where((row < blk) & (n == 0), NEG, 0.0)
    nt = (((1,), (1,)), ((), ()))
    tn = (((0,), (0,)), ((), ()))
    k_all = jnp.concatenate([kp_ref[...], kc_ref[...]], axis=0)
    v_all = jnp.concatenate([vp_ref[...], vc_ref[...]], axis=0)
    q_all = q_ref[...]
    chains = [(sb, hp, a) for sb in range(A_STEP) for hp in range(A_HEADS // 2) for a in range(2)]

    def window(x, sb, hp):
        return x[sb * blk:(sb + 2) * blk, hp * pair:(hp + 1) * pair]

    keep = [lane < HEAD_DIM, lane >= HEAD_DIM]
    ka = [jnp.where(keep[a], window(k_all, sb, hp), jnp.zeros((2 * blk, pair), BF16))
          for sb, hp, a in chains]
    s = [lax.dot_general(k_, q_all[sb * blk:(sb + 1) * blk, hp * pair:(hp + 1) * pair], nt,
                         preferred_element_type=F32) + bias_ref[2 * hp + a]
         for k_, (sb, hp, a) in zip(ka, chains)]
    s = [s_ + pen if sb == 0 else s_ for s_, (sb, _, _) in zip(s, chains)]
    m = [jnp.max(s_, axis=0, keepdims=True) for s_ in s]
    p = [jnp.exp2(s_ - m_) for s_, m_ in zip(s, m)]
    l = [jnp.sum(p_, axis=0, keepdims=True) for p_ in p]
    pv = [lax.dot_general(window(v_all, sb, hp), p_.astype(BF16), tn, preferred_element_type=F32)
          for p_, (sb, hp, _) in zip(p, chains)]
    for idx in range(0, len(chains), 2):
        sb, hp, _ = chains[idx]
        o_t = jnp.concatenate([(pv[idx + a] / l[idx + a])[a * HEAD_DIM:(a + 1) * HEAD_DIM] for a in range(2)],
                              axis=0)
        l_t = jnp.concatenate([jnp.broadcast_to(m[idx + a] + jnp.log2(l[idx + a]), (HEAD_DIM, blk))
                               for a in range(2)], axis=0)
        o_ref[sb * blk:(sb + 1) * blk, hp * pair:(hp + 1) * pair] = o_t.T
        lse_ref[sb * blk:(sb + 1) * blk, hp * pair:(hp + 1) * pair] = l_t.T


def _attn_a_group(arr, bias_t, dilation, batch, seq):
    sub = seq // dilation
    rows = A_STEP * A_BLOCK
    nb = sub // rows

    def spec(which, prev):
        if prev:
            return pl.BlockSpec((None, None, A_BLOCK, A_GW),
                                lambda b, r, n: (b, r, jnp.maximum(n * A_STEP - 1, 0), which))
        return pl.BlockSpec((None, None, rows, A_GW), lambda b, r, n: (b, r, n, which))

    out_spec = pl.BlockSpec((None, None, rows, A_GW), lambda b, r, n: (b, r, n, 0))
    out_sds = jax.ShapeDtypeStruct((batch, dilation, sub, A_GW), F32)
    return pl.pallas_call(
        _attn_a_kernel,
        out_shape=[out_sds, out_sds],
        grid=(batch, dilation, nb),
        in_specs=[spec(0, False), spec(1, True), spec(1, False), spec(2, True), spec(2, False),
                  pl.BlockSpec((A_HEADS, 2 * A_BLOCK, A_BLOCK), lambda b, r, n: (0, 0, 0))],
        out_specs=[out_spec, out_spec],
        compiler_params=_cparams(("parallel", "parallel", "arbitrary")),
        name=f"attn_a_d{dilation}",
    )(arr, arr, arr, arr, arr, bias_t)


def _bias_a(tab_g, dilation):
    step = np.arange(3 * A_BLOCK - 1) - (A_BLOCK - 1)
    vec = tab_g[_t5_bucket(jnp.asarray(np.maximum(step, 0) * dilation, I32))].astype(F32) * LOG2E
    vec = jnp.where(jnp.asarray((step >= 0) & (step <= A_BLOCK))[:, None], vec, NEG)
    return _toeplitz(vec.T, 2 * A_BLOCK, A_BLOCK)


def _mixer_a(x2, gain, w_in, q_gain, k_gain, w_out, bias_a, batch, seq, tm=512):
    n, d = x2.shape
    n_groups = len(A_GROUPS)
    reps = n_groups * A_HEADS
    hg = jnp.concatenate([jnp.tile(q_gain * (HEAD_DIM ** -0.5 * LOG2E), reps), jnp.tile(k_gain, reps),
                          jnp.ones((n_groups * A_GW,), F32)]).reshape(1, -1)
    kout = 3 * n_groups * A_GW
    pb = V7X_MXU_DIM
    bd = np.kron(np.eye(pb // HEAD_DIM), np.ones((HEAD_DIM, HEAD_DIM))).astype(np.float32)
    perms = [_residue_major(pb, dil) for _, dil in A_GROUPS[1:]]
    tiles = seq // tm
    const = lambda shape: pl.BlockSpec(shape, lambda i: (0, 0), pipeline_mode=pl.Buffered(1))
    grouped = lambda dil, width: pl.BlockSpec((None, dil, tm // dil, width),
                                              lambda i: (i // tiles, 0, i % tiles, 0))
    arrs = pl.pallas_call(
        _in_proj_a_kernel,
        out_shape=[jax.ShapeDtypeStruct((batch, dil, seq // dil, 3 * A_GW), BF16) for _, dil in A_GROUPS],
        grid=(n // tm,),
        in_specs=[pl.BlockSpec((tm, d), lambda i: (i, 0)), const((1, d)), const((d, kout)),
                  const((1, kout)), const((pb, pb))] + [const((pb, pb)) for _ in perms],
        out_specs=[grouped(dil, 3 * A_GW) for _, dil in A_GROUPS],
        compiler_params=_cparams(("parallel",)),
        name="in_proj_a",
    )(x2, gain.reshape(1, d), w_in.astype(BF16), hg, jnp.asarray(bd, BF16),
      *[jnp.asarray(p, BF16) for p in perms])

    os_, ls_ = [], []
    for g, (window, dilation) in enumerate(A_GROUPS):
        assert window // dilation == A_BLOCK and (seq // dilation) % (A_STEP * A_BLOCK) == 0 and dilation <= pb // 16
        tab_g = bias_a[:, g * A_HEADS:(g + 1) * A_HEADS]
        o, lse = _attn_a_group(arrs[g], _bias_a(tab_g, dilation), dilation, batch, seq)
        os_.append(o)
        ls_.append(lse)

    return pl.pallas_call(
        _out_proj_a_kernel,
        out_shape=jax.ShapeDtypeStruct((n, d), F32),
        grid=(n // tm,),
        in_specs=[grouped(dil, A_GW) for _, dil in A_GROUPS] * 2 + [const((pb, pb)) for _ in perms]
        + [const((A_GW, d)), pl.BlockSpec((tm, d), lambda i: (i, 0))],
        out_specs=pl.BlockSpec((tm, d), lambda i: (i, 0)),
        compiler_params=_cparams(("parallel",)),
        name="out_proj_a",
    )(*os_, *ls_, *[jnp.asarray(p.T, BF16) for p in perms], w_out.astype(BF16), x2)


def _float_key(s):
    i = lax.bitcast_convert_type(s, I32)
    k = jnp.where(i < 0, i ^ jnp.int32(0x7FFFFFFF), i)
    return jnp.where(s == 0.0, 0, k)


def _split_bf16(x):
    hi = x.astype(BF16)
    return hi, (x - hi.astype(F32)).astype(BF16)


def _attn_b_kernel(q_ref, k_ref, vt_ref, qi_ref, kw_ref, bias_ref, o_ref, key_ref, msk_ref, *, bounded):
    qb = pl.program_id(1)
    kc, qt = B_KC, B_QT
    nch = qb // 2 + 1
    t_q = qb * qt + lax.broadcasted_iota(I32, (kc, qt), 1)
    sub = lax.broadcasted_iota(I32, (kc, qt), 0)
    nt = (((1,), (1,)), ((), ()))

    qh, ql = _split_bf16(qi_ref[...])
    qi3 = jnp.concatenate(
        [jnp.concatenate([x[:, h * B_IDX_DIM:(h + 1) * B_IDX_DIM] for x in (qh, qh, ql)], axis=1)
         for h in range(B_IDX_HEADS)], axis=0)
    wt = kw_ref[pl.ds(pl.multiple_of(qb * qt, qt), qt), 0:qt].T
    wt = wt * (B_IDX_HEADS ** -0.5 * B_IDX_DIM ** -0.5)

    npair = (nch + 1) // 2

    def score_pair(c2, carry):
        offs = [pl.multiple_of((2 * c2 + a) * kc, kc) for a in range(2)]
        ks = [_split_bf16(kw_ref[pl.ds(off, kc), 0:B_IDX_DIM]) for off in offs]
        sc = [lax.dot_general(jnp.concatenate([kh, kl, kh], axis=1), qi3, nt, preferred_element_type=F32)
              for kh, kl in ks]
        for off, s in zip(offs, sc):
            acc = jnp.zeros((kc, qt), F32)
            for h in range(B_IDX_HEADS):
                acc = acc + wt[B_IDX_DIM + h:B_IDX_DIM + h + 1, :] * jnp.maximum(s[:, h * qt:(h + 1) * qt], 0.0)
            key_ref[pl.ds(off, kc), :] = jnp.where(off + sub <= t_q, _float_key(acc), INT_MIN)
        return carry

    lax.fori_loop(0, npair, score_pair, 0)

    def count(pred_fn):
        def body(c2, acc):
            for a in range(2):
                off = pl.multiple_of((2 * c2 + a) * kc, kc)
                hit = jnp.where(pred_fn(key_ref[pl.ds(off, kc), :], off + sub), 1, 0)
                acc = acc + jnp.sum(hit.reshape(kc // 8, 8, qt), axis=0)
            return acc
        acc = lax.fori_loop(0, npair, body, jnp.zeros((8, qt), I32))
        return jnp.sum(acc, axis=0, keepdims=True)

    def thr_bit(it, lo):
        cand = lo + jnp.left_shift(jnp.int32(1), 31 - it)
        cnt = count(lambda key, _: key >= cand)
        return jnp.where(cnt >= B_TOPK, cand, lo)

    thr = lax.fori_loop(0, 32, thr_bit, jnp.full((1, qt), INT_MIN, I32))
    need = B_TOPK - count(lambda key, _: key > thr)
    n_eq = count(lambda key, _: key == thr)

    def tie_search(_):
        def bit(it, j):
            cand = j + jnp.left_shift(jnp.int32(1), 12 - it)
            cnt = count(lambda key, idx: (key == thr) & (idx < cand))
            return jnp.where(cnt <= need, cand, j)
        return lax.fori_loop(0, 13, bit, jnp.zeros((1, qt), I32))

    any_tie = jnp.max(jnp.where(n_eq > need, 1, 0)) > 0
    j_max = lax.cond(any_tie, tie_search, lambda _: jnp.full((1, qt), 2 ** 13, I32), 0)

    def mask_chunk(c, carry):
        off = pl.multiple_of(c * kc, kc)
        key = key_ref[pl.ds(off, kc), :]
        idx = off + sub
        sel = ((key > thr) | ((key == thr) & (idx < j_max))) & (idx <= t_q)
        msk_ref[pl.ds(off, kc), :] = jnp.where(sel, 0.0, NEG)
        return carry

    lax.fori_loop(0, nch, mask_chunk, 0)

    grp = B_HEADS // B_KV_HEADS
    q = q_ref[...]
    groups = range(B_KV_HEADS)
    qg = [jnp.concatenate([q[:, (g * grp + j) * HEAD_DIM:(g * grp + j + 1) * HEAD_DIM]
                           for j in range(grp)], axis=0) for g in groups]
    ksl = [slice(g * HEAD_DIM, (g + 1) * HEAD_DIM) for g in groups]

    def flash(c, carry):
        m, l, acc = carry
        off = pl.multiple_of(c * kc, kc)
        d_lo = jnp.clip(qb - 2 * c, 0, B_NDELTA - 1)
        d_hi = jnp.clip(qb - 2 * c - 1, 0, B_NDELTA - 1)
        mk = msk_ref[pl.ds(off, kc), :]
        mk = jnp.concatenate([mk] * grp, axis=1)
        s = [lax.dot_general(k_ref[pl.ds(off, kc), ksl[g]], qg[g], nt, preferred_element_type=F32)
             for g in groups]
        s = [s[g] + jnp.concatenate([bias_ref[d_lo, g], bias_ref[d_hi, g]], axis=0) + mk for g in groups]
        if bounded:
            p = [jnp.exp2(s[g]) for g in groups]
        else:
            m_new = [jnp.maximum(m[g], jnp.max(s[g], axis=0, keepdims=True)) for g in groups]
            p = [jnp.exp2(s[g] - m_new[g]) for g in groups]
        pv = [jnp.dot(vt_ref[ksl[g], pl.ds(off, kc)], p[g].astype(BF16), preferred_element_type=F32)
              for g in groups]
        if bounded:
            l = [l[g] + jnp.sum(p[g], axis=0, keepdims=True) for g in groups]
            acc = [acc[g] + pv[g] for g in groups]
            return m, l, acc
        alpha = [jnp.exp2(m[g] - m_new[g]) for g in groups]
        l = [alpha[g] * l[g] + jnp.sum(p[g], axis=0, keepdims=True) for g in groups]
        acc = [alpha[g] * acc[g] + pv[g] for g in groups]
        return m_new, l, acc

    m0 = [jnp.full((1, grp * qt), NEG, F32) for _ in groups]
    l0 = [jnp.zeros((1, grp * qt), F32) for _ in groups]
    a0 = [jnp.zeros((HEAD_DIM, grp * qt), F32) for _ in groups]
    _, l, acc = lax.fori_loop(0, nch, flash, (m0, l0, a0))
    for g in groups:
        o = acc[g] / l[g]
        for j in range(grp):
            h = g * grp + j
            o_ref[:, h * HEAD_DIM:(h + 1) * HEAD_DIM] = o[:, j * qt:(j + 1) * qt].T.astype(o_ref.dtype)


def _toeplitz(vec, n_rows, n_cols):
    span = n_rows + n_cols - 1
    assert vec.shape[-1] == span
    lead = vec.shape[:-1]
    padded = jnp.concatenate([vec, jnp.zeros(lead + (1,), vec.dtype)], axis=-1)
    flat = jnp.tile(padded, n_rows)[..., :n_rows * span]
    return flat.reshape(lead + (n_rows, span))[..., n_rows - 1:]


def _bias_b(tab):
    grp = B_HEADS // B_KV_HEADS
    span = 2 * B_QT - 1
    dist = np.arange(B_NDELTA)[:, None] * B_QT + np.arange(span)[None, :] - (B_QT - 1)
    assert (B_NDELTA - 1) * B_QT - (B_QT - 1) >= 16 * 128 ** (15.0 / 16.0) + 1
    vec = tab[_t5_bucket(jnp.asarray(np.maximum(dist, 0), I32))].astype(F32) * LOG2E
    t = _toeplitz(vec.transpose(0, 2, 1), B_QT, B_QT)
    t = t.reshape(B_NDELTA, B_KV_HEADS, grp, B_QT, B_QT).transpose(0, 1, 3, 2, 4)
    return t.reshape(B_NDELTA, B_KV_HEADS, B_QT, grp * B_QT)


def _mixer_b(x2, gain, w_in, q_gain, k_gain, w_out, bias_b, batch, seq):
    d = x2.shape[1]
    nq, nkv = B_HEADS * HEAD_DIM, B_KV_HEADS * HEAD_DIM
    nidx = B_IDX_HEADS * B_IDX_DIM
    tn = 256
    used = 2 * nkv + nq + nidx + B_IDX_DIM + B_IDX_HEADS
    kout = -(-used // tn) * tn
    w = jnp.concatenate([w_in, jnp.zeros((d, kout - used), F32)], axis=1).astype(BF16)
    hg = jnp.concatenate([jnp.tile(q_gain * (HEAD_DIM ** -0.5 * LOG2E), B_HEADS), jnp.tile(k_gain, B_KV_HEADS),
                          jnp.ones((kout - nq - nkv,), F32)]).reshape(1, -1)
    p32, p16 = _in_proj(x2, gain, w, hg, n_norm_cols=nq + nkv, tn=tn, out_dtypes=(F32, BF16))
    p32 = p32.reshape(batch, seq, kout)
    p16 = p16.reshape(batch, seq, kout)
    assert seq % (2 * B_QT) == 0 and seq <= 2 ** 12
    kw_blk = (nq + 2 * nkv + nidx) // tn
    vt = p16[:, :, nq + nkv:nq + 2 * nkv].transpose(0, 2, 1)
    def attend(bias_t, bounded):
        return pl.pallas_call(
            functools.partial(_attn_b_kernel, bounded=bounded),
            out_shape=jax.ShapeDtypeStruct((batch, seq, nq), BF16),
            grid=(batch, seq // B_QT),
            in_specs=[
                pl.BlockSpec((None, B_QT, nq), lambda b, i: (b, i, 0)),
                pl.BlockSpec((None, seq, nkv), lambda b, i: (b, 0, nq // nkv)),
                pl.BlockSpec((None, nkv, seq), lambda b, i: (b, 0, 0)),
                pl.BlockSpec((None, B_QT, nidx), lambda b, i: (b, i, (nq + 2 * nkv) // nidx)),
                pl.BlockSpec((None, seq, tn), lambda b, i: (b, 0, kw_blk)),
                pl.BlockSpec((B_NDELTA, B_KV_HEADS, B_QT, (B_HEADS // B_KV_HEADS) * B_QT),
                             lambda b, i: (0, 0, 0, 0), pipeline_mode=pl.Buffered(1)),
            ],
            out_specs=pl.BlockSpec((None, B_QT, nq), lambda b, i: (b, i, 0)),
            scratch_shapes=[pltpu.VMEM((seq, B_QT), I32), pltpu.VMEM((seq, B_QT), F32)],
            compiler_params=_cparams(("parallel", "arbitrary")),
            name="attn_b_bounded" if bounded else "attn_b",
        )(p16, p16, vt, p32, p32, bias_t)

    bias_t = _bias_b(bias_b)
    qk_bound = 1.02 * HEAD_DIM * (HEAD_DIM ** -0.5 * LOG2E) * jnp.max(jnp.abs(q_gain)) * jnp.max(jnp.abs(k_gain))
    bound = qk_bound + jnp.max(bias_b) * LOG2E
    spread = bound + qk_bound - jnp.min(bias_b) * LOG2E
    y = lax.cond(spread <= B_SAFE_SPREAD,
                 lambda: attend(bias_t - bound, True),
                 lambda: attend(bias_t, False))
    return _out_proj(y.reshape(batch * seq, nq), w_out.astype(BF16), x2)


def _in_proj_c_kernel(x_ref, g_ref, w_ref, cw_ref, qkv_ref, rest_ref, tail_ref, xs_ref, *, tn):
    i = pl.program_id(1)
    tm = x_ref.shape[0]
    hw = C_HEADS * C_DK
    hn = _rms_rows(x_ref[...], g_ref[...]).astype(BF16)
    for j in range(w_ref.shape[1] // tn):
        cols = slice(j * tn, (j + 1) * tn)
        y = jnp.dot(hn, w_ref[:, cols], preferred_element_type=F32)
        if j * tn >= 3 * hw:
            rest_ref[:, j * tn - 3 * hw:(j + 1) * tn - 3 * hw] = y
            continue
        xs_ref[0:8, cols] = jnp.where(i > 0, tail_ref[:, cols], 0.0)
        xs_ref[8:, cols] = y
        tail_ref[:, cols] = y[tm - 8:, :]
        cw = cw_ref[:, cols]
        conv = sum(cw[t:t + 1, :] * xs_ref[8 - (C_CONV - 1) + t:8 - (C_CONV - 1) + t + tm, cols]
                   for t in range(C_CONV))
        z = conv * jax.nn.sigmoid(conv)
        if j * tn >= 2 * hw:
            qkv_ref[:, cols] = z
            continue
        scale = C_DK ** -0.5 if j * tn < hw else 1.0
        for h in range(tn // C_DK):
            zh = z[:, h * C_DK:(h + 1) * C_DK]
            ss = jnp.sum(zh * zh, axis=-1, keepdims=True)
            qkv_ref[:, j * tn + h * C_DK:j * tn + (h + 1) * C_DK] = zh * (lax.rsqrt(ss + EPS) * scale)


def _delta_kernel(q_ref, k_ref, v_ref, gate_ref, a_ref, b_ref, alog_ref, dtb_ref, og_ref, y_ref,
                  rows_s, p_s, r_s, m_s, n_s, el_s, o_s):
    cs, gs = C_CHUNK, C_GROUP
    per = gs // cs
    seq, dv = v_ref.shape
    n_groups = seq // gs
    hi = lax.Precision.HIGHEST
    r = lax.broadcasted_iota(I32, (gs, gs), 0)
    c_ = lax.broadcasted_iota(I32, (gs, gs), 1)

    def same_block(size):
        sh = int(math.log2(size))
        return (r >> sh) == (c_ >> sh)

    chunk = same_block(cs)
    lower, strict, eye = chunk & (r >= c_), chunk & (r > c_), r == c_

    z = a_ref[...] + dtb_ref[...]
    softplus = jnp.maximum(z, 0.0) + jnp.log(1.0 + jnp.exp(-jnp.abs(z)))
    g_all = -jnp.exp(alog_ref[...]) * softplus
    gc_all = jnp.dot(g_all, jnp.where(chunk & (r <= c_), 1.0, 0.0), preferred_element_type=F32, precision=hi)
    gl_all = jnp.dot(g_all, jnp.where(chunk, 1.0, 0.0), preferred_element_type=F32, precision=hi)
    rows_s[0] = jax.nn.sigmoid(b_ref[...])
    rows_s[1] = gc_all
    rows_s[2] = jnp.exp(gc_all)
    rows_s[3] = jnp.exp(gl_all - gc_all)
    first = (lax.broadcasted_iota(I32, (gs, per * dv), 0)
             == (lax.broadcasted_iota(I32, (gs, per * dv), 1) // dv) * cs)
    el_s[...] = jnp.exp(jnp.dot(gl_all, jnp.where(first, 1.0, 0.0), preferred_element_type=F32, precision=hi))

    def mm(a, b):
        return jnp.dot(a.astype(BF16), b.astype(BF16), preferred_element_type=F32)

    def mm_nt(a, b):
        return lax.dot_general(a.astype(BF16), b.astype(BF16), (((1,), (1,)), ((), ())),
                               preferred_element_type=F32)

    ways = 4

    def prep(m):
        ids = [m * ways + a for a in range(ways)]
        rows = [pl.ds(pl.multiple_of(i * gs, gs), gs) for i in ids]
        rw = [jnp.concatenate([rows_s[j, pl.ds(i, 1), :] for j in range(4)] + [jnp.zeros((4, gs), F32)], axis=0)
              for i in ids]
        cl = [x.T for x in rw]
        beta_col, gc_col, eg_col, ekg_col = ([x[:, j:j + 1] for x in cl] for j in range(4))
        q, k, v = ([ref[rw_, :] for rw_ in rows] for ref in (q_ref, k_ref, v_ref))
        decay = [jnp.where(lower, jnp.exp(jnp.where(lower, gcc - x[1:2, :], 0.0)), 0.0)
                 for gcc, x in zip(gc_col, rw)]
        kb = [k_ * b_ for k_, b_ in zip(k, beta_col)]
        yield
        a_mat = [jnp.where(strict, mm_nt(kb_, k_) * d_, 0.0) for kb_, k_, d_ in zip(kb, k, decay)]
        yield
        a8 = [jnp.where(same_block(8), a_, 0.0) for a_ in a_mat]
        t = [jnp.where(eye, 1.0, 0.0) - a_ for a_ in a8]
        pw = [mm(a_, a_) for a_ in a8]
        yield
        t = [t_ + mm(t_, p_) for t_, p_ in zip(t, pw)]
        yield
        pw = [mm(p_, p_) for p_ in pw]
        yield
        t = [t_ + mm(t_, p_) for t_, p_ in zip(t, pw)]
        yield
        size = 8
        while size < cs:
            sel = same_block(2 * size) & jnp.logical_not(same_block(size))
            nt = [mm(jnp.where(sel, a_, 0.0), t_) for a_, t_ in zip(a_mat, t)]
            yield
            t = [t_ - mm(t_, n_) for t_, n_ in zip(t, nt)]
            yield
            size *= 2
        uw = [mm(t_, jnp.concatenate([v_ * b_, kb_ * e_], axis=1))
              for t_, v_, b_, kb_, e_ in zip(t, v, beta_col, kb, eg_col)]
        yield
        qk = [jnp.where(lower, mm_nt(q_, k_) * d_, 0.0) for q_, k_, d_ in zip(q, k, decay)]
        yield
        qkwu = [mm(qk_, uw_) for qk_, uw_ in zip(qk, uw)]
        kgt = [(k_ * e_).T for k_, e_ in zip(k, ekg_col)]
        yield
        nm = [[mm(kgt[a][:, j * cs:(j + 1) * cs], uw[a][j * cs:(j + 1) * cs, :]) for j in range(per)]
              for a in range(ways)]

        def write():
            for a, i in enumerate(ids):
                r_s[rows[a], :] = qkwu[a][:, :dv]
                p_s[rows[a], :] = (q[a] * eg_col[a] - qkwu[a][:, dv:]).astype(BF16)
                for j in range(per):
                    n_s[i * per + j] = nm[a][j][:, :dv]
                    m_s[i * per + j] = nm[a][j][:, dv:].astype(BF16)
        return write

    def scan(m, state):
        for a in range(ways):
            i = m * ways + a
            el = el_s[pl.ds(i, 1), :]
            for j in range(per):
                rows = pl.ds(pl.multiple_of(i * gs + j * cs, cs), cs)
                c = i * per + j
                sb = state[0].astype(BF16)
                o_s[rows, :] = jnp.dot(p_s[rows, :], sb, preferred_element_type=F32) + r_s[rows, :]
                state[0] = (state[0] * el[:, j * dv:(j + 1) * dv] + n_s[c]
                            - jnp.dot(m_s[c], sb, preferred_element_type=F32))
                yield

    def weave(*gens):
        gens, results = list(gens), [None] * len(gens)
        live = list(range(len(gens)))
        while live:
            for idx in list(live):
                try:
                    next(gens[idx])
                except StopIteration as stop:
                    results[idx] = stop.value
                    live.remove(idx)
        return results

    n_blocks = n_groups // ways
    weave(prep(0))[0]()

    def step(m, state):
        holder = [state]
        _, write = weave(scan(m - 1, holder), prep(m))
        write()
        return holder[0]

    state = lax.fori_loop(1, n_blocks, step, jnp.zeros((C_DK, dv), F32))

    def finish(first, last):
        for t in range(first, last):
            rows = pl.ds(t * gs, gs)
            gate = gate_ref[rows, :]
            y_ref[rows, :] = (_rms_rows(o_s[rows, :], og_ref[...])
                              * (gate * jax.nn.sigmoid(gate))).astype(y_ref.dtype)
            yield

    done = (n_blocks - 1) * ways
    weave(scan(n_blocks - 1, [state]), finish(0, done))
    weave(finish(done, n_groups))


def _mixer_c(x2, gain, w_in, conv_w, a_log, dt_bias, o_gain, w_out, batch, seq):
    d = x2.shape[1]
    hw = C_HEADS * C_DK
    tn = 256
    used = 4 * hw + 2 * C_HEADS
    kout = -(-used // tn) * tn
    w = jnp.concatenate([w_in, jnp.zeros((d, kout - used), F32)], axis=1).astype(BF16)
    tm = 512
    const = lambda shape: pl.BlockSpec(shape, lambda b, i: (0, 0), pipeline_mode=pl.Buffered(1))
    qkv, rest = pl.pallas_call(
        functools.partial(_in_proj_c_kernel, tn=tn),
        out_shape=[jax.ShapeDtypeStruct((batch, seq, 3 * hw), F32),
                   jax.ShapeDtypeStruct((batch, seq, kout - 3 * hw), F32)],
        grid=(batch, seq // tm),
        in_specs=[pl.BlockSpec((None, tm, d), lambda b, i: (b, i, 0)), const((1, d)), const((d, kout)),
                  const((C_CONV, 3 * hw))],
        out_specs=[pl.BlockSpec((None, tm, 3 * hw), lambda b, i: (b, i, 0)),
                   pl.BlockSpec((None, tm, kout - 3 * hw), lambda b, i: (b, i, 0))],
        scratch_shapes=[pltpu.VMEM((8, 3 * hw), F32), pltpu.VMEM((tm + 8, 3 * hw), F32)],
        compiler_params=_cparams(("parallel", "arbitrary")),
        name="in_proj_c",
    )(x2.reshape(batch, seq, d), gain.reshape(1, d), w, conv_w)

    n_groups = seq // C_GROUP
    ba = rest[:, :, hw:hw + 2 * C_HEADS].transpose(0, 2, 1).reshape(batch, 2 * C_HEADS, n_groups, C_GROUP)
    per_head = lambda v: jnp.broadcast_to(v.reshape(C_HEADS, 1, 1), (C_HEADS, 1, C_GROUP))
    head_cols = lambda off: pl.BlockSpec((None, seq, C_DK), lambda b, h: (b, 0, off + h))
    small = lambda off: pl.BlockSpec((None, None, n_groups, C_GROUP), lambda b, h: (b, off + h, 0, 0))
    scalar_row = pl.BlockSpec((None, 1, C_GROUP), lambda b, h: (h, 0, 0))
    y = pl.pallas_call(
        _delta_kernel,
        out_shape=jax.ShapeDtypeStruct((batch, seq, hw), BF16),
        grid=(batch, C_HEADS),
        in_specs=[head_cols(0), head_cols(C_HEADS), head_cols(2 * C_HEADS),
                  head_cols(0),
                  small(C_HEADS), small(0), scalar_row, scalar_row,
                  pl.BlockSpec((1, C_DK), lambda b, h: (0, 0))],
        out_specs=head_cols(0),
        scratch_shapes=[
            pltpu.VMEM((4, n_groups, C_GROUP), F32),
            pltpu.VMEM((seq, C_DK), BF16),
            pltpu.VMEM((seq, C_DK), F32),
            pltpu.VMEM((seq // C_CHUNK, C_DK, C_DK), BF16),
            pltpu.VMEM((seq // C_CHUNK, C_DK, C_DK), F32),
            pltpu.VMEM((n_groups, (C_GROUP // C_CHUNK) * C_DK), F32),
            pltpu.VMEM((seq, C_DK), F32),
        ],
        compiler_params=_cparams(("parallel", "parallel")),
        name="delta_c",
    )(qkv, qkv, qkv, rest, ba, ba, per_head(a_log), per_head(dt_bias), o_gain.reshape(1, C_DK))
    return _out_proj(y.reshape(batch * seq, hw), w_out.astype(BF16), x2)


def kernel(x, rel_bias, norm_mix, norm_mlp, mlp_w1, mlp_w2, a_w_in, a_q_gain, a_k_gain, a_w_out,
           b_w_in, b_q_gain, b_k_gain, b_w_out, c_w_in, c_conv_w, c_a_log, c_dt_bias, c_o_gain, c_w_out):
    batch, seq, d = x.shape
    depth = norm_mix.shape[0]
    a_cols = len(A_GROUPS) * A_HEADS
    bias_a, bias_b = rel_bias[:, :a_cols], rel_bias[:, a_cols:]
    x2 = x.reshape(batch * seq, d)
    for i in range(depth):
        kind, j = i % 3, i // 3
        if kind == 0:
            x2 = _mixer_a(x2, norm_mix[i], a_w_in[j], a_q_gain[j], a_k_gain[j], a_w_out[j], bias_a, batch, seq)
        elif kind == 1:
            x2 = _mixer_b(x2, norm_mix[i], b_w_in[j], b_q_gain[j], b_k_gain[j], b_w_out[j], bias_b, batch, seq)
        else:
            x2 = _mixer_c(x2, norm_mix[i], c_w_in[j], c_conv_w[j], c_a_log[j], c_dt_bias[j], c_o_gain[j],
                          c_w_out[j], batch, seq)
        x2 = _mlp(x2, norm_mlp[i], mlp_w1[i].astype(BF16), mlp_w2[i].astype(BF16))
    return x2.reshape(batch, seq, d)
```

```python
import functools
import math

import numpy as np
import jax
import jax.numpy as jnp
from jax import lax
from jax.experimental import pallas as pl
from jax.experimental.pallas import tpu as pltpu

F32 = jnp.float32
BF16 = jnp.bfloat16
I32 = jnp.int32

EPS = 1e-6
HEAD_DIM = 64
NEG = -1e30
INT_MIN = -2 ** 31
LOG2E = math.log2(math.e)

V7X_VMEM_BYTES = 64 * 1024 * 1024
VMEM_LIMIT = V7X_VMEM_BYTES - 8 * 1024 * 1024
V7X_MXU_DIM = 256

NUM_BUCKETS = 32
MAX_DISTANCE = 2048

A_GROUPS = ((128, 1), (512, 4), (2048, 16))
A_HEADS = 8
A_BLOCK = 128
A_GW = A_HEADS * HEAD_DIM
A_STEP = 2

B_HEADS = 16
B_KV_HEADS = 4
B_IDX_HEADS = 8
B_IDX_DIM = 64
B_TOPK = 256
B_QT = 128
B_KC = 256
B_NDELTA = 14
B_SAFE_SPREAD = 100.0

C_HEADS = 8
C_DK = 128
C_CONV = 4
C_CHUNK = 64
C_GROUP = 4 * C_CHUNK


def _cparams(sem):
    return pltpu.CompilerParams(dimension_semantics=sem, vmem_limit_bytes=VMEM_LIMIT)


def _t5_bucket(dist):
    max_exact = NUM_BUCKETS // 2
    d = jnp.maximum(dist, 1).astype(F32)
    log_part = jnp.log(d / max_exact) / math.log(MAX_DISTANCE / max_exact) * (NUM_BUCKETS - max_exact)
    large = jnp.minimum(max_exact + log_part.astype(I32), NUM_BUCKETS - 1)
    return jnp.where(dist < max_exact, dist, large)


def _rms_rows(x, gain_row):
    ms = jnp.mean(x * x, axis=-1, keepdims=True)
    return x * lax.rsqrt(ms + EPS) * gain_row


def _in_proj_kernel(x_ref, g_ref, w_ref, hg_ref, bd_ref, *outs, n_norm, tn):
    hn = _rms_rows(x_ref[...], g_ref[...]).astype(BF16)
    for j in range(w_ref.shape[1] // tn):
        cols = slice(j * tn, (j + 1) * tn)
        y = jnp.dot(hn, w_ref[:, cols], preferred_element_type=F32)
        if j < n_norm:
            ms = jnp.dot((y * y).astype(BF16), bd_ref[...], preferred_element_type=F32) * (1.0 / HEAD_DIM)
            y = y * lax.rsqrt(ms + EPS) * hg_ref[:, cols]
        for o in outs:
            o[:, cols] = y.astype(o.dtype)


def _in_proj(x2, gain, w, head_gain, *, n_norm_cols, tn, out_dtypes, tm=512):
    n, d = x2.shape
    kout = w.shape[1]
    bd = np.kron(np.eye(tn // HEAD_DIM), np.ones((HEAD_DIM, HEAD_DIM))).astype(np.float32)
    const = lambda shape: pl.BlockSpec(shape, lambda i: (0, 0), pipeline_mode=pl.Buffered(1))
    return pl.pallas_call(
        functools.partial(_in_proj_kernel, n_norm=n_norm_cols // tn, tn=tn),
        out_shape=[jax.ShapeDtypeStruct((n, kout), dt) for dt in out_dtypes],
        grid=(n // tm,),
        in_specs=[pl.BlockSpec((tm, d), lambda i: (i, 0)), const((1, d)), const((d, kout)),
                  const((1, kout)), const((tn, tn))],
        out_specs=[pl.BlockSpec((tm, kout), lambda i: (i, 0)) for _ in out_dtypes],
        compiler_params=_cparams(("parallel",)),
        name="in_proj",
    )(x2, gain.reshape(1, d), w, head_gain, jnp.asarray(bd, BF16))


def _mix_groups_a(tm, o0, o1, o2, l0, l1, l2, pt4_ref, pt16_ref):

    def token_order(ref, pt_ref):
        v = ref[...]
        if pt_ref is None:
            return v.reshape(tm, A_GW)
        pb = pt_ref.shape[0]
        per = pb // v.shape[0]
        pieces = []
        for h in range(tm // pb):
            hi, lo = _split_bf16(v[:, h * per:(h + 1) * per, :].reshape(pb, A_GW))
            pieces.append(jnp.dot(pt_ref[...], hi, preferred_element_type=F32)
                          + jnp.dot(pt_ref[...], lo, preferred_element_type=F32))
        return jnp.concatenate(pieces, axis=0)

    pts = (None, pt4_ref, pt16_ref)
    a, b, c = (token_order(r, pt) for r, pt in zip((l0, l1, l2), pts))
    m = jnp.maximum(jnp.maximum(a, b), c)
    ea, eb, ec = jnp.exp2(a - m), jnp.exp2(b - m), jnp.exp2(c - m)
    oa, ob, oc = (token_order(r, pt) for r, pt in zip((o0, o1, o2), pts))
    return (ea * oa + eb * ob + ec * oc) / (ea + eb + ec)


def _proj_mlp_kernel(*refs, tf, groups_a):
    wo_ref, x_ref, g_ref, w1_ref, w2_ref, o_ref = refs[-6:]
    y = _mix_groups_a(x_ref.shape[0], *refs[:-6]) if groups_a else refs[0][...]
    x = x_ref[...] + jnp.dot(y.astype(BF16), wo_ref[...], preferred_element_type=F32)
    hn = _rms_rows(x, g_ref[...]).astype(BF16)
    acc = x
    for f in range(w1_ref.shape[1] // tf):
        cols = slice(f * tf, (f + 1) * tf)
        h = jnp.maximum(jnp.dot(hn, w1_ref[:, cols], preferred_element_type=F32), 0.0)
        acc = acc + jnp.dot((h * h).astype(BF16), w2_ref[cols, :], preferred_element_type=F32)
    o_ref[...] = acc


def _proj_mlp(mix, mix_specs, w_out, x2, gain, w1, w2, *, groups_a, tm, tf=1024):
    n, d = x2.shape
    dff = w1.shape[1]
    const = lambda shape: pl.BlockSpec(shape, lambda i: (0, 0), pipeline_mode=pl.Buffered(1))
    return pl.pallas_call(
        functools.partial(_proj_mlp_kernel, tf=tf, groups_a=groups_a),
        out_shape=jax.ShapeDtypeStruct((n, d), F32),
        grid=(n // tm,),
        in_specs=list(mix_specs) + [const(w_out.shape), pl.BlockSpec((tm, d), lambda i: (i, 0)), const((1, d)),
                                    const((d, dff)), const((dff, d))],
        out_specs=pl.BlockSpec((tm, d), lambda i: (i, 0)),
        compiler_params=_cparams(("parallel",)),
        name="proj_mlp_a" if groups_a else "proj_mlp",
    )(*mix, w_out, x2, gain.reshape(1, d), w1, w2)


def _residue_major(tm, dilation):
    p = np.zeros((tm, tm), np.float32)
    j, r = np.meshgrid(np.arange(tm // dilation), np.arange(dilation), indexing="ij")
    p[(r * (tm // dilation) + j).ravel(), (j * dilation + r).ravel()] = 1.0
    return p


def _in_proj_a_kernel(x_ref, g_ref, w_ref, hg_ref, bd_ref, *rest):
    n_groups = len(A_GROUPS)
    perms, outs = rest[:n_groups - 1], rest[n_groups - 1:]
    tm = x_ref.shape[0]
    hn = _rms_rows(x_ref[...], g_ref[...]).astype(BF16)
    for which in range(3):
        for g, (_, dilation) in enumerate(A_GROUPS):
            j = which * n_groups + g
            cols = slice(j * A_GW, (j + 1) * A_GW)
            y = jnp.dot(hn, w_ref[:, cols], preferred_element_type=F32)
            if which < 2:
                nb = bd_ref.shape[0]
                y2 = (y * y).astype(BF16)
                ms = jnp.concatenate([jnp.dot(y2[:, c:c + nb], bd_ref[...], preferred_element_type=F32)
                                      for c in range(0, A_GW, nb)], axis=1) * (1.0 / HEAD_DIM)
                y = y * lax.rsqrt(ms + EPS) * hg_ref[:, cols]
            y = y.astype(BF16)
            if dilation == 1:
                y = y.reshape(1, tm, A_GW)
            else:
                perm = perms[g - 1]
                pb = perm.shape[0]
                y = jnp.concatenate(
                    [jnp.dot(perm[...], y[h:h + pb], preferred_element_type=F32).astype(BF16)
                     .reshape(dilation, pb // dilation, A_GW) for h in range(0, tm, pb)], axis=1)
            outs[g][:, :, which * A_GW:(which + 1) * A_GW] = y


def _attn_a_kernel(q_ref, kp_ref, kc_ref, vp_ref, vc_ref, bias_ref, o_ref, lse_ref):
    n = pl.program_id(2)
    blk = A_BLOCK
    pair = 2 * HEAD_DIM
    lane = lax.broadcasted_iota(I32, (2 * blk, pair), 1)
    row = lax.broadcasted_iota(I32, (2 * blk, blk), 0)
    pen = jnp.where((row < blk) & (n == 0), NEG, 0.0)
    nt = (((1,), (1,)), ((), ()))
    tn = (((0,), (0,)), ((), ()))
    k_all = jnp.concatenate([kp_ref[...], kc_ref[...]], axis=0)
    v_all = jnp.concatenate([vp_ref[...], vc_ref[...]], axis=0)
    q_all = q_ref[...]
    chains = [(sb, hp, a) for sb in range(A_STEP) for hp in range(A_HEADS // 2) for a in range(2)]

    def window(x, sb, hp):
        return x[sb * blk:(sb + 2) * blk, hp * pair:(hp + 1) * pair]

    keep = [lane < HEAD_DIM, lane >= HEAD_DIM]
    ka = [jnp.where(keep[a], window(k_all, sb, hp), jnp.zeros((2 * blk, pair), BF16))
          for sb, hp, a in chains]
    s = [lax.dot_general(k_, q_all[sb * blk:(sb + 1) * blk, hp * pair:(hp + 1) * pair], nt,
                         preferred_element_type=F32) + bias_ref[2 * hp + a]
         for k_, (sb, hp, a) in zip(ka, chains)]
    s = [s_ + pen if sb == 0 else s_ for s_, (sb, _, _) in zip(s, chains)]
    m = [jnp.max(s_, axis=0, keepdims=True) for s_ in s]
    p = [jnp.exp2(s_ - m_) for s_, m_ in zip(s, m)]
    l = [jnp.sum(p_, axis=0, keepdims=True) for p_ in p]
    pv = [lax.dot_general(window(v_all, sb, hp), p_.astype(BF16), tn, preferred_element_type=F32)
          for p_, (sb, hp, _) in zip(p, chains)]
    for idx in range(0, len(chains), 2):
        sb, hp, _ = chains[idx]
        o_t = jnp.concatenate([(pv[idx + a] / l[idx + a])[a * HEAD_DIM:(a + 1) * HEAD_DIM] for a in range(2)],
                              axis=0)
        l_t = jnp.concatenate([jnp.broadcast_to(m[idx + a] + jnp.log2(l[idx + a]), (HEAD_DIM, blk))
                               for a in range(2)], axis=0)
        o_ref[sb * blk:(sb + 1) * blk, hp * pair:(hp + 1) * pair] = o_t.T
        lse_ref[sb * blk:(sb + 1) * blk, hp * pair:(hp + 1) * pair] = l_t.T


def _attn_a_group(arr, bias_t, dilation, batch, seq):
    sub = seq // dilation
    rows = A_STEP * A_BLOCK
    nb = sub // rows

    def spec(which, prev):
        if prev:
            return pl.BlockSpec((None, None, A_BLOCK, A_GW),
                                lambda b, r, n: (b, r, jnp.maximum(n * A_STEP - 1, 0), which))
        return pl.BlockSpec((None, None, rows, A_GW), lambda b, r, n: (b, r, n, which))

    out_spec = pl.BlockSpec((None, None, rows, A_GW), lambda b, r, n: (b, r, n, 0))
    out_sds = jax.ShapeDtypeStruct((batch, dilation, sub, A_GW), F32)
    return pl.pallas_call(
        _attn_a_kernel,
        out_shape=[out_sds, out_sds],
        grid=(batch, dilation, nb),
        in_specs=[spec(0, False), spec(1, True), spec(1, False), spec(2, True), spec(2, False),
                  pl.BlockSpec((A_HEADS, 2 * A_BLOCK, A_BLOCK), lambda b, r, n: (0, 0, 0))],
        out_specs=[out_spec, out_spec],
        compiler_params=_cparams(("parallel", "parallel", "arbitrary")),
        name=f"attn_a_d{dilation}",
    )(arr, arr, arr, arr, arr, bias_t)


def _bias_a(tab_g, dilation):
    step = np.arange(3 * A_BLOCK - 1) - (A_BLOCK - 1)
    vec = tab_g[_t5_bucket(jnp.asarray(np.maximum(step, 0) * dilation, I32))].astype(F32) * LOG2E
    vec = jnp.where(jnp.asarray((step >= 0) & (step <= A_BLOCK))[:, None], vec, NEG)
    return _toeplitz(vec.T, 2 * A_BLOCK, A_BLOCK)


def _mixer_a(x2, gain, w_in, q_gain, k_gain, bias_a, batch, seq, tm):
    n, d = x2.shape
    n_groups = len(A_GROUPS)
    reps = n_groups * A_HEADS
    hg = jnp.concatenate([jnp.tile(q_gain * (HEAD_DIM ** -0.5 * LOG2E), reps), jnp.tile(k_gain, reps),
                          jnp.ones((n_groups * A_GW,), F32)]).reshape(1, -1)
    kout = 3 * n_groups * A_GW
    pb = V7X_MXU_DIM
    bd = np.kron(np.eye(pb // HEAD_DIM), np.ones((HEAD_DIM, HEAD_DIM))).astype(np.float32)
    perms = [_residue_major(pb, dil) for _, dil in A_GROUPS[1:]]
    tiles = seq // tm
    const = lambda shape: pl.BlockSpec(shape, lambda i: (0, 0), pipeline_mode=pl.Buffered(1))
    grouped = lambda dil, width: pl.BlockSpec((None, dil, tm // dil, width),
                                              lambda i: (i // tiles, 0, i % tiles, 0))
    arrs = pl.pallas_call(
        _in_proj_a_kernel,
        out_shape=[jax.ShapeDtypeStruct((batch, dil, seq // dil, 3 * A_GW), BF16) for _, dil in A_GROUPS],
        grid=(n // tm,),
        in_specs=[pl.BlockSpec((tm, d), lambda i: (i, 0)), const((1, d)), const((d, kout)),
                  const((1, kout)), const((pb, pb))] + [const((pb, pb)) for _ in perms],
        out_specs=[grouped(dil, 3 * A_GW) for _, dil in A_GROUPS],
        compiler_params=_cparams(("parallel",)),
        name="in_proj_a",
    )(x2, gain.reshape(1, d), w_in.astype(BF16), hg, jnp.asarray(bd, BF16),
      *[jnp.asarray(p, BF16) for p in perms])

    os_, ls_ = [], []
    for g, (window, dilation) in enumerate(A_GROUPS):
        assert window // dilation == A_BLOCK and (seq // dilation) % (A_STEP * A_BLOCK) == 0 and dilation <= pb // 16
        tab_g = bias_a[:, g * A_HEADS:(g + 1) * A_HEADS]
        o, lse = _attn_a_group(arrs[g], _bias_a(tab_g, dilation), dilation, batch, seq)
        os_.append(o)
        ls_.append(lse)

    mix = [*os_, *ls_, *[jnp.asarray(p.T, BF16) for p in perms]]
    specs = [grouped(dil, A_GW) for _, dil in A_GROUPS] * 2 + [const((pb, pb)) for _ in perms]
    return mix, specs


def _float_key(s):
    i = lax.bitcast_convert_type(s, I32)
    k = jnp.where(i < 0, i ^ jnp.int32(0x7FFFFFFF), i)
    return jnp.where(s == 0.0, 0, k)


def _split_bf16(x):
    hi = x.astype(BF16)
    return hi, (x - hi.astype(F32)).astype(BF16)


def _attn_b_kernel(q_ref, k_ref, vt_ref, qi_ref, kw_ref, bias_ref, o_ref, key_ref, msk_ref, *, bounded):
    qb = pl.program_id(1)
    kc, qt = B_KC, B_QT
    nch = qb // 2 + 1
    t_q = qb * qt + lax.broadcasted_iota(I32, (kc, qt), 1)
    sub = lax.broadcasted_iota(I32, (kc, qt), 0)
    nt = (((1,), (1,)), ((), ()))

    qh, ql = _split_bf16(qi_ref[...])
    qi3 = jnp.concatenate(
        [jnp.concatenate([x[:, h * B_IDX_DIM:(h + 1) * B_IDX_DIM] for x in (qh, qh, ql)], axis=1)
         for h in range(B_IDX_HEADS)], axis=0)
    wt = kw_ref[pl.ds(pl.multiple_of(qb * qt, qt), qt), 0:qt].T
    wt = wt * (B_IDX_HEADS ** -0.5 * B_IDX_DIM ** -0.5)

    npair = (nch + 1) // 2

    def score_pair(c2, carry):
        offs = [pl.multiple_of((2 * c2 + a) * kc, kc) for a in range(2)]
        ks = [_split_bf16(kw_ref[pl.ds(off, kc), 0:B_IDX_DIM]) for off in offs]
        sc = [lax.dot_general(jnp.concatenate([kh, kl, kh], axis=1), qi3, nt, preferred_element_type=F32)
              for kh, kl in ks]
        for off, s in zip(offs, sc):
            acc = jnp.zeros((kc, qt), F32)
            for h in range(B_IDX_HEADS):
                acc = acc + wt[B_IDX_DIM + h:B_IDX_DIM + h + 1, :] * jnp.maximum(s[:, h * qt:(h + 1) * qt], 0.0)
            key_ref[pl.ds(off, kc), :] = jnp.where(off + sub <= t_q, _float_key(acc), INT_MIN)
        return carry

    lax.fori_loop(0, npair, score_pair, 0)

    def count(pred_fn):
        def body(c2, acc):
            for a in range(2):
                off = pl.multiple_of((2 * c2 + a) * kc, kc)
                hit = jnp.where(pred_fn(key_ref[pl.ds(off, kc), :], off + sub), 1, 0)
                acc = acc + jnp.sum(hit.reshape(kc // 8, 8, qt), axis=0)
            return acc
        acc = lax.fori_loop(0, npair, body, jnp.zeros((8, qt), I32))
        return jnp.sum(acc, axis=0, keepdims=True)

    def thr_bit(it, lo):
        cand = lo + jnp.left_shift(jnp.int32(1), 31 - it)
        cnt = count(lambda key, _: key >= cand)
        return jnp.where(cnt >= B_TOPK, cand, lo)

    thr = lax.fori_loop(0, 32, thr_bit, jnp.full((1, qt), INT_MIN, I32))
    need = B_TOPK - count(lambda key, _: key > thr)
    n_eq = count(lambda key, _: key == thr)

    def tie_search(_):
        def bit(it, j):
            cand = j + jnp.left_shift(jnp.int32(1), 12 - it)
            cnt = count(lambda key, idx: (key == thr) & (idx < cand))
            return jnp.where(cnt <= need, cand, j)
        return lax.fori_loop(0, 13, bit, jnp.zeros((1, qt), I32))

    any_tie = jnp.max(jnp.where(n_eq > need, 1, 0)) > 0
    j_max = lax.cond(any_tie, tie_search, lambda _: jnp.full((1, qt), 2 ** 13, I32), 0)

    def mask_chunk(c, carry):
        off = pl.multiple_of(c * kc, kc)
        key = key_ref[pl.ds(off, kc), :]
        idx = off + sub
        sel = ((key > thr) | ((key == thr) & (idx < j_max))) & (idx <= t_q)
        msk_ref[pl.ds(off, kc), :] = jnp.where(sel, 0.0, NEG)
        return carry

    lax.fori_loop(0, nch, mask_chunk, 0)

    grp = B_HEADS // B_KV_HEADS
    q = q_ref[...]
    groups = range(B_KV_HEADS)
    qg = [jnp.concatenate([q[:, (g * grp + j) * HEAD_DIM:(g * grp + j + 1) * HEAD_DIM]
                           for j in range(grp)], axis=0) for g in groups]
    ksl = [slice(g * HEAD_DIM, (g + 1) * HEAD_DIM) for g in groups]

    def flash(c, carry):
        m, l, acc = carry
        off = pl.multiple_of(c * kc, kc)
        d_lo = jnp.clip(qb - 2 * c, 0, B_NDELTA - 1)
        d_hi = jnp.clip(qb - 2 * c - 1, 0, B_NDELTA - 1)
        mk = msk_ref[pl.ds(off, kc), :]
        mk = jnp.concatenate([mk] * grp, axis=1)
        s = [lax.dot_general(k_ref[pl.ds(off, kc), ksl[g]], qg[g], nt, preferred_element_type=F32)
             for g in groups]
        s = [s[g] + jnp.concatenate([bias_ref[d_lo, g], bias_ref[d_hi, g]], axis=0) + mk for g in groups]
        if bounded:
            p = [jnp.exp2(s[g]) for g in groups]
        else:
            m_new = [jnp.maximum(m[g], jnp.max(s[g], axis=0, keepdims=True)) for g in groups]
            p = [jnp.exp2(s[g] - m_new[g]) for g in groups]
        pv = [jnp.dot(vt_ref[ksl[g], pl.ds(off, kc)], p[g].astype(BF16), preferred_element_type=F32)
              for g in groups]
        if bounded:
            l = [l[g] + jnp.sum(p[g], axis=0, keepdims=True) for g in groups]
            acc = [acc[g] + pv[g] for g in groups]
            return m, l, acc
        alpha = [jnp.exp2(m[g] - m_new[g]) for g in groups]
        l = [alpha[g] * l[g] + jnp.sum(p[g], axis=0, keepdims=True) for g in groups]
        acc = [alpha[g] * acc[g] + pv[g] for g in groups]
        return m_new, l, acc

    m0 = [jnp.full((1, grp * qt), NEG, F32) for _ in groups]
    l0 = [jnp.zeros((1, grp * qt), F32) for _ in groups]
    a0 = [jnp.zeros((HEAD_DIM, grp * qt), F32) for _ in groups]
    _, l, acc = lax.fori_loop(0, nch, flash, (m0, l0, a0))
    for g in groups:
        o = acc[g] / l[g]
        for j in range(grp):
            h = g * grp + j
            o_ref[:, h * HEAD_DIM:(h + 1) * HEAD_DIM] = o[:, j * qt:(j + 1) * qt].T.astype(o_ref.dtype)


def _toeplitz(vec, n_rows, n_cols):
    span = n_rows + n_cols - 1
    assert vec.shape[-1] == span
    lead = vec.shape[:-1]
    padded = jnp.concatenate([vec, jnp.zeros(lead + (1,), vec.dtype)], axis=-1)
    flat = jnp.tile(padded, n_rows)[..., :n_rows * span]
    return flat.reshape(lead + (n_rows, span))[..., n_rows - 1:]


def _bias_b(tab):
    grp = B_HEADS // B_KV_HEADS
    span = 2 * B_QT - 1
    dist = np.arange(B_NDELTA)[:, None] * B_QT + np.arange(span)[None, :] - (B_QT - 1)
    assert (B_NDELTA - 1) * B_QT - (B_QT - 1) >= 16 * 128 ** (15.0 / 16.0) + 1
    vec = tab[_t5_bucket(jnp.asarray(np.maximum(dist, 0), I32))].astype(F32) * LOG2E
    t = _toeplitz(vec.transpose(0, 2, 1), B_QT, B_QT)
    t = t.reshape(B_NDELTA, B_KV_HEADS, grp, B_QT, B_QT).transpose(0, 1, 3, 2, 4)
    return t.reshape(B_NDELTA, B_KV_HEADS, B_QT, grp * B_QT)


def _mixer_b(x2, gain, w_in, q_gain, k_gain, bias_b, batch, seq, tm):
    d = x2.shape[1]
    nq, nkv = B_HEADS * HEAD_DIM, B_KV_HEADS * HEAD_DIM
    nidx = B_IDX_HEADS * B_IDX_DIM
    tn = 256
    used = 2 * nkv + nq + nidx + B_IDX_DIM + B_IDX_HEADS
    kout = -(-used // tn) * tn
    w = jnp.concatenate([w_in, jnp.zeros((d, kout - used), F32)], axis=1).astype(BF16)
    hg = jnp.concatenate([jnp.tile(q_gain * (HEAD_DIM ** -0.5 * LOG2E), B_HEADS), jnp.tile(k_gain, B_KV_HEADS),
                          jnp.ones((kout - nq - nkv,), F32)]).reshape(1, -1)
    p32, p16 = _in_proj(x2, gain, w, hg, n_norm_cols=nq + nkv, tn=tn, out_dtypes=(F32, BF16))
    p32 = p32.reshape(batch, seq, kout)
    p16 = p16.reshape(batch, seq, kout)
    assert seq % (2 * B_QT) == 0 and seq <= 2 ** 12
    kw_blk = (nq + 2 * nkv + nidx) // tn
    vt = p16[:, :, nq + nkv:nq + 2 * nkv].transpose(0, 2, 1)
    def attend(bias_t, bounded):
        return pl.pallas_call(
            functools.partial(_attn_b_kernel, bounded=bounded),
            out_shape=jax.ShapeDtypeStruct((batch, seq, nq), BF16),
            grid=(batch, seq // B_QT),
            in_specs=[
                pl.BlockSpec((None, B_QT, nq), lambda b, i: (b, i, 0)),
                pl.BlockSpec((None, seq, nkv), lambda b, i: (b, 0, nq // nkv)),
                pl.BlockSpec((None, nkv, seq), lambda b, i: (b, 0, 0)),
                pl.BlockSpec((None, B_QT, nidx), lambda b, i: (b, i, (nq + 2 * nkv) // nidx)),
                pl.BlockSpec((None, seq, tn), lambda b, i: (b, 0, kw_blk)),
                pl.BlockSpec((B_NDELTA, B_KV_HEADS, B_QT, (B_HEADS // B_KV_HEADS) * B_QT),
                             lambda b, i: (0, 0, 0, 0), pipeline_mode=pl.Buffered(1)),
            ],
            out_specs=pl.BlockSpec((None, B_QT, nq), lambda b, i: (b, i, 0)),
            scratch_shapes=[pltpu.VMEM((seq, B_QT), I32), pltpu.VMEM((seq, B_QT), F32)],
            compiler_params=_cparams(("parallel", "arbitrary")),
            name="attn_b_bounded" if bounded else "attn_b",
        )(p16, p16, vt, p32, p32, bias_t)

    bias_t = _bias_b(bias_b)
    qk_bound = 1.02 * HEAD_DIM * (HEAD_DIM ** -0.5 * LOG2E) * jnp.max(jnp.abs(q_gain)) * jnp.max(jnp.abs(k_gain))
    bound = qk_bound + jnp.max(bias_b) * LOG2E
    spread = bound + qk_bound - jnp.min(bias_b) * LOG2E
    y = lax.cond(spread <= B_SAFE_SPREAD,
                 lambda: attend(bias_t - bound, True),
                 lambda: attend(bias_t, False))
    return [y.reshape(batch * seq, nq)], [pl.BlockSpec((tm, nq), lambda i: (i, 0))]


def _in_proj_c_kernel(x_ref, g_ref, w_ref, cw_ref, qkv_ref, rest_ref, tail_ref, xs_ref, *, tn):
    i = pl.program_id(1)
    tm = x_ref.shape[0]
    hw = C_HEADS * C_DK
    hn = _rms_rows(x_ref[...], g_ref[...]).astype(BF16)
    for j in range(w_ref.shape[1] // tn):
        cols = slice(j * tn, (j + 1) * tn)
        y = jnp.dot(hn, w_ref[:, cols], preferred_element_type=F32)
        if j * tn >= 3 * hw:
            rest_ref[:, j * tn - 3 * hw:(j + 1) * tn - 3 * hw] = y
            continue
        xs_ref[0:8, cols] = jnp.where(i > 0, tail_ref[:, cols], 0.0)
        xs_ref[8:, cols] = y
        tail_ref[:, cols] = y[tm - 8:, :]
        cw = cw_ref[:, cols]
        conv = sum(cw[t:t + 1, :] * xs_ref[8 - (C_CONV - 1) + t:8 - (C_CONV - 1) + t + tm, cols]
                   for t in range(C_CONV))
        z = conv * jax.nn.sigmoid(conv)
        if j * tn >= 2 * hw:
            qkv_ref[:, cols] = z
            continue
        scale = C_DK ** -0.5 if j * tn < hw else 1.0
        for h in range(tn // C_DK):
            zh = z[:, h * C_DK:(h + 1) * C_DK]
            ss = jnp.sum(zh * zh, axis=-1, keepdims=True)
            qkv_ref[:, j * tn + h * C_DK:j * tn + (h + 1) * C_DK] = zh * (lax.rsqrt(ss + EPS) * scale)


def _delta_kernel(q_ref, k_ref, v_ref, gate_ref, a_ref, b_ref, alog_ref, dtb_ref, og_ref, y_ref,
                  rows_s, p_s, r_s, m_s, n_s, el_s, o_s):
    cs, gs = C_CHUNK, C_GROUP
    per = gs // cs
    seq, dv = v_ref.shape
    n_groups = seq // gs
    hi = lax.Precision.HIGHEST
    r = lax.broadcasted_iota(I32, (gs, gs), 0)
    c_ = lax.broadcasted_iota(I32, (gs, gs), 1)

    def same_block(size):
        sh = int(math.log2(size))
        return (r >> sh) == (c_ >> sh)

    chunk = same_block(cs)
    lower, strict, eye = chunk & (r >= c_), chunk & (r > c_), r == c_

    z = a_ref[...] + dtb_ref[...]
    softplus = jnp.maximum(z, 0.0) + jnp.log(1.0 + jnp.exp(-jnp.abs(z)))
    g_all = -jnp.exp(alog_ref[...]) * softplus
    gc_all = jnp.dot(g_all, jnp.where(chunk & (r <= c_), 1.0, 0.0), preferred_element_type=F32, precision=hi)
    gl_all = jnp.dot(g_all, jnp.where(chunk, 1.0, 0.0), preferred_element_type=F32, precision=hi)
    rows_s[0] = jax.nn.sigmoid(b_ref[...])
    rows_s[1] = gc_all
    rows_s[2] = jnp.exp(gc_all)
    rows_s[3] = jnp.exp(gl_all - gc_all)
    first = (lax.broadcasted_iota(I32, (gs, per * dv), 0)
             == (lax.broadcasted_iota(I32, (gs, per * dv), 1) // dv) * cs)
    el_s[...] = jnp.exp(jnp.dot(gl_all, jnp.where(first, 1.0, 0.0), preferred_element_type=F32, precision=hi))

    def mm(a, b):
        return jnp.dot(a.astype(BF16), b.astype(BF16), preferred_element_type=F32)

    def mm_nt(a, b):
        return lax.dot_general(a.astype(BF16), b.astype(BF16), (((1,), (1,)), ((), ())),
                               preferred_element_type=F32)

    ways = 4

    def prep(m):
        ids = [m * ways + a for a in range(ways)]
        rows = [pl.ds(pl.multiple_of(i * gs, gs), gs) for i in ids]
        rw = [jnp.concatenate([rows_s[j, pl.ds(i, 1), :] for j in range(4)] + [jnp.zeros((4, gs), F32)], axis=0)
              for i in ids]
        cl = [x.T for x in rw]
        beta_col, gc_col, eg_col, ekg_col = ([x[:, j:j + 1] for x in cl] for j in range(4))
        q, k, v = ([ref[rw_, :] for rw_ in rows] for ref in (q_ref, k_ref, v_ref))
        decay = [jnp.where(lower, jnp.exp(jnp.where(lower, gcc - x[1:2, :], 0.0)), 0.0)
                 for gcc, x in zip(gc_col, rw)]
        kb = [k_ * b_ for k_, b_ in zip(k, beta_col)]
        yield
        a_mat = [jnp.where(strict, mm_nt(kb_, k_) * d_, 0.0) for kb_, k_, d_ in zip(kb, k, decay)]
        yield
        a8 = [jnp.where(same_block(8), a_, 0.0) for a_ in a_mat]
        t = [jnp.where(eye, 1.0, 0.0) - a_ for a_ in a8]
        pw = [mm(a_, a_) for a_ in a8]
        yield
        t = [t_ + mm(t_, p_) for t_, p_ in zip(t, pw)]
        yield
        pw = [mm(p_, p_) for p_ in pw]
        yield
        t = [t_ + mm(t_, p_) for t_, p_ in zip(t, pw)]
        yield
        size = 8
        while size < cs:
            sel = same_block(2 * size) & jnp.logical_not(same_block(size))
            nt = [mm(jnp.where(sel, a_, 0.0), t_) for a_, t_ in zip(a_mat, t)]
            yield
            t = [t_ - mm(t_, n_) for t_, n_ in zip(t, nt)]
            yield
            size *= 2
        uw = [mm(t_, jnp.concatenate([v_ * b_, kb_ * e_], axis=1))
              for t_, v_, b_, kb_, e_ in zip(t, v, beta_col, kb, eg_col)]
        yield
        qk = [jnp.where(lower, mm_nt(q_, k_) * d_, 0.0) for q_, k_, d_ in zip(q, k, decay)]
        yield
        qkwu = [mm(qk_, uw_) for qk_, uw_ in zip(qk, uw)]
        kgt = [(k_ * e_).T for k_, e_ in zip(k, ekg_col)]
        yield
        nm = [[mm(kgt[a][:, j * cs:(j + 1) * cs], uw[a][j * cs:(j + 1) * cs, :]) for j in range(per)]
              for a in range(ways)]

        def write():
            for a, i in enumerate(ids):
                r_s[rows[a], :] = qkwu[a][:, :dv]
                p_s[rows[a], :] = (q[a] * eg_col[a] - qkwu[a][:, dv:]).astype(BF16)
                for j in range(per):
                    n_s[i * per + j] = nm[a][j][:, :dv]
                    m_s[i * per + j] = nm[a][j][:, dv:].astype(BF16)
        return write

    def scan(m, state):
        for a in range(ways):
            i = m * ways + a
            el = el_s[pl.ds(i, 1), :]
            for j in range(per):
                rows = pl.ds(pl.multiple_of(i * gs + j * cs, cs), cs)
                c = i * per + j
                sb = state[0].astype(BF16)
                o_s[rows, :] = jnp.dot(p_s[rows, :], sb, preferred_element_type=F32) + r_s[rows, :]
                state[0] = (state[0] * el[:, j * dv:(j + 1) * dv] + n_s[c]
                            - jnp.dot(m_s[c], sb, preferred_element_type=F32))
                yield

    def weave(*gens):
        gens, results = list(gens), [None] * len(gens)
        live = list(range(len(gens)))
        while live:
            for idx in list(live):
                try:
                    next(gens[idx])
                except StopIteration as stop:
                    results[idx] = stop.value
                    live.remove(idx)
        return results

    n_blocks = n_groups // ways
    weave(prep(0))[0]()

    def step(m, state):
        holder = [state]
        _, write = weave(scan(m - 1, holder), prep(m))
        write()
        return holder[0]

    state = lax.fori_loop(1, n_blocks, step, jnp.zeros((C_DK, dv), F32))

    def finish(first, last):
        for t in range(first, last):
            rows = pl.ds(t * gs, gs)
            gate = gate_ref[rows, :]
            y_ref[rows, :] = (_rms_rows(o_s[rows, :], og_ref[...])
                              * (gate * jax.nn.sigmoid(gate))).astype(y_ref.dtype)
            yield

    done = (n_blocks - 1) * ways
    weave(scan(n_blocks - 1, [state]), finish(0, done))
    weave(finish(done, n_groups))


def _mixer_c(x2, gain, w_in, conv_w, a_log, dt_bias, o_gain, batch, seq, tm):
    d = x2.shape[1]
    hw = C_HEADS * C_DK
    tn = 256
    used = 4 * hw + 2 * C_HEADS
    kout = -(-used // tn) * tn
    w = jnp.concatenate([w_in, jnp.zeros((d, kout - used), F32)], axis=1).astype(BF16)
    const = lambda shape: pl.BlockSpec(shape, lambda b, i: (0, 0), pipeline_mode=pl.Buffered(1))
    qkv, rest = pl.pallas_call(
        functools.partial(_in_proj_c_kernel, tn=tn),
        out_shape=[jax.ShapeDtypeStruct((batch, seq, 3 * hw), F32),
                   jax.ShapeDtypeStruct((batch, seq, kout - 3 * hw), F32)],
        grid=(batch, seq // tm),
        in_specs=[pl.BlockSpec((None, tm, d), lambda b, i: (b, i, 0)), const((1, d)), const((d, kout)),
                  const((C_CONV, 3 * hw))],
        out_specs=[pl.BlockSpec((None, tm, 3 * hw), lambda b, i: (b, i, 0)),
                   pl.BlockSpec((None, tm, kout - 3 * hw), lambda b, i: (b, i, 0))],
        scratch_shapes=[pltpu.VMEM((8, 3 * hw), F32), pltpu.VMEM((tm + 8, 3 * hw), F32)],
        compiler_params=_cparams(("parallel", "arbitrary")),
        name="in_proj_c",
    )(x2.reshape(batch, seq, d), gain.reshape(1, d), w, conv_w)

    n_groups = seq // C_GROUP
    ba = rest[:, :, hw:hw + 2 * C_HEADS].transpose(0, 2, 1).reshape(batch, 2 * C_HEADS, n_groups, C_GROUP)
    per_head = lambda v: jnp.broadcast_to(v.reshape(C_HEADS, 1, 1), (C_HEADS, 1, C_GROUP))
    head_cols = lambda off: pl.BlockSpec((None, seq, C_DK), lambda b, h: (b, 0, off + h))
    small = lambda off: pl.BlockSpec((None, None, n_groups, C_GROUP), lambda b, h: (b, off + h, 0, 0))
    scalar_row = pl.BlockSpec((None, 1, C_GROUP), lambda b, h: (h, 0, 0))
    y = pl.pallas_call(
        _delta_kernel,
        out_shape=jax.ShapeDtypeStruct((batch, seq, hw), BF16),
        grid=(batch, C_HEADS),
        in_specs=[head_cols(0), head_cols(C_HEADS), head_cols(2 * C_HEADS),
                  head_cols(0),
                  small(C_HEADS), small(0), scalar_row, scalar_row,
                  pl.BlockSpec((1, C_DK), lambda b, h: (0, 0))],
        out_specs=head_cols(0),
        scratch_shapes=[
            pltpu.VMEM((4, n_groups, C_GROUP), F32),
            pltpu.VMEM((seq, C_DK), BF16),
            pltpu.VMEM((seq, C_DK), F32),
            pltpu.VMEM((seq // C_CHUNK, C_DK, C_DK), BF16),
            pltpu.VMEM((seq // C_CHUNK, C_DK, C_DK), F32),
            pltpu.VMEM((n_groups, (C_GROUP // C_CHUNK) * C_DK), F32),
            pltpu.VMEM((seq, C_DK), F32),
        ],
        compiler_params=_cparams(("parallel", "parallel")),
        name="delta_c",
    )(qkv, qkv, qkv, rest, ba, ba, per_head(a_log), per_head(dt_bias), o_gain.reshape(1, C_DK))
    return [y.reshape(batch * seq, hw)], [pl.BlockSpec((tm, hw), lambda i: (i, 0))]


def kernel(x, rel_bias, norm_mix, norm_mlp, mlp_w1, mlp_w2, a_w_in, a_q_gain, a_k_gain, a_w_out,
           b_w_in, b_q_gain, b_k_gain, b_w_out, c_w_in, c_conv_w, c_a_log, c_dt_bias, c_o_gain, c_w_out):
    batch, seq, d = x.shape
    depth = norm_mix.shape[0]
    a_cols = len(A_GROUPS) * A_HEADS
    bias_a, bias_b = rel_bias[:, :a_cols], rel_bias[:, a_cols:]
    x2 = x.reshape(batch * seq, d)
    tm = 512
    for i in range(depth):
        kind, j = i % 3, i // 3
        if kind == 0:
            mix, specs = _mixer_a(x2, norm_mix[i], a_w_in[j], a_q_gain[j], a_k_gain[j], bias_a, batch, seq, tm)
            w_out = a_w_out[j]
        elif kind == 1:
            mix, specs = _mixer_b(x2, norm_mix[i], b_w_in[j], b_q_gain[j], b_k_gain[j], bias_b, batch, seq, tm)
            w_out = b_w_out[j]
        else:
            mix, specs = _mixer_c(x2, norm_mix[i], c_w_in[j], c_conv_w[j], c_a_log[j], c_dt_bias[j],
                                  c_o_gain[j], batch, seq, tm)
            w_out = c_w_out[j]
        x2 = _proj_mlp(mix, specs, w_out.astype(BF16), x2, norm_mlp[i], mlp_w1[i].astype(BF16),
                       mlp_w2[i].astype(BF16), groups_a=kind == 0, tm=tm)
    return x2.reshape(batch, seq, d)
```

```python
import functools
import math

import numpy as np
import jax
import jax.numpy as jnp
from jax import lax
from jax.experimental import pallas as pl
from jax.experimental.pallas import tpu as pltpu

F32 = jnp.float32
BF16 = jnp.bfloat16
I32 = jnp.int32

EPS = 1e-6
HEAD_DIM = 64
NEG = -1e30
INT_MIN = -2 ** 31
LOG2E = math.log2(math.e)

V7X_VMEM_BYTES = 64 * 1024 * 1024
VMEM_LIMIT = V7X_VMEM_BYTES - 8 * 1024 * 1024
V7X_MXU_DIM = 256

NUM_BUCKETS = 32
MAX_DISTANCE = 2048

A_GROUPS = ((128, 1), (512, 4), (2048, 16))
A_HEADS = 8
A_BLOCK = 128
A_GW = A_HEADS * HEAD_DIM
A_STEP = 2

B_HEADS = 16
B_KV_HEADS = 4
B_IDX_HEADS = 8
B_IDX_DIM = 64
B_TOPK = 256
B_QT = 128
B_KC = 256
B_NDELTA = 14
B_SAFE_SPREAD = 100.0

C_HEADS = 8
C_DK = 128
C_CONV = 4
C_CHUNK = 64
C_GROUP = 4 * C_CHUNK


def _cparams(sem):
    return pltpu.CompilerParams(dimension_semantics=sem, vmem_limit_bytes=VMEM_LIMIT)


def _t5_bucket(dist):
    max_exact = NUM_BUCKETS // 2
    d = jnp.maximum(dist, 1).astype(F32)
    log_part = jnp.log(d / max_exact) / math.log(MAX_DISTANCE / max_exact) * (NUM_BUCKETS - max_exact)
    large = jnp.minimum(max_exact + log_part.astype(I32), NUM_BUCKETS - 1)
    return jnp.where(dist < max_exact, dist, large)


def _rms_rows(x, gain_row):
    ms = jnp.mean(x * x, axis=-1, keepdims=True)
    return x * lax.rsqrt(ms + EPS) * gain_row


def _in_proj_b_kernel(x_ref, g_ref, w_ref, hg_ref, bd_ref, qkv_ref, idx_ref, *, tn, n_norm):
    n_qkv = qkv_ref.shape[1] // tn
    hn = _rms_rows(x_ref[...], g_ref[...]).astype(BF16)
    for j in range(w_ref.shape[1] // tn):
        cols = slice(j * tn, (j + 1) * tn)
        y = jnp.dot(hn, w_ref[:, cols], preferred_element_type=F32)
        if j < n_norm:
            ms = jnp.dot((y * y).astype(BF16), bd_ref[...], preferred_element_type=F32) * (1.0 / HEAD_DIM)
            y = y * lax.rsqrt(ms + EPS) * hg_ref[:, cols]
        if j < n_qkv:
            qkv_ref[:, cols] = y.astype(BF16)
        else:
            idx_ref[:, (j - n_qkv) * tn:(j - n_qkv + 1) * tn] = y


def _mix_groups_a(tm, o0, o1, o2, l0, l1, l2, pt4_ref, pt16_ref):

    def token_order(ref, pt_ref):
        v = ref[...]
        if pt_ref is None:
            return v.reshape(tm, A_GW)
        pb = pt_ref.shape[0]
        per = pb // v.shape[0]
        pieces = []
        for h in range(tm // pb):
            hi, lo = _split_bf16(v[:, h * per:(h + 1) * per, :].reshape(pb, A_GW))
            pieces.append(jnp.dot(pt_ref[...], hi, preferred_element_type=F32)
                          + jnp.dot(pt_ref[...], lo, preferred_element_type=F32))
        return jnp.concatenate(pieces, axis=0)

    pts = (None, pt4_ref, pt16_ref)
    a, b, c = (token_order(r, pt) for r, pt in zip((l0, l1, l2), pts))
    m = jnp.maximum(jnp.maximum(a, b), c)
    ea, eb, ec = jnp.exp2(a - m), jnp.exp2(b - m), jnp.exp2(c - m)
    oa, ob, oc = (token_order(r, pt) for r, pt in zip((o0, o1, o2), pts))
    return (ea * oa + eb * ob + ec * oc) / (ea + eb + ec)


def _proj_mlp_kernel(*refs, tf, groups_a):
    wo_ref, x_ref, g_ref, w1_ref, w2_ref, o_ref = refs[-6:]
    y = _mix_groups_a(x_ref.shape[0], *refs[:-6]) if groups_a else refs[0][...]
    x = x_ref[...] + jnp.dot(y.astype(BF16), wo_ref[...], preferred_element_type=F32)
    hn = _rms_rows(x, g_ref[...]).astype(BF16)
    acc = x
    for f in range(w1_ref.shape[1] // tf):
        cols = slice(f * tf, (f + 1) * tf)
        h = jnp.maximum(jnp.dot(hn, w1_ref[:, cols], preferred_element_type=F32), 0.0)
        acc = acc + jnp.dot((h * h).astype(BF16), w2_ref[cols, :], preferred_element_type=F32)
    o_ref[...] = acc


def _proj_mlp(mix, mix_specs, w_out, x2, gain, w1, w2, *, groups_a, tm, tf=1024):
    n, d = x2.shape
    dff = w1.shape[1]
    const = lambda shape: pl.BlockSpec(shape, lambda i: (0, 0), pipeline_mode=pl.Buffered(1))
    return pl.pallas_call(
        functools.partial(_proj_mlp_kernel, tf=tf, groups_a=groups_a),
        out_shape=jax.ShapeDtypeStruct((n, d), F32),
        grid=(n // tm,),
        in_specs=list(mix_specs) + [const(w_out.shape), pl.BlockSpec((tm, d), lambda i: (i, 0)), const((1, d)),
                                    const((d, dff)), const((dff, d))],
        out_specs=pl.BlockSpec((tm, d), lambda i: (i, 0)),
        compiler_params=_cparams(("parallel",)),
        name="proj_mlp_a" if groups_a else "proj_mlp",
    )(*mix, w_out, x2, gain.reshape(1, d), w1, w2)


def _residue_major(tm, dilation):
    p = np.zeros((tm, tm), np.float32)
    j, r = np.meshgrid(np.arange(tm // dilation), np.arange(dilation), indexing="ij")
    p[(r * (tm // dilation) + j).ravel(), (j * dilation + r).ravel()] = 1.0
    return p


def _in_proj_a_kernel(x_ref, g_ref, w_ref, hg_ref, bd_ref, *rest):
    n_groups = len(A_GROUPS)
    perms, outs = rest[:n_groups - 1], rest[n_groups - 1:]
    tm = x_ref.shape[0]
    hn = _rms_rows(x_ref[...], g_ref[...]).astype(BF16)
    for which in range(3):
        for g, (_, dilation) in enumerate(A_GROUPS):
            j = which * n_groups + g
            cols = slice(j * A_GW, (j + 1) * A_GW)
            y = jnp.dot(hn, w_ref[:, cols], preferred_element_type=F32)
            if which < 2:
                nb = bd_ref.shape[0]
                y2 = (y * y).astype(BF16)
                ms = jnp.concatenate([jnp.dot(y2[:, c:c + nb], bd_ref[...], preferred_element_type=F32)
                                      for c in range(0, A_GW, nb)], axis=1) * (1.0 / HEAD_DIM)
                y = y * lax.rsqrt(ms + EPS) * hg_ref[:, cols]
            y = y.astype(BF16)
            if dilation == 1:
                y = y.reshape(1, tm, A_GW)
            else:
                perm = perms[g - 1]
                pb = perm.shape[0]
                y = jnp.concatenate(
                    [jnp.dot(perm[...], y[h:h + pb], preferred_element_type=F32).astype(BF16)
                     .reshape(dilation, pb // dilation, A_GW) for h in range(0, tm, pb)], axis=1)
            outs[g][:, :, which * A_GW:(which + 1) * A_GW] = y


def _attn_a_kernel(q_ref, kp_ref, kc_ref, vp_ref, vc_ref, bias_ref, o_ref, lse_ref):
    n = pl.program_id(2)
    blk = A_BLOCK
    pair = 2 * HEAD_DIM
    lane = lax.broadcasted_iota(I32, (2 * blk, pair), 1)
    row = lax.broadcasted_iota(I32, (2 * blk, blk), 0)
    pen = jnp.where((row < blk) & (n == 0), NEG, 0.0)
    nt = (((1,), (1,)), ((), ()))
    tn = (((0,), (0,)), ((), ()))
    k_all = jnp.concatenate([kp_ref[...], kc_ref[...]], axis=0)
    v_all = jnp.concatenate([vp_ref[...], vc_ref[...]], axis=0)
    q_all = q_ref[...]
    chains = [(sb, hp, a) for sb in range(A_STEP) for hp in range(A_HEADS // 2) for a in range(2)]

    def window(x, sb, hp):
        return x[sb * blk:(sb + 2) * blk, hp * pair:(hp + 1) * pair]

    keep = [lane < HEAD_DIM, lane >= HEAD_DIM]
    ka = [jnp.where(keep[a], window(k_all, sb, hp), jnp.zeros((2 * blk, pair), BF16))
          for sb, hp, a in chains]
    s = [lax.dot_general(k_, q_all[sb * blk:(sb + 1) * blk, hp * pair:(hp + 1) * pair], nt,
                         preferred_element_type=F32) + bias_ref[2 * hp + a]
         for k_, (sb, hp, a) in zip(ka, chains)]
    s = [s_ + pen if sb == 0 else s_ for s_, (sb, _, _) in zip(s, chains)]
    m = [jnp.max(s_, axis=0, keepdims=True) for s_ in s]
    p = [jnp.exp2(s_ - m_) for s_, m_ in zip(s, m)]
    l = [jnp.sum(p_, axis=0, keepdims=True) for p_ in p]
    pv = [lax.dot_general(window(v_all, sb, hp), p_.astype(BF16), tn, preferred_element_type=F32)
          for p_, (sb, hp, _) in zip(p, chains)]
    for idx in range(0, len(chains), 2):
        sb, hp, _ = chains[idx]
        o_t = jnp.concatenate([(pv[idx + a] / l[idx + a])[a * HEAD_DIM:(a + 1) * HEAD_DIM] for a in range(2)],
                              axis=0)
        l_t = jnp.concatenate([jnp.broadcast_to(m[idx + a] + jnp.log2(l[idx + a]), (HEAD_DIM, blk))
                               for a in range(2)], axis=0)
        o_ref[sb * blk:(sb + 1) * blk, hp * pair:(hp + 1) * pair] = o_t.T
        lse_ref[sb * blk:(sb + 1) * blk, hp * pair:(hp + 1) * pair] = l_t.T


def _attn_a_group(arr, bias_t, dilation, batch, seq):
    sub = seq // dilation
    rows = A_STEP * A_BLOCK
    nb = sub // rows

    def spec(which, prev):
        if prev:
            return pl.BlockSpec((None, None, A_BLOCK, A_GW),
                                lambda b, r, n: (b, r, jnp.maximum(n * A_STEP - 1, 0), which))
        return pl.BlockSpec((None, None, rows, A_GW), lambda b, r, n: (b, r, n, which))

    out_spec = pl.BlockSpec((None, None, rows, A_GW), lambda b, r, n: (b, r, n, 0))
    out_sds = jax.ShapeDtypeStruct((batch, dilation, sub, A_GW), F32)
    return pl.pallas_call(
        _attn_a_kernel,
        out_shape=[out_sds, out_sds],
        grid=(batch, dilation, nb),
        in_specs=[spec(0, False), spec(1, True), spec(1, False), spec(2, True), spec(2, False),
                  pl.BlockSpec((A_HEADS, 2 * A_BLOCK, A_BLOCK), lambda b, r, n: (0, 0, 0))],
        out_specs=[out_spec, out_spec],
        compiler_params=_cparams(("parallel", "parallel", "arbitrary")),
        name=f"attn_a_d{dilation}",
    )(arr, arr, arr, arr, arr, bias_t)


def _bias_a(tab_g, dilation):
    step = np.arange(3 * A_BLOCK - 1) - (A_BLOCK - 1)
    vec = tab_g[_t5_bucket(jnp.asarray(np.maximum(step, 0) * dilation, I32))].astype(F32) * LOG2E
    vec = jnp.where(jnp.asarray((step >= 0) & (step <= A_BLOCK))[:, None], vec, NEG)
    return _toeplitz(vec.T, 2 * A_BLOCK, A_BLOCK)


def _mixer_a(x2, gain, w_in, q_gain, k_gain, bias_a, batch, seq, tm):
    n, d = x2.shape
    n_groups = len(A_GROUPS)
    reps = n_groups * A_HEADS
    hg = jnp.concatenate([jnp.tile(q_gain * (HEAD_DIM ** -0.5 * LOG2E), reps), jnp.tile(k_gain, reps),
                          jnp.ones((n_groups * A_GW,), F32)]).reshape(1, -1)
    kout = 3 * n_groups * A_GW
    pb = V7X_MXU_DIM
    bd = np.kron(np.eye(pb // HEAD_DIM), np.ones((HEAD_DIM, HEAD_DIM))).astype(np.float32)
    perms = [_residue_major(pb, dil) for _, dil in A_GROUPS[1:]]
    tiles = seq // tm
    const = lambda shape: pl.BlockSpec(shape, lambda i: (0, 0), pipeline_mode=pl.Buffered(1))
    grouped = lambda dil, width: pl.BlockSpec((None, dil, tm // dil, width),
                                              lambda i: (i // tiles, 0, i % tiles, 0))
    arrs = pl.pallas_call(
        _in_proj_a_kernel,
        out_shape=[jax.ShapeDtypeStruct((batch, dil, seq // dil, 3 * A_GW), BF16) for _, dil in A_GROUPS],
        grid=(n // tm,),
        in_specs=[pl.BlockSpec((tm, d), lambda i: (i, 0)), const((1, d)), const((d, kout)),
                  const((1, kout)), const((pb, pb))] + [const((pb, pb)) for _ in perms],
        out_specs=[grouped(dil, 3 * A_GW) for _, dil in A_GROUPS],
        compiler_params=_cparams(("parallel",)),
        name="in_proj_a",
    )(x2, gain.reshape(1, d), w_in.astype(BF16), hg, jnp.asarray(bd, BF16),
      *[jnp.asarray(p, BF16) for p in perms])

    os_, ls_ = [], []
    for g, (window, dilation) in enumerate(A_GROUPS):
        assert window // dilation == A_BLOCK and (seq // dilation) % (A_STEP * A_BLOCK) == 0 and dilation <= pb // 16
        tab_g = bias_a[:, g * A_HEADS:(g + 1) * A_HEADS]
        o, lse = _attn_a_group(arrs[g], _bias_a(tab_g, dilation), dilation, batch, seq)
        os_.append(o)
        ls_.append(lse)

    mix = [*os_, *ls_, *[jnp.asarray(p.T, BF16) for p in perms]]
    specs = [grouped(dil, A_GW) for _, dil in A_GROUPS] * 2 + [const((pb, pb)) for _ in perms]
    return mix, specs


def _float_key(s):
    i = lax.bitcast_convert_type(s, I32)
    k = jnp.where(i < 0, i ^ jnp.int32(0x7FFFFFFF), i)
    return jnp.where(s == 0.0, 0, k)


def _split_bf16(x):
    hi = x.astype(BF16)
    return hi, (x - hi.astype(F32)).astype(BF16)


def _attn_b_kernel(q_ref, k_ref, vt_ref, qi_ref, kw_ref, bias_ref, o_ref, key_ref, msk_ref, *, bounded):
    qb = pl.program_id(1)
    kc, qt = B_KC, B_QT
    nch = qb // 2 + 1
    t_q = qb * qt + lax.broadcasted_iota(I32, (kc, qt), 1)
    sub = lax.broadcasted_iota(I32, (kc, qt), 0)
    nt = (((1,), (1,)), ((), ()))

    qh, ql = _split_bf16(qi_ref[...])
    qi3 = jnp.concatenate(
        [jnp.concatenate([x[:, h * B_IDX_DIM:(h + 1) * B_IDX_DIM] for x in (qh, qh, ql)], axis=1)
         for h in range(B_IDX_HEADS)], axis=0)
    wt = kw_ref[pl.ds(pl.multiple_of(qb * qt, qt), qt), 0:qt].T
    wt = wt * (B_IDX_HEADS ** -0.5 * B_IDX_DIM ** -0.5)

    npair = (nch + 1) // 2

    def score_pair(c2, carry):
        offs = [pl.multiple_of((2 * c2 + a) * kc, kc) for a in range(2)]
        ks = [_split_bf16(kw_ref[pl.ds(off, kc), 0:B_IDX_DIM]) for off in offs]
        sc = [lax.dot_general(jnp.concatenate([kh, kl, kh], axis=1), qi3, nt, preferred_element_type=F32)
              for kh, kl in ks]
        for off, s in zip(offs, sc):
            acc = jnp.zeros((kc, qt), F32)
            for h in range(B_IDX_HEADS):
                acc = acc + wt[B_IDX_DIM + h:B_IDX_DIM + h + 1, :] * jnp.maximum(s[:, h * qt:(h + 1) * qt], 0.0)
            key_ref[pl.ds(off, kc), :] = jnp.where(off + sub <= t_q, _float_key(acc), INT_MIN)
        return carry

    lax.fori_loop(0, npair, score_pair, 0)

    def count(pred_fn):
        def body(c2, acc):
            for a in range(2):
                off = pl.multiple_of((2 * c2 + a) * kc, kc)
                hit = jnp.where(pred_fn(key_ref[pl.ds(off, kc), :], off + sub), 1, 0)
                acc = acc + jnp.sum(hit.reshape(kc // 8, 8, qt), axis=0)
            return acc
        acc = lax.fori_loop(0, npair, body, jnp.zeros((8, qt), I32))
        return jnp.sum(acc, axis=0, keepdims=True)

    def thr_bit(it, lo):
        cand = lo + jnp.left_shift(jnp.int32(1), 31 - it)
        cnt = count(lambda key, _: key >= cand)
        return jnp.where(cnt >= B_TOPK, cand, lo)

    thr = lax.fori_loop(0, 32, thr_bit, jnp.full((1, qt), INT_MIN, I32))
    need = B_TOPK - count(lambda key, _: key > thr)
    n_eq = count(lambda key, _: key == thr)

    def tie_search(_):
        def bit(it, j):
            cand = j + jnp.left_shift(jnp.int32(1), 12 - it)
            cnt = count(lambda key, idx: (key == thr) & (idx < cand))
            return jnp.where(cnt <= need, cand, j)
        return lax.fori_loop(0, 13, bit, jnp.zeros((1, qt), I32))

    any_tie = jnp.max(jnp.where(n_eq > need, 1, 0)) > 0
    j_max = lax.cond(any_tie, tie_search, lambda _: jnp.full((1, qt), 2 ** 13, I32), 0)

    def mask_chunk(c, carry):
        off = pl.multiple_of(c * kc, kc)
        key = key_ref[pl.ds(off, kc), :]
        idx = off + sub
        sel = ((key > thr) | ((key == thr) & (idx < j_max))) & (idx <= t_q)
        msk_ref[pl.ds(off, kc), :] = jnp.where(sel, 0.0, NEG)
        return carry

    lax.fori_loop(0, nch, mask_chunk, 0)

    grp = B_HEADS // B_KV_HEADS
    q = q_ref[...]
    groups = range(B_KV_HEADS)
    qg = [jnp.concatenate([q[:, (g * grp + j) * HEAD_DIM:(g * grp + j + 1) * HEAD_DIM]
                           for j in range(grp)], axis=0) for g in groups]
    ksl = [slice(g * HEAD_DIM, (g + 1) * HEAD_DIM) for g in groups]

    def flash(c, carry):
        m, l, acc = carry
        off = pl.multiple_of(c * kc, kc)
        d_lo = jnp.clip(qb - 2 * c, 0, B_NDELTA - 1)
        d_hi = jnp.clip(qb - 2 * c - 1, 0, B_NDELTA - 1)
        mk = msk_ref[pl.ds(off, kc), :]
        mk = jnp.concatenate([mk] * grp, axis=1)
        s = [lax.dot_general(k_ref[pl.ds(off, kc), ksl[g]], qg[g], nt, preferred_element_type=F32)
             for g in groups]
        s = [s[g] + jnp.concatenate([bias_ref[d_lo, g], bias_ref[d_hi, g]], axis=0) + mk for g in groups]
        if bounded:
            p = [jnp.exp2(s[g]) for g in groups]
        else:
            m_new = [jnp.maximum(m[g], jnp.max(s[g], axis=0, keepdims=True)) for g in groups]
            p = [jnp.exp2(s[g] - m_new[g]) for g in groups]
        pv = [jnp.dot(vt_ref[ksl[g], pl.ds(off, kc)], p[g].astype(BF16), preferred_element_type=F32)
              for g in groups]
        if bounded:
            l = [l[g] + jnp.sum(p[g], axis=0, keepdims=True) for g in groups]
            acc = [acc[g] + pv[g] for g in groups]
            return m, l, acc
        alpha = [jnp.exp2(m[g] - m_new[g]) for g in groups]
        l = [alpha[g] * l[g] + jnp.sum(p[g], axis=0, keepdims=True) for g in groups]
        acc = [alpha[g] * acc[g] + pv[g] for g in groups]
        return m_new, l, acc

    m0 = [jnp.full((1, grp * qt), NEG, F32) for _ in groups]
    l0 = [jnp.zeros((1, grp * qt), F32) for _ in groups]
    a0 = [jnp.zeros((HEAD_DIM, grp * qt), F32) for _ in groups]
    _, l, acc = lax.fori_loop(0, nch, flash, (m0, l0, a0))
    for g in groups:
        o = acc[g] / l[g]
        for j in range(0, grp, 2):
            h = g * grp + j
            pair = jnp.concatenate([o[:, j * qt:(j + 1) * qt], o[:, (j + 1) * qt:(j + 2) * qt]], axis=0)
            o_ref[:, h * HEAD_DIM:(h + 2) * HEAD_DIM] = pair.T.astype(o_ref.dtype)


def _toeplitz(vec, n_rows, n_cols):
    span = n_rows + n_cols - 1
    assert vec.shape[-1] == span
    lead = vec.shape[:-1]
    padded = jnp.concatenate([vec, jnp.zeros(lead + (1,), vec.dtype)], axis=-1)
    flat = jnp.tile(padded, n_rows)[..., :n_rows * span]
    return flat.reshape(lead + (n_rows, span))[..., n_rows - 1:]


def _bias_b(tab):
    grp = B_HEADS // B_KV_HEADS
    span = 2 * B_QT - 1
    dist = np.arange(B_NDELTA)[:, None] * B_QT + np.arange(span)[None, :] - (B_QT - 1)
    assert (B_NDELTA - 1) * B_QT - (B_QT - 1) >= 16 * 128 ** (15.0 / 16.0) + 1
    vec = tab[_t5_bucket(jnp.asarray(np.maximum(dist, 0), I32))].astype(F32) * LOG2E
    t = _toeplitz(vec.transpose(0, 2, 1), B_QT, B_QT)
    t = t.reshape(B_NDELTA, B_KV_HEADS, grp, B_QT, B_QT).transpose(0, 1, 3, 2, 4)
    return t.reshape(B_NDELTA, B_KV_HEADS, B_QT, grp * B_QT)


def _mixer_b(x2, gain, w_in, q_gain, k_gain, bias_b, batch, seq, tm):
    d = x2.shape[1]
    nq, nkv = B_HEADS * HEAD_DIM, B_KV_HEADS * HEAD_DIM
    nidx = B_IDX_HEADS * B_IDX_DIM
    tn = 256
    used = 2 * nkv + nq + nidx + B_IDX_DIM + B_IDX_HEADS
    kout = -(-used // tn) * tn
    w = jnp.concatenate([w_in, jnp.zeros((d, kout - used), F32)], axis=1).astype(BF16)
    hg = jnp.concatenate([jnp.tile(q_gain * (HEAD_DIM ** -0.5 * LOG2E), B_HEADS), jnp.tile(k_gain, B_KV_HEADS),
                          jnp.ones((kout - nq - nkv,), F32)]).reshape(1, -1)
    n = batch * seq
    nrest = kout - nq - 2 * nkv
    bd = np.kron(np.eye(tn // HEAD_DIM), np.ones((HEAD_DIM, HEAD_DIM))).astype(np.float32)
    const = lambda shape: pl.BlockSpec(shape, lambda i: (0, 0), pipeline_mode=pl.Buffered(1))
    qk, idx = pl.pallas_call(
        functools.partial(_in_proj_b_kernel, tn=tn, n_norm=(nq + nkv) // tn),
        out_shape=[jax.ShapeDtypeStruct((n, nq + 2 * nkv), BF16), jax.ShapeDtypeStruct((n, nrest), F32)],
        grid=(n // tm,),
        in_specs=[pl.BlockSpec((tm, d), lambda i: (i, 0)), const((1, d)), const((d, kout)),
                  const((1, kout)), const((tn, tn))],
        out_specs=[pl.BlockSpec((tm, nq + 2 * nkv), lambda i: (i, 0)),
                   pl.BlockSpec((tm, nrest), lambda i: (i, 0))],
        compiler_params=_cparams(("parallel",)),
        name="in_proj_b",
    )(x2, gain.reshape(1, d), w, hg, jnp.asarray(bd, BF16))
    qk = qk.reshape(batch, seq, nq + 2 * nkv)
    idx = idx.reshape(batch, seq, nrest)
    vt = qk[:, :, nq + nkv:].transpose(0, 2, 1)
    assert seq % (2 * B_QT) == 0 and seq <= 2 ** 12 and nkv == tn and nrest == nidx + tn

    def attend(bias_t, bounded):
        return pl.pallas_call(
            functools.partial(_attn_b_kernel, bounded=bounded),
            out_shape=jax.ShapeDtypeStruct((batch, seq, nq), BF16),
            grid=(batch, seq // B_QT),
            in_specs=[
                pl.BlockSpec((None, B_QT, nq), lambda b, i: (b, i, 0)),
                pl.BlockSpec((None, seq, nkv), lambda b, i: (b, 0, nq // nkv)),
                pl.BlockSpec((None, nkv, seq), lambda b, i: (b, 0, 0)),
                pl.BlockSpec((None, B_QT, nidx), lambda b, i: (b, i, 0)),
                pl.BlockSpec((None, seq, tn), lambda b, i: (b, 0, nidx // tn)),
                pl.BlockSpec((B_NDELTA, B_KV_HEADS, B_QT, (B_HEADS // B_KV_HEADS) * B_QT),
                             lambda b, i: (0, 0, 0, 0), pipeline_mode=pl.Buffered(1)),
            ],
            out_specs=pl.BlockSpec((None, B_QT, nq), lambda b, i: (b, i, 0)),
            scratch_shapes=[pltpu.VMEM((seq, B_QT), I32), pltpu.VMEM((seq, B_QT), F32)],
            compiler_params=_cparams(("parallel", "arbitrary")),
            name="attn_b_bounded" if bounded else "attn_b",
        )(qk, qk, vt, idx, idx, bias_t)

    bias_t = _bias_b(bias_b)
    qk_bound = 1.02 * HEAD_DIM * (HEAD_DIM ** -0.5 * LOG2E) * jnp.max(jnp.abs(q_gain)) * jnp.max(jnp.abs(k_gain))
    bound = qk_bound + jnp.max(bias_b) * LOG2E
    spread = bound + qk_bound - jnp.min(bias_b) * LOG2E
    y = lax.cond(spread <= B_SAFE_SPREAD,
                 lambda: attend(bias_t - bound, True),
                 lambda: attend(bias_t, False))
    return [y.reshape(batch * seq, nq)], [pl.BlockSpec((tm, nq), lambda i: (i, 0))]


def _in_proj_c_kernel(x_ref, g_ref, w_ref, cw_ref, qkv_ref, rest_ref, tail_ref, xs_ref, *, tn):
    i = pl.program_id(1)
    tm = x_ref.shape[0]
    hw = C_HEADS * C_DK
    hn = _rms_rows(x_ref[...], g_ref[...]).astype(BF16)
    for j in range(w_ref.shape[1] // tn):
        cols = slice(j * tn, (j + 1) * tn)
        y = jnp.dot(hn, w_ref[:, cols], preferred_element_type=F32)
        if j * tn >= 3 * hw:
            rest_ref[:, j * tn - 3 * hw:(j + 1) * tn - 3 * hw] = y
            continue
        xs_ref[0:8, cols] = jnp.where(i > 0, tail_ref[:, cols], 0.0)
        xs_ref[8:, cols] = y
        tail_ref[:, cols] = y[tm - 8:, :]
        cw = cw_ref[:, cols]
        conv = sum(cw[t:t + 1, :] * xs_ref[8 - (C_CONV - 1) + t:8 - (C_CONV - 1) + t + tm, cols]
                   for t in range(C_CONV))
        z = conv * jax.nn.sigmoid(conv)
        if j * tn >= 2 * hw:
            qkv_ref[:, cols] = z
            continue
        scale = C_DK ** -0.5 if j * tn < hw else 1.0
        for h in range(tn // C_DK):
            zh = z[:, h * C_DK:(h + 1) * C_DK]
            ss = jnp.sum(zh * zh, axis=-1, keepdims=True)
            qkv_ref[:, j * tn + h * C_DK:j * tn + (h + 1) * C_DK] = zh * (lax.rsqrt(ss + EPS) * scale)


def _delta_kernel(q_ref, k_ref, v_ref, gate_ref, a_ref, b_ref, alog_ref, dtb_ref, og_ref, y_ref,
                  rows_s, p_s, r_s, m_s, n_s, el_s, o_s):
    cs, gs = C_CHUNK, C_GROUP
    per = gs // cs
    seq, dv = v_ref.shape
    n_groups = seq // gs
    hi = lax.Precision.HIGHEST
    r = lax.broadcasted_iota(I32, (gs, gs), 0)
    c_ = lax.broadcasted_iota(I32, (gs, gs), 1)

    def same_block(size):
        sh = int(math.log2(size))
        return (r >> sh) == (c_ >> sh)

    chunk = same_block(cs)
    lower, strict, eye = chunk & (r >= c_), chunk & (r > c_), r == c_

    z = a_ref[...] + dtb_ref[...]
    softplus = jnp.maximum(z, 0.0) + jnp.log(1.0 + jnp.exp(-jnp.abs(z)))
    g_all = -jnp.exp(alog_ref[...]) * softplus
    gc_all = jnp.dot(g_all, jnp.where(chunk & (r <= c_), 1.0, 0.0), preferred_element_type=F32, precision=hi)
    gl_all = jnp.dot(g_all, jnp.where(chunk, 1.0, 0.0), preferred_element_type=F32, precision=hi)
    rows_s[0] = jax.nn.sigmoid(b_ref[...])
    rows_s[1] = gc_all
    rows_s[2] = jnp.exp(gc_all)
    rows_s[3] = jnp.exp(gl_all - gc_all)
    first = (lax.broadcasted_iota(I32, (gs, per * dv), 0)
             == (lax.broadcasted_iota(I32, (gs, per * dv), 1) // dv) * cs)
    el_s[...] = jnp.exp(jnp.dot(gl_all, jnp.where(first, 1.0, 0.0), preferred_element_type=F32, precision=hi))

    def mm(a, b):
        return jnp.dot(a.astype(BF16), b.astype(BF16), preferred_element_type=F32)

    def mm_nt(a, b):
        return lax.dot_general(a.astype(BF16), b.astype(BF16), (((1,), (1,)), ((), ())),
                               preferred_element_type=F32)

    ways = 4

    def prep(m):
        ids = [m * ways + a for a in range(ways)]
        rows = [pl.ds(pl.multiple_of(i * gs, gs), gs) for i in ids]
        rw = [jnp.concatenate([rows_s[j, pl.ds(i, 1), :] for j in range(4)] + [jnp.zeros((4, gs), F32)], axis=0)
              for i in ids]
        cl = [x.T for x in rw]
        beta_col, gc_col, eg_col, ekg_col = ([x[:, j:j + 1] for x in cl] for j in range(4))
        q, k, v = ([ref[rw_, :] for rw_ in rows] for ref in (q_ref, k_ref, v_ref))
        decay = [jnp.where(lower, jnp.exp(jnp.where(lower, gcc - x[1:2, :], 0.0)), 0.0)
                 for gcc, x in zip(gc_col, rw)]
        kb = [k_ * b_ for k_, b_ in zip(k, beta_col)]
        yield
        a_mat = [jnp.where(strict, mm_nt(kb_, k_) * d_, 0.0) for kb_, k_, d_ in zip(kb, k, decay)]
        yield
        a8 = [jnp.where(same_block(8), a_, 0.0) for a_ in a_mat]
        t = [jnp.where(eye, 1.0, 0.0) - a_ for a_ in a8]
        pw = [mm(a_, a_) for a_ in a8]
        yield
        t = [t_ + mm(t_, p_) for t_, p_ in zip(t, pw)]
        yield
        pw = [mm(p_, p_) for p_ in pw]
        yield
        t = [t_ + mm(t_, p_) for t_, p_ in zip(t, pw)]
        yield
        size = 8
        while size < cs:
            sel = same_block(2 * size) & jnp.logical_not(same_block(size))
            nt = [mm(jnp.where(sel, a_, 0.0), t_) for a_, t_ in zip(a_mat, t)]
            yield
            t = [t_ - mm(t_, n_) for t_, n_ in zip(t, nt)]
            yield
            size *= 2
        uw = [mm(t_, jnp.concatenate([v_ * b_, kb_ * e_], axis=1))
              for t_, v_, b_, kb_, e_ in zip(t, v, beta_col, kb, eg_col)]
        yield
        qk = [jnp.where(lower, mm_nt(q_, k_) * d_, 0.0) for q_, k_, d_ in zip(q, k, decay)]
        yield
        qkwu = [mm(qk_, uw_) for qk_, uw_ in zip(qk, uw)]
        kgt = [(k_ * e_).T for k_, e_ in zip(k, ekg_col)]
        yield
        nm = [[mm(kgt[a][:, j * cs:(j + 1) * cs], uw[a][j * cs:(j + 1) * cs, :]) for j in range(per)]
              for a in range(ways)]

        def write():
            for a, i in enumerate(ids):
                r_s[rows[a], :] = qkwu[a][:, :dv]
                p_s[rows[a], :] = (q[a] * eg_col[a] - qkwu[a][:, dv:]).astype(BF16)
                for j in range(per):
                    n_s[i * per + j] = nm[a][j][:, :dv]
                    m_s[i * per + j] = nm[a][j][:, dv:].astype(BF16)
        return write

    def scan(m, state):
        for a in range(ways):
            i = m * ways + a
            el = el_s[pl.ds(i, 1), :]
            for j in range(per):
                rows = pl.ds(pl.multiple_of(i * gs + j * cs, cs), cs)
                c = i * per + j
                sb = state[0].astype(BF16)
                o_s[rows, :] = jnp.dot(p_s[rows, :], sb, preferred_element_type=F32) + r_s[rows, :]
                state[0] = (state[0] * el[:, j * dv:(j + 1) * dv] + n_s[c]
                            - jnp.dot(m_s[c], sb, preferred_element_type=F32))
                yield

    def weave(*gens):
        gens, results = list(gens), [None] * len(gens)
        live = list(range(len(gens)))
        while live:
            for idx in list(live):
                try:
                    next(gens[idx])
                except StopIteration as stop:
                    results[idx] = stop.value
                    live.remove(idx)
        return results

    n_blocks = n_groups // ways
    weave(prep(0))[0]()

    def step(m, state):
        holder = [state]
        _, write = weave(scan(m - 1, holder), prep(m))
        write()
        return holder[0]

    state = lax.fori_loop(1, n_blocks, step, jnp.zeros((C_DK, dv), F32))

    def finish(first, last):
        for t in range(first, last):
            rows = pl.ds(t * gs, gs)
            gate = gate_ref[rows, :]
            y_ref[rows, :] = (_rms_rows(o_s[rows, :], og_ref[...])
                              * (gate * jax.nn.sigmoid(gate))).astype(y_ref.dtype)
            yield

    done = (n_blocks - 1) * ways
    weave(scan(n_blocks - 1, [state]), finish(0, done))
    weave(finish(done, n_groups))


def _mixer_c(x2, gain, w_in, conv_w, a_log, dt_bias, o_gain, batch, seq, tm):
    d = x2.shape[1]
    hw = C_HEADS * C_DK
    tn = 256
    used = 4 * hw + 2 * C_HEADS
    kout = -(-used // tn) * tn
    w = jnp.concatenate([w_in, jnp.zeros((d, kout - used), F32)], axis=1).astype(BF16)
    const = lambda shape: pl.BlockSpec(shape, lambda b, i: (0, 0), pipeline_mode=pl.Buffered(1))
    qkv, rest = pl.pallas_call(
        functools.partial(_in_proj_c_kernel, tn=tn),
        out_shape=[jax.ShapeDtypeStruct((batch, seq, 3 * hw), F32),
                   jax.ShapeDtypeStruct((batch, seq, kout - 3 * hw), F32)],
        grid=(batch, seq // tm),
        in_specs=[pl.BlockSpec((None, tm, d), lambda b, i: (b, i, 0)), const((1, d)), const((d, kout)),
                  const((C_CONV, 3 * hw))],
        out_specs=[pl.BlockSpec((None, tm, 3 * hw), lambda b, i: (b, i, 0)),
                   pl.BlockSpec((None, tm, kout - 3 * hw), lambda b, i: (b, i, 0))],
        scratch_shapes=[pltpu.VMEM((8, 3 * hw), F32), pltpu.VMEM((tm + 8, 3 * hw), F32)],
        compiler_params=_cparams(("parallel", "arbitrary")),
        name="in_proj_c",
    )(x2.reshape(batch, seq, d), gain.reshape(1, d), w, conv_w)

    n_groups = seq // C_GROUP
    ba = rest[:, :, hw:hw + 2 * C_HEADS].transpose(0, 2, 1).reshape(batch, 2 * C_HEADS, n_groups, C_GROUP)
    per_head = lambda v: jnp.broadcast_to(v.reshape(C_HEADS, 1, 1), (C_HEADS, 1, C_GROUP))
    head_cols = lambda off: pl.BlockSpec((None, seq, C_DK), lambda b, h: (b, 0, off + h))
    small = lambda off: pl.BlockSpec((None, None, n_groups, C_GROUP), lambda b, h: (b, off + h, 0, 0))
    scalar_row = pl.BlockSpec((None, 1, C_GROUP), lambda b, h: (h, 0, 0))
    y = pl.pallas_call(
        _delta_kernel,
        out_shape=jax.ShapeDtypeStruct((batch, seq, hw), BF16),
        grid=(batch, C_HEADS),
        in_specs=[head_cols(0), head_cols(C_HEADS), head_cols(2 * C_HEADS),
                  head_cols(0),
                  small(C_HEADS), small(0), scalar_row, scalar_row,
                  pl.BlockSpec((1, C_DK), lambda b, h: (0, 0))],
        out_specs=head_cols(0),
        scratch_shapes=[
            pltpu.VMEM((4, n_groups, C_GROUP), F32),
            pltpu.VMEM((seq, C_DK), BF16),
            pltpu.VMEM((seq, C_DK), F32),
            pltpu.VMEM((seq // C_CHUNK, C_DK, C_DK), BF16),
            pltpu.VMEM((seq // C_CHUNK, C_DK, C_DK), F32),
            pltpu.VMEM((n_groups, (C_GROUP // C_CHUNK) * C_DK), F32),
            pltpu.VMEM((seq, C_DK), F32),
        ],
        compiler_params=_cparams(("parallel", "parallel")),
        name="delta_c",
    )(qkv, qkv, qkv, rest, ba, ba, per_head(a_log), per_head(dt_bias), o_gain.reshape(1, C_DK))
    return [y.reshape(batch * seq, hw)], [pl.BlockSpec((tm, hw), lambda i: (i, 0))]


def kernel(x, rel_bias, norm_mix, norm_mlp, mlp_w1, mlp_w2, a_w_in, a_q_gain, a_k_gain, a_w_out,
           b_w_in, b_q_gain, b_k_gain, b_w_out, c_w_in, c_conv_w, c_a_log, c_dt_bias, c_o_gain, c_w_out):
    batch, seq, d = x.shape
    depth = norm_mix.shape[0]
    a_cols = len(A_GROUPS) * A_HEADS
    bias_a, bias_b = rel_bias[:, :a_cols], rel_bias[:, a_cols:]
    x2 = x.reshape(batch * seq, d)
    tm = 512
    for i in range(depth):
        kind, j = i % 3, i // 3
        if kind == 0:
            mix, specs = _mixer_a(x2, norm_mix[i], a_w_in[j], a_q_gain[j], a_k_gain[j], bias_a, batch, seq, tm)
            w_out = a_w_out[j]
        elif kind == 1:
            mix, specs = _mixer_b(x2, norm_mix[i], b_w_in[j], b_q_gain[j], b_k_gain[j], bias_b, batch, seq, tm)
            w_out = b_w_out[j]
        else:
            mix, specs = _mixer_c(x2, norm_mix[i], c_w_in[j], c_conv_w[j], c_a_log[j], c_dt_bias[j],
                                  c_o_gain[j], batch, seq, tm)
            w_out = c_w_out[j]
        x2 = _proj_mlp(mix, specs, w_out.astype(BF16), x2, norm_mlp[i], mlp_w1[i].astype(BF16),
                       mlp_w2[i].astype(BF16), groups_a=kind == 0, tm=tm)
    return x2.reshape(batch, seq, d)
```

```python
import functools
import math

import numpy as np
import jax
import jax.numpy as jnp
from jax import lax
from jax.experimental import pallas as pl
from jax.experimental.pallas import tpu as pltpu

F32 = jnp.float32
BF16 = jnp.bfloat16
I32 = jnp.int32

EPS = 1e-6
HEAD_DIM = 64
NEG = -1e30
INT_MIN = -2 ** 31
LOG2E = math.log2(math.e)

V7X_VMEM_BYTES = 64 * 1024 * 1024
VMEM_LIMIT = V7X_VMEM_BYTES - 8 * 1024 * 1024
V7X_MXU_DIM = 256

NUM_BUCKETS = 32
MAX_DISTANCE = 2048

A_GROUPS = ((128, 1), (512, 4), (2048, 16))
A_HEADS = 8
A_BLOCK = 128
A_GW = A_HEADS * HEAD_DIM
A_MAX_STEP = 4

B_HEADS = 16
B_KV_HEADS = 4
B_IDX_HEADS = 8
B_IDX_DIM = 64
B_TOPK = 256
B_QT = 128
B_KC = 256
B_NDELTA = 14
B_SAFE_SPREAD = 100.0

C_HEADS = 8
C_DK = 128
C_CONV = 4
C_CHUNK = 64
C_GROUP = 4 * C_CHUNK


def _cparams(sem):
    return pltpu.CompilerParams(dimension_semantics=sem, vmem_limit_bytes=VMEM_LIMIT)


def _t5_bucket(dist):
    max_exact = NUM_BUCKETS // 2
    d = jnp.maximum(dist, 1).astype(F32)
    log_part = jnp.log(d / max_exact) / math.log(MAX_DISTANCE / max_exact) * (NUM_BUCKETS - max_exact)
    large = jnp.minimum(max_exact + log_part.astype(I32), NUM_BUCKETS - 1)
    return jnp.where(dist < max_exact, dist, large)


def _rms_rows(x, gain_row):
    ms = jnp.mean(x * x, axis=-1, keepdims=True)
    return x * lax.rsqrt(ms + EPS) * gain_row


def _in_proj_b_kernel(x_ref, g_ref, w_ref, hg_ref, bd_ref, qkv_ref, idx_ref, *, tn, n_norm):
    n_qkv = qkv_ref.shape[1] // tn
    hn = _rms_rows(x_ref[...], g_ref[...]).astype(BF16)
    for j in range(w_ref.shape[1] // tn):
        cols = slice(j * tn, (j + 1) * tn)
        y = jnp.dot(hn, w_ref[:, cols], preferred_element_type=F32)
        if j < n_norm:
            ms = jnp.dot((y * y).astype(BF16), bd_ref[...], preferred_element_type=F32) * (1.0 / HEAD_DIM)
            y = y * lax.rsqrt(ms + EPS) * hg_ref[:, cols]
        if j < n_qkv:
            qkv_ref[:, cols] = y.astype(BF16)
        else:
            idx_ref[:, (j - n_qkv) * tn:(j - n_qkv + 1) * tn] = y


def _mix_groups_a(tm, o0, o1, o2, l0, l1, l2, pt4_ref, pt16_ref):

    def token_order(ref, pt_ref):
        v = ref[...]
        if pt_ref is None:
            return v.reshape(tm, A_GW)
        pb = pt_ref.shape[0]
        per = pb // v.shape[0]
        pieces = []
        for h in range(tm // pb):
            hi, lo = _split_bf16(v[:, h * per:(h + 1) * per, :].reshape(pb, A_GW))
            pieces.append(jnp.dot(pt_ref[...], hi, preferred_element_type=F32)
                          + jnp.dot(pt_ref[...], lo, preferred_element_type=F32))
        return jnp.concatenate(pieces, axis=0)

    pts = (None, pt4_ref, pt16_ref)
    a, b, c = (token_order(r, pt) for r, pt in zip((l0, l1, l2), pts))
    m = jnp.maximum(jnp.maximum(a, b), c)
    ea, eb, ec = jnp.exp2(a - m), jnp.exp2(b - m), jnp.exp2(c - m)
    oa, ob, oc = (token_order(r, pt) for r, pt in zip((o0, o1, o2), pts))
    return (ea * oa + eb * ob + ec * oc) / (ea + eb + ec)


def _proj_mlp_kernel(*refs, tf, groups_a):
    wo_ref, x_ref, g_ref, w1_ref, w2_ref, o_ref = refs[-6:]
    y = _mix_groups_a(x_ref.shape[0], *refs[:-6]) if groups_a else refs[0][...]
    x = x_ref[...] + jnp.dot(y.astype(BF16), wo_ref[...], preferred_element_type=F32)
    hn = _rms_rows(x, g_ref[...]).astype(BF16)
    acc = x
    for f in range(w1_ref.shape[1] // tf):
        cols = slice(f * tf, (f + 1) * tf)
        h = jnp.maximum(jnp.dot(hn, w1_ref[:, cols], preferred_element_type=F32), 0.0)
        acc = acc + jnp.dot((h * h).astype(BF16), w2_ref[cols, :], preferred_element_type=F32)
    o_ref[...] = acc


def _proj_mlp(mix, mix_specs, w_out, x2, gain, w1, w2, *, groups_a, tm, tf=1024):
    n, d = x2.shape
    dff = w1.shape[1]
    const = lambda shape: pl.BlockSpec(shape, lambda i: (0, 0), pipeline_mode=pl.Buffered(1))
    return pl.pallas_call(
        functools.partial(_proj_mlp_kernel, tf=tf, groups_a=groups_a),
        out_shape=jax.ShapeDtypeStruct((n, d), F32),
        grid=(n // tm,),
        in_specs=list(mix_specs) + [const(w_out.shape), pl.BlockSpec((tm, d), lambda i: (i, 0)), const((1, d)),
                                    const((d, dff)), const((dff, d))],
        out_specs=pl.BlockSpec((tm, d), lambda i: (i, 0)),
        compiler_params=_cparams(("parallel",)),
        name="proj_mlp_a" if groups_a else "proj_mlp",
    )(*mix, w_out, x2, gain.reshape(1, d), w1, w2)


def _residue_major(tm, dilation):
    p = np.zeros((tm, tm), np.float32)
    j, r = np.meshgrid(np.arange(tm // dilation), np.arange(dilation), indexing="ij")
    p[(r * (tm // dilation) + j).ravel(), (j * dilation + r).ravel()] = 1.0
    return p


def _in_proj_a_kernel(x_ref, g_ref, w_ref, hg_ref, bd_ref, *rest):
    n_groups = len(A_GROUPS)
    perms, outs = rest[:n_groups - 1], rest[n_groups - 1:]
    tm = x_ref.shape[0]
    hn = _rms_rows(x_ref[...], g_ref[...]).astype(BF16)
    for which in range(3):
        for g, (_, dilation) in enumerate(A_GROUPS):
            j = which * n_groups + g
            cols = slice(j * A_GW, (j + 1) * A_GW)
            y = jnp.dot(hn, w_ref[:, cols], preferred_element_type=F32)
            if which < 2:
                nb = bd_ref.shape[0]
                y2 = (y * y).astype(BF16)
                ms = jnp.concatenate([jnp.dot(y2[:, c:c + nb], bd_ref[...], preferred_element_type=F32)
                                      for c in range(0, A_GW, nb)], axis=1) * (1.0 / HEAD_DIM)
                y = y * lax.rsqrt(ms + EPS) * hg_ref[:, cols]
            y = y.astype(BF16)
            if dilation == 1:
                y = y.reshape(1, tm, A_GW)
            else:
                perm = perms[g - 1]
                pb = perm.shape[0]
                y = jnp.concatenate(
                    [jnp.dot(perm[...], y[h:h + pb], preferred_element_type=F32).astype(BF16)
                     .reshape(dilation, pb // dilation, A_GW) for h in range(0, tm, pb)], axis=1)
            outs[g][:, :, which * A_GW:(which + 1) * A_GW] = y


def _attn_a_kernel(q_ref, kp_ref, kc_ref, vp_ref, vc_ref, bias_ref, o_ref, lse_ref):
    n = pl.program_id(2)
    blk = A_BLOCK
    pair = 2 * HEAD_DIM
    lane = lax.broadcasted_iota(I32, (2 * blk, pair), 1)
    row = lax.broadcasted_iota(I32, (2 * blk, blk), 0)
    pen = jnp.where((row < blk) & (n == 0), NEG, 0.0)
    nt = (((1,), (1,)), ((), ()))
    tn = (((0,), (0,)), ((), ()))
    k_all = jnp.concatenate([kp_ref[...], kc_ref[...]], axis=0)
    v_all = jnp.concatenate([vp_ref[...], vc_ref[...]], axis=0)
    q_all = q_ref[...]
    chains = [(sb, hp, a) for sb in range(q_ref.shape[0] // blk) for hp in range(A_HEADS // 2) for a in range(2)]

    def window(x, sb, hp):
        return x[sb * blk:(sb + 2) * blk, hp * pair:(hp + 1) * pair]

    keep = [lane < HEAD_DIM, lane >= HEAD_DIM]
    ka = [jnp.where(keep[a], window(k_all, sb, hp), jnp.zeros((2 * blk, pair), BF16))
          for sb, hp, a in chains]
    s = [lax.dot_general(k_, q_all[sb * blk:(sb + 1) * blk, hp * pair:(hp + 1) * pair], nt,
                         preferred_element_type=F32) + bias_ref[2 * hp + a]
         for k_, (sb, hp, a) in zip(ka, chains)]
    s = [s_ + pen if sb == 0 else s_ for s_, (sb, _, _) in zip(s, chains)]
    m = [jnp.max(s_, axis=0, keepdims=True) for s_ in s]
    p = [jnp.exp2(s_ - m_) for s_, m_ in zip(s, m)]
    l = [jnp.sum(p_, axis=0, keepdims=True) for p_ in p]
    pv = [lax.dot_general(window(v_all, sb, hp), p_.astype(BF16), tn, preferred_element_type=F32)
          for p_, (sb, hp, _) in zip(p, chains)]
    for idx in range(0, len(chains), 2):
        sb, hp, _ = chains[idx]
        o_t = jnp.concatenate([(pv[idx + a] / l[idx + a])[a * HEAD_DIM:(a + 1) * HEAD_DIM] for a in range(2)],
                              axis=0)
        l_t = jnp.concatenate([jnp.broadcast_to(m[idx + a] + jnp.log2(l[idx + a]), (HEAD_DIM, blk))
                               for a in range(2)], axis=0)
        o_ref[sb * blk:(sb + 1) * blk, hp * pair:(hp + 1) * pair] = o_t.T
        lse_ref[sb * blk:(sb + 1) * blk, hp * pair:(hp + 1) * pair] = l_t.T


def _attn_a_group(arr, bias_t, dilation, batch, seq):
    sub = seq // dilation
    step = min(A_MAX_STEP, sub // A_BLOCK)
    rows = step * A_BLOCK
    nb = sub // rows

    def spec(which, prev):
        if prev:
            return pl.BlockSpec((None, None, A_BLOCK, A_GW),
                                lambda b, r, n: (b, r, jnp.maximum(n * step - 1, 0), which))
        return pl.BlockSpec((None, None, rows, A_GW), lambda b, r, n: (b, r, n, which))

    out_spec = pl.BlockSpec((None, None, rows, A_GW), lambda b, r, n: (b, r, n, 0))
    out_sds = jax.ShapeDtypeStruct((batch, dilation, sub, A_GW), F32)
    return pl.pallas_call(
        _attn_a_kernel,
        out_shape=[out_sds, out_sds],
        grid=(batch, dilation, nb),
        in_specs=[spec(0, False), spec(1, True), spec(1, False), spec(2, True), spec(2, False),
                  pl.BlockSpec((A_HEADS, 2 * A_BLOCK, A_BLOCK), lambda b, r, n: (0, 0, 0))],
        out_specs=[out_spec, out_spec],
        compiler_params=_cparams(("parallel", "parallel", "arbitrary")),
        name=f"attn_a_d{dilation}",
    )(arr, arr, arr, arr, arr, bias_t)


def _bias_a(tab_g, dilation):
    step = np.arange(3 * A_BLOCK - 1) - (A_BLOCK - 1)
    vec = tab_g[_t5_bucket(jnp.asarray(np.maximum(step, 0) * dilation, I32))].astype(F32) * LOG2E
    vec = jnp.where(jnp.asarray((step >= 0) & (step <= A_BLOCK))[:, None], vec, NEG)
    return _toeplitz(vec.T, 2 * A_BLOCK, A_BLOCK)


def _mixer_a(x2, gain, w_in, q_gain, k_gain, bias_a, batch, seq, tm):
    n, d = x2.shape
    n_groups = len(A_GROUPS)
    reps = n_groups * A_HEADS
    hg = jnp.concatenate([jnp.tile(q_gain * (HEAD_DIM ** -0.5 * LOG2E), reps), jnp.tile(k_gain, reps),
                          jnp.ones((n_groups * A_GW,), F32)]).reshape(1, -1)
    kout = 3 * n_groups * A_GW
    pb = V7X_MXU_DIM
    bd = np.kron(np.eye(pb // HEAD_DIM), np.ones((HEAD_DIM, HEAD_DIM))).astype(np.float32)
    perms = [_residue_major(pb, dil) for _, dil in A_GROUPS[1:]]
    tiles = seq // tm
    const = lambda shape: pl.BlockSpec(shape, lambda i: (0, 0), pipeline_mode=pl.Buffered(1))
    grouped = lambda dil, width: pl.BlockSpec((None, dil, tm // dil, width),
                                              lambda i: (i // tiles, 0, i % tiles, 0))
    arrs = pl.pallas_call(
        _in_proj_a_kernel,
        out_shape=[jax.ShapeDtypeStruct((batch, dil, seq // dil, 3 * A_GW), BF16) for _, dil in A_GROUPS],
        grid=(n // tm,),
        in_specs=[pl.BlockSpec((tm, d), lambda i: (i, 0)), const((1, d)), const((d, kout)),
                  const((1, kout)), const((pb, pb))] + [const((pb, pb)) for _ in perms],
        out_specs=[grouped(dil, 3 * A_GW) for _, dil in A_GROUPS],
        compiler_params=_cparams(("parallel",)),
        name="in_proj_a",
    )(x2, gain.reshape(1, d), w_in.astype(BF16), hg, jnp.asarray(bd, BF16),
      *[jnp.asarray(p, BF16) for p in perms])

    os_, ls_ = [], []
    for g, (window, dilation) in enumerate(A_GROUPS):
        assert window // dilation == A_BLOCK and (seq // dilation) % (min(A_MAX_STEP, seq // dilation // A_BLOCK) * A_BLOCK) == 0 and dilation <= pb // 16
        tab_g = bias_a[:, g * A_HEADS:(g + 1) * A_HEADS]
        o, lse = _attn_a_group(arrs[g], _bias_a(tab_g, dilation), dilation, batch, seq)
        os_.append(o)
        ls_.append(lse)

    mix = [*os_, *ls_, *[jnp.asarray(p.T, BF16) for p in perms]]
    specs = [grouped(dil, A_GW) for _, dil in A_GROUPS] * 2 + [const((pb, pb)) for _ in perms]
    return mix, specs


def _float_key(s):
    i = lax.bitcast_convert_type(s, I32)
    k = jnp.where(i < 0, i ^ jnp.int32(0x7FFFFFFF), i)
    return jnp.where(s == 0.0, 0, k)


def _split_bf16(x):
    hi = x.astype(BF16)
    return hi, (x - hi.astype(F32)).astype(BF16)


def _attn_b_kernel(q_ref, k_ref, vt_ref, qi_ref, kw_ref, bias_ref, shift_ref, o_ref, key_ref, msk_ref, *,
                   bounded):
    qb = pl.program_id(1)
    kc, qt = B_KC, B_QT
    nch = qb // 2 + 1
    t_q = qb * qt + lax.broadcasted_iota(I32, (kc, qt), 1)
    sub = lax.broadcasted_iota(I32, (kc, qt), 0)
    nt = (((1,), (1,)), ((), ()))

    qh, ql = _split_bf16(qi_ref[...])
    qi3 = jnp.concatenate(
        [jnp.concatenate([x[:, h * B_IDX_DIM:(h + 1) * B_IDX_DIM] for x in (qh, qh, ql)], axis=1)
         for h in range(B_IDX_HEADS)], axis=0)
    wt = kw_ref[pl.ds(pl.multiple_of(qb * qt, qt), qt), 0:qt].T
    wt = wt * (B_IDX_HEADS ** -0.5 * B_IDX_DIM ** -0.5)

    npair = (nch + 1) // 2

    def score_pair(c2, carry):
        offs = [pl.multiple_of((2 * c2 + a) * kc, kc) for a in range(2)]
        ks = [_split_bf16(kw_ref[pl.ds(off, kc), 0:B_IDX_DIM]) for off in offs]
        sc = [lax.dot_general(jnp.concatenate([kh, kl, kh], axis=1), qi3, nt, preferred_element_type=F32)
              for kh, kl in ks]
        for off, s in zip(offs, sc):
            acc = jnp.zeros((kc, qt), F32)
            for h in range(B_IDX_HEADS):
                acc = acc + wt[B_IDX_DIM + h:B_IDX_DIM + h + 1, :] * jnp.maximum(s[:, h * qt:(h + 1) * qt], 0.0)
            key_ref[pl.ds(off, kc), :] = jnp.where(off + sub <= t_q, _float_key(acc), INT_MIN)
        return carry

    lax.fori_loop(0, npair, score_pair, 0)

    def count(pred_fn):
        def body(c2, acc):
            for a in range(2):
                off = pl.multiple_of((2 * c2 + a) * kc, kc)
                hit = jnp.where(pred_fn(key_ref[pl.ds(off, kc), :], off + sub), 1, 0)
                acc = acc + jnp.sum(hit.reshape(kc // 8, 8, qt), axis=0)
            return acc
        acc = lax.fori_loop(0, npair, body, jnp.zeros((8, qt), I32))
        return jnp.sum(acc, axis=0, keepdims=True)

    def thr_bit(it, lo):
        cand = lo + jnp.left_shift(jnp.int32(1), 31 - it)
        cnt = count(lambda key, _: key >= cand)
        return jnp.where(cnt >= B_TOPK, cand, lo)

    thr = lax.fori_loop(0, 32, thr_bit, jnp.full((1, qt), INT_MIN, I32))
    need = B_TOPK - count(lambda key, _: key > thr)
    n_eq = count(lambda key, _: key == thr)

    def tie_search(_):
        def bit(it, j):
            cand = j + jnp.left_shift(jnp.int32(1), 12 - it)
            cnt = count(lambda key, idx: (key == thr) & (idx < cand))
            return jnp.where(cnt <= need, cand, j)
        return lax.fori_loop(0, 13, bit, jnp.zeros((1, qt), I32))

    any_tie = jnp.max(jnp.where(n_eq > need, 1, 0)) > 0
    j_max = lax.cond(any_tie, tie_search, lambda _: jnp.full((1, qt), 2 ** 13, I32), 0)

    def mask_chunk(c, carry):
        off = pl.multiple_of(c * kc, kc)
        key = key_ref[pl.ds(off, kc), :]
        idx = off + sub
        sel = ((key > thr) | ((key == thr) & (idx < j_max))) & (idx <= t_q)
        msk_ref[pl.ds(off, kc), :] = jnp.where(sel, -shift_ref[...], NEG)
        return carry

    lax.fori_loop(0, nch, mask_chunk, 0)

    grp = B_HEADS // B_KV_HEADS
    q = q_ref[...]
    groups = range(B_KV_HEADS)
    qg = [jnp.concatenate([q[:, (g * grp + j) * HEAD_DIM:(g * grp + j + 1) * HEAD_DIM]
                           for j in range(grp)], axis=0) for g in groups]
    ksl = [slice(g * HEAD_DIM, (g + 1) * HEAD_DIM) for g in groups]

    def flash(c, carry):
        m, l, acc = carry
        off = pl.multiple_of(c * kc, kc)
        d_lo = jnp.clip(qb - 2 * c, 0, B_NDELTA - 1)
        d_hi = jnp.clip(qb - 2 * c - 1, 0, B_NDELTA - 1)
        mk = msk_ref[pl.ds(off, kc), :]
        mk = jnp.concatenate([mk] * grp, axis=1)
        s = [lax.dot_general(k_ref[pl.ds(off, kc), ksl[g]], qg[g], nt, preferred_element_type=F32)
             for g in groups]
        s = [s[g] + jnp.concatenate([bias_ref[d_lo, g], bias_ref[d_hi, g]], axis=0) + mk for g in groups]
        if bounded:
            p = [jnp.exp2(s[g]) for g in groups]
            l = [l[g] + jnp.sum(p[g], axis=0, keepdims=True) for g in groups]
            p = [p[g].astype(BF16) for g in groups]
            pv = [jnp.dot(vt_ref[ksl[g], pl.ds(off, kc)], p[g], preferred_element_type=F32) for g in groups]
            return m, l, [acc[g] + pv[g] for g in groups]
        else:
            m_new = [jnp.maximum(m[g], jnp.max(s[g], axis=0, keepdims=True)) for g in groups]
            p = [jnp.exp2(s[g] - m_new[g]) for g in groups]
        pv = [jnp.dot(vt_ref[ksl[g], pl.ds(off, kc)], p[g].astype(BF16), preferred_element_type=F32)
              for g in groups]
        alpha = [jnp.exp2(m[g] - m_new[g]) for g in groups]
        l = [alpha[g] * l[g] + jnp.sum(p[g], axis=0, keepdims=True) for g in groups]
        acc = [alpha[g] * acc[g] + pv[g] for g in groups]
        return m_new, l, acc

    m0 = [jnp.full((1, grp * qt), NEG, F32) for _ in groups]
    l0 = [jnp.zeros((1, grp * qt), F32) for _ in groups]
    a0 = [jnp.zeros((HEAD_DIM, grp * qt), F32) for _ in groups]
    _, l, acc = lax.fori_loop(0, nch, flash, (m0, l0, a0))
    for g in groups:
        o = acc[g] / l[g]
        for j in range(0, grp, 2):
            h = g * grp + j
            pair = jnp.concatenate([o[:, j * qt:(j + 1) * qt], o[:, (j + 1) * qt:(j + 2) * qt]], axis=0)
            o_ref[:, h * HEAD_DIM:(h + 2) * HEAD_DIM] = pair.T.astype(o_ref.dtype)


def _toeplitz(vec, n_rows, n_cols):
    span = n_rows + n_cols - 1
    assert vec.shape[-1] == span
    lead = vec.shape[:-1]
    padded = jnp.concatenate([vec, jnp.zeros(lead + (1,), vec.dtype)], axis=-1)
    flat = jnp.tile(padded, n_rows)[..., :n_rows * span]
    return flat.reshape(lead + (n_rows, span))[..., n_rows - 1:]


def _bias_b(tab):
    grp = B_HEADS // B_KV_HEADS
    span = 2 * B_QT - 1
    dist = np.arange(B_NDELTA)[:, None] * B_QT + np.arange(span)[None, :] - (B_QT - 1)
    assert (B_NDELTA - 1) * B_QT - (B_QT - 1) >= 16 * 128 ** (15.0 / 16.0) + 1
    vec = tab[_t5_bucket(jnp.asarray(np.maximum(dist, 0), I32))].astype(F32) * LOG2E
    t = _toeplitz(vec.transpose(0, 2, 1), B_QT, B_QT)
    t = t.reshape(B_NDELTA, B_KV_HEADS, grp, B_QT, B_QT).transpose(0, 1, 3, 2, 4)
    return t.reshape(B_NDELTA, B_KV_HEADS, B_QT, grp * B_QT)


def _mixer_b(x2, gain, w_in, q_gain, k_gain, bias_b, batch, seq, tm):
    d = x2.shape[1]
    nq, nkv = B_HEADS * HEAD_DIM, B_KV_HEADS * HEAD_DIM
    nidx = B_IDX_HEADS * B_IDX_DIM
    tn = 256
    used = 2 * nkv + nq + nidx + B_IDX_DIM + B_IDX_HEADS
    kout = -(-used // tn) * tn
    w = jnp.concatenate([w_in, jnp.zeros((d, kout - used), F32)], axis=1).astype(BF16)
    hg = jnp.concatenate([jnp.tile(q_gain * (HEAD_DIM ** -0.5 * LOG2E), B_HEADS), jnp.tile(k_gain, B_KV_HEADS),
                          jnp.ones((kout - nq - nkv,), F32)]).reshape(1, -1)
    n = batch * seq
    nrest = kout - nq - 2 * nkv
    bd = np.kron(np.eye(tn // HEAD_DIM), np.ones((HEAD_DIM, HEAD_DIM))).astype(np.float32)
    const = lambda shape: pl.BlockSpec(shape, lambda i: (0, 0), pipeline_mode=pl.Buffered(1))
    qk, idx = pl.pallas_call(
        functools.partial(_in_proj_b_kernel, tn=tn, n_norm=(nq + nkv) // tn),
        out_shape=[jax.ShapeDtypeStruct((n, nq + 2 * nkv), BF16), jax.ShapeDtypeStruct((n, nrest), F32)],
        grid=(n // tm,),
        in_specs=[pl.BlockSpec((tm, d), lambda i: (i, 0)), const((1, d)), const((d, kout)),
                  const((1, kout)), const((tn, tn))],
        out_specs=[pl.BlockSpec((tm, nq + 2 * nkv), lambda i: (i, 0)),
                   pl.BlockSpec((tm, nrest), lambda i: (i, 0))],
        compiler_params=_cparams(("parallel",)),
        name="in_proj_b",
    )(x2, gain.reshape(1, d), w, hg, jnp.asarray(bd, BF16))
    qk = qk.reshape(batch, seq, nq + 2 * nkv)
    idx = idx.reshape(batch, seq, nrest)
    vt = qk[:, :, nq + nkv:].transpose(0, 2, 1)
    assert seq % (2 * B_QT) == 0 and seq <= 2 ** 12 and nkv == tn and nrest == nidx + tn

    def attend(shift, bounded):
        return pl.pallas_call(
            functools.partial(_attn_b_kernel, bounded=bounded),
            out_shape=jax.ShapeDtypeStruct((batch, seq, nq), BF16),
            grid=(batch, seq // B_QT),
            in_specs=[
                pl.BlockSpec((None, B_QT, nq), lambda b, i: (b, i, 0)),
                pl.BlockSpec((None, seq, nkv), lambda b, i: (b, 0, nq // nkv)),
                pl.BlockSpec((None, nkv, seq), lambda b, i: (b, 0, 0)),
                pl.BlockSpec((None, B_QT, nidx), lambda b, i: (b, i, 0)),
                pl.BlockSpec((None, seq, tn), lambda b, i: (b, 0, nidx // tn)),
                pl.BlockSpec((B_NDELTA, B_KV_HEADS, B_QT, (B_HEADS // B_KV_HEADS) * B_QT),
                             lambda b, i: (0, 0, 0, 0), pipeline_mode=pl.Buffered(1)),
                pl.BlockSpec((1, B_QT), lambda b, i: (0, 0)),
            ],
            out_specs=pl.BlockSpec((None, B_QT, nq), lambda b, i: (b, i, 0)),
            scratch_shapes=[pltpu.VMEM((seq, B_QT), I32), pltpu.VMEM((seq, B_QT), F32)],
            compiler_params=_cparams(("parallel", "arbitrary")),
            name="attn_b_bounded" if bounded else "attn_b",
        )(qk, qk, vt, idx, idx, bias_t, shift)

    bias_t = _bias_b(bias_b)
    qk_bound = 1.02 * HEAD_DIM * (HEAD_DIM ** -0.5 * LOG2E) * jnp.max(jnp.abs(q_gain)) * jnp.max(jnp.abs(k_gain))
    bound = qk_bound + jnp.max(bias_b) * LOG2E
    spread = bound + qk_bound - jnp.min(bias_b) * LOG2E
    y = lax.cond(spread <= B_SAFE_SPREAD,
                 lambda: attend(jnp.full((1, B_QT), bound, F32), True),
                 lambda: attend(jnp.zeros((1, B_QT), F32), False))
    return [y.reshape(batch * seq, nq)], [pl.BlockSpec((tm, nq), lambda i: (i, 0))]


def _in_proj_c_kernel(x_ref, g_ref, w_ref, cw_ref, qkv_ref, rest_ref, tail_ref, xs_ref, *, tn):
    i = pl.program_id(1)
    tm = x_ref.shape[0]
    hw = C_HEADS * C_DK
    hn = _rms_rows(x_ref[...], g_ref[...]).astype(BF16)
    for j in range(w_ref.shape[1] // tn):
        cols = slice(j * tn, (j + 1) * tn)
        y = jnp.dot(hn, w_ref[:, cols], preferred_element_type=F32)
        if j * tn >= 3 * hw:
            rest_ref[:, j * tn - 3 * hw:(j + 1) * tn - 3 * hw] = y
            continue
        xs_ref[0:8, cols] = jnp.where(i > 0, tail_ref[:, cols], 0.0)
        xs_ref[8:, cols] = y
        tail_ref[:, cols] = y[tm - 8:, :]
        cw = cw_ref[:, cols]
        conv = sum(cw[t:t + 1, :] * xs_ref[8 - (C_CONV - 1) + t:8 - (C_CONV - 1) + t + tm, cols]
                   for t in range(C_CONV))
        z = conv * jax.nn.sigmoid(conv)
        if j * tn >= 2 * hw:
            qkv_ref[:, cols] = z
            continue
        scale = C_DK ** -0.5 if j * tn < hw else 1.0
        for h in range(tn // C_DK):
            zh = z[:, h * C_DK:(h + 1) * C_DK]
            ss = jnp.sum(zh * zh, axis=-1, keepdims=True)
            qkv_ref[:, j * tn + h * C_DK:j * tn + (h + 1) * C_DK] = zh * (lax.rsqrt(ss + EPS) * scale)


def _delta_kernel(q_ref, k_ref, v_ref, gate_ref, a_ref, b_ref, alog_ref, dtb_ref, og_ref, y_ref,
                  rows_s, p_s, r_s, m_s, n_s, el_s, o_s):
    cs, gs = C_CHUNK, C_GROUP
    per = gs // cs
    seq, dv = v_ref.shape
    n_groups = seq // gs
    hi = lax.Precision.HIGHEST
    r = lax.broadcasted_iota(I32, (gs, gs), 0)
    c_ = lax.broadcasted_iota(I32, (gs, gs), 1)

    def same_block(size):
        sh = int(math.log2(size))
        return (r >> sh) == (c_ >> sh)

    chunk = same_block(cs)
    lower, strict, eye = chunk & (r >= c_), chunk & (r > c_), r == c_

    z = a_ref[...] + dtb_ref[...]
    softplus = jnp.maximum(z, 0.0) + jnp.log(1.0 + jnp.exp(-jnp.abs(z)))
    g_all = -jnp.exp(alog_ref[...]) * softplus
    gc_all = jnp.dot(g_all, jnp.where(chunk & (r <= c_), 1.0, 0.0), preferred_element_type=F32, precision=hi)
    gl_all = jnp.dot(g_all, jnp.where(chunk, 1.0, 0.0), preferred_element_type=F32, precision=hi)
    rows_s[0] = jax.nn.sigmoid(b_ref[...])
    rows_s[1] = gc_all
    rows_s[2] = jnp.exp(gc_all)
    rows_s[3] = jnp.exp(gl_all - gc_all)
    first = (lax.broadcasted_iota(I32, (gs, per * dv), 0)
             == (lax.broadcasted_iota(I32, (gs, per * dv), 1) // dv) * cs)
    el_s[...] = jnp.exp(jnp.dot(gl_all, jnp.where(first, 1.0, 0.0), preferred_element_type=F32, precision=hi))

    def mm(a, b):
        return jnp.dot(a.astype(BF16), b.astype(BF16), preferred_element_type=F32)

    def mm_nt(a, b):
        return lax.dot_general(a.astype(BF16), b.astype(BF16), (((1,), (1,)), ((), ())),
                               preferred_element_type=F32)

    ways = 4

    def prep(m):
        ids = [m * ways + a for a in range(ways)]
        rows = [pl.ds(pl.multiple_of(i * gs, gs), gs) for i in ids]
        rw = [jnp.concatenate([rows_s[j, pl.ds(i, 1), :] for j in range(4)] + [jnp.zeros((4, gs), F32)], axis=0)
              for i in ids]
        cl = [x.T for x in rw]
        beta_col, gc_col, eg_col, ekg_col = ([x[:, j:j + 1] for x in cl] for j in range(4))
        q, k, v = ([ref[rw_, :] for rw_ in rows] for ref in (q_ref, k_ref, v_ref))
        decay = [jnp.where(lower, jnp.exp(jnp.where(lower, gcc - x[1:2, :], 0.0)), 0.0)
                 for gcc, x in zip(gc_col, rw)]
        kb = [k_ * b_ for k_, b_ in zip(k, beta_col)]
        yield
        a_mat = [jnp.where(strict, mm_nt(kb_, k_) * d_, 0.0) for kb_, k_, d_ in zip(kb, k, decay)]
        yield
        a8 = [jnp.where(same_block(8), a_, 0.0) for a_ in a_mat]
        t = [jnp.where(eye, 1.0, 0.0) - a_ for a_ in a8]
        pw = [mm(a_, a_) for a_ in a8]
        yield
        t = [t_ + mm(t_, p_) for t_, p_ in zip(t, pw)]
        yield
        pw = [mm(p_, p_) for p_ in pw]
        yield
        t = [t_ + mm(t_, p_) for t_, p_ in zip(t, pw)]
        yield
        size = 8
        while size < cs:
            sel = same_block(2 * size) & jnp.logical_not(same_block(size))
            nt = [mm(jnp.where(sel, a_, 0.0), t_) for a_, t_ in zip(a_mat, t)]
            yield
            t = [t_ - mm(t_, n_) for t_, n_ in zip(t, nt)]
            yield
            size *= 2
        uw = [mm(t_, jnp.concatenate([v_ * b_, kb_ * e_], axis=1))
              for t_, v_, b_, kb_, e_ in zip(t, v, beta_col, kb, eg_col)]
        yield
        qk = [jnp.where(lower, mm_nt(q_, k_) * d_, 0.0) for q_, k_, d_ in zip(q, k, decay)]
        yield
        qkwu = [mm(qk_, uw_) for qk_, uw_ in zip(qk, uw)]
        kgt = [(k_ * e_).T for k_, e_ in zip(k, ekg_col)]
        yield
        nm = [[mm(kgt[a][:, j * cs:(j + 1) * cs], uw[a][j * cs:(j + 1) * cs, :]) for j in range(per)]
              for a in range(ways)]

        def write():
            for a, i in enumerate(ids):
                r_s[rows[a], :] = qkwu[a][:, :dv]
                p_s[rows[a], :] = (q[a] * eg_col[a] - qkwu[a][:, dv:]).astype(BF16)
                for j in range(per):
                    n_s[i * per + j] = nm[a][j][:, :dv]
                    m_s[i * per + j] = nm[a][j][:, dv:].astype(BF16)
        return write

    def scan(m, state):
        for a in range(ways):
            i = m * ways + a
            el = el_s[pl.ds(i, 1), :]
            for j in range(per):
                rows = pl.ds(pl.multiple_of(i * gs + j * cs, cs), cs)
                c = i * per + j
                sb = state[0].astype(BF16)
                o_s[rows, :] = jnp.dot(p_s[rows, :], sb, preferred_element_type=F32) + r_s[rows, :]
                state[0] = (state[0] * el[:, j * dv:(j + 1) * dv] + n_s[c]
                            - jnp.dot(m_s[c], sb, preferred_element_type=F32))
                yield

    def weave(*gens):
        gens, results = list(gens), [None] * len(gens)
        live = list(range(len(gens)))
        while live:
            for idx in list(live):
                try:
                    next(gens[idx])
                except StopIteration as stop:
                    results[idx] = stop.value
                    live.remove(idx)
        return results

    n_blocks = n_groups // ways
    weave(prep(0))[0]()

    def step(m, state):
        holder = [state]
        _, write = weave(scan(m - 1, holder), prep(m))
        write()
        return holder[0]

    state = lax.fori_loop(1, n_blocks, step, jnp.zeros((C_DK, dv), F32))

    def finish(first, last):
        for t in range(first, last):
            rows = pl.ds(t * gs, gs)
            gate = gate_ref[rows, :]
            y_ref[rows, :] = (_rms_rows(o_s[rows, :], og_ref[...])
                              * (gate * jax.nn.sigmoid(gate))).astype(y_ref.dtype)
            yield

    done = (n_blocks - 1) * ways
    weave(scan(n_blocks - 1, [state]), finish(0, done))
    weave(finish(done, n_groups))


def _mixer_c(x2, gain, w_in, conv_w, a_log, dt_bias, o_gain, batch, seq, tm):
    d = x2.shape[1]
    hw = C_HEADS * C_DK
    tn = 256
    used = 4 * hw + 2 * C_HEADS
    kout = -(-used // tn) * tn
    w = jnp.concatenate([w_in, jnp.zeros((d, kout - used), F32)], axis=1).astype(BF16)
    const = lambda shape: pl.BlockSpec(shape, lambda b, i: (0, 0), pipeline_mode=pl.Buffered(1))
    qkv, rest = pl.pallas_call(
        functools.partial(_in_proj_c_kernel, tn=tn),
        out_shape=[jax.ShapeDtypeStruct((batch, seq, 3 * hw), F32),
                   jax.ShapeDtypeStruct((batch, seq, kout - 3 * hw), F32)],
        grid=(batch, seq // tm),
        in_specs=[pl.BlockSpec((None, tm, d), lambda b, i: (b, i, 0)), const((1, d)), const((d, kout)),
                  const((C_CONV, 3 * hw))],
        out_specs=[pl.BlockSpec((None, tm, 3 * hw), lambda b, i: (b, i, 0)),
                   pl.BlockSpec((None, tm, kout - 3 * hw), lambda b, i: (b, i, 0))],
        scratch_shapes=[pltpu.VMEM((8, 3 * hw), F32), pltpu.VMEM((tm + 8, 3 * hw), F32)],
        compiler_params=_cparams(("parallel", "arbitrary")),
        name="in_proj_c",
    )(x2.reshape(batch, seq, d), gain.reshape(1, d), w, conv_w)

    n_groups = seq // C_GROUP
    ba = rest[:, :, hw:hw + 2 * C_HEADS].transpose(0, 2, 1).reshape(batch, 2 * C_HEADS, n_groups, C_GROUP)
    per_head = lambda v: jnp.broadcast_to(v.reshape(C_HEADS, 1, 1), (C_HEADS, 1, C_GROUP))
    head_cols = lambda off: pl.BlockSpec((None, seq, C_DK), lambda b, h: (b, 0, off + h))
    small = lambda off: pl.BlockSpec((None, None, n_groups, C_GROUP), lambda b, h: (b, off + h, 0, 0))
    scalar_row = pl.BlockSpec((None, 1, C_GROUP), lambda b, h: (h, 0, 0))
    y = pl.pallas_call(
        _delta_kernel,
        out_shape=jax.ShapeDtypeStruct((batch, seq, hw), BF16),
        grid=(batch, C_HEADS),
        in_specs=[head_cols(0), head_cols(C_HEADS), head_cols(2 * C_HEADS),
                  head_cols(0),
                  small(C_HEADS), small(0), scalar_row, scalar_row,
                  pl.BlockSpec((1, C_DK), lambda b, h: (0, 0))],
        out_specs=head_cols(0),
        scratch_shapes=[
            pltpu.VMEM((4, n_groups, C_GROUP), F32),
            pltpu.VMEM((seq, C_DK), BF16),
            pltpu.VMEM((seq, C_DK), F32),
            pltpu.VMEM((seq // C_CHUNK, C_DK, C_DK), BF16),
            pltpu.VMEM((seq // C_CHUNK, C_DK, C_DK), F32),
            pltpu.VMEM((n_groups, (C_GROUP // C_CHUNK) * C_DK), F32),
            pltpu.VMEM((seq, C_DK), F32),
        ],
        compiler_params=_cparams(("parallel", "parallel")),
        name="delta_c",
    )(qkv, qkv, qkv, rest, ba, ba, per_head(a_log), per_head(dt_bias), o_gain.reshape(1, C_DK))
    return [y.reshape(batch * seq, hw)], [pl.BlockSpec((tm, hw), lambda i: (i, 0))]


def kernel(x, rel_bias, norm_mix, norm_mlp, mlp_w1, mlp_w2, a_w_in, a_q_gain, a_k_gain, a_w_out,
           b_w_in, b_q_gain, b_k_gain, b_w_out, c_w_in, c_conv_w, c_a_log, c_dt_bias, c_o_gain, c_w_out):
    batch, seq, d = x.shape
    depth = norm_mix.shape[0]
    a_cols = len(A_GROUPS) * A_HEADS
    bias_a, bias_b = rel_bias[:, :a_cols], rel_bias[:, a_cols:]
    x2 = x.reshape(batch * seq, d)
    tm = 512
    for i in range(depth):
        kind, j = i % 3, i // 3
        if kind == 0:
            mix, specs = _mixer_a(x2, norm_mix[i], a_w_in[j], a_q_gain[j], a_k_gain[j], bias_a, batch, seq, tm)
            w_out = a_w_out[j]
        elif kind == 1:
            mix, specs = _mixer_b(x2, norm_mix[i], b_w_in[j], b_q_gain[j], b_k_gain[j], bias_b, batch, seq, tm)
            w_out = b_w_out[j]
        else:
            mix, specs = _mixer_c(x2, norm_mix[i], c_w_in[j], c_conv_w[j], c_a_log[j], c_dt_bias[j],
                                  c_o_gain[j], batch, seq, tm)
            w_out = c_w_out[j]
        x2 = _proj_mlp(mix, specs, w_out.astype(BF16), x2, norm_mlp[i], mlp_w1[i].astype(BF16),
                       mlp_w2[i].astype(BF16), groups_a=kind == 0, tm=tm)
    return x2.reshape(batch, seq, d)
```

```python
import functools
import math

import numpy as np
import jax
import jax.numpy as jnp
from jax import lax
from jax.experimental import pallas as pl
from jax.experimental.pallas import tpu as pltpu

F32 = jnp.float32
BF16 = jnp.bfloat16
I32 = jnp.int32

EPS = 1e-6
HEAD_DIM = 64
NEG = -1e30
INT_MIN = -2 ** 31
LOG2E = math.log2(math.e)

V7X_VMEM_BYTES = 64 * 1024 * 1024
VMEM_LIMIT = V7X_VMEM_BYTES - 8 * 1024 * 1024
V7X_MXU_DIM = 256

NUM_BUCKETS = 32
MAX_DISTANCE = 2048

A_GROUPS = ((128, 1), (512, 4), (2048, 16))
A_HEADS = 8
A_BLOCK = 128
A_GW = A_HEADS * HEAD_DIM
A_MAX_STEP = 4

B_HEADS = 16
B_KV_HEADS = 4
B_IDX_HEADS = 8
B_IDX_DIM = 64
B_TOPK = 256
B_QT = 128
B_KC = 256
B_NDELTA = 14
B_SAFE_SPREAD = 100.0

C_HEADS = 8
C_DK = 128
C_CONV = 4
C_CHUNK = 64
C_GROUP = 4 * C_CHUNK


def _cparams(sem):
    return pltpu.CompilerParams(dimension_semantics=sem, vmem_limit_bytes=VMEM_LIMIT)


def _t5_bucket(dist):
    max_exact = NUM_BUCKETS // 2
    d = jnp.maximum(dist, 1).astype(F32)
    log_part = jnp.log(d / max_exact) / math.log(MAX_DISTANCE / max_exact) * (NUM_BUCKETS - max_exact)
    large = jnp.minimum(max_exact + log_part.astype(I32), NUM_BUCKETS - 1)
    return jnp.where(dist < max_exact, dist, large)


def _rms_rows(x, gain_row):
    ms = jnp.mean(x * x, axis=-1, keepdims=True)
    return x * lax.rsqrt(ms + EPS) * gain_row


def _in_proj_b_kernel(x_ref, g_ref, w_ref, hg_ref, bd_ref, qkv_ref, idx_ref, *, tn, n_norm):
    n_qkv = qkv_ref.shape[1] // tn
    hn = _rms_rows(x_ref[...], g_ref[...]).astype(BF16)
    for j in range(w_ref.shape[1] // tn):
        cols = slice(j * tn, (j + 1) * tn)
        y = jnp.dot(hn, w_ref[:, cols], preferred_element_type=F32)
        if j < n_norm:
            ms = jnp.dot((y * y).astype(BF16), bd_ref[...], preferred_element_type=F32) * (1.0 / HEAD_DIM)
            y = y * lax.rsqrt(ms + EPS) * hg_ref[:, cols]
        if j < n_qkv:
            qkv_ref[:, cols] = y.astype(BF16)
        else:
            idx_ref[:, (j - n_qkv) * tn:(j - n_qkv + 1) * tn] = y


def _mix_groups_a(tm, o0, o1, o2, l0, l1, l2, pt4_ref, pt16_ref):

    def token_order(ref, pt_ref):
        v = ref[...]
        if pt_ref is None:
            return v.reshape(tm, A_GW)
        pb = pt_ref.shape[0]
        per = pb // v.shape[0]
        pieces = []
        for h in range(tm // pb):
            hi, lo = _split_bf16(v[:, h * per:(h + 1) * per, :].reshape(pb, A_GW))
            pieces.append(jnp.dot(pt_ref[...], hi, preferred_element_type=F32)
                          + jnp.dot(pt_ref[...], lo, preferred_element_type=F32))
        return jnp.concatenate(pieces, axis=0)

    pts = (None, pt4_ref, pt16_ref)
    a, b, c = (token_order(r, pt) for r, pt in zip((l0, l1, l2), pts))
    m = jnp.maximum(jnp.maximum(a, b), c)
    ea, eb, ec = jnp.exp2(a - m), jnp.exp2(b - m), jnp.exp2(c - m)
    oa, ob, oc = (token_order(r, pt) for r, pt in zip((o0, o1, o2), pts))
    return (ea * oa + eb * ob + ec * oc) / (ea + eb + ec)


def _proj_mlp_kernel(*refs, tf, groups_a):
    wo_ref, x_ref, g_ref, w1_ref, w2_ref, o_ref = refs[-6:]
    y = _mix_groups_a(x_ref.shape[0], *refs[:-6]) if groups_a else refs[0][...]
    x = x_ref[...] + jnp.dot(y.astype(BF16), wo_ref[...], preferred_element_type=F32)
    hn = _rms_rows(x, g_ref[...]).astype(BF16)
    acc = x
    for f in range(w1_ref.shape[1] // tf):
        cols = slice(f * tf, (f + 1) * tf)
        h = jnp.maximum(jnp.dot(hn, w1_ref[:, cols], preferred_element_type=F32), 0.0)
        acc = acc + jnp.dot((h * h).astype(BF16), w2_ref[cols, :], preferred_element_type=F32)
    o_ref[...] = acc


def _proj_mlp(mix, mix_specs, w_out, x2, gain, w1, w2, *, groups_a, tm, tf=1024):
    n, d = x2.shape
    dff = w1.shape[1]
    const = lambda shape: pl.BlockSpec(shape, lambda i: (0, 0), pipeline_mode=pl.Buffered(1))
    return pl.pallas_call(
        functools.partial(_proj_mlp_kernel, tf=tf, groups_a=groups_a),
        out_shape=jax.ShapeDtypeStruct((n, d), F32),
        grid=(n // tm,),
        in_specs=list(mix_specs) + [const(w_out.shape), pl.BlockSpec((tm, d), lambda i: (i, 0)), const((1, d)),
                                    const((d, dff)), const((dff, d))],
        out_specs=pl.BlockSpec((tm, d), lambda i: (i, 0)),
        compiler_params=_cparams(("parallel",)),
        name="proj_mlp_a" if groups_a else "proj_mlp",
    )(*mix, w_out, x2, gain.reshape(1, d), w1, w2)


def _residue_major(tm, dilation):
    p = np.zeros((tm, tm), np.float32)
    j, r = np.meshgrid(np.arange(tm // dilation), np.arange(dilation), indexing="ij")
    p[(r * (tm // dilation) + j).ravel(), (j * dilation + r).ravel()] = 1.0
    return p


def _in_proj_a_kernel(x_ref, g_ref, w_ref, hg_ref, bd_ref, *rest):
    n_groups = len(A_GROUPS)
    perms, outs = rest[:n_groups - 1], rest[n_groups - 1:]
    tm = x_ref.shape[0]
    hn = _rms_rows(x_ref[...], g_ref[...]).astype(BF16)
    for which in range(3):
        for g, (_, dilation) in enumerate(A_GROUPS):
            j = which * n_groups + g
            cols = slice(j * A_GW, (j + 1) * A_GW)
            y = jnp.dot(hn, w_ref[:, cols], preferred_element_type=F32)
            if which < 2:
                nb = bd_ref.shape[0]
                y2 = (y * y).astype(BF16)
                ms = jnp.concatenate([jnp.dot(y2[:, c:c + nb], bd_ref[...], preferred_element_type=F32)
                                      for c in range(0, A_GW, nb)], axis=1) * (1.0 / HEAD_DIM)
                y = y * lax.rsqrt(ms + EPS) * hg_ref[:, cols]
            y = y.astype(BF16)
            if dilation == 1:
                y = y.reshape(1, tm, A_GW)
            else:
                perm = perms[g - 1]
                pb = perm.shape[0]
                y = jnp.concatenate(
                    [jnp.dot(perm[...], y[h:h + pb], preferred_element_type=F32).astype(BF16)
                     .reshape(dilation, pb // dilation, A_GW) for h in range(0, tm, pb)], axis=1)
            outs[g][:, :, which * A_GW:(which + 1) * A_GW] = y


def _attn_a_kernel(q_ref, kp_ref, kc_ref, vp_ref, vc_ref, bias_ref, o_ref, lse_ref):
    n = pl.program_id(2)
    blk = A_BLOCK
    pair = 2 * HEAD_DIM
    lane = lax.broadcasted_iota(I32, (2 * blk, pair), 1)
    row = lax.broadcasted_iota(I32, (2 * blk, blk), 0)
    pen = jnp.where((row < blk) & (n == 0), NEG, 0.0)
    nt = (((1,), (1,)), ((), ()))
    tn = (((0,), (0,)), ((), ()))
    n_res, rows = q_ref.shape[0], q_ref.shape[1]
    k_all = [jnp.concatenate([kp_ref[r], kc_ref[r]], axis=0) for r in range(n_res)]
    v_all = [jnp.concatenate([vp_ref[r], vc_ref[r]], axis=0) for r in range(n_res)]
    chains = [(r, sb, hp, a) for r in range(n_res) for sb in range(rows // blk)
              for hp in range(A_HEADS // 2) for a in range(2)]

    def window(x, sb, hp):
        return x[sb * blk:(sb + 2) * blk, hp * pair:(hp + 1) * pair]

    keep = [lane < HEAD_DIM, lane >= HEAD_DIM]
    ka = [jnp.where(keep[a], window(k_all[r], sb, hp), jnp.zeros((2 * blk, pair), BF16))
          for r, sb, hp, a in chains]
    s = [lax.dot_general(k_, q_ref[r, sb * blk:(sb + 1) * blk, hp * pair:(hp + 1) * pair], nt,
                         preferred_element_type=F32) + bias_ref[2 * hp + a]
         for k_, (r, sb, hp, a) in zip(ka, chains)]
    s = [s_ + pen if sb == 0 else s_ for s_, (_, sb, _, _) in zip(s, chains)]
    m = [jnp.max(s_, axis=0, keepdims=True) for s_ in s]
    p = [jnp.exp2(s_ - m_) for s_, m_ in zip(s, m)]
    l = [jnp.sum(p_, axis=0, keepdims=True) for p_ in p]
    pv = [lax.dot_general(window(v_all[r], sb, hp), p_.astype(BF16), tn, preferred_element_type=F32)
          for p_, (r, sb, hp, _) in zip(p, chains)]
    for idx in range(0, len(chains), 2):
        r, sb, hp, _ = chains[idx]
        o_t = jnp.concatenate([(pv[idx + a] / l[idx + a])[a * HEAD_DIM:(a + 1) * HEAD_DIM] for a in range(2)],
                              axis=0)
        l_t = jnp.concatenate([jnp.broadcast_to(m[idx + a] + jnp.log2(l[idx + a]), (HEAD_DIM, blk))
                               for a in range(2)], axis=0)
        o_ref[r, sb * blk:(sb + 1) * blk, hp * pair:(hp + 1) * pair] = o_t.T
        lse_ref[r, sb * blk:(sb + 1) * blk, hp * pair:(hp + 1) * pair] = l_t.T


def _attn_a_group(arr, bias_t, dilation, batch, seq):
    sub = seq // dilation
    step = min(A_MAX_STEP, sub // A_BLOCK)
    rows = step * A_BLOCK
    nb = sub // rows
    n_res = min(dilation, A_MAX_STEP // step)

    def spec(which, prev):
        if prev:
            return pl.BlockSpec((None, n_res, A_BLOCK, A_GW),
                                lambda b, r, n: (b, r, jnp.maximum(n * step - 1, 0), which))
        return pl.BlockSpec((None, n_res, rows, A_GW), lambda b, r, n: (b, r, n, which))

    out_spec = pl.BlockSpec((None, n_res, rows, A_GW), lambda b, r, n: (b, r, n, 0))
    out_sds = jax.ShapeDtypeStruct((batch, dilation, sub, A_GW), F32)
    return pl.pallas_call(
        _attn_a_kernel,
        out_shape=[out_sds, out_sds],
        grid=(batch, dilation // n_res, nb),
        in_specs=[spec(0, False), spec(1, True), spec(1, False), spec(2, True), spec(2, False),
                  pl.BlockSpec((A_HEADS, 2 * A_BLOCK, A_BLOCK), lambda b, r, n: (0, 0, 0))],
        out_specs=[out_spec, out_spec],
        compiler_params=_cparams(("parallel", "parallel", "arbitrary")),
        name=f"attn_a_d{dilation}",
    )(arr, arr, arr, arr, arr, bias_t)


def _bias_a(tab_g, dilation):
    step = np.arange(3 * A_BLOCK - 1) - (A_BLOCK - 1)
    vec = tab_g[_t5_bucket(jnp.asarray(np.maximum(step, 0) * dilation, I32))].astype(F32) * LOG2E
    vec = jnp.where(jnp.asarray((step >= 0) & (step <= A_BLOCK))[:, None], vec, NEG)
    return _toeplitz(vec.T, 2 * A_BLOCK, A_BLOCK)


def _mixer_a(x2, gain, w_in, q_gain, k_gain, bias_a, batch, seq, tm):
    n, d = x2.shape
    n_groups = len(A_GROUPS)
    reps = n_groups * A_HEADS
    hg = jnp.concatenate([jnp.tile(q_gain * (HEAD_DIM ** -0.5 * LOG2E), reps), jnp.tile(k_gain, reps),
                          jnp.ones((n_groups * A_GW,), F32)]).reshape(1, -1)
    kout = 3 * n_groups * A_GW
    pb = V7X_MXU_DIM
    bd = np.kron(np.eye(pb // HEAD_DIM), np.ones((HEAD_DIM, HEAD_DIM))).astype(np.float32)
    perms = [_residue_major(pb, dil) for _, dil in A_GROUPS[1:]]
    tiles = seq // tm
    const = lambda shape: pl.BlockSpec(shape, lambda i: (0, 0), pipeline_mode=pl.Buffered(1))
    grouped = lambda dil, width: pl.BlockSpec((None, dil, tm // dil, width),
                                              lambda i: (i // tiles, 0, i % tiles, 0))
    arrs = pl.pallas_call(
        _in_proj_a_kernel,
        out_shape=[jax.ShapeDtypeStruct((batch, dil, seq // dil, 3 * A_GW), BF16) for _, dil in A_GROUPS],
        grid=(n // tm,),
        in_specs=[pl.BlockSpec((tm, d), lambda i: (i, 0)), const((1, d)), const((d, kout)),
                  const((1, kout)), const((pb, pb))] + [const((pb, pb)) for _ in perms],
        out_specs=[grouped(dil, 3 * A_GW) for _, dil in A_GROUPS],
        compiler_params=_cparams(("parallel",)),
        name="in_proj_a",
    )(x2, gain.reshape(1, d), w_in.astype(BF16), hg, jnp.asarray(bd, BF16),
      *[jnp.asarray(p, BF16) for p in perms])

    os_, ls_ = [], []
    for g, (window, dilation) in enumerate(A_GROUPS):
        assert window // dilation == A_BLOCK and (seq // dilation) % (min(A_MAX_STEP, seq // dilation // A_BLOCK) * A_BLOCK) == 0 and dilation <= pb // 16
        tab_g = bias_a[:, g * A_HEADS:(g + 1) * A_HEADS]
        o, lse = _attn_a_group(arrs[g], _bias_a(tab_g, dilation), dilation, batch, seq)
        os_.append(o)
        ls_.append(lse)

    mix = [*os_, *ls_, *[jnp.asarray(p.T, BF16) for p in perms]]
    specs = [grouped(dil, A_GW) for _, dil in A_GROUPS] * 2 + [const((pb, pb)) for _ in perms]
    return mix, specs


def _float_key(s):
    i = lax.bitcast_convert_type(s, I32)
    k = jnp.where(i < 0, i ^ jnp.int32(0x7FFFFFFF), i)
    return jnp.where(s == 0.0, 0, k)


def _split_bf16(x):
    hi = x.astype(BF16)
    return hi, (x - hi.astype(F32)).astype(BF16)


def _attn_b_kernel(q_ref, k_ref, vt_ref, qi_ref, kw_ref, bias_ref, shift_ref, o_ref, key_ref, msk_ref, *,
                   bounded):
    qb = pl.program_id(1)
    kc, qt = B_KC, B_QT
    nch = qb // 2 + 1
    t_q = qb * qt + lax.broadcasted_iota(I32, (kc, qt), 1)
    sub = lax.broadcasted_iota(I32, (kc, qt), 0)
    nt = (((1,), (1,)), ((), ()))

    qh, ql = _split_bf16(qi_ref[...])
    qi3 = jnp.concatenate(
        [jnp.concatenate([x[:, h * B_IDX_DIM:(h + 1) * B_IDX_DIM] for x in (qh, qh, ql)], axis=1)
         for h in range(B_IDX_HEADS)], axis=0)
    wt = kw_ref[pl.ds(pl.multiple_of(qb * qt, qt), qt), 0:qt].T
    wt = wt * (B_IDX_HEADS ** -0.5 * B_IDX_DIM ** -0.5)

    npair = (nch + 1) // 2

    def score_pair(c2, carry):
        offs = [pl.multiple_of((2 * c2 + a) * kc, kc) for a in range(2)]
        ks = [_split_bf16(kw_ref[pl.ds(off, kc), 0:B_IDX_DIM]) for off in offs]
        sc = [lax.dot_general(jnp.concatenate([kh, kl, kh], axis=1), qi3, nt, preferred_element_type=F32)
              for kh, kl in ks]
        for off, s in zip(offs, sc):
            acc = jnp.zeros((kc, qt), F32)
            for h in range(B_IDX_HEADS):
                acc = acc + wt[B_IDX_DIM + h:B_IDX_DIM + h + 1, :] * jnp.maximum(s[:, h * qt:(h + 1) * qt], 0.0)
            key_ref[pl.ds(off, kc), :] = jnp.where(off + sub <= t_q, _float_key(acc), INT_MIN)
        return carry

    lax.fori_loop(0, npair, score_pair, 0)

    def count(pred_fn):
        def body(c2, acc):
            for a in range(2):
                off = pl.multiple_of((2 * c2 + a) * kc, kc)
                hit = jnp.where(pred_fn(key_ref[pl.ds(off, kc), :], off + sub), 1, 0)
                acc = acc + jnp.sum(hit.reshape(kc // 8, 8, qt), axis=0)
            return acc
        acc = lax.fori_loop(0, npair, body, jnp.zeros((8, qt), I32))
        return jnp.sum(acc, axis=0, keepdims=True)

    def thr_bit(it, lo):
        cand = lo + jnp.left_shift(jnp.int32(1), 31 - it)
        cnt = count(lambda key, _: key >= cand)
        return jnp.where(cnt >= B_TOPK, cand, lo)

    thr = lax.fori_loop(0, 32, thr_bit, jnp.full((1, qt), INT_MIN, I32))
    need = B_TOPK - count(lambda key, _: key > thr)
    n_eq = count(lambda key, _: key == thr)

    def tie_search(_):
        def bit(it, j):
            cand = j + jnp.left_shift(jnp.int32(1), 12 - it)
            cnt = count(lambda key, idx: (key == thr) & (idx < cand))
            return jnp.where(cnt <= need, cand, j)
        return lax.fori_loop(0, 13, bit, jnp.zeros((1, qt), I32))

    any_tie = jnp.max(jnp.where(n_eq > need, 1, 0)) > 0
    j_max = lax.cond(any_tie, tie_search, lambda _: jnp.full((1, qt), 2 ** 13, I32), 0)

    def mask_chunk(c, carry):
        off = pl.multiple_of(c * kc, kc)
        key = key_ref[pl.ds(off, kc), :]
        idx = off + sub
        sel = ((key > thr) | ((key == thr) & (idx < j_max))) & (idx <= t_q)
        msk_ref[pl.ds(off, kc), :] = jnp.where(sel, -shift_ref[...], NEG)
        return carry

    lax.fori_loop(0, nch, mask_chunk, 0)

    grp = B_HEADS // B_KV_HEADS
    q = q_ref[...]
    groups = range(B_KV_HEADS)
    qg = [jnp.concatenate([q[:, (g * grp + j) * HEAD_DIM:(g * grp + j + 1) * HEAD_DIM]
                           for j in range(grp)], axis=0) for g in groups]
    ksl = [slice(g * HEAD_DIM, (g + 1) * HEAD_DIM) for g in groups]

    def flash(c, carry):
        m, l, acc = carry
        off = pl.multiple_of(c * kc, kc)
        d_lo = jnp.clip(qb - 2 * c, 0, B_NDELTA - 1)
        d_hi = jnp.clip(qb - 2 * c - 1, 0, B_NDELTA - 1)
        mk = msk_ref[pl.ds(off, kc), :]
        mk = jnp.concatenate([mk] * grp, axis=1)
        s = [lax.dot_general(k_ref[pl.ds(off, kc), ksl[g]], qg[g], nt, preferred_element_type=F32)
             for g in groups]
        s = [s[g] + jnp.concatenate([bias_ref[d_lo, g], bias_ref[d_hi, g]], axis=0) + mk for g in groups]
        if bounded:
            p = [jnp.exp2(s[g]) for g in groups]
            l = [l[g] + jnp.sum(p[g], axis=0, keepdims=True) for g in groups]
            p = [p[g].astype(BF16) for g in groups]
            pv = [jnp.dot(vt_ref[ksl[g], pl.ds(off, kc)], p[g], preferred_element_type=F32) for g in groups]
            return m, l, [acc[g] + pv[g] for g in groups]
        else:
            m_new = [jnp.maximum(m[g], jnp.max(s[g], axis=0, keepdims=True)) for g in groups]
            p = [jnp.exp2(s[g] - m_new[g]) for g in groups]
        pv = [jnp.dot(vt_ref[ksl[g], pl.ds(off, kc)], p[g].astype(BF16), preferred_element_type=F32)
              for g in groups]
        alpha = [jnp.exp2(m[g] - m_new[g]) for g in groups]
        l = [alpha[g] * l[g] + jnp.sum(p[g], axis=0, keepdims=True) for g in groups]
        acc = [alpha[g] * acc[g] + pv[g] for g in groups]
        return m_new, l, acc

    m0 = [jnp.full((1, grp * qt), NEG, F32) for _ in groups]
    l0 = [jnp.zeros((1, grp * qt), F32) for _ in groups]
    a0 = [jnp.zeros((HEAD_DIM, grp * qt), F32) for _ in groups]
    _, l, acc = lax.fori_loop(0, nch, flash, (m0, l0, a0))
    for g in groups:
        o = acc[g] / l[g]
        for j in range(0, grp, 2):
            h = g * grp + j
            pair = jnp.concatenate([o[:, j * qt:(j + 1) * qt], o[:, (j + 1) * qt:(j + 2) * qt]], axis=0)
            o_ref[:, h * HEAD_DIM:(h + 2) * HEAD_DIM] = pair.T.astype(o_ref.dtype)


def _toeplitz(vec, n_rows, n_cols):
    span = n_rows + n_cols - 1
    assert vec.shape[-1] == span
    lead = vec.shape[:-1]
    padded = jnp.concatenate([vec, jnp.zeros(lead + (1,), vec.dtype)], axis=-1)
    flat = jnp.tile(padded, n_rows)[..., :n_rows * span]
    return flat.reshape(lead + (n_rows, span))[..., n_rows - 1:]


def _bias_b(tab):
    grp = B_HEADS // B_KV_HEADS
    span = 2 * B_QT - 1
    dist = np.arange(B_NDELTA)[:, None] * B_QT + np.arange(span)[None, :] - (B_QT - 1)
    assert (B_NDELTA - 1) * B_QT - (B_QT - 1) >= 16 * 128 ** (15.0 / 16.0) + 1
    vec = tab[_t5_bucket(jnp.asarray(np.maximum(dist, 0), I32))].astype(F32) * LOG2E
    t = _toeplitz(vec.transpose(0, 2, 1), B_QT, B_QT)
    t = t.reshape(B_NDELTA, B_KV_HEADS, grp, B_QT, B_QT).transpose(0, 1, 3, 2, 4)
    return t.reshape(B_NDELTA, B_KV_HEADS, B_QT, grp * B_QT)


def _mixer_b(x2, gain, w_in, q_gain, k_gain, bias_b, batch, seq, tm):
    d = x2.shape[1]
    nq, nkv = B_HEADS * HEAD_DIM, B_KV_HEADS * HEAD_DIM
    nidx = B_IDX_HEADS * B_IDX_DIM
    tn = 256
    used = 2 * nkv + nq + nidx + B_IDX_DIM + B_IDX_HEADS
    kout = -(-used // tn) * tn
    w = jnp.concatenate([w_in, jnp.zeros((d, kout - used), F32)], axis=1).astype(BF16)
    hg = jnp.concatenate([jnp.tile(q_gain * (HEAD_DIM ** -0.5 * LOG2E), B_HEADS), jnp.tile(k_gain, B_KV_HEADS),
                          jnp.ones((kout - nq - nkv,), F32)]).reshape(1, -1)
    n = batch * seq
    nrest = kout - nq - 2 * nkv
    bd = np.kron(np.eye(tn // HEAD_DIM), np.ones((HEAD_DIM, HEAD_DIM))).astype(np.float32)
    const = lambda shape: pl.BlockSpec(shape, lambda i: (0, 0), pipeline_mode=pl.Buffered(1))
    qk, idx = pl.pallas_call(
        functools.partial(_in_proj_b_kernel, tn=tn, n_norm=(nq + nkv) // tn),
        out_shape=[jax.ShapeDtypeStruct((n, nq + 2 * nkv), BF16), jax.ShapeDtypeStruct((n, nrest), F32)],
        grid=(n // tm,),
        in_specs=[pl.BlockSpec((tm, d), lambda i: (i, 0)), const((1, d)), const((d, kout)),
                  const((1, kout)), const((tn, tn))],
        out_specs=[pl.BlockSpec((tm, nq + 2 * nkv), lambda i: (i, 0)),
                   pl.BlockSpec((tm, nrest), lambda i: (i, 0))],
        compiler_params=_cparams(("parallel",)),
        name="in_proj_b",
    )(x2, gain.reshape(1, d), w, hg, jnp.asarray(bd, BF16))
    qk = qk.reshape(batch, seq, nq + 2 * nkv)
    idx = idx.reshape(batch, seq, nrest)
    vt = qk[:, :, nq + nkv:].transpose(0, 2, 1)
    assert seq % (2 * B_QT) == 0 and seq <= 2 ** 12 and nkv == tn and nrest == nidx + tn

    def attend(shift, bounded):
        return pl.pallas_call(
            functools.partial(_attn_b_kernel, bounded=bounded),
            out_shape=jax.ShapeDtypeStruct((batch, seq, nq), BF16),
            grid=(batch, seq // B_QT),
            in_specs=[
                pl.BlockSpec((None, B_QT, nq), lambda b, i: (b, i, 0)),
                pl.BlockSpec((None, seq, nkv), lambda b, i: (b, 0, nq // nkv)),
                pl.BlockSpec((None, nkv, seq), lambda b, i: (b, 0, 0)),
                pl.BlockSpec((None, B_QT, nidx), lambda b, i: (b, i, 0)),
                pl.BlockSpec((None, seq, tn), lambda b, i: (b, 0, nidx // tn)),
                pl.BlockSpec((B_NDELTA, B_KV_HEADS, B_QT, (B_HEADS // B_KV_HEADS) * B_QT),
                             lambda b, i: (0, 0, 0, 0), pipeline_mode=pl.Buffered(1)),
                pl.BlockSpec((1, B_QT), lambda b, i: (0, 0)),
            ],
            out_specs=pl.BlockSpec((None, B_QT, nq), lambda b, i: (b, i, 0)),
            scratch_shapes=[pltpu.VMEM((seq, B_QT), I32), pltpu.VMEM((seq, B_QT), F32)],
            compiler_params=_cparams(("parallel", "arbitrary")),
            name="attn_b_bounded" if bounded else "attn_b",
        )(qk, qk, vt, idx, idx, bias_t, shift)

    bias_t = _bias_b(bias_b)
    qk_bound = 1.02 * HEAD_DIM * (HEAD_DIM ** -0.5 * LOG2E) * jnp.max(jnp.abs(q_gain)) * jnp.max(jnp.abs(k_gain))
    bound = qk_bound + jnp.max(bias_b) * LOG2E
    spread = bound + qk_bound - jnp.min(bias_b) * LOG2E
    y = lax.cond(spread <= B_SAFE_SPREAD,
                 lambda: attend(jnp.full((1, B_QT), bound, F32), True),
                 lambda: attend(jnp.zeros((1, B_QT), F32), False))
    return [y.reshape(batch * seq, nq)], [pl.BlockSpec((tm, nq), lambda i: (i, 0))]


def _in_proj_c_kernel(x_ref, g_ref, w_ref, cw_ref, qkv_ref, rest_ref, tail_ref, xs_ref, *, tn):
    i = pl.program_id(1)
    tm = x_ref.shape[0]
    hw = C_HEADS * C_DK
    hn = _rms_rows(x_ref[...], g_ref[...]).astype(BF16)
    for j in range(w_ref.shape[1] // tn):
        cols = slice(j * tn, (j + 1) * tn)
        y = jnp.dot(hn, w_ref[:, cols], preferred_element_type=F32)
        if j * tn >= 3 * hw:
            rest_ref[:, j * tn - 3 * hw:(j + 1) * tn - 3 * hw] = y
            continue
        xs_ref[0:8, cols] = jnp.where(i > 0, tail_ref[:, cols], 0.0)
        xs_ref[8:, cols] = y
        tail_ref[:, cols] = y[tm - 8:, :]
        cw = cw_ref[:, cols]
        conv = sum(cw[t:t + 1, :] * xs_ref[8 - (C_CONV - 1) + t:8 - (C_CONV - 1) + t + tm, cols]
                   for t in range(C_CONV))
        z = conv * jax.nn.sigmoid(conv)
        if j * tn >= 2 * hw:
            qkv_ref[:, cols] = z
            continue
        scale = C_DK ** -0.5 if j * tn < hw else 1.0
        for h in range(tn // C_DK):
            zh = z[:, h * C_DK:(h + 1) * C_DK]
            ss = jnp.sum(zh * zh, axis=-1, keepdims=True)
            qkv_ref[:, j * tn + h * C_DK:j * tn + (h + 1) * C_DK] = zh * (lax.rsqrt(ss + EPS) * scale)


def _delta_kernel(q_ref, k_ref, v_ref, gate_ref, a_ref, b_ref, alog_ref, dtb_ref, og_ref, y_ref,
                  rows_s, p_s, r_s, m_s, n_s, el_s, o_s):
    cs, gs = C_CHUNK, C_GROUP
    per = gs // cs
    seq, dv = v_ref.shape
    n_groups = seq // gs
    hi = lax.Precision.HIGHEST
    r = lax.broadcasted_iota(I32, (gs, gs), 0)
    c_ = lax.broadcasted_iota(I32, (gs, gs), 1)

    def same_block(size):
        sh = int(math.log2(size))
        return (r >> sh) == (c_ >> sh)

    chunk = same_block(cs)
    lower, strict, eye = chunk & (r >= c_), chunk & (r > c_), r == c_

    z = a_ref[...] + dtb_ref[...]
    softplus = jnp.maximum(z, 0.0) + jnp.log(1.0 + jnp.exp(-jnp.abs(z)))
    g_all = -jnp.exp(alog_ref[...]) * softplus
    gc_all = jnp.dot(g_all, jnp.where(chunk & (r <= c_), 1.0, 0.0), preferred_element_type=F32, precision=hi)
    gl_all = jnp.dot(g_all, jnp.where(chunk, 1.0, 0.0), preferred_element_type=F32, precision=hi)
    rows_s[0] = jax.nn.sigmoid(b_ref[...])
    rows_s[1] = gc_all
    rows_s[2] = jnp.exp(gc_all)
    rows_s[3] = jnp.exp(gl_all - gc_all)
    first = (lax.broadcasted_iota(I32, (gs, per * dv), 0)
             == (lax.broadcasted_iota(I32, (gs, per * dv), 1) // dv) * cs)
    el_s[...] = jnp.exp(jnp.dot(gl_all, jnp.where(first, 1.0, 0.0), preferred_element_type=F32, precision=hi))

    def mm(a, b):
        return jnp.dot(a.astype(BF16), b.astype(BF16), preferred_element_type=F32)

    def mm_nt(a, b):
        return lax.dot_general(a.astype(BF16), b.astype(BF16), (((1,), (1,)), ((), ())),
                               preferred_element_type=F32)

    ways = 4

    def prep(m):
        ids = [m * ways + a for a in range(ways)]
        rows = [pl.ds(pl.multiple_of(i * gs, gs), gs) for i in ids]
        rw = [jnp.concatenate([rows_s[j, pl.ds(i, 1), :] for j in range(4)] + [jnp.zeros((4, gs), F32)], axis=0)
              for i in ids]
        cl = [x.T for x in rw]
        beta_col, gc_col, eg_col, ekg_col = ([x[:, j:j + 1] for x in cl] for j in range(4))
        q, k, v = ([ref[rw_, :] for rw_ in rows] for ref in (q_ref, k_ref, v_ref))
        decay = [jnp.where(lower, jnp.exp(jnp.where(lower, gcc - x[1:2, :], 0.0)), 0.0)
                 for gcc, x in zip(gc_col, rw)]
        kb = [k_ * b_ for k_, b_ in zip(k, beta_col)]
        yield
        a_mat = [jnp.where(strict, mm_nt(kb_, k_) * d_, 0.0) for kb_, k_, d_ in zip(kb, k, decay)]
        yield
        a8 = [jnp.where(same_block(8), a_, 0.0) for a_ in a_mat]
        t = [jnp.where(eye, 1.0, 0.0) - a_ for a_ in a8]
        pw = [mm(a_, a_) for a_ in a8]
        yield
        t = [t_ + mm(t_, p_) for t_, p_ in zip(t, pw)]
        yield
        pw = [mm(p_, p_) for p_ in pw]
        yield
        t = [t_ + mm(t_, p_) for t_, p_ in zip(t, pw)]
        yield
        size = 8
        while size < cs:
            sel = same_block(2 * size) & jnp.logical_not(same_block(size))
            nt = [mm(jnp.where(sel, a_, 0.0), t_) for a_, t_ in zip(a_mat, t)]
            yield
            t = [t_ - mm(t_, n_) for t_, n_ in zip(t, nt)]
            yield
            size *= 2
        uw = [mm(t_, jnp.concatenate([v_ * b_, kb_ * e_], axis=1))
              for t_, v_, b_, kb_, e_ in zip(t, v, beta_col, kb, eg_col)]
        yield
        qk = [jnp.where(lower, mm_nt(q_, k_) * d_, 0.0) for q_, k_, d_ in zip(q, k, decay)]
        yield
        qkwu = [mm(qk_, uw_) for qk_, uw_ in zip(qk, uw)]
        kgt = [(k_ * e_).T for k_, e_ in zip(k, ekg_col)]
        yield
        nm = [[mm(kgt[a][:, j * cs:(j + 1) * cs], uw[a][j * cs:(j + 1) * cs, :]) for j in range(per)]
              for a in range(ways)]

        def write():
            for a, i in enumerate(ids):
                r_s[rows[a], :] = qkwu[a][:, :dv]
                p_s[rows[a], :] = (q[a] * eg_col[a] - qkwu[a][:, dv:]).astype(BF16)
                for j in range(per):
                    n_s[i * per + j] = nm[a][j][:, :dv]
                    m_s[i * per + j] = nm[a][j][:, dv:].astype(BF16)
        return write

    def scan(m, state):
        for a in range(ways):
            i = m * ways + a
            el = el_s[pl.ds(i, 1), :]
            for j in range(per):
                rows = pl.ds(pl.multiple_of(i * gs + j * cs, cs), cs)
                c = i * per + j
                sb = state[0].astype(BF16)
                o_s[rows, :] = jnp.dot(p_s[rows, :], sb, preferred_element_type=F32) + r_s[rows, :]
                state[0] = (state[0] * el[:, j * dv:(j + 1) * dv] + n_s[c]
                            - jnp.dot(m_s[c], sb, preferred_element_type=F32))
                yield

    def weave(*gens):
        gens, results = list(gens), [None] * len(gens)
        live = list(range(len(gens)))
        while live:
            for idx in list(live):
                try:
                    next(gens[idx])
                except StopIteration as stop:
                    results[idx] = stop.value
                    live.remove(idx)
        return results

    n_blocks = n_groups // ways
    weave(prep(0))[0]()

    def step(m, state):
        holder = [state]
        _, write = weave(scan(m - 1, holder), prep(m))
        write()
        return holder[0]

    state = lax.fori_loop(1, n_blocks, step, jnp.zeros((C_DK, dv), F32))

    def finish(first, last):
        for t in range(first, last):
            rows = pl.ds(t * gs, gs)
            gate = gate_ref[rows, :]
            y_ref[rows, :] = (_rms_rows(o_s[rows, :], og_ref[...])
                              * (gate * jax.nn.sigmoid(gate))).astype(y_ref.dtype)
            yield

    done = (n_blocks - 1) * ways
    weave(scan(n_blocks - 1, [state]), finish(0, done))
    weave(finish(done, n_groups))


def _mixer_c(x2, gain, w_in, conv_w, a_log, dt_bias, o_gain, batch, seq, tm):
    d = x2.shape[1]
    hw = C_HEADS * C_DK
    tn = 256
    used = 4 * hw + 2 * C_HEADS
    kout = -(-used // tn) * tn
    w = jnp.concatenate([w_in, jnp.zeros((d, kout - used), F32)], axis=1).astype(BF16)
    const = lambda shape: pl.BlockSpec(shape, lambda b, i: (0, 0), pipeline_mode=pl.Buffered(1))
    qkv, rest = pl.pallas_call(
        functools.partial(_in_proj_c_kernel, tn=tn),
        out_shape=[jax.ShapeDtypeStruct((batch, seq, 3 * hw), F32),
                   jax.ShapeDtypeStruct((batch, seq, kout - 3 * hw), F32)],
        grid=(batch, seq // tm),
        in_specs=[pl.BlockSpec((None, tm, d), lambda b, i: (b, i, 0)), const((1, d)), const((d, kout)),
                  const((C_CONV, 3 * hw))],
        out_specs=[pl.BlockSpec((None, tm, 3 * hw), lambda b, i: (b, i, 0)),
                   pl.BlockSpec((None, tm, kout - 3 * hw), lambda b, i: (b, i, 0))],
        scratch_shapes=[pltpu.VMEM((8, 3 * hw), F32), pltpu.VMEM((tm + 8, 3 * hw), F32)],
        compiler_params=_cparams(("parallel", "arbitrary")),
        name="in_proj_c",
    )(x2.reshape(batch, seq, d), gain.reshape(1, d), w, conv_w)

    n_groups = seq // C_GROUP
    ba = rest[:, :, hw:hw + 2 * C_HEADS].transpose(0, 2, 1).reshape(batch, 2 * C_HEADS, n_groups, C_GROUP)
    per_head = lambda v: jnp.broadcast_to(v.reshape(C_HEADS, 1, 1), (C_HEADS, 1, C_GROUP))
    head_cols = lambda off: pl.BlockSpec((None, seq, C_DK), lambda b, h: (b, 0, off + h))
    small = lambda off: pl.BlockSpec((None, None, n_groups, C_GROUP), lambda b, h: (b, off + h, 0, 0))
    scalar_row = pl.BlockSpec((None, 1, C_GROUP), lambda b, h: (h, 0, 0))
    y = pl.pallas_call(
        _delta_kernel,
        out_shape=jax.ShapeDtypeStruct((batch, seq, hw), BF16),
        grid=(batch, C_HEADS),
        in_specs=[head_cols(0), head_cols(C_HEADS), head_cols(2 * C_HEADS),
                  head_cols(0),
                  small(C_HEADS), small(0), scalar_row, scalar_row,
                  pl.BlockSpec((1, C_DK), lambda b, h: (0, 0))],
        out_specs=head_cols(0),
        scratch_shapes=[
            pltpu.VMEM((4, n_groups, C_GROUP), F32),
            pltpu.VMEM((seq, C_DK), BF16),
            pltpu.VMEM((seq, C_DK), F32),
            pltpu.VMEM((seq // C_CHUNK, C_DK, C_DK), BF16),
            pltpu.VMEM((seq // C_CHUNK, C_DK, C_DK), F32),
            pltpu.VMEM((n_groups, (C_GROUP // C_CHUNK) * C_DK), F32),
            pltpu.VMEM((seq, C_DK), F32),
        ],
        compiler_params=_cparams(("parallel", "parallel")),
        name="delta_c",
    )(qkv, qkv, qkv, rest, ba, ba, per_head(a_log), per_head(dt_bias), o_gain.reshape(1, C_DK))
    return [y.reshape(batch * seq, hw)], [pl.BlockSpec((tm, hw), lambda i: (i, 0))]


def kernel(x, rel_bias, norm_mix, norm_mlp, mlp_w1, mlp_w2, a_w_in, a_q_gain, a_k_gain, a_w_out,
           b_w_in, b_q_gain, b_k_gain, b_w_out, c_w_in, c_conv_w, c_a_log, c_dt_bias, c_o_gain, c_w_out):
    batch, seq, d = x.shape
    depth = norm_mix.shape[0]
    a_cols = len(A_GROUPS) * A_HEADS
    bias_a, bias_b = rel_bias[:, :a_cols], rel_bias[:, a_cols:]
    x2 = x.reshape(batch * seq, d)
    tm = 512
    for i in range(depth):
        kind, j = i % 3, i // 3
        if kind == 0:
            mix, specs = _mixer_a(x2, norm_mix[i], a_w_in[j], a_q_gain[j], a_k_gain[j], bias_a, batch, seq, tm)
            w_out = a_w_out[j]
        elif kind == 1:
            mix, specs = _mixer_b(x2, norm_mix[i], b_w_in[j], b_q_gain[j], b_k_gain[j], bias_b, batch, seq, tm)
            w_out = b_w_out[j]
        else:
            mix, specs = _mixer_c(x2, norm_mix[i], c_w_in[j], c_conv_w[j], c_a_log[j], c_dt_bias[j],
                                  c_o_gain[j], batch, seq, tm)
            w_out = c_w_out[j]
        x2 = _proj_mlp(mix, specs, w_out.astype(BF16), x2, norm_mlp[i], mlp_w1[i].astype(BF16),
                       mlp_w2[i].astype(BF16), groups_a=kind == 0, tm=tm)
    return x2.reshape(batch, seq, d)
```

```python
import functools
import math

import numpy as np
import jax
import jax.numpy as jnp
from jax import lax
from jax.experimental import pallas as pl
from jax.experimental.pallas import tpu as pltpu

F32 = jnp.float32
BF16 = jnp.bfloat16
I32 = jnp.int32

EPS = 1e-6
HEAD_DIM = 64
NEG = -1e30
INT_MIN = -2 ** 31
LOG2E = math.log2(math.e)

V7X_VMEM_BYTES = 64 * 1024 * 1024
VMEM_LIMIT = V7X_VMEM_BYTES - 8 * 1024 * 1024
V7X_MXU_DIM = 256

NUM_BUCKETS = 32
MAX_DISTANCE = 2048

A_GROUPS = ((128, 1), (512, 4), (2048, 16))
A_HEADS = 8
A_BLOCK = 128
A_GW = A_HEADS * HEAD_DIM
A_MAX_STEP = 4

B_HEADS = 16
B_KV_HEADS = 4
B_IDX_HEADS = 8
B_IDX_DIM = 64
B_TOPK = 256
B_QT = 128
B_KC = 256
B_NDELTA = 14
B_SAFE_SPREAD = 100.0

C_HEADS = 8
C_DK = 128
C_CONV = 4
C_CHUNK = 64
C_GROUP = 4 * C_CHUNK


def _cparams(sem):
    return pltpu.CompilerParams(dimension_semantics=sem, vmem_limit_bytes=VMEM_LIMIT)


def _t5_bucket(dist):
    max_exact = NUM_BUCKETS // 2
    d = jnp.maximum(dist, 1).astype(F32)
    log_part = jnp.log(d / max_exact) / math.log(MAX_DISTANCE / max_exact) * (NUM_BUCKETS - max_exact)
    large = jnp.minimum(max_exact + log_part.astype(I32), NUM_BUCKETS - 1)
    return jnp.where(dist < max_exact, dist, large)


def _rms_rows(x, gain_row):
    ms = jnp.mean(x * x, axis=-1, keepdims=True)
    return x * lax.rsqrt(ms + EPS) * gain_row


def _in_proj_b_kernel(x_ref, g_ref, w_ref, hg_ref, bd_ref, qkv_ref, idx_ref, *, tn, n_norm):
    n_qkv = qkv_ref.shape[1] // tn
    hn = _rms_rows(x_ref[...], g_ref[...]).astype(BF16)
    for j in range(w_ref.shape[1] // tn):
        cols = slice(j * tn, (j + 1) * tn)
        y = jnp.dot(hn, w_ref[:, cols], preferred_element_type=F32)
        if j < n_norm:
            ms = jnp.dot((y * y).astype(BF16), bd_ref[...], preferred_element_type=F32) * (1.0 / HEAD_DIM)
            y = y * lax.rsqrt(ms + EPS) * hg_ref[:, cols]
        if j < n_qkv:
            qkv_ref[:, cols] = y.astype(BF16)
        else:
            idx_ref[:, (j - n_qkv) * tn:(j - n_qkv + 1) * tn] = y


def _mix_groups_a(tm, o0, o1, o2, l0, l1, l2, pt4_ref, pt16_ref):

    def token_order(ref, pt_ref):
        v = ref[...]
        if pt_ref is None:
            return v.reshape(tm, A_GW)
        pb = pt_ref.shape[0]
        per = pb // v.shape[0]
        pieces = []
        for h in range(tm // pb):
            hi, lo = _split_bf16(v[:, h * per:(h + 1) * per, :].reshape(pb, A_GW))
            pieces.append(jnp.dot(pt_ref[...], hi, preferred_element_type=F32)
                          + jnp.dot(pt_ref[...], lo, preferred_element_type=F32))
        return jnp.concatenate(pieces, axis=0)

    pts = (None, pt4_ref, pt16_ref)
    a, b, c = (token_order(r, pt) for r, pt in zip((l0, l1, l2), pts))
    m = jnp.maximum(jnp.maximum(a, b), c)
    ea, eb, ec = jnp.exp2(a - m), jnp.exp2(b - m), jnp.exp2(c - m)
    oa, ob, oc = (token_order(r, pt) for r, pt in zip((o0, o1, o2), pts))
    return (ea * oa + eb * ob + ec * oc) / (ea + eb + ec)


def _proj_mlp_kernel(*refs, tf, groups_a):
    wo_ref, x_ref, g_ref, w1_ref, w2_ref, o_ref = refs[-6:]
    y = _mix_groups_a(x_ref.shape[0], *refs[:-6]) if groups_a else refs[0][...]
    x = x_ref[...] + jnp.dot(y.astype(BF16), wo_ref[...], preferred_element_type=F32)
    hn = _rms_rows(x, g_ref[...]).astype(BF16)
    acc = x
    for f in range(w1_ref.shape[1] // tf):
        cols = slice(f * tf, (f + 1) * tf)
        h = jnp.maximum(jnp.dot(hn, w1_ref[:, cols], preferred_element_type=F32), 0.0)
        acc = acc + jnp.dot((h * h).astype(BF16), w2_ref[cols, :], preferred_element_type=F32)
    o_ref[...] = acc


def _proj_mlp(mix, mix_specs, w_out, x2, gain, w1, w2, *, groups_a, tm, tf=1024):
    n, d = x2.shape
    dff = w1.shape[1]
    const = lambda shape: pl.BlockSpec(shape, lambda i: (0, 0), pipeline_mode=pl.Buffered(1))
    return pl.pallas_call(
        functools.partial(_proj_mlp_kernel, tf=tf, groups_a=groups_a),
        out_shape=jax.ShapeDtypeStruct((n, d), F32),
        grid=(n // tm,),
        in_specs=list(mix_specs) + [const(w_out.shape), pl.BlockSpec((tm, d), lambda i: (i, 0)), const((1, d)),
                                    const((d, dff)), const((dff, d))],
        out_specs=pl.BlockSpec((tm, d), lambda i: (i, 0)),
        compiler_params=_cparams(("parallel",)),
        name="proj_mlp_a" if groups_a else "proj_mlp",
    )(*mix, w_out, x2, gain.reshape(1, d), w1, w2)


def _residue_major(tm, dilation):
    p = np.zeros((tm, tm), np.float32)
    j, r = np.meshgrid(np.arange(tm // dilation), np.arange(dilation), indexing="ij")
    p[(r * (tm // dilation) + j).ravel(), (j * dilation + r).ravel()] = 1.0
    return p


def _in_proj_a_kernel(x_ref, g_ref, w_ref, hg_ref, bd_ref, *rest):
    n_groups = len(A_GROUPS)
    perms, outs = rest[:n_groups - 1], rest[n_groups - 1:]
    tm = x_ref.shape[0]
    hn = _rms_rows(x_ref[...], g_ref[...]).astype(BF16)
    for which in range(3):
        for g, (_, dilation) in enumerate(A_GROUPS):
            j = which * n_groups + g
            cols = slice(j * A_GW, (j + 1) * A_GW)
            y = jnp.dot(hn, w_ref[:, cols], preferred_element_type=F32)
            if which < 2:
                nb = bd_ref.shape[0]
                y2 = (y * y).astype(BF16)
                ms = jnp.concatenate([jnp.dot(y2[:, c:c + nb], bd_ref[...], preferred_element_type=F32)
                                      for c in range(0, A_GW, nb)], axis=1) * (1.0 / HEAD_DIM)
                y = y * lax.rsqrt(ms + EPS) * hg_ref[:, cols]
            y = y.astype(BF16)
            if dilation == 1:
                y = y.reshape(1, tm, A_GW)
            else:
                perm = perms[g - 1]
                pb = perm.shape[0]
                y = jnp.concatenate(
                    [jnp.dot(perm[...], y[h:h + pb], preferred_element_type=F32).astype(BF16)
                     .reshape(dilation, pb // dilation, A_GW) for h in range(0, tm, pb)], axis=1)
            outs[g][:, :, which * A_GW:(which + 1) * A_GW] = y


def _attn_a_kernel(q_ref, kp_ref, kc_ref, vp_ref, vc_ref, bias_ref, o_ref, lse_ref):
    n = pl.program_id(2)
    blk = A_BLOCK
    pair = 2 * HEAD_DIM
    lane = lax.broadcasted_iota(I32, (2 * blk, pair), 1)
    row = lax.broadcasted_iota(I32, (2 * blk, blk), 0)
    pen = jnp.where((row < blk) & (n == 0), NEG, 0.0)
    nt = (((1,), (1,)), ((), ()))
    tn = (((0,), (0,)), ((), ()))
    n_res, rows = q_ref.shape[0], q_ref.shape[1]
    k_all = [jnp.concatenate([kp_ref[r], kc_ref[r]], axis=0) for r in range(n_res)]
    v_all = [jnp.concatenate([vp_ref[r], vc_ref[r]], axis=0) for r in range(n_res)]
    chains = [(r, sb, hp, a) for r in range(n_res) for sb in range(rows // blk)
              for hp in range(A_HEADS // 2) for a in range(2)]

    def window(x, sb, hp):
        return x[sb * blk:(sb + 2) * blk, hp * pair:(hp + 1) * pair]

    keep = [lane < HEAD_DIM, lane >= HEAD_DIM]
    ka = [jnp.where(keep[a], window(k_all[r], sb, hp), jnp.zeros((2 * blk, pair), BF16))
          for r, sb, hp, a in chains]
    s = [lax.dot_general(k_, q_ref[r, sb * blk:(sb + 1) * blk, hp * pair:(hp + 1) * pair], nt,
                         preferred_element_type=F32) + bias_ref[2 * hp + a]
         for k_, (r, sb, hp, a) in zip(ka, chains)]
    s = [s_ + pen if sb == 0 else s_ for s_, (_, sb, _, _) in zip(s, chains)]
    m = [jnp.max(s_, axis=0, keepdims=True) for s_ in s]
    p = [jnp.exp2(s_ - m_) for s_, m_ in zip(s, m)]
    l = [jnp.sum(p_, axis=0, keepdims=True) for p_ in p]
    pv = [lax.dot_general(window(v_all[r], sb, hp), p_.astype(BF16), tn, preferred_element_type=F32)
          for p_, (r, sb, hp, _) in zip(p, chains)]
    for idx in range(0, len(chains), 2):
        r, sb, hp, _ = chains[idx]
        o_t = jnp.concatenate([(pv[idx + a] / l[idx + a])[a * HEAD_DIM:(a + 1) * HEAD_DIM] for a in range(2)],
                              axis=0)
        l_t = jnp.concatenate([jnp.broadcast_to(m[idx + a] + jnp.log2(l[idx + a]), (HEAD_DIM, blk))
                               for a in range(2)], axis=0)
        o_ref[r, sb * blk:(sb + 1) * blk, hp * pair:(hp + 1) * pair] = o_t.T
        lse_ref[r, sb * blk:(sb + 1) * blk, hp * pair:(hp + 1) * pair] = l_t.T


def _attn_a_group(arr, bias_t, dilation, batch, seq):
    sub = seq // dilation
    step = min(A_MAX_STEP, sub // A_BLOCK)
    rows = step * A_BLOCK
    nb = sub // rows
    n_res = min(dilation, A_MAX_STEP // step)

    def spec(which, prev):
        if prev:
            return pl.BlockSpec((None, n_res, A_BLOCK, A_GW),
                                lambda b, r, n: (b, r, jnp.maximum(n * step - 1, 0), which))
        return pl.BlockSpec((None, n_res, rows, A_GW), lambda b, r, n: (b, r, n, which))

    out_spec = pl.BlockSpec((None, n_res, rows, A_GW), lambda b, r, n: (b, r, n, 0))
    out_sds = jax.ShapeDtypeStruct((batch, dilation, sub, A_GW), F32)
    return pl.pallas_call(
        _attn_a_kernel,
        out_shape=[out_sds, out_sds],
        grid=(batch, dilation // n_res, nb),
        in_specs=[spec(0, False), spec(1, True), spec(1, False), spec(2, True), spec(2, False),
                  pl.BlockSpec((A_HEADS, 2 * A_BLOCK, A_BLOCK), lambda b, r, n: (0, 0, 0))],
        out_specs=[out_spec, out_spec],
        compiler_params=_cparams(("parallel", "parallel", "arbitrary")),
        name=f"attn_a_d{dilation}",
    )(arr, arr, arr, arr, arr, bias_t)


def _bias_a(tab_g, dilation):
    step = np.arange(3 * A_BLOCK - 1) - (A_BLOCK - 1)
    vec = tab_g[_t5_bucket(jnp.asarray(np.maximum(step, 0) * dilation, I32))].astype(F32) * LOG2E
    vec = jnp.where(jnp.asarray((step >= 0) & (step <= A_BLOCK))[:, None], vec, NEG)
    return _toeplitz(vec.T, 2 * A_BLOCK, A_BLOCK)


def _mixer_a(x2, gain, w_in, q_gain, k_gain, bias_a, batch, seq, tm):
    n, d = x2.shape
    n_groups = len(A_GROUPS)
    reps = n_groups * A_HEADS
    hg = jnp.concatenate([jnp.tile(q_gain * (HEAD_DIM ** -0.5 * LOG2E), reps), jnp.tile(k_gain, reps),
                          jnp.ones((n_groups * A_GW,), F32)]).reshape(1, -1)
    kout = 3 * n_groups * A_GW
    pb = V7X_MXU_DIM
    bd = np.kron(np.eye(pb // HEAD_DIM), np.ones((HEAD_DIM, HEAD_DIM))).astype(np.float32)
    perms = [_residue_major(pb, dil) for _, dil in A_GROUPS[1:]]
    tiles = seq // tm
    const = lambda shape: pl.BlockSpec(shape, lambda i: (0, 0), pipeline_mode=pl.Buffered(1))
    grouped = lambda dil, width: pl.BlockSpec((None, dil, tm // dil, width),
                                              lambda i: (i // tiles, 0, i % tiles, 0))
    arrs = pl.pallas_call(
        _in_proj_a_kernel,
        out_shape=[jax.ShapeDtypeStruct((batch, dil, seq // dil, 3 * A_GW), BF16) for _, dil in A_GROUPS],
        grid=(n // tm,),
        in_specs=[pl.BlockSpec((tm, d), lambda i: (i, 0)), const((1, d)), const((d, kout)),
                  const((1, kout)), const((pb, pb))] + [const((pb, pb)) for _ in perms],
        out_specs=[grouped(dil, 3 * A_GW) for _, dil in A_GROUPS],
        compiler_params=_cparams(("parallel",)),
        name="in_proj_a",
    )(x2, gain.reshape(1, d), w_in.astype(BF16), hg, jnp.asarray(bd, BF16),
      *[jnp.asarray(p, BF16) for p in perms])

    os_, ls_ = [], []
    for g, (window, dilation) in enumerate(A_GROUPS):
        assert window // dilation == A_BLOCK and (seq // dilation) % (min(A_MAX_STEP, seq // dilation // A_BLOCK) * A_BLOCK) == 0 and dilation <= pb // 16
        tab_g = bias_a[:, g * A_HEADS:(g + 1) * A_HEADS]
        o, lse = _attn_a_group(arrs[g], _bias_a(tab_g, dilation), dilation, batch, seq)
        os_.append(o)
        ls_.append(lse)

    mix = [*os_, *ls_, *[jnp.asarray(p.T, BF16) for p in perms]]
    specs = [grouped(dil, A_GW) for _, dil in A_GROUPS] * 2 + [const((pb, pb)) for _ in perms]
    return mix, specs


def _float_key(s):
    i = lax.bitcast_convert_type(s, I32)
    k = jnp.where(i < 0, i ^ jnp.int32(0x7FFFFFFF), i)
    return jnp.where(s == 0.0, 0, k)


def _split_bf16(x):
    hi = x.astype(BF16)
    return hi, (x - hi.astype(F32)).astype(BF16)


def _attn_b_kernel(q_ref, k_ref, vt_ref, qi_ref, kw_ref, bias_ref, shift_ref, o_ref, key_ref, msk_ref, *,
                   bounded):
    qb = pl.program_id(1)
    kc, qt = B_KC, B_QT
    nch = qb // 2 + 1
    t_q = qb * qt + lax.broadcasted_iota(I32, (kc, qt), 1)
    sub = lax.broadcasted_iota(I32, (kc, qt), 0)
    nt = (((1,), (1,)), ((), ()))

    qh, ql = _split_bf16(qi_ref[...])
    qi3 = jnp.concatenate(
        [jnp.concatenate([x[:, h * B_IDX_DIM:(h + 1) * B_IDX_DIM] for x in (qh, qh, ql)], axis=1)
         for h in range(B_IDX_HEADS)], axis=0)
    wt = kw_ref[pl.ds(pl.multiple_of(qb * qt, qt), qt), 0:qt].T
    wt = wt * (B_IDX_HEADS ** -0.5 * B_IDX_DIM ** -0.5)

    npair = (nch + 1) // 2

    def score_pair(c2, carry):
        offs = [pl.multiple_of((2 * c2 + a) * kc, kc) for a in range(2)]
        ks = [_split_bf16(kw_ref[pl.ds(off, kc), 0:B_IDX_DIM]) for off in offs]
        sc = [lax.dot_general(jnp.concatenate([kh, kl, kh], axis=1), qi3, nt, preferred_element_type=F32)
              for kh, kl in ks]
        for off, s in zip(offs, sc):
            acc = jnp.zeros((kc, qt), F32)
            for h in range(B_IDX_HEADS):
                acc = acc + wt[B_IDX_DIM + h:B_IDX_DIM + h + 1, :] * jnp.maximum(s[:, h * qt:(h + 1) * qt], 0.0)
            key_ref[pl.ds(off, kc), :] = jnp.where(off + sub <= t_q, _float_key(acc), INT_MIN)
        return carry

    lax.fori_loop(0, npair, score_pair, 0)

    def count(pred_fn):
        def body(c2, acc):
            for a in range(2):
                off = pl.multiple_of((2 * c2 + a) * kc, kc)
                hit = jnp.where(pred_fn(key_ref[pl.ds(off, kc), :], off + sub), 1, 0)
                acc = acc + jnp.sum(hit.reshape(kc // 8, 8, qt), axis=0)
            return acc
        acc = lax.fori_loop(0, npair, body, jnp.zeros((8, qt), I32))
        return jnp.sum(acc, axis=0, keepdims=True)

    def thr_bit(it, lo):
        cand = lo + jnp.left_shift(jnp.int32(1), 31 - it)
        cnt = count(lambda key, _: key >= cand)
        return jnp.where(cnt >= B_TOPK, cand, lo)

    thr = lax.fori_loop(0, 32, thr_bit, jnp.full((1, qt), INT_MIN, I32))
    need = B_TOPK - count(lambda key, _: key > thr)
    n_eq = count(lambda key, _: key == thr)

    def tie_search(_):
        def bit(it, j):
            cand = j + jnp.left_shift(jnp.int32(1), 12 - it)
            cnt = count(lambda key, idx: (key == thr) & (idx < cand))
            return jnp.where(cnt <= need, cand, j)
        return lax.fori_loop(0, 13, bit, jnp.zeros((1, qt), I32))

    any_tie = jnp.max(jnp.where(n_eq > need, 1, 0)) > 0
    j_max = lax.cond(any_tie, tie_search, lambda _: jnp.full((1, qt), 2 ** 13, I32), 0)

    def mask_chunk(c, carry):
        off = pl.multiple_of(c * kc, kc)
        key = key_ref[pl.ds(off, kc), :]
        idx = off + sub
        sel = ((key > thr) | ((key == thr) & (idx < j_max))) & (idx <= t_q)
        msk_ref[pl.ds(off, kc), :] = jnp.where(sel, -shift_ref[...], NEG)
        return carry

    lax.fori_loop(0, 2 * npair, mask_chunk, 0)

    grp = B_HEADS // B_KV_HEADS
    q = q_ref[...]
    groups = range(B_KV_HEADS)
    qg = [jnp.concatenate([q[:, (g * grp + j) * HEAD_DIM:(g * grp + j + 1) * HEAD_DIM]
                           for j in range(grp)], axis=0) for g in groups]
    ksl = [slice(g * HEAD_DIM, (g + 1) * HEAD_DIM) for g in groups]

    fk = 2 * kc

    def flash(c2, carry):
        m, l, acc = carry
        off = pl.multiple_of(c2 * fk, fk)
        tiles = [jnp.clip(qb - (fk // qt) * c2 - a, 0, B_NDELTA - 1) for a in range(fk // qt)]
        mk = msk_ref[pl.ds(off, fk), :]
        mk = jnp.concatenate([mk] * grp, axis=1)
        s = [lax.dot_general(k_ref[pl.ds(off, fk), ksl[g]], qg[g], nt, preferred_element_type=F32)
             for g in groups]
        s = [s[g] + jnp.concatenate([bias_ref[d, g] for d in tiles], axis=0) + mk for g in groups]
        if bounded:
            p = [jnp.exp2(s[g]) for g in groups]
            l = [l[g] + jnp.sum(p[g], axis=0, keepdims=True) for g in groups]
            p = [p[g].astype(BF16) for g in groups]
            pv = [jnp.dot(vt_ref[ksl[g], pl.ds(off, fk)], p[g], preferred_element_type=F32) for g in groups]
            return m, l, [acc[g] + pv[g] for g in groups]
        else:
            m_new = [jnp.maximum(m[g], jnp.max(s[g], axis=0, keepdims=True)) for g in groups]
            p = [jnp.exp2(s[g] - m_new[g]) for g in groups]
        pv = [jnp.dot(vt_ref[ksl[g], pl.ds(off, fk)], p[g].astype(BF16), preferred_element_type=F32)
              for g in groups]
        alpha = [jnp.exp2(m[g] - m_new[g]) for g in groups]
        l = [alpha[g] * l[g] + jnp.sum(p[g], axis=0, keepdims=True) for g in groups]
        acc = [alpha[g] * acc[g] + pv[g] for g in groups]
        return m_new, l, acc

    m0 = [jnp.full((1, grp * qt), NEG, F32) for _ in groups]
    l0 = [jnp.zeros((1, grp * qt), F32) for _ in groups]
    a0 = [jnp.zeros((HEAD_DIM, grp * qt), F32) for _ in groups]
    _, l, acc = lax.fori_loop(0, npair, flash, (m0, l0, a0))
    for g in groups:
        o = acc[g] / l[g]
        for j in range(0, grp, 2):
            h = g * grp + j
            pair = jnp.concatenate([o[:, j * qt:(j + 1) * qt], o[:, (j + 1) * qt:(j + 2) * qt]], axis=0)
            o_ref[:, h * HEAD_DIM:(h + 2) * HEAD_DIM] = pair.T.astype(o_ref.dtype)


def _toeplitz(vec, n_rows, n_cols):
    span = n_rows + n_cols - 1
    assert vec.shape[-1] == span
    lead = vec.shape[:-1]
    padded = jnp.concatenate([vec, jnp.zeros(lead + (1,), vec.dtype)], axis=-1)
    flat = jnp.tile(padded, n_rows)[..., :n_rows * span]
    return flat.reshape(lead + (n_rows, span))[..., n_rows - 1:]


def _bias_b(tab):
    grp = B_HEADS // B_KV_HEADS
    span = 2 * B_QT - 1
    dist = np.arange(B_NDELTA)[:, None] * B_QT + np.arange(span)[None, :] - (B_QT - 1)
    assert (B_NDELTA - 1) * B_QT - (B_QT - 1) >= 16 * 128 ** (15.0 / 16.0) + 1
    vec = tab[_t5_bucket(jnp.asarray(np.maximum(dist, 0), I32))].astype(F32) * LOG2E
    t = _toeplitz(vec.transpose(0, 2, 1), B_QT, B_QT)
    t = t.reshape(B_NDELTA, B_KV_HEADS, grp, B_QT, B_QT).transpose(0, 1, 3, 2, 4)
    return t.reshape(B_NDELTA, B_KV_HEADS, B_QT, grp * B_QT)


def _mixer_b(x2, gain, w_in, q_gain, k_gain, bias_b, batch, seq, tm):
    d = x2.shape[1]
    nq, nkv = B_HEADS * HEAD_DIM, B_KV_HEADS * HEAD_DIM
    nidx = B_IDX_HEADS * B_IDX_DIM
    tn = 256
    used = 2 * nkv + nq + nidx + B_IDX_DIM + B_IDX_HEADS
    kout = -(-used // tn) * tn
    w = jnp.concatenate([w_in, jnp.zeros((d, kout - used), F32)], axis=1).astype(BF16)
    hg = jnp.concatenate([jnp.tile(q_gain * (HEAD_DIM ** -0.5 * LOG2E), B_HEADS), jnp.tile(k_gain, B_KV_HEADS),
                          jnp.ones((kout - nq - nkv,), F32)]).reshape(1, -1)
    n = batch * seq
    nrest = kout - nq - 2 * nkv
    bd = np.kron(np.eye(tn // HEAD_DIM), np.ones((HEAD_DIM, HEAD_DIM))).astype(np.float32)
    const = lambda shape: pl.BlockSpec(shape, lambda i: (0, 0), pipeline_mode=pl.Buffered(1))
    qk, idx = pl.pallas_call(
        functools.partial(_in_proj_b_kernel, tn=tn, n_norm=(nq + nkv) // tn),
        out_shape=[jax.ShapeDtypeStruct((n, nq + 2 * nkv), BF16), jax.ShapeDtypeStruct((n, nrest), F32)],
        grid=(n // tm,),
        in_specs=[pl.BlockSpec((tm, d), lambda i: (i, 0)), const((1, d)), const((d, kout)),
                  const((1, kout)), const((tn, tn))],
        out_specs=[pl.BlockSpec((tm, nq + 2 * nkv), lambda i: (i, 0)),
                   pl.BlockSpec((tm, nrest), lambda i: (i, 0))],
        compiler_params=_cparams(("parallel",)),
        name="in_proj_b",
    )(x2, gain.reshape(1, d), w, hg, jnp.asarray(bd, BF16))
    qk = qk.reshape(batch, seq, nq + 2 * nkv)
    idx = idx.reshape(batch, seq, nrest)
    vt = qk[:, :, nq + nkv:].transpose(0, 2, 1)
    assert seq % (2 * B_QT) == 0 and seq <= 2 ** 12 and nkv == tn and nrest == nidx + tn

    def attend(shift, bounded):
        return pl.pallas_call(
            functools.partial(_attn_b_kernel, bounded=bounded),
            out_shape=jax.ShapeDtypeStruct((batch, seq, nq), BF16),
            grid=(batch, seq // B_QT),
            in_specs=[
                pl.BlockSpec((None, B_QT, nq), lambda b, i: (b, i, 0)),
                pl.BlockSpec((None, seq, nkv), lambda b, i: (b, 0, nq // nkv)),
                pl.BlockSpec((None, nkv, seq), lambda b, i: (b, 0, 0)),
                pl.BlockSpec((None, B_QT, nidx), lambda b, i: (b, i, 0)),
                pl.BlockSpec((None, seq, tn), lambda b, i: (b, 0, nidx // tn)),
                pl.BlockSpec((B_NDELTA, B_KV_HEADS, B_QT, (B_HEADS // B_KV_HEADS) * B_QT),
                             lambda b, i: (0, 0, 0, 0), pipeline_mode=pl.Buffered(1)),
                pl.BlockSpec((1, B_QT), lambda b, i: (0, 0)),
            ],
            out_specs=pl.BlockSpec((None, B_QT, nq), lambda b, i: (b, i, 0)),
            scratch_shapes=[pltpu.VMEM((seq, B_QT), I32), pltpu.VMEM((seq, B_QT), F32)],
            compiler_params=_cparams(("parallel", "arbitrary")),
            name="attn_b_bounded" if bounded else "attn_b",
        )(qk, qk, vt, idx, idx, bias_t, shift)

    bias_t = _bias_b(bias_b)
    qk_bound = 1.02 * HEAD_DIM * (HEAD_DIM ** -0.5 * LOG2E) * jnp.max(jnp.abs(q_gain)) * jnp.max(jnp.abs(k_gain))
    bound = qk_bound + jnp.max(bias_b) * LOG2E
    spread = bound + qk_bound - jnp.min(bias_b) * LOG2E
    y = lax.cond(spread <= B_SAFE_SPREAD,
                 lambda: attend(jnp.full((1, B_QT), bound, F32), True),
                 lambda: attend(jnp.zeros((1, B_QT), F32), False))
    return [y.reshape(batch * seq, nq)], [pl.BlockSpec((tm, nq), lambda i: (i, 0))]


def _in_proj_c_kernel(x_ref, g_ref, w_ref, cw_ref, qkv_ref, rest_ref, tail_ref, xs_ref, *, tn):
    i = pl.program_id(1)
    tm = x_ref.shape[0]
    hw = C_HEADS * C_DK
    hn = _rms_rows(x_ref[...], g_ref[...]).astype(BF16)
    for j in range(w_ref.shape[1] // tn):
        cols = slice(j * tn, (j + 1) * tn)
        y = jnp.dot(hn, w_ref[:, cols], preferred_element_type=F32)
        if j * tn >= 3 * hw:
            rest_ref[:, j * tn - 3 * hw:(j + 1) * tn - 3 * hw] = y
            continue
        xs_ref[0:8, cols] = jnp.where(i > 0, tail_ref[:, cols], 0.0)
        xs_ref[8:, cols] = y
        tail_ref[:, cols] = y[tm - 8:, :]
        cw = cw_ref[:, cols]
        conv = sum(cw[t:t + 1, :] * xs_ref[8 - (C_CONV - 1) + t:8 - (C_CONV - 1) + t + tm, cols]
                   for t in range(C_CONV))
        z = conv * jax.nn.sigmoid(conv)
        if j * tn >= 2 * hw:
            qkv_ref[:, cols] = z
            continue
        scale = C_DK ** -0.5 if j * tn < hw else 1.0
        for h in range(tn // C_DK):
            zh = z[:, h * C_DK:(h + 1) * C_DK]
            ss = jnp.sum(zh * zh, axis=-1, keepdims=True)
            qkv_ref[:, j * tn + h * C_DK:j * tn + (h + 1) * C_DK] = zh * (lax.rsqrt(ss + EPS) * scale)


def _delta_kernel(q_ref, k_ref, v_ref, gate_ref, a_ref, b_ref, alog_ref, dtb_ref, og_ref, y_ref,
                  rows_s, p_s, r_s, m_s, n_s, el_s, o_s):
    cs, gs = C_CHUNK, C_GROUP
    per = gs // cs
    seq, dv = v_ref.shape
    n_groups = seq // gs
    hi = lax.Precision.HIGHEST
    r = lax.broadcasted_iota(I32, (gs, gs), 0)
    c_ = lax.broadcasted_iota(I32, (gs, gs), 1)

    def same_block(size):
        sh = int(math.log2(size))
        return (r >> sh) == (c_ >> sh)

    chunk = same_block(cs)
    lower, strict, eye = chunk & (r >= c_), chunk & (r > c_), r == c_

    z = a_ref[...] + dtb_ref[...]
    softplus = jnp.maximum(z, 0.0) + jnp.log(1.0 + jnp.exp(-jnp.abs(z)))
    g_all = -jnp.exp(alog_ref[...]) * softplus
    gc_all = jnp.dot(g_all, jnp.where(chunk & (r <= c_), 1.0, 0.0), preferred_element_type=F32, precision=hi)
    gl_all = jnp.dot(g_all, jnp.where(chunk, 1.0, 0.0), preferred_element_type=F32, precision=hi)
    rows_s[0] = jax.nn.sigmoid(b_ref[...])
    rows_s[1] = gc_all
    rows_s[2] = jnp.exp(gc_all)
    rows_s[3] = jnp.exp(gl_all - gc_all)
    first = (lax.broadcasted_iota(I32, (gs, per * dv), 0)
             == (lax.broadcasted_iota(I32, (gs, per * dv), 1) // dv) * cs)
    el_s[...] = jnp.exp(jnp.dot(gl_all, jnp.where(first, 1.0, 0.0), preferred_element_type=F32, precision=hi))

    def mm(a, b):
        return jnp.dot(a.astype(BF16), b.astype(BF16), preferred_element_type=F32)

    def mm_nt(a, b):
        return lax.dot_general(a.astype(BF16), b.astype(BF16), (((1,), (1,)), ((), ())),
                               preferred_element_type=F32)

    ways = 4

    def prep(m):
        ids = [m * ways + a for a in range(ways)]
        rows = [pl.ds(pl.multiple_of(i * gs, gs), gs) for i in ids]
        rw = [jnp.concatenate([rows_s[j, pl.ds(i, 1), :] for j in range(4)] + [jnp.zeros((4, gs), F32)], axis=0)
              for i in ids]
        cl = [x.T for x in rw]
        beta_col, gc_col, eg_col, ekg_col = ([x[:, j:j + 1] for x in cl] for j in range(4))
        q, k, v = ([ref[rw_, :] for rw_ in rows] for ref in (q_ref, k_ref, v_ref))
        decay = [jnp.where(lower, jnp.exp(jnp.where(lower, gcc - x[1:2, :], 0.0)), 0.0)
                 for gcc, x in zip(gc_col, rw)]
        kb = [k_ * b_ for k_, b_ in zip(k, beta_col)]
        yield
        a_mat = [jnp.where(strict, mm_nt(kb_, k_) * d_, 0.0) for kb_, k_, d_ in zip(kb, k, decay)]
        yield
        a8 = [jnp.where(same_block(8), a_, 0.0) for a_ in a_mat]
        t = [jnp.where(eye, 1.0, 0.0) - a_ for a_ in a8]
        pw = [mm(a_, a_) for a_ in a8]
        yield
        t = [t_ + mm(t_, p_) for t_, p_ in zip(t, pw)]
        yield
        pw = [mm(p_, p_) for p_ in pw]
        yield
        t = [t_ + mm(t_, p_) for t_, p_ in zip(t, pw)]
        yield
        size = 8
        while size < cs:
            sel = same_block(2 * size) & jnp.logical_not(same_block(size))
            nt = [mm(jnp.where(sel, a_, 0.0), t_) for a_, t_ in zip(a_mat, t)]
            yield
            t = [t_ - mm(t_, n_) for t_, n_ in zip(t, nt)]
            yield
            size *= 2
        uw = [mm(t_, jnp.concatenate([v_ * b_, kb_ * e_], axis=1))
              for t_, v_, b_, kb_, e_ in zip(t, v, beta_col, kb, eg_col)]
        yield
        qk = [jnp.where(lower, mm_nt(q_, k_) * d_, 0.0) for q_, k_, d_ in zip(q, k, decay)]
        yield
        qkwu = [mm(qk_, uw_) for qk_, uw_ in zip(qk, uw)]
        kgt = [(k_ * e_).T for k_, e_ in zip(k, ekg_col)]
        yield
        nm = [[mm(kgt[a][:, j * cs:(j + 1) * cs], uw[a][j * cs:(j + 1) * cs, :]) for j in range(per)]
              for a in range(ways)]

        def write():
            for a, i in enumerate(ids):
                r_s[rows[a], :] = qkwu[a][:, :dv]
                p_s[rows[a], :] = (q[a] * eg_col[a] - qkwu[a][:, dv:]).astype(BF16)
                for j in range(per):
                    n_s[i * per + j] = nm[a][j][:, :dv]
                    m_s[i * per + j] = nm[a][j][:, dv:].astype(BF16)
        return write

    def scan(m, state):
        for a in range(ways):
            i = m * ways + a
            el = el_s[pl.ds(i, 1), :]
            for j in range(per):
                rows = pl.ds(pl.multiple_of(i * gs + j * cs, cs), cs)
                c = i * per + j
                sb = state[0].astype(BF16)
                o_s[rows, :] = jnp.dot(p_s[rows, :], sb, preferred_element_type=F32) + r_s[rows, :]
                state[0] = (state[0] * el[:, j * dv:(j + 1) * dv] + n_s[c]
                            - jnp.dot(m_s[c], sb, preferred_element_type=F32))
                yield

    def weave(*gens):
        gens, results = list(gens), [None] * len(gens)
        live = list(range(len(gens)))
        while live:
            for idx in list(live):
                try:
                    next(gens[idx])
                except StopIteration as stop:
                    results[idx] = stop.value
                    live.remove(idx)
        return results

    n_blocks = n_groups // ways
    weave(prep(0))[0]()

    def step(m, state):
        holder = [state]
        _, write = weave(scan(m - 1, holder), prep(m))
        write()
        return holder[0]

    state = lax.fori_loop(1, n_blocks, step, jnp.zeros((C_DK, dv), F32))

    def finish(first, last):
        for t in range(first, last):
            rows = pl.ds(t * gs, gs)
            gate = gate_ref[rows, :]
            y_ref[rows, :] = (_rms_rows(o_s[rows, :], og_ref[...])
                              * (gate * jax.nn.sigmoid(gate))).astype(y_ref.dtype)
            yield

    done = (n_blocks - 1) * ways
    weave(scan(n_blocks - 1, [state]), finish(0, done))
    weave(finish(done, n_groups))


def _mixer_c(x2, gain, w_in, conv_w, a_log, dt_bias, o_gain, batch, seq, tm):
    d = x2.shape[1]
    hw = C_HEADS * C_DK
    tn = 256
    used = 4 * hw + 2 * C_HEADS
    kout = -(-used // tn) * tn
    w = jnp.concatenate([w_in, jnp.zeros((d, kout - used), F32)], axis=1).astype(BF16)
    const = lambda shape: pl.BlockSpec(shape, lambda b, i: (0, 0), pipeline_mode=pl.Buffered(1))
    qkv, rest = pl.pallas_call(
        functools.partial(_in_proj_c_kernel, tn=tn),
        out_shape=[jax.ShapeDtypeStruct((batch, seq, 3 * hw), F32),
                   jax.ShapeDtypeStruct((batch, seq, kout - 3 * hw), F32)],
        grid=(batch, seq // tm),
        in_specs=[pl.BlockSpec((None, tm, d), lambda b, i: (b, i, 0)), const((1, d)), const((d, kout)),
                  const((C_CONV, 3 * hw))],
        out_specs=[pl.BlockSpec((None, tm, 3 * hw), lambda b, i: (b, i, 0)),
                   pl.BlockSpec((None, tm, kout - 3 * hw), lambda b, i: (b, i, 0))],
        scratch_shapes=[pltpu.VMEM((8, 3 * hw), F32), pltpu.VMEM((tm + 8, 3 * hw), F32)],
        compiler_params=_cparams(("parallel", "arbitrary")),
        name="in_proj_c",
    )(x2.reshape(batch, seq, d), gain.reshape(1, d), w, conv_w)

    n_groups = seq // C_GROUP
    ba = rest[:, :, hw:hw + 2 * C_HEADS].transpose(0, 2, 1).reshape(batch, 2 * C_HEADS, n_groups, C_GROUP)
    per_head = lambda v: jnp.broadcast_to(v.reshape(C_HEADS, 1, 1), (C_HEADS, 1, C_GROUP))
    head_cols = lambda off: pl.BlockSpec((None, seq, C_DK), lambda b, h: (b, 0, off + h))
    small = lambda off: pl.BlockSpec((None, None, n_groups, C_GROUP), lambda b, h: (b, off + h, 0, 0))
    scalar_row = pl.BlockSpec((None, 1, C_GROUP), lambda b, h: (h, 0, 0))
    y = pl.pallas_call(
        _delta_kernel,
        out_shape=jax.ShapeDtypeStruct((batch, seq, hw), BF16),
        grid=(batch, C_HEADS),
        in_specs=[head_cols(0), head_cols(C_HEADS), head_cols(2 * C_HEADS),
                  head_cols(0),
                  small(C_HEADS), small(0), scalar_row, scalar_row,
                  pl.BlockSpec((1, C_DK), lambda b, h: (0, 0))],
        out_specs=head_cols(0),
        scratch_shapes=[
            pltpu.VMEM((4, n_groups, C_GROUP), F32),
            pltpu.VMEM((seq, C_DK), BF16),
            pltpu.VMEM((seq, C_DK), F32),
            pltpu.VMEM((seq // C_CHUNK, C_DK, C_DK), BF16),
            pltpu.VMEM((seq // C_CHUNK, C_DK, C_DK), F32),
            pltpu.VMEM((n_groups, (C_GROUP // C_CHUNK) * C_DK), F32),
            pltpu.VMEM((seq, C_DK), F32),
        ],
        compiler_params=_cparams(("parallel", "parallel")),
        name="delta_c",
    )(qkv, qkv, qkv, rest, ba, ba, per_head(a_log), per_head(dt_bias), o_gain.reshape(1, C_DK))
    return [y.reshape(batch * seq, hw)], [pl.BlockSpec((tm, hw), lambda i: (i, 0))]


def kernel(x, rel_bias, norm_mix, norm_mlp, mlp_w1, mlp_w2, a_w_in, a_q_gain, a_k_gain, a_w_out,
           b_w_in, b_q_gain, b_k_gain, b_w_out, c_w_in, c_conv_w, c_a_log, c_dt_bias, c_o_gain, c_w_out):
    batch, seq, d = x.shape
    depth = norm_mix.shape[0]
    a_cols = len(A_GROUPS) * A_HEADS
    bias_a, bias_b = rel_bias[:, :a_cols], rel_bias[:, a_cols:]
    x2 = x.reshape(batch * seq, d)
    tm = 512
    for i in range(depth):
        kind, j = i % 3, i // 3
        if kind == 0:
            mix, specs = _mixer_a(x2, norm_mix[i], a_w_in[j], a_q_gain[j], a_k_gain[j], bias_a, batch, seq, tm)
            w_out = a_w_out[j]
        elif kind == 1:
            mix, specs = _mixer_b(x2, norm_mix[i], b_w_in[j], b_q_gain[j], b_k_gain[j], bias_b, batch, seq, tm)
            w_out = b_w_out[j]
        else:
            mix, specs = _mixer_c(x2, norm_mix[i], c_w_in[j], c_conv_w[j], c_a_log[j], c_dt_bias[j],
                                  c_o_gain[j], batch, seq, tm)
            w_out = c_w_out[j]
        x2 = _proj_mlp(mix, specs, w_out.astype(BF16), x2, norm_mlp[i], mlp_w1[i].astype(BF16),
                       mlp_w2[i].astype(BF16), groups_a=kind == 0, tm=tm)
    return x2.reshape(batch, seq, d)
```

```python
import functools
import math

import numpy as np
import jax
import jax.numpy as jnp
from jax import lax
from jax.experimental import pallas as pl
from jax.experimental.pallas import tpu as pltpu

F32 = jnp.float32
BF16 = jnp.bfloat16
I32 = jnp.int32

EPS = 1e-6
HEAD_DIM = 64
NEG = -1e30
INT_MIN = -2 ** 31
LOG2E = math.log2(math.e)

V7X_VMEM_BYTES = 64 * 1024 * 1024
VMEM_LIMIT = V7X_VMEM_BYTES - 8 * 1024 * 1024
V7X_MXU_DIM = 256
V7X_SUBLANES = 8

NUM_BUCKETS = 32
MAX_DISTANCE = 2048

A_GROUPS = ((128, 1), (512, 4), (2048, 16))
A_HEADS = 8
A_BLOCK = 128
A_GW = A_HEADS * HEAD_DIM
A_MAX_STEP = 4

B_HEADS = 16
B_KV_HEADS = 4
B_IDX_HEADS = 8
B_IDX_DIM = 64
B_TOPK = 256
B_QT = 128
B_KC = 256
B_NDELTA = 14
B_SAFE_SPREAD = 100.0

C_HEADS = 8
C_DK = 128
C_CONV = 4
C_CHUNK = 64
C_GROUP = 4 * C_CHUNK


def _cparams(sem):
    return pltpu.CompilerParams(dimension_semantics=sem, vmem_limit_bytes=VMEM_LIMIT)


def _t5_bucket(dist):
    max_exact = NUM_BUCKETS // 2
    d = jnp.maximum(dist, 1).astype(F32)
    log_part = jnp.log(d / max_exact) / math.log(MAX_DISTANCE / max_exact) * (NUM_BUCKETS - max_exact)
    large = jnp.minimum(max_exact + log_part.astype(I32), NUM_BUCKETS - 1)
    return jnp.where(dist < max_exact, dist, large)


def _rms_rows(x, gain_row):
    ms = jnp.mean(x * x, axis=-1, keepdims=True)
    return x * lax.rsqrt(ms + EPS) * gain_row


def _in_proj_b_kernel(x_ref, g_ref, w_ref, hg_ref, bd_ref, qkv_ref, idx_ref, *, tn, n_norm):
    n_qkv = qkv_ref.shape[1] // tn
    hn = _rms_rows(x_ref[...], g_ref[...]).astype(BF16)
    for j in range(w_ref.shape[1] // tn):
        cols = slice(j * tn, (j + 1) * tn)
        y = jnp.dot(hn, w_ref[:, cols], preferred_element_type=F32)
        if j < n_norm:
            ms = jnp.dot((y * y).astype(BF16), bd_ref[...], preferred_element_type=F32) * (1.0 / HEAD_DIM)
            y = y * lax.rsqrt(ms + EPS) * hg_ref[:, cols]
        if j < n_qkv:
            qkv_ref[:, cols] = y.astype(BF16)
        else:
            idx_ref[:, (j - n_qkv) * tn:(j - n_qkv + 1) * tn] = y


def _mix_groups_a(tm, o0, o1, o2, l0, l1, l2, pt4_ref, pt16_ref):

    def token_order(ref, pt_ref):
        v = ref[...]
        if pt_ref is None:
            return v.reshape(tm, A_GW)
        pb = pt_ref.shape[0]
        per = pb // v.shape[0]
        pieces = []
        for h in range(tm // pb):
            hi, lo = _split_bf16(v[:, h * per:(h + 1) * per, :].reshape(pb, A_GW))
            pieces.append(jnp.dot(pt_ref[...], hi, preferred_element_type=F32)
                          + jnp.dot(pt_ref[...], lo, preferred_element_type=F32))
        return jnp.concatenate(pieces, axis=0)

    pts = (None, pt4_ref, pt16_ref)
    a, b, c = (token_order(r, pt) for r, pt in zip((l0, l1, l2), pts))
    m = jnp.maximum(jnp.maximum(a, b), c)
    ea, eb, ec = jnp.exp2(a - m), jnp.exp2(b - m), jnp.exp2(c - m)
    oa, ob, oc = (token_order(r, pt) for r, pt in zip((o0, o1, o2), pts))
    return (ea * oa + eb * ob + ec * oc) / (ea + eb + ec)


def _proj_mlp_kernel(*refs, tf, groups_a):
    wo_ref, x_ref, g_ref, w1_ref, w2_ref, o_ref = refs[-6:]
    y = _mix_groups_a(x_ref.shape[0], *refs[:-6]) if groups_a else refs[0][...]
    x = x_ref[...] + jnp.dot(y.astype(BF16), wo_ref[...], preferred_element_type=F32)
    hn = _rms_rows(x, g_ref[...]).astype(BF16)
    acc = x
    for f in range(w1_ref.shape[1] // tf):
        cols = slice(f * tf, (f + 1) * tf)
        h = jnp.maximum(jnp.dot(hn, w1_ref[:, cols], preferred_element_type=F32), 0.0)
        acc = acc + jnp.dot((h * h).astype(BF16), w2_ref[cols, :], preferred_element_type=F32)
    o_ref[...] = acc


def _proj_mlp(mix, mix_specs, w_out, x2, gain, w1, w2, *, groups_a, tm, tf=1024):
    n, d = x2.shape
    dff = w1.shape[1]
    const = lambda shape: pl.BlockSpec(shape, lambda i: (0, 0), pipeline_mode=pl.Buffered(1))
    return pl.pallas_call(
        functools.partial(_proj_mlp_kernel, tf=tf, groups_a=groups_a),
        out_shape=jax.ShapeDtypeStruct((n, d), F32),
        grid=(n // tm,),
        in_specs=list(mix_specs) + [const(w_out.shape), pl.BlockSpec((tm, d), lambda i: (i, 0)), const((1, d)),
                                    const((d, dff)), const((dff, d))],
        out_specs=pl.BlockSpec((tm, d), lambda i: (i, 0)),
        compiler_params=_cparams(("parallel",)),
        name="proj_mlp_a" if groups_a else "proj_mlp",
    )(*mix, w_out, x2, gain.reshape(1, d), w1, w2)


def _residue_major(tm, dilation):
    p = np.zeros((tm, tm), np.float32)
    j, r = np.meshgrid(np.arange(tm // dilation), np.arange(dilation), indexing="ij")
    p[(r * (tm // dilation) + j).ravel(), (j * dilation + r).ravel()] = 1.0
    return p


def _in_proj_a_kernel(x_ref, g_ref, w_ref, hg_ref, bd_ref, *rest):
    n_groups = len(A_GROUPS)
    perms, outs = rest[:n_groups - 1], rest[n_groups - 1:]
    tm = x_ref.shape[0]
    hn = _rms_rows(x_ref[...], g_ref[...]).astype(BF16)
    for which in range(3):
        for g, (_, dilation) in enumerate(A_GROUPS):
            j = which * n_groups + g
            cols = slice(j * A_GW, (j + 1) * A_GW)
            y = jnp.dot(hn, w_ref[:, cols], preferred_element_type=F32)
            if which < 2:
                nb = bd_ref.shape[0]
                y2 = (y * y).astype(BF16)
                ms = jnp.concatenate([jnp.dot(y2[:, c:c + nb], bd_ref[...], preferred_element_type=F32)
                                      for c in range(0, A_GW, nb)], axis=1) * (1.0 / HEAD_DIM)
                y = y * lax.rsqrt(ms + EPS) * hg_ref[:, cols]
            y = y.astype(BF16)
            if dilation == 1:
                y = y.reshape(1, tm, A_GW)
            else:
                perm = perms[g - 1]
                pb = perm.shape[0]
                y = jnp.concatenate(
                    [jnp.dot(perm[...], y[h:h + pb], preferred_element_type=F32).astype(BF16)
                     .reshape(dilation, pb // dilation, A_GW) for h in range(0, tm, pb)], axis=1)
            outs[g][:, :, which * A_GW:(which + 1) * A_GW] = y


def _attn_a_kernel(q_ref, kp_ref, kc_ref, vp_ref, vc_ref, bias_ref, o_ref, lse_ref):
    n = pl.program_id(2)
    blk = A_BLOCK
    pair = 2 * HEAD_DIM
    lane = lax.broadcasted_iota(I32, (2 * blk, pair), 1)
    row = lax.broadcasted_iota(I32, (2 * blk, blk), 0)
    pen = jnp.where((row < blk) & (n == 0), NEG, 0.0)
    nt = (((1,), (1,)), ((), ()))
    tn = (((0,), (0,)), ((), ()))
    n_res, rows = q_ref.shape[0], q_ref.shape[1]
    k_all = [jnp.concatenate([kp_ref[r], kc_ref[r]], axis=0) for r in range(n_res)]
    v_all = [jnp.concatenate([vp_ref[r], vc_ref[r]], axis=0) for r in range(n_res)]
    chains = [(r, sb, hp, a) for r in range(n_res) for sb in range(rows // blk)
              for hp in range(A_HEADS // 2) for a in range(2)]

    def window(x, sb, hp):
        return x[sb * blk:(sb + 2) * blk, hp * pair:(hp + 1) * pair]

    keep = [lane < HEAD_DIM, lane >= HEAD_DIM]
    ka = [jnp.where(keep[a], window(k_all[r], sb, hp), jnp.zeros((2 * blk, pair), BF16))
          for r, sb, hp, a in chains]
    s = [lax.dot_general(k_, q_ref[r, sb * blk:(sb + 1) * blk, hp * pair:(hp + 1) * pair], nt,
                         preferred_element_type=F32) + bias_ref[2 * hp + a]
         for k_, (r, sb, hp, a) in zip(ka, chains)]
    s = [s_ + pen if sb == 0 else s_ for s_, (_, sb, _, _) in zip(s, chains)]
    m = [jnp.max(s_, axis=0, keepdims=True) for s_ in s]
    p = [jnp.exp2(s_ - m_) for s_, m_ in zip(s, m)]
    l = [jnp.sum(p_, axis=0, keepdims=True) for p_ in p]
    pv = [lax.dot_general(window(v_all[r], sb, hp), p_.astype(BF16), tn, preferred_element_type=F32)
          for p_, (r, sb, hp, _) in zip(p, chains)]
    for idx in range(0, len(chains), 2):
        r, sb, hp, _ = chains[idx]
        o_t = jnp.concatenate([(pv[idx + a] / l[idx + a])[a * HEAD_DIM:(a + 1) * HEAD_DIM] for a in range(2)],
                              axis=0)
        l_t = jnp.concatenate([jnp.broadcast_to(m[idx + a] + jnp.log2(l[idx + a]), (HEAD_DIM, blk))
                               for a in range(2)], axis=0)
        o_ref[r, sb * blk:(sb + 1) * blk, hp * pair:(hp + 1) * pair] = o_t.T
        lse_ref[r, sb * blk:(sb + 1) * blk, hp * pair:(hp + 1) * pair] = l_t.T


def _attn_a_group(arr, bias_t, dilation, batch, seq):
    sub = seq // dilation
    step = min(A_MAX_STEP, sub // A_BLOCK)
    rows = step * A_BLOCK
    nb = sub // rows
    n_res = min(dilation, A_MAX_STEP // step)

    def spec(which, prev):
        if prev:
            return pl.BlockSpec((None, n_res, A_BLOCK, A_GW),
                                lambda b, r, n: (b, r, jnp.maximum(n * step - 1, 0), which))
        return pl.BlockSpec((None, n_res, rows, A_GW), lambda b, r, n: (b, r, n, which))

    out_spec = pl.BlockSpec((None, n_res, rows, A_GW), lambda b, r, n: (b, r, n, 0))
    out_sds = jax.ShapeDtypeStruct((batch, dilation, sub, A_GW), F32)
    return pl.pallas_call(
        _attn_a_kernel,
        out_shape=[out_sds, out_sds],
        grid=(batch, dilation // n_res, nb),
        in_specs=[spec(0, False), spec(1, True), spec(1, False), spec(2, True), spec(2, False),
                  pl.BlockSpec((A_HEADS, 2 * A_BLOCK, A_BLOCK), lambda b, r, n: (0, 0, 0))],
        out_specs=[out_spec, out_spec],
        compiler_params=_cparams(("parallel", "parallel", "arbitrary")),
        name=f"attn_a_d{dilation}",
    )(arr, arr, arr, arr, arr, bias_t)


def _bias_a(tab_g, dilation):
    step = np.arange(3 * A_BLOCK - 1) - (A_BLOCK - 1)
    vec = tab_g[_t5_bucket(jnp.asarray(np.maximum(step, 0) * dilation, I32))].astype(F32) * LOG2E
    vec = jnp.where(jnp.asarray((step >= 0) & (step <= A_BLOCK))[:, None], vec, NEG)
    return _toeplitz(vec.T, 2 * A_BLOCK, A_BLOCK)


def _mixer_a(x2, gain, w_in, q_gain, k_gain, bias_a, batch, seq, tm):
    n, d = x2.shape
    n_groups = len(A_GROUPS)
    reps = n_groups * A_HEADS
    hg = jnp.concatenate([jnp.tile(q_gain * (HEAD_DIM ** -0.5 * LOG2E), reps), jnp.tile(k_gain, reps),
                          jnp.ones((n_groups * A_GW,), F32)]).reshape(1, -1)
    kout = 3 * n_groups * A_GW
    pb = V7X_MXU_DIM
    bd = np.kron(np.eye(pb // HEAD_DIM), np.ones((HEAD_DIM, HEAD_DIM))).astype(np.float32)
    perms = [_residue_major(pb, dil) for _, dil in A_GROUPS[1:]]
    tiles = seq // tm
    const = lambda shape: pl.BlockSpec(shape, lambda i: (0, 0), pipeline_mode=pl.Buffered(1))
    grouped = lambda dil, width: pl.BlockSpec((None, dil, tm // dil, width),
                                              lambda i: (i // tiles, 0, i % tiles, 0))
    arrs = pl.pallas_call(
        _in_proj_a_kernel,
        out_shape=[jax.ShapeDtypeStruct((batch, dil, seq // dil, 3 * A_GW), BF16) for _, dil in A_GROUPS],
        grid=(n // tm,),
        in_specs=[pl.BlockSpec((tm, d), lambda i: (i, 0)), const((1, d)), const((d, kout)),
                  const((1, kout)), const((pb, pb))] + [const((pb, pb)) for _ in perms],
        out_specs=[grouped(dil, 3 * A_GW) for _, dil in A_GROUPS],
        compiler_params=_cparams(("parallel",)),
        name="in_proj_a",
    )(x2, gain.reshape(1, d), w_in.astype(BF16), hg, jnp.asarray(bd, BF16),
      *[jnp.asarray(p, BF16) for p in perms])

    os_, ls_ = [], []
    for g, (window, dilation) in enumerate(A_GROUPS):
        assert window // dilation == A_BLOCK and (seq // dilation) % (min(A_MAX_STEP, seq // dilation // A_BLOCK) * A_BLOCK) == 0 and dilation <= pb // 16
        tab_g = bias_a[:, g * A_HEADS:(g + 1) * A_HEADS]
        o, lse = _attn_a_group(arrs[g], _bias_a(tab_g, dilation), dilation, batch, seq)
        os_.append(o)
        ls_.append(lse)

    mix = [*os_, *ls_, *[jnp.asarray(p.T, BF16) for p in perms]]
    specs = [grouped(dil, A_GW) for _, dil in A_GROUPS] * 2 + [const((pb, pb)) for _ in perms]
    return mix, specs


def _float_key(s):
    i = lax.bitcast_convert_type(s, I32)
    k = jnp.where(i < 0, i ^ jnp.int32(0x7FFFFFFF), i)
    return jnp.where(s == 0.0, 0, k)


def _split_bf16(x):
    hi = x.astype(BF16)
    return hi, (x - hi.astype(F32)).astype(BF16)


def _attn_b_kernel(q_ref, k_ref, vt_ref, qi_ref, kw_ref, bias_ref, shift_ref, o_ref, key_ref, msk_ref, *,
                   bounded):
    qb = pl.program_id(1)
    kc, qt = B_KC, B_QT
    nch = qb // 2 + 1
    t_q = qb * qt + lax.broadcasted_iota(I32, (kc, qt), 1)
    sub = lax.broadcasted_iota(I32, (kc, qt), 0)
    nt = (((1,), (1,)), ((), ()))

    qh, ql = _split_bf16(qi_ref[...])
    qi3 = jnp.concatenate(
        [jnp.concatenate([x[:, h * B_IDX_DIM:(h + 1) * B_IDX_DIM] for x in (qh, qh, ql)], axis=1)
         for h in range(B_IDX_HEADS)], axis=0)
    wt = kw_ref[pl.ds(pl.multiple_of(qb * qt, qt), qt), 0:qt].T
    wt = wt * (B_IDX_HEADS ** -0.5 * B_IDX_DIM ** -0.5)

    npair = (nch + 1) // 2

    def score_pair(c2, carry):
        offs = [pl.multiple_of((2 * c2 + a) * kc, kc) for a in range(2)]
        ks = [_split_bf16(kw_ref[pl.ds(off, kc), 0:B_IDX_DIM]) for off in offs]
        sc = [lax.dot_general(jnp.concatenate([kh, kl, kh], axis=1), qi3, nt, preferred_element_type=F32)
              for kh, kl in ks]
        for off, s in zip(offs, sc):
            acc = jnp.zeros((kc, qt), F32)
            for h in range(B_IDX_HEADS):
                acc = acc + wt[B_IDX_DIM + h:B_IDX_DIM + h + 1, :] * jnp.maximum(s[:, h * qt:(h + 1) * qt], 0.0)
            key_ref[pl.ds(off, kc), :] = jnp.where(off + sub <= t_q, _float_key(acc), INT_MIN)
        return carry

    lax.fori_loop(0, npair, score_pair, 0)

    def count(pred_fn):
        def body(c2, acc):
            for a in range(2):
                off = pl.multiple_of((2 * c2 + a) * kc, kc)
                hit = jnp.where(pred_fn(key_ref[pl.ds(off, kc), :], off + sub), 1, 0)
                acc = acc + jnp.sum(hit.reshape(kc // 8, 8, qt), axis=0)
            return acc
        acc = lax.fori_loop(0, npair, body, jnp.zeros((8, qt), I32))
        return jnp.sum(acc, axis=0, keepdims=True)

    def thr_bit(it, lo):
        cand = lo + jnp.left_shift(jnp.int32(1), 31 - it)
        cnt = count(lambda key, _: key >= cand)
        return jnp.where(cnt >= B_TOPK, cand, lo)

    thr = lax.fori_loop(0, 32, thr_bit, jnp.full((1, qt), INT_MIN, I32))
    need = B_TOPK - count(lambda key, _: key > thr)
    n_eq = count(lambda key, _: key == thr)

    def tie_search(_):
        def bit(it, j):
            cand = j + jnp.left_shift(jnp.int32(1), 12 - it)
            cnt = count(lambda key, idx: (key == thr) & (idx < cand))
            return jnp.where(cnt <= need, cand, j)
        return lax.fori_loop(0, 13, bit, jnp.zeros((1, qt), I32))

    any_tie = jnp.max(jnp.where(n_eq > need, 1, 0)) > 0
    j_max = lax.cond(any_tie, tie_search, lambda _: jnp.full((1, qt), 2 ** 13, I32), 0)

    def mask_chunk(c, carry):
        off = pl.multiple_of(c * kc, kc)
        key = key_ref[pl.ds(off, kc), :]
        idx = off + sub
        sel = ((key > thr) | ((key == thr) & (idx < j_max))) & (idx <= t_q)
        msk_ref[pl.ds(off, kc), :] = jnp.where(sel, -shift_ref[...], NEG)
        return carry

    lax.fori_loop(0, 2 * npair, mask_chunk, 0)

    grp = B_HEADS // B_KV_HEADS
    q = q_ref[...]
    groups = range(B_KV_HEADS)
    qg = [jnp.concatenate([q[:, (g * grp + j) * HEAD_DIM:(g * grp + j + 1) * HEAD_DIM]
                           for j in range(grp)], axis=0) for g in groups]
    ksl = [slice(g * HEAD_DIM, (g + 1) * HEAD_DIM) for g in groups]

    fk = 2 * kc

    def flash(c2, carry):
        m, l, acc = carry
        off = pl.multiple_of(c2 * fk, fk)
        tiles = [jnp.clip(qb - (fk // qt) * c2 - a, 0, B_NDELTA - 1) for a in range(fk // qt)]
        mk = msk_ref[pl.ds(off, fk), :]
        mk = jnp.concatenate([mk] * grp, axis=1)
        s = [lax.dot_general(k_ref[pl.ds(off, fk), ksl[g]], qg[g], nt, preferred_element_type=F32)
             for g in groups]
        s = [s[g] + jnp.concatenate([bias_ref[d, g] for d in tiles], axis=0) + mk for g in groups]
        if bounded:
            p = [jnp.exp2(s[g]) for g in groups]
            l = [l[g] + jnp.sum(p[g], axis=0, keepdims=True) for g in groups]
            p = [p[g].astype(BF16) for g in groups]
            pv = [jnp.dot(vt_ref[ksl[g], pl.ds(off, fk)], p[g], preferred_element_type=F32) for g in groups]
            return m, l, [acc[g] + pv[g] for g in groups]
        else:
            m_new = [jnp.maximum(m[g], jnp.max(s[g], axis=0, keepdims=True)) for g in groups]
            p = [jnp.exp2(s[g] - m_new[g]) for g in groups]
        pv = [jnp.dot(vt_ref[ksl[g], pl.ds(off, fk)], p[g].astype(BF16), preferred_element_type=F32)
              for g in groups]
        alpha = [jnp.exp2(m[g] - m_new[g]) for g in groups]
        l = [alpha[g] * l[g] + jnp.sum(p[g], axis=0, keepdims=True) for g in groups]
        acc = [alpha[g] * acc[g] + pv[g] for g in groups]
        return m_new, l, acc

    m0 = [jnp.full((1, grp * qt), NEG, F32) for _ in groups]
    l0 = [jnp.zeros((1, grp * qt), F32) for _ in groups]
    a0 = [jnp.zeros((HEAD_DIM, grp * qt), F32) for _ in groups]
    _, l, acc = lax.fori_loop(0, npair, flash, (m0, l0, a0))
    for g in groups:
        o = acc[g] / l[g]
        for j in range(0, grp, 2):
            h = g * grp + j
            pair = jnp.concatenate([o[:, j * qt:(j + 1) * qt], o[:, (j + 1) * qt:(j + 2) * qt]], axis=0)
            o_ref[:, h * HEAD_DIM:(h + 2) * HEAD_DIM] = pair.T.astype(o_ref.dtype)


def _toeplitz(vec, n_rows, n_cols):
    span = n_rows + n_cols - 1
    assert vec.shape[-1] == span
    lead = vec.shape[:-1]
    padded = jnp.concatenate([vec, jnp.zeros(lead + (1,), vec.dtype)], axis=-1)
    flat = jnp.tile(padded, n_rows)[..., :n_rows * span]
    return flat.reshape(lead + (n_rows, span))[..., n_rows - 1:]


def _bias_b(tab):
    grp = B_HEADS // B_KV_HEADS
    span = 2 * B_QT - 1
    dist = np.arange(B_NDELTA)[:, None] * B_QT + np.arange(span)[None, :] - (B_QT - 1)
    assert (B_NDELTA - 1) * B_QT - (B_QT - 1) >= 16 * 128 ** (15.0 / 16.0) + 1
    vec = tab[_t5_bucket(jnp.asarray(np.maximum(dist, 0), I32))].astype(F32) * LOG2E
    t = _toeplitz(vec.transpose(0, 2, 1), B_QT, B_QT)
    t = t.reshape(B_NDELTA, B_KV_HEADS, grp, B_QT, B_QT).transpose(0, 1, 3, 2, 4)
    return t.reshape(B_NDELTA, B_KV_HEADS, B_QT, grp * B_QT)


def _mixer_b(x2, gain, w_in, q_gain, k_gain, bias_b, batch, seq, tm):
    d = x2.shape[1]
    nq, nkv = B_HEADS * HEAD_DIM, B_KV_HEADS * HEAD_DIM
    nidx = B_IDX_HEADS * B_IDX_DIM
    tn = V7X_MXU_DIM
    used = 2 * nkv + nq + nidx + B_IDX_DIM + B_IDX_HEADS
    kout = -(-used // tn) * tn
    w = jnp.concatenate([w_in, jnp.zeros((d, kout - used), F32)], axis=1).astype(BF16)
    hg = jnp.concatenate([jnp.tile(q_gain * (HEAD_DIM ** -0.5 * LOG2E), B_HEADS), jnp.tile(k_gain, B_KV_HEADS),
                          jnp.ones((kout - nq - nkv,), F32)]).reshape(1, -1)
    n = batch * seq
    nrest = kout - nq - 2 * nkv
    bd = np.kron(np.eye(tn // HEAD_DIM), np.ones((HEAD_DIM, HEAD_DIM))).astype(np.float32)
    const = lambda shape: pl.BlockSpec(shape, lambda i: (0, 0), pipeline_mode=pl.Buffered(1))
    qk, idx = pl.pallas_call(
        functools.partial(_in_proj_b_kernel, tn=tn, n_norm=(nq + nkv) // tn),
        out_shape=[jax.ShapeDtypeStruct((n, nq + 2 * nkv), BF16), jax.ShapeDtypeStruct((n, nrest), F32)],
        grid=(n // tm,),
        in_specs=[pl.BlockSpec((tm, d), lambda i: (i, 0)), const((1, d)), const((d, kout)),
                  const((1, kout)), const((tn, tn))],
        out_specs=[pl.BlockSpec((tm, nq + 2 * nkv), lambda i: (i, 0)),
                   pl.BlockSpec((tm, nrest), lambda i: (i, 0))],
        compiler_params=_cparams(("parallel",)),
        name="in_proj_b",
    )(x2, gain.reshape(1, d), w, hg, jnp.asarray(bd, BF16))
    qk = qk.reshape(batch, seq, nq + 2 * nkv)
    idx = idx.reshape(batch, seq, nrest)
    vt = qk[:, :, nq + nkv:].transpose(0, 2, 1)
    assert seq % (2 * B_QT) == 0 and seq <= 2 ** 12 and nkv == tn and nrest == nidx + tn

    def attend(shift, bounded):
        return pl.pallas_call(
            functools.partial(_attn_b_kernel, bounded=bounded),
            out_shape=jax.ShapeDtypeStruct((batch, seq, nq), BF16),
            grid=(batch, seq // B_QT),
            in_specs=[
                pl.BlockSpec((None, B_QT, nq), lambda b, i: (b, i, 0)),
                pl.BlockSpec((None, seq, nkv), lambda b, i: (b, 0, nq // nkv)),
                pl.BlockSpec((None, nkv, seq), lambda b, i: (b, 0, 0)),
                pl.BlockSpec((None, B_QT, nidx), lambda b, i: (b, i, 0)),
                pl.BlockSpec((None, seq, tn), lambda b, i: (b, 0, nidx // tn)),
                pl.BlockSpec((B_NDELTA, B_KV_HEADS, B_QT, (B_HEADS // B_KV_HEADS) * B_QT),
                             lambda b, i: (0, 0, 0, 0), pipeline_mode=pl.Buffered(1)),
                pl.BlockSpec((1, B_QT), lambda b, i: (0, 0)),
            ],
            out_specs=pl.BlockSpec((None, B_QT, nq), lambda b, i: (b, i, 0)),
            scratch_shapes=[pltpu.VMEM((seq, B_QT), I32), pltpu.VMEM((seq, B_QT), F32)],
            compiler_params=_cparams(("parallel", "arbitrary")),
            name="attn_b_bounded" if bounded else "attn_b",
        )(qk, qk, vt, idx, idx, bias_t, shift)

    bias_t = _bias_b(bias_b)
    qk_bound = 1.02 * HEAD_DIM * (HEAD_DIM ** -0.5 * LOG2E) * jnp.max(jnp.abs(q_gain)) * jnp.max(jnp.abs(k_gain))
    bound = qk_bound + jnp.max(bias_b) * LOG2E
    spread = bound + qk_bound - jnp.min(bias_b) * LOG2E
    y = lax.cond(spread <= B_SAFE_SPREAD,
                 lambda: attend(jnp.full((1, B_QT), bound, F32), True),
                 lambda: attend(jnp.zeros((1, B_QT), F32), False))
    return [y.reshape(batch * seq, nq)], [pl.BlockSpec((tm, nq), lambda i: (i, 0))]


def _in_proj_c_kernel(x_ref, g_ref, w_ref, cw_ref, qkv_ref, rest_ref, tail_ref, xs_ref, *, tn):
    i = pl.program_id(1)
    tm = x_ref.shape[0]
    hw = C_HEADS * C_DK
    hn = _rms_rows(x_ref[...], g_ref[...]).astype(BF16)
    for j in range(w_ref.shape[1] // tn):
        cols = slice(j * tn, (j + 1) * tn)
        y = jnp.dot(hn, w_ref[:, cols], preferred_element_type=F32)
        if j * tn >= 3 * hw:
            rest_ref[:, j * tn - 3 * hw:(j + 1) * tn - 3 * hw] = y
            continue
        hist = tail_ref.shape[0]
        xs_ref[0:hist, cols] = jnp.where(i > 0, tail_ref[:, cols], 0.0)
        xs_ref[hist:, cols] = y
        tail_ref[:, cols] = y[tm - hist:, :]
        cw = cw_ref[:, cols]
        conv = sum(cw[t:t + 1, :] * xs_ref[hist - (C_CONV - 1) + t:hist - (C_CONV - 1) + t + tm, cols]
                   for t in range(C_CONV))
        z = conv * jax.nn.sigmoid(conv)
        if j * tn >= 2 * hw:
            qkv_ref[:, cols] = z
            continue
        scale = C_DK ** -0.5 if j * tn < hw else 1.0
        for h in range(tn // C_DK):
            zh = z[:, h * C_DK:(h + 1) * C_DK]
            ss = jnp.sum(zh * zh, axis=-1, keepdims=True)
            qkv_ref[:, j * tn + h * C_DK:j * tn + (h + 1) * C_DK] = zh * (lax.rsqrt(ss + EPS) * scale)


def _delta_kernel(q_ref, k_ref, v_ref, gate_ref, a_ref, b_ref, alog_ref, dtb_ref, og_ref, y_ref,
                  rows_s, p_s, r_s, m_s, n_s, el_s, o_s):
    cs, gs = C_CHUNK, C_GROUP
    per = gs // cs
    seq, dv = v_ref.shape
    n_groups = seq // gs
    hi = lax.Precision.HIGHEST
    r = lax.broadcasted_iota(I32, (gs, gs), 0)
    c_ = lax.broadcasted_iota(I32, (gs, gs), 1)

    def same_block(size):
        sh = int(math.log2(size))
        return (r >> sh) == (c_ >> sh)

    chunk = same_block(cs)
    lower, strict, eye = chunk & (r >= c_), chunk & (r > c_), r == c_

    z = a_ref[...] + dtb_ref[...]
    softplus = jnp.maximum(z, 0.0) + jnp.log(1.0 + jnp.exp(-jnp.abs(z)))
    g_all = -jnp.exp(alog_ref[...]) * softplus
    gc_all = jnp.dot(g_all, jnp.where(chunk & (r <= c_), 1.0, 0.0), preferred_element_type=F32, precision=hi)
    gl_all = jnp.dot(g_all, jnp.where(chunk, 1.0, 0.0), preferred_element_type=F32, precision=hi)
    rows_s[0] = jax.nn.sigmoid(b_ref[...])
    rows_s[1] = gc_all
    rows_s[2] = jnp.exp(gc_all)
    rows_s[3] = jnp.exp(gl_all - gc_all)
    first = (lax.broadcasted_iota(I32, (gs, per * dv), 0)
             == (lax.broadcasted_iota(I32, (gs, per * dv), 1) // dv) * cs)
    el_s[...] = jnp.exp(jnp.dot(gl_all, jnp.where(first, 1.0, 0.0), preferred_element_type=F32, precision=hi))

    def mm(a, b):
        return jnp.dot(a.astype(BF16), b.astype(BF16), preferred_element_type=F32)

    def mm_nt(a, b):
        return lax.dot_general(a.astype(BF16), b.astype(BF16), (((1,), (1,)), ((), ())),
                               preferred_element_type=F32)

    ways = 4

    def prep(m):
        ids = [m * ways + a for a in range(ways)]
        rows = [pl.ds(pl.multiple_of(i * gs, gs), gs) for i in ids]
        rw = [jnp.concatenate([rows_s[j, pl.ds(i, 1), :] for j in range(4)] + [jnp.zeros((4, gs), F32)], axis=0)
              for i in ids]
        cl = [x.T for x in rw]
        beta_col, gc_col, eg_col, ekg_col = ([x[:, j:j + 1] for x in cl] for j in range(4))
        q, k, v = ([ref[rw_, :] for rw_ in rows] for ref in (q_ref, k_ref, v_ref))
        decay = [jnp.where(lower, jnp.exp(jnp.where(lower, gcc - x[1:2, :], 0.0)), 0.0)
                 for gcc, x in zip(gc_col, rw)]
        kb = [k_ * b_ for k_, b_ in zip(k, beta_col)]
        yield
        a_mat = [jnp.where(strict, mm_nt(kb_, k_) * d_, 0.0) for kb_, k_, d_ in zip(kb, k, decay)]
        yield
        a8 = [jnp.where(same_block(8), a_, 0.0) for a_ in a_mat]
        t = [jnp.where(eye, 1.0, 0.0) - a_ for a_ in a8]
        pw = [mm(a_, a_) for a_ in a8]
        yield
        t = [t_ + mm(t_, p_) for t_, p_ in zip(t, pw)]
        yield
        pw = [mm(p_, p_) for p_ in pw]
        yield
        t = [t_ + mm(t_, p_) for t_, p_ in zip(t, pw)]
        yield
        size = 8
        while size < cs:
            sel = same_block(2 * size) & jnp.logical_not(same_block(size))
            nt = [mm(jnp.where(sel, a_, 0.0), t_) for a_, t_ in zip(a_mat, t)]
            yield
            t = [t_ - mm(t_, n_) for t_, n_ in zip(t, nt)]
            yield
            size *= 2
        uw = [mm(t_, jnp.concatenate([v_ * b_, kb_ * e_], axis=1))
              for t_, v_, b_, kb_, e_ in zip(t, v, beta_col, kb, eg_col)]
        yield
        qk = [jnp.where(lower, mm_nt(q_, k_) * d_, 0.0) for q_, k_, d_ in zip(q, k, decay)]
        yield
        qkwu = [mm(qk_, uw_) for qk_, uw_ in zip(qk, uw)]
        kgt = [(k_ * e_).T for k_, e_ in zip(k, ekg_col)]
        yield
        nm = [[mm(kgt[a][:, j * cs:(j + 1) * cs], uw[a][j * cs:(j + 1) * cs, :]) for j in range(per)]
              for a in range(ways)]

        def write():
            for a, i in enumerate(ids):
                r_s[rows[a], :] = qkwu[a][:, :dv]
                p_s[rows[a], :] = (q[a] * eg_col[a] - qkwu[a][:, dv:]).astype(BF16)
                for j in range(per):
                    n_s[i * per + j] = nm[a][j][:, :dv]
                    m_s[i * per + j] = nm[a][j][:, dv:].astype(BF16)
        return write

    def scan(m, state):
        for a in range(ways):
            i = m * ways + a
            el = el_s[pl.ds(i, 1), :]
            for j in range(per):
                rows = pl.ds(pl.multiple_of(i * gs + j * cs, cs), cs)
                c = i * per + j
                sb = state[0].astype(BF16)
                o_s[rows, :] = jnp.dot(p_s[rows, :], sb, preferred_element_type=F32) + r_s[rows, :]
                state[0] = (state[0] * el[:, j * dv:(j + 1) * dv] + n_s[c]
                            - jnp.dot(m_s[c], sb, preferred_element_type=F32))
                yield

    def weave(*gens):
        gens, results = list(gens), [None] * len(gens)
        live = list(range(len(gens)))
        while live:
            for idx in list(live):
                try:
                    next(gens[idx])
                except StopIteration as stop:
                    results[idx] = stop.value
                    live.remove(idx)
        return results

    n_blocks = n_groups // ways
    weave(prep(0))[0]()

    def step(m, state):
        holder = [state]
        _, write = weave(scan(m - 1, holder), prep(m))
        write()
        return holder[0]

    state = lax.fori_loop(1, n_blocks, step, jnp.zeros((C_DK, dv), F32))

    def finish(first, last):
        for t in range(first, last):
            rows = pl.ds(t * gs, gs)
            gate = gate_ref[rows, :]
            y_ref[rows, :] = (_rms_rows(o_s[rows, :], og_ref[...])
                              * (gate * jax.nn.sigmoid(gate))).astype(y_ref.dtype)
            yield

    done = (n_blocks - 1) * ways
    weave(scan(n_blocks - 1, [state]), finish(0, done))
    weave(finish(done, n_groups))


def _mixer_c(x2, gain, w_in, conv_w, a_log, dt_bias, o_gain, batch, seq, tm):
    d = x2.shape[1]
    hw = C_HEADS * C_DK
    tn = V7X_MXU_DIM
    used = 4 * hw + 2 * C_HEADS
    kout = -(-used // tn) * tn
    w = jnp.concatenate([w_in, jnp.zeros((d, kout - used), F32)], axis=1).astype(BF16)
    const = lambda shape: pl.BlockSpec(shape, lambda b, i: (0, 0), pipeline_mode=pl.Buffered(1))
    qkv, rest = pl.pallas_call(
        functools.partial(_in_proj_c_kernel, tn=tn),
        out_shape=[jax.ShapeDtypeStruct((batch, seq, 3 * hw), F32),
                   jax.ShapeDtypeStruct((batch, seq, kout - 3 * hw), F32)],
        grid=(batch, seq // tm),
        in_specs=[pl.BlockSpec((None, tm, d), lambda b, i: (b, i, 0)), const((1, d)), const((d, kout)),
                  const((C_CONV, 3 * hw))],
        out_specs=[pl.BlockSpec((None, tm, 3 * hw), lambda b, i: (b, i, 0)),
                   pl.BlockSpec((None, tm, kout - 3 * hw), lambda b, i: (b, i, 0))],
        scratch_shapes=[pltpu.VMEM((V7X_SUBLANES, 3 * hw), F32), pltpu.VMEM((tm + V7X_SUBLANES, 3 * hw), F32)],
        compiler_params=_cparams(("parallel", "arbitrary")),
        name="in_proj_c",
    )(x2.reshape(batch, seq, d), gain.reshape(1, d), w, conv_w)

    n_groups = seq // C_GROUP
    ba = rest[:, :, hw:hw + 2 * C_HEADS].transpose(0, 2, 1).reshape(batch, 2 * C_HEADS, n_groups, C_GROUP)
    per_head = lambda v: jnp.broadcast_to(v.reshape(C_HEADS, 1, 1), (C_HEADS, 1, C_GROUP))
    head_cols = lambda off: pl.BlockSpec((None, seq, C_DK), lambda b, h: (b, 0, off + h))
    small = lambda off: pl.BlockSpec((None, None, n_groups, C_GROUP), lambda b, h: (b, off + h, 0, 0))
    scalar_row = pl.BlockSpec((None, 1, C_GROUP), lambda b, h: (h, 0, 0))
    y = pl.pallas_call(
        _delta_kernel,
        out_shape=jax.ShapeDtypeStruct((batch, seq, hw), BF16),
        grid=(batch, C_HEADS),
        in_specs=[head_cols(0), head_cols(C_HEADS), head_cols(2 * C_HEADS),
                  head_cols(0),
                  small(C_HEADS), small(0), scalar_row, scalar_row,
                  pl.BlockSpec((1, C_DK), lambda b, h: (0, 0))],
        out_specs=head_cols(0),
        scratch_shapes=[
            pltpu.VMEM((4, n_groups, C_GROUP), F32),
            pltpu.VMEM((seq, C_DK), BF16),
            pltpu.VMEM((seq, C_DK), F32),
            pltpu.VMEM((seq // C_CHUNK, C_DK, C_DK), BF16),
            pltpu.VMEM((seq // C_CHUNK, C_DK, C_DK), F32),
            pltpu.VMEM((n_groups, (C_GROUP // C_CHUNK) * C_DK), F32),
            pltpu.VMEM((seq, C_DK), F32),
        ],
        compiler_params=_cparams(("parallel", "parallel")),
        name="delta_c",
    )(qkv, qkv, qkv, rest, ba, ba, per_head(a_log), per_head(dt_bias), o_gain.reshape(1, C_DK))
    return [y.reshape(batch * seq, hw)], [pl.BlockSpec((tm, hw), lambda i: (i, 0))]


def kernel(x, rel_bias, norm_mix, norm_mlp, mlp_w1, mlp_w2, a_w_in, a_q_gain, a_k_gain, a_w_out,
           b_w_in, b_q_gain, b_k_gain, b_w_out, c_w_in, c_conv_w, c_a_log, c_dt_bias, c_o_gain, c_w_out):
    batch, seq, d = x.shape
    depth = norm_mix.shape[0]
    a_cols = len(A_GROUPS) * A_HEADS
    bias_a, bias_b = rel_bias[:, :a_cols], rel_bias[:, a_cols:]
    x2 = x.reshape(batch * seq, d)
    tm = 512
    for i in range(depth):
        kind, j = i % 3, i // 3
        if kind == 0:
            mix, specs = _mixer_a(x2, norm_mix[i], a_w_in[j], a_q_gain[j], a_k_gain[j], bias_a, batch, seq, tm)
            w_out = a_w_out[j]
        elif kind == 1:
            mix, specs = _mixer_b(x2, norm_mix[i], b_w_in[j], b_q_gain[j], b_k_gain[j], bias_b, batch, seq, tm)
            w_out = b_w_out[j]
        else:
            mix, specs = _mixer_c(x2, norm_mix[i], c_w_in[j], c_conv_w[j], c_a_log[j], c_dt_bias[j],
                                  c_o_gain[j], batch, seq, tm)
            w_out = c_w_out[j]
        x2 = _proj_mlp(mix, specs, w_out.astype(BF16), x2, norm_mlp[i], mlp_w1[i].astype(BF16),
                       mlp_w2[i].astype(BF16), groups_a=kind == 0, tm=tm)
    return x2.reshape(batch, seq, d)
```

```python
import functools
import math

import numpy as np
import jax
import jax.numpy as jnp
from jax import lax
from jax.experimental import pallas as pl
from jax.experimental.pallas import tpu as pltpu

F32 = jnp.float32
BF16 = jnp.bfloat16
I32 = jnp.int32

EPS = 1e-6
HEAD_DIM = 64
NEG = -1e30
INT_MIN = -2 ** 31
LOG2E = math.log2(math.e)

V7X_VMEM_BYTES = 64 * 1024 * 1024
VMEM_LIMIT = V7X_VMEM_BYTES - 8 * 1024 * 1024
V7X_MXU_DIM = 256
V7X_SUBLANES = 8

NUM_BUCKETS = 32
MAX_DISTANCE = 2048

A_GROUPS = ((128, 1), (512, 4), (2048, 16))
A_HEADS = 8
A_BLOCK = 128
A_GW = A_HEADS * HEAD_DIM
A_MAX_STEP = 8

B_HEADS = 16
B_KV_HEADS = 4
B_IDX_HEADS = 8
B_IDX_DIM = 64
B_TOPK = 256
B_QT = 128
B_KC = 256
B_NDELTA = 14
B_SAFE_SPREAD = 100.0

C_HEADS = 8
C_DK = 128
C_CONV = 4
C_CHUNK = 64
C_GROUP = 4 * C_CHUNK


def _cparams(sem):
    return pltpu.CompilerParams(dimension_semantics=sem, vmem_limit_bytes=VMEM_LIMIT)


def _t5_bucket(dist):
    max_exact = NUM_BUCKETS // 2
    d = jnp.maximum(dist, 1).astype(F32)
    log_part = jnp.log(d / max_exact) / math.log(MAX_DISTANCE / max_exact) * (NUM_BUCKETS - max_exact)
    large = jnp.minimum(max_exact + log_part.astype(I32), NUM_BUCKETS - 1)
    return jnp.where(dist < max_exact, dist, large)


def _rms_rows(x, gain_row):
    ms = jnp.mean(x * x, axis=-1, keepdims=True)
    return x * lax.rsqrt(ms + EPS) * gain_row


def _in_proj_b_kernel(x_ref, g_ref, w_ref, hg_ref, bd_ref, qkv_ref, idx_ref, *, tn, n_norm):
    n_qkv = qkv_ref.shape[1] // tn
    hn = _rms_rows(x_ref[...], g_ref[...]).astype(BF16)
    for j in range(w_ref.shape[1] // tn):
        cols = slice(j * tn, (j + 1) * tn)
        y = jnp.dot(hn, w_ref[:, cols], preferred_element_type=F32)
        if j < n_norm:
            ms = jnp.dot((y * y).astype(BF16), bd_ref[...], preferred_element_type=F32) * (1.0 / HEAD_DIM)
            y = y * lax.rsqrt(ms + EPS) * hg_ref[:, cols]
        if j < n_qkv:
            qkv_ref[:, cols] = y.astype(BF16)
        else:
            idx_ref[:, (j - n_qkv) * tn:(j - n_qkv + 1) * tn] = y


def _mix_groups_a(tm, o0, o1, o2, l0, l1, l2, pt4_ref, pt16_ref):

    def token_order(ref, pt_ref):
        v = ref[...]
        if pt_ref is None:
            return v.reshape(tm, A_GW)
        pb = pt_ref.shape[0]
        per = pb // v.shape[0]
        pieces = []
        for h in range(tm // pb):
            hi, lo = _split_bf16(v[:, h * per:(h + 1) * per, :].reshape(pb, A_GW))
            pieces.append(jnp.dot(pt_ref[...], hi, preferred_element_type=F32)
                          + jnp.dot(pt_ref[...], lo, preferred_element_type=F32))
        return jnp.concatenate(pieces, axis=0)

    pts = (None, pt4_ref, pt16_ref)
    a, b, c = (token_order(r, pt) for r, pt in zip((l0, l1, l2), pts))
    m = jnp.maximum(jnp.maximum(a, b), c)
    ea, eb, ec = jnp.exp2(a - m), jnp.exp2(b - m), jnp.exp2(c - m)
    oa, ob, oc = (token_order(r, pt) for r, pt in zip((o0, o1, o2), pts))
    return (ea * oa + eb * ob + ec * oc) / (ea + eb + ec)


def _proj_mlp_kernel(*refs, tf, groups_a):
    wo_ref, x_ref, g_ref, w1_ref, w2_ref, o_ref = refs[-6:]
    y = _mix_groups_a(x_ref.shape[0], *refs[:-6]) if groups_a else refs[0][...]
    x = x_ref[...] + jnp.dot(y.astype(BF16), wo_ref[...], preferred_element_type=F32)
    hn = _rms_rows(x, g_ref[...]).astype(BF16)
    acc = x
    for f in range(w1_ref.shape[1] // tf):
        cols = slice(f * tf, (f + 1) * tf)
        h = jnp.maximum(jnp.dot(hn, w1_ref[:, cols], preferred_element_type=F32), 0.0)
        acc = acc + jnp.dot((h * h).astype(BF16), w2_ref[cols, :], preferred_element_type=F32)
    o_ref[...] = acc


def _proj_mlp(mix, mix_specs, w_out, x2, gain, w1, w2, *, groups_a, tm, tf=1024):
    n, d = x2.shape
    dff = w1.shape[1]
    const = lambda shape: pl.BlockSpec(shape, lambda i: (0, 0), pipeline_mode=pl.Buffered(1))
    return pl.pallas_call(
        functools.partial(_proj_mlp_kernel, tf=tf, groups_a=groups_a),
        out_shape=jax.ShapeDtypeStruct((n, d), F32),
        grid=(n // tm,),
        in_specs=list(mix_specs) + [const(w_out.shape), pl.BlockSpec((tm, d), lambda i: (i, 0)), const((1, d)),
                                    const((d, dff)), const((dff, d))],
        out_specs=pl.BlockSpec((tm, d), lambda i: (i, 0)),
        compiler_params=_cparams(("parallel",)),
        name="proj_mlp_a" if groups_a else "proj_mlp",
    )(*mix, w_out, x2, gain.reshape(1, d), w1, w2)


def _residue_major(tm, dilation):
    p = np.zeros((tm, tm), np.float32)
    j, r = np.meshgrid(np.arange(tm // dilation), np.arange(dilation), indexing="ij")
    p[(r * (tm // dilation) + j).ravel(), (j * dilation + r).ravel()] = 1.0
    return p


def _in_proj_a_kernel(x_ref, g_ref, w_ref, hg_ref, bd_ref, *rest):
    n_groups = len(A_GROUPS)
    perms, outs = rest[:n_groups - 1], rest[n_groups - 1:]
    tm = x_ref.shape[0]
    hn = _rms_rows(x_ref[...], g_ref[...]).astype(BF16)
    for which in range(3):
        for g, (_, dilation) in enumerate(A_GROUPS):
            j = which * n_groups + g
            cols = slice(j * A_GW, (j + 1) * A_GW)
            y = jnp.dot(hn, w_ref[:, cols], preferred_element_type=F32)
            if which < 2:
                nb = bd_ref.shape[0]
                y2 = (y * y).astype(BF16)
                ms = jnp.concatenate([jnp.dot(y2[:, c:c + nb], bd_ref[...], preferred_element_type=F32)
                                      for c in range(0, A_GW, nb)], axis=1) * (1.0 / HEAD_DIM)
                y = y * lax.rsqrt(ms + EPS) * hg_ref[:, cols]
            y = y.astype(BF16)
            if dilation == 1:
                y = y.reshape(1, tm, A_GW)
            else:
                perm = perms[g - 1]
                pb = perm.shape[0]
                y = jnp.concatenate(
                    [jnp.dot(perm[...], y[h:h + pb], preferred_element_type=F32).astype(BF16)
                     .reshape(dilation, pb // dilation, A_GW) for h in range(0, tm, pb)], axis=1)
            outs[g][:, :, which * A_GW:(which + 1) * A_GW] = y


def _attn_a_kernel(q_ref, kp_ref, kc_ref, vp_ref, vc_ref, bias_ref, o_ref, lse_ref):
    n = pl.program_id(2)
    blk = A_BLOCK
    pair = 2 * HEAD_DIM
    lane = lax.broadcasted_iota(I32, (2 * blk, pair), 1)
    row = lax.broadcasted_iota(I32, (2 * blk, blk), 0)
    pen = jnp.where((row < blk) & (n == 0), NEG, 0.0)
    nt = (((1,), (1,)), ((), ()))
    tn = (((0,), (0,)), ((), ()))
    n_res, rows = q_ref.shape[0], q_ref.shape[1]
    k_all = [jnp.concatenate([kp_ref[r], kc_ref[r]], axis=0) for r in range(n_res)]
    v_all = [jnp.concatenate([vp_ref[r], vc_ref[r]], axis=0) for r in range(n_res)]
    chains = [(r, sb, hp, a) for r in range(n_res) for sb in range(rows // blk)
              for hp in range(A_HEADS // 2) for a in range(2)]

    def window(x, sb, hp):
        return x[sb * blk:(sb + 2) * blk, hp * pair:(hp + 1) * pair]

    keep = [lane < HEAD_DIM, lane >= HEAD_DIM]
    ka = [jnp.where(keep[a], window(k_all[r], sb, hp), jnp.zeros((2 * blk, pair), BF16))
          for r, sb, hp, a in chains]
    s = [lax.dot_general(k_, q_ref[r, sb * blk:(sb + 1) * blk, hp * pair:(hp + 1) * pair], nt,
                         preferred_element_type=F32) + bias_ref[2 * hp + a]
         for k_, (r, sb, hp, a) in zip(ka, chains)]
    s = [s_ + pen if sb == 0 else s_ for s_, (_, sb, _, _) in zip(s, chains)]
    m = [jnp.max(s_, axis=0, keepdims=True) for s_ in s]
    p = [jnp.exp2(s_ - m_) for s_, m_ in zip(s, m)]
    l = [jnp.sum(p_, axis=0, keepdims=True) for p_ in p]
    pv = [lax.dot_general(window(v_all[r], sb, hp), p_.astype(BF16), tn, preferred_element_type=F32)
          for p_, (r, sb, hp, _) in zip(p, chains)]
    for idx in range(0, len(chains), 2):
        r, sb, hp, _ = chains[idx]
        o_t = jnp.concatenate([(pv[idx + a] / l[idx + a])[a * HEAD_DIM:(a + 1) * HEAD_DIM] for a in range(2)],
                              axis=0)
        l_t = jnp.concatenate([jnp.broadcast_to(m[idx + a] + jnp.log2(l[idx + a]), (HEAD_DIM, blk))
                               for a in range(2)], axis=0)
        o_ref[r, sb * blk:(sb + 1) * blk, hp * pair:(hp + 1) * pair] = o_t.T
        lse_ref[r, sb * blk:(sb + 1) * blk, hp * pair:(hp + 1) * pair] = l_t.T


def _attn_a_group(arr, bias_t, dilation, batch, seq):
    sub = seq // dilation
    step = min(A_MAX_STEP, sub // A_BLOCK)
    rows = step * A_BLOCK
    nb = sub // rows
    n_res = min(dilation, A_MAX_STEP // step)

    def spec(which, prev):
        if prev:
            return pl.BlockSpec((None, n_res, A_BLOCK, A_GW),
                                lambda b, r, n: (b, r, jnp.maximum(n * step - 1, 0), which))
        return pl.BlockSpec((None, n_res, rows, A_GW), lambda b, r, n: (b, r, n, which))

    out_spec = pl.BlockSpec((None, n_res, rows, A_GW), lambda b, r, n: (b, r, n, 0))
    out_sds = jax.ShapeDtypeStruct((batch, dilation, sub, A_GW), F32)
    return pl.pallas_call(
        _attn_a_kernel,
        out_shape=[out_sds, out_sds],
        grid=(batch, dilation // n_res, nb),
        in_specs=[spec(0, False), spec(1, True), spec(1, False), spec(2, True), spec(2, False),
                  pl.BlockSpec((A_HEADS, 2 * A_BLOCK, A_BLOCK), lambda b, r, n: (0, 0, 0))],
        out_specs=[out_spec, out_spec],
        compiler_params=_cparams(("parallel", "parallel", "arbitrary")),
        name=f"attn_a_d{dilation}",
    )(arr, arr, arr, arr, arr, bias_t)


def _bias_a(tab_g, dilation):
    step = np.arange(3 * A_BLOCK - 1) - (A_BLOCK - 1)
    vec = tab_g[_t5_bucket(jnp.asarray(np.maximum(step, 0) * dilation, I32))].astype(F32) * LOG2E
    vec = jnp.where(jnp.asarray((step >= 0) & (step <= A_BLOCK))[:, None], vec, NEG)
    return _toeplitz(vec.T, 2 * A_BLOCK, A_BLOCK)


def _mixer_a(x2, gain, w_in, q_gain, k_gain, bias_a, batch, seq, tm):
    n, d = x2.shape
    n_groups = len(A_GROUPS)
    reps = n_groups * A_HEADS
    hg = jnp.concatenate([jnp.tile(q_gain * (HEAD_DIM ** -0.5 * LOG2E), reps), jnp.tile(k_gain, reps),
                          jnp.ones((n_groups * A_GW,), F32)]).reshape(1, -1)
    kout = 3 * n_groups * A_GW
    pb = V7X_MXU_DIM
    bd = np.kron(np.eye(pb // HEAD_DIM), np.ones((HEAD_DIM, HEAD_DIM))).astype(np.float32)
    perms = [_residue_major(pb, dil) for _, dil in A_GROUPS[1:]]
    tiles = seq // tm
    const = lambda shape: pl.BlockSpec(shape, lambda i: (0, 0), pipeline_mode=pl.Buffered(1))
    grouped = lambda dil, width: pl.BlockSpec((None, dil, tm // dil, width),
                                              lambda i: (i // tiles, 0, i % tiles, 0))
    arrs = pl.pallas_call(
        _in_proj_a_kernel,
        out_shape=[jax.ShapeDtypeStruct((batch, dil, seq // dil, 3 * A_GW), BF16) for _, dil in A_GROUPS],
        grid=(n // tm,),
        in_specs=[pl.BlockSpec((tm, d), lambda i: (i, 0)), const((1, d)), const((d, kout)),
                  const((1, kout)), const((pb, pb))] + [const((pb, pb)) for _ in perms],
        out_specs=[grouped(dil, 3 * A_GW) for _, dil in A_GROUPS],
        compiler_params=_cparams(("parallel",)),
        name="in_proj_a",
    )(x2, gain.reshape(1, d), w_in.astype(BF16), hg, jnp.asarray(bd, BF16),
      *[jnp.asarray(p, BF16) for p in perms])

    os_, ls_ = [], []
    for g, (window, dilation) in enumerate(A_GROUPS):
        assert window // dilation == A_BLOCK and (seq // dilation) % (min(A_MAX_STEP, seq // dilation // A_BLOCK) * A_BLOCK) == 0 and dilation <= pb // 16
        tab_g = bias_a[:, g * A_HEADS:(g + 1) * A_HEADS]
        o, lse = _attn_a_group(arrs[g], _bias_a(tab_g, dilation), dilation, batch, seq)
        os_.append(o)
        ls_.append(lse)

    mix = [*os_, *ls_, *[jnp.asarray(p.T, BF16) for p in perms]]
    specs = [grouped(dil, A_GW) for _, dil in A_GROUPS] * 2 + [const((pb, pb)) for _ in perms]
    return mix, specs


def _float_key(s):
    i = lax.bitcast_convert_type(s, I32)
    k = jnp.where(i < 0, i ^ jnp.int32(0x7FFFFFFF), i)
    return jnp.where(s == 0.0, 0, k)


def _split_bf16(x):
    hi = x.astype(BF16)
    return hi, (x - hi.astype(F32)).astype(BF16)


def _attn_b_kernel(q_ref, k_ref, vt_ref, qi_ref, kw_ref, bias_ref, shift_ref, o_ref, key_ref, msk_ref, *,
                   bounded):
    qb = pl.program_id(1)
    kc, qt = B_KC, B_QT
    nch = qb // 2 + 1
    t_q = qb * qt + lax.broadcasted_iota(I32, (kc, qt), 1)
    sub = lax.broadcasted_iota(I32, (kc, qt), 0)
    nt = (((1,), (1,)), ((), ()))

    qh, ql = _split_bf16(qi_ref[...])
    qi3 = jnp.concatenate(
        [jnp.concatenate([x[:, h * B_IDX_DIM:(h + 1) * B_IDX_DIM] for x in (qh, qh, ql)], axis=1)
         for h in range(B_IDX_HEADS)], axis=0)
    wt = kw_ref[pl.ds(pl.multiple_of(qb * qt, qt), qt), 0:qt].T
    wt = wt * (B_IDX_HEADS ** -0.5 * B_IDX_DIM ** -0.5)

    npair = (nch + 1) // 2

    def score_pair(c2, carry):
        offs = [pl.multiple_of((2 * c2 + a) * kc, kc) for a in range(2)]
        ks = [_split_bf16(kw_ref[pl.ds(off, kc), 0:B_IDX_DIM]) for off in offs]
        sc = [lax.dot_general(jnp.concatenate([kh, kl, kh], axis=1), qi3, nt, preferred_element_type=F32)
              for kh, kl in ks]
        for off, s in zip(offs, sc):
            acc = jnp.zeros((kc, qt), F32)
            for h in range(B_IDX_HEADS):
                acc = acc + wt[B_IDX_DIM + h:B_IDX_DIM + h + 1, :] * jnp.maximum(s[:, h * qt:(h + 1) * qt], 0.0)
            key_ref[pl.ds(off, kc), :] = jnp.where(off + sub <= t_q, _float_key(acc), INT_MIN)
        return carry

    lax.fori_loop(0, npair, score_pair, 0)

    def count(pred_fn):
        def body(c2, acc):
            for a in range(2):
                off = pl.multiple_of((2 * c2 + a) * kc, kc)
                hit = jnp.where(pred_fn(key_ref[pl.ds(off, kc), :], off + sub), 1, 0)
                acc = acc + jnp.sum(hit.reshape(kc // 8, 8, qt), axis=0)
            return acc
        acc = lax.fori_loop(0, npair, body, jnp.zeros((8, qt), I32))
        return jnp.sum(acc, axis=0, keepdims=True)

    def thr_bit(it, lo):
        cand = lo + jnp.left_shift(jnp.int32(1), 31 - it)
        cnt = count(lambda key, _: key >= cand)
        return jnp.where(cnt >= B_TOPK, cand, lo)

    thr = lax.fori_loop(0, 32, thr_bit, jnp.full((1, qt), INT_MIN, I32))
    need = B_TOPK - count(lambda key, _: key > thr)
    n_eq = count(lambda key, _: key == thr)

    def tie_search(_):
        def bit(it, j):
            cand = j + jnp.left_shift(jnp.int32(1), 12 - it)
            cnt = count(lambda key, idx: (key == thr) & (idx < cand))
            return jnp.where(cnt <= need, cand, j)
        return lax.fori_loop(0, 13, bit, jnp.zeros((1, qt), I32))

    any_tie = jnp.max(jnp.where(n_eq > need, 1, 0)) > 0
    j_max = lax.cond(any_tie, tie_search, lambda _: jnp.full((1, qt), 2 ** 13, I32), 0)

    def mask_chunk(c, carry):
        off = pl.multiple_of(c * kc, kc)
        key = key_ref[pl.ds(off, kc), :]
        idx = off + sub
        sel = ((key > thr) | ((key == thr) & (idx < j_max))) & (idx <= t_q)
        msk_ref[pl.ds(off, kc), :] = jnp.where(sel, -shift_ref[...], NEG)
        return carry

    lax.fori_loop(0, 2 * npair, mask_chunk, 0)

    grp = B_HEADS // B_KV_HEADS
    q = q_ref[...]
    groups = range(B_KV_HEADS)
    qg = [jnp.concatenate([q[:, (g * grp + j) * HEAD_DIM:(g * grp + j + 1) * HEAD_DIM]
                           for j in range(grp)], axis=0) for g in groups]
    ksl = [slice(g * HEAD_DIM, (g + 1) * HEAD_DIM) for g in groups]

    fk = 2 * kc

    def flash(c2, carry):
        m, l, acc = carry
        off = pl.multiple_of(c2 * fk, fk)
        tiles = [jnp.clip(qb - (fk // qt) * c2 - a, 0, B_NDELTA - 1) for a in range(fk // qt)]
        mk = msk_ref[pl.ds(off, fk), :]
        mk = jnp.concatenate([mk] * grp, axis=1)
        s = [lax.dot_general(k_ref[pl.ds(off, fk), ksl[g]], qg[g], nt, preferred_element_type=F32)
             for g in groups]
        s = [s[g] + jnp.concatenate([bias_ref[d, g] for d in tiles], axis=0) + mk for g in groups]
        if bounded:
            p = [jnp.exp2(s[g]) for g in groups]
            l = [l[g] + jnp.sum(p[g], axis=0, keepdims=True) for g in groups]
            p = [p[g].astype(BF16) for g in groups]
            pv = [jnp.dot(vt_ref[ksl[g], pl.ds(off, fk)], p[g], preferred_element_type=F32) for g in groups]
            return m, l, [acc[g] + pv[g] for g in groups]
        else:
            m_new = [jnp.maximum(m[g], jnp.max(s[g], axis=0, keepdims=True)) for g in groups]
            p = [jnp.exp2(s[g] - m_new[g]) for g in groups]
        pv = [jnp.dot(vt_ref[ksl[g], pl.ds(off, fk)], p[g].astype(BF16), preferred_element_type=F32)
              for g in groups]
        alpha = [jnp.exp2(m[g] - m_new[g]) for g in groups]
        l = [alpha[g] * l[g] + jnp.sum(p[g], axis=0, keepdims=True) for g in groups]
        acc = [alpha[g] * acc[g] + pv[g] for g in groups]
        return m_new, l, acc

    m0 = [jnp.full((1, grp * qt), NEG, F32) for _ in groups]
    l0 = [jnp.zeros((1, grp * qt), F32) for _ in groups]
    a0 = [jnp.zeros((HEAD_DIM, grp * qt), F32) for _ in groups]
    _, l, acc = lax.fori_loop(0, npair, flash, (m0, l0, a0))
    for g in groups:
        o = acc[g] / l[g]
        for j in range(0, grp, 2):
            h = g * grp + j
            pair = jnp.concatenate([o[:, j * qt:(j + 1) * qt], o[:, (j + 1) * qt:(j + 2) * qt]], axis=0)
            o_ref[:, h * HEAD_DIM:(h + 2) * HEAD_DIM] = pair.T.astype(o_ref.dtype)


def _toeplitz(vec, n_rows, n_cols):
    span = n_rows + n_cols - 1
    assert vec.shape[-1] == span
    lead = vec.shape[:-1]
    padded = jnp.concatenate([vec, jnp.zeros(lead + (1,), vec.dtype)], axis=-1)
    flat = jnp.tile(padded, n_rows)[..., :n_rows * span]
    return flat.reshape(lead + (n_rows, span))[..., n_rows - 1:]


def _bias_b(tab):
    grp = B_HEADS // B_KV_HEADS
    span = 2 * B_QT - 1
    dist = np.arange(B_NDELTA)[:, None] * B_QT + np.arange(span)[None, :] - (B_QT - 1)
    assert (B_NDELTA - 1) * B_QT - (B_QT - 1) >= 16 * 128 ** (15.0 / 16.0) + 1
    vec = tab[_t5_bucket(jnp.asarray(np.maximum(dist, 0), I32))].astype(F32) * LOG2E
    t = _toeplitz(vec.transpose(0, 2, 1), B_QT, B_QT)
    t = t.reshape(B_NDELTA, B_KV_HEADS, grp, B_QT, B_QT).transpose(0, 1, 3, 2, 4)
    return t.reshape(B_NDELTA, B_KV_HEADS, B_QT, grp * B_QT)


def _mixer_b(x2, gain, w_in, q_gain, k_gain, bias_b, batch, seq, tm):
    d = x2.shape[1]
    nq, nkv = B_HEADS * HEAD_DIM, B_KV_HEADS * HEAD_DIM
    nidx = B_IDX_HEADS * B_IDX_DIM
    tn = V7X_MXU_DIM
    used = 2 * nkv + nq + nidx + B_IDX_DIM + B_IDX_HEADS
    kout = -(-used // tn) * tn
    w = jnp.concatenate([w_in, jnp.zeros((d, kout - used), F32)], axis=1).astype(BF16)
    hg = jnp.concatenate([jnp.tile(q_gain * (HEAD_DIM ** -0.5 * LOG2E), B_HEADS), jnp.tile(k_gain, B_KV_HEADS),
                          jnp.ones((kout - nq - nkv,), F32)]).reshape(1, -1)
    n = batch * seq
    nrest = kout - nq - 2 * nkv
    bd = np.kron(np.eye(tn // HEAD_DIM), np.ones((HEAD_DIM, HEAD_DIM))).astype(np.float32)
    const = lambda shape: pl.BlockSpec(shape, lambda i: (0, 0), pipeline_mode=pl.Buffered(1))
    qk, idx = pl.pallas_call(
        functools.partial(_in_proj_b_kernel, tn=tn, n_norm=(nq + nkv) // tn),
        out_shape=[jax.ShapeDtypeStruct((n, nq + 2 * nkv), BF16), jax.ShapeDtypeStruct((n, nrest), F32)],
        grid=(n // tm,),
        in_specs=[pl.BlockSpec((tm, d), lambda i: (i, 0)), const((1, d)), const((d, kout)),
                  const((1, kout)), const((tn, tn))],
        out_specs=[pl.BlockSpec((tm, nq + 2 * nkv), lambda i: (i, 0)),
                   pl.BlockSpec((tm, nrest), lambda i: (i, 0))],
        compiler_params=_cparams(("parallel",)),
        name="in_proj_b",
    )(x2, gain.reshape(1, d), w, hg, jnp.asarray(bd, BF16))
    qk = qk.reshape(batch, seq, nq + 2 * nkv)
    idx = idx.reshape(batch, seq, nrest)
    vt = qk[:, :, nq + nkv:].transpose(0, 2, 1)
    assert seq % (2 * B_QT) == 0 and seq <= 2 ** 12 and nkv == tn and nrest == nidx + tn

    def attend(shift, bounded):
        return pl.pallas_call(
            functools.partial(_attn_b_kernel, bounded=bounded),
            out_shape=jax.ShapeDtypeStruct((batch, seq, nq), BF16),
            grid=(batch, seq // B_QT),
            in_specs=[
                pl.BlockSpec((None, B_QT, nq), lambda b, i: (b, i, 0)),
                pl.BlockSpec((None, seq, nkv), lambda b, i: (b, 0, nq // nkv)),
                pl.BlockSpec((None, nkv, seq), lambda b, i: (b, 0, 0)),
                pl.BlockSpec((None, B_QT, nidx), lambda b, i: (b, i, 0)),
                pl.BlockSpec((None, seq, tn), lambda b, i: (b, 0, nidx // tn)),
                pl.BlockSpec((B_NDELTA, B_KV_HEADS, B_QT, (B_HEADS // B_KV_HEADS) * B_QT),
                             lambda b, i: (0, 0, 0, 0), pipeline_mode=pl.Buffered(1)),
                pl.BlockSpec((1, B_QT), lambda b, i: (0, 0)),
            ],
            out_specs=pl.BlockSpec((None, B_QT, nq), lambda b, i: (b, i, 0)),
            scratch_shapes=[pltpu.VMEM((seq, B_QT), I32), pltpu.VMEM((seq, B_QT), F32)],
            compiler_params=_cparams(("parallel", "arbitrary")),
            name="attn_b_bounded" if bounded else "attn_b",
        )(qk, qk, vt, idx, idx, bias_t, shift)

    bias_t = _bias_b(bias_b)
    qk_bound = 1.02 * HEAD_DIM * (HEAD_DIM ** -0.5 * LOG2E) * jnp.max(jnp.abs(q_gain)) * jnp.max(jnp.abs(k_gain))
    bound = qk_bound + jnp.max(bias_b) * LOG2E
    spread = bound + qk_bound - jnp.min(bias_b) * LOG2E
    y = lax.cond(spread <= B_SAFE_SPREAD,
                 lambda: attend(jnp.full((1, B_QT), bound, F32), True),
                 lambda: attend(jnp.zeros((1, B_QT), F32), False))
    return [y.reshape(batch * seq, nq)], [pl.BlockSpec((tm, nq), lambda i: (i, 0))]


def _in_proj_c_kernel(x_ref, g_ref, w_ref, cw_ref, qkv_ref, rest_ref, tail_ref, xs_ref, *, tn):
    i = pl.program_id(1)
    tm = x_ref.shape[0]
    hw = C_HEADS * C_DK
    hn = _rms_rows(x_ref[...], g_ref[...]).astype(BF16)
    for j in range(w_ref.shape[1] // tn):
        cols = slice(j * tn, (j + 1) * tn)
        y = jnp.dot(hn, w_ref[:, cols], preferred_element_type=F32)
        if j * tn >= 3 * hw:
            rest_ref[:, j * tn - 3 * hw:(j + 1) * tn - 3 * hw] = y
            continue
        hist = tail_ref.shape[0]
        xs_ref[0:hist, cols] = jnp.where(i > 0, tail_ref[:, cols], 0.0)
        xs_ref[hist:, cols] = y
        tail_ref[:, cols] = y[tm - hist:, :]
        cw = cw_ref[:, cols]
        conv = sum(cw[t:t + 1, :] * xs_ref[hist - (C_CONV - 1) + t:hist - (C_CONV - 1) + t + tm, cols]
                   for t in range(C_CONV))
        z = conv * jax.nn.sigmoid(conv)
        if j * tn >= 2 * hw:
            qkv_ref[:, cols] = z
            continue
        scale = C_DK ** -0.5 if j * tn < hw else 1.0
        for h in range(tn // C_DK):
            zh = z[:, h * C_DK:(h + 1) * C_DK]
            ss = jnp.sum(zh * zh, axis=-1, keepdims=True)
            qkv_ref[:, j * tn + h * C_DK:j * tn + (h + 1) * C_DK] = zh * (lax.rsqrt(ss + EPS) * scale)


def _delta_kernel(q_ref, k_ref, v_ref, gate_ref, a_ref, b_ref, alog_ref, dtb_ref, og_ref, y_ref,
                  rows_s, p_s, r_s, m_s, n_s, el_s, o_s):
    cs, gs = C_CHUNK, C_GROUP
    per = gs // cs
    seq, dv = v_ref.shape
    n_groups = seq // gs
    hi = lax.Precision.HIGHEST
    r = lax.broadcasted_iota(I32, (gs, gs), 0)
    c_ = lax.broadcasted_iota(I32, (gs, gs), 1)

    def same_block(size):
        sh = int(math.log2(size))
        return (r >> sh) == (c_ >> sh)

    chunk = same_block(cs)
    lower, strict, eye = chunk & (r >= c_), chunk & (r > c_), r == c_

    z = a_ref[...] + dtb_ref[...]
    softplus = jnp.maximum(z, 0.0) + jnp.log(1.0 + jnp.exp(-jnp.abs(z)))
    g_all = -jnp.exp(alog_ref[...]) * softplus
    gc_all = jnp.dot(g_all, jnp.where(chunk & (r <= c_), 1.0, 0.0), preferred_element_type=F32, precision=hi)
    gl_all = jnp.dot(g_all, jnp.where(chunk, 1.0, 0.0), preferred_element_type=F32, precision=hi)
    rows_s[0] = jax.nn.sigmoid(b_ref[...])
    rows_s[1] = gc_all
    rows_s[2] = jnp.exp(gc_all)
    rows_s[3] = jnp.exp(gl_all - gc_all)
    first = (lax.broadcasted_iota(I32, (gs, per * dv), 0)
             == (lax.broadcasted_iota(I32, (gs, per * dv), 1) // dv) * cs)
    el_s[...] = jnp.exp(jnp.dot(gl_all, jnp.where(first, 1.0, 0.0), preferred_element_type=F32, precision=hi))

    def mm(a, b):
        return jnp.dot(a.astype(BF16), b.astype(BF16), preferred_element_type=F32)

    def mm_nt(a, b):
        return lax.dot_general(a.astype(BF16), b.astype(BF16), (((1,), (1,)), ((), ())),
                               preferred_element_type=F32)

    ways = 4

    def prep(m):
        ids = [m * ways + a for a in range(ways)]
        rows = [pl.ds(pl.multiple_of(i * gs, gs), gs) for i in ids]
        rw = [jnp.concatenate([rows_s[j, pl.ds(i, 1), :] for j in range(4)] + [jnp.zeros((4, gs), F32)], axis=0)
              for i in ids]
        cl = [x.T for x in rw]
        beta_col, gc_col, eg_col, ekg_col = ([x[:, j:j + 1] for x in cl] for j in range(4))
        q, k, v = ([ref[rw_, :] for rw_ in rows] for ref in (q_ref, k_ref, v_ref))
        decay = [jnp.where(lower, jnp.exp(jnp.where(lower, gcc - x[1:2, :], 0.0)), 0.0)
                 for gcc, x in zip(gc_col, rw)]
        kb = [k_ * b_ for k_, b_ in zip(k, beta_col)]
        yield
        a_mat = [jnp.where(strict, mm_nt(kb_, k_) * d_, 0.0) for kb_, k_, d_ in zip(kb, k, decay)]
        yield
        a8 = [jnp.where(same_block(8), a_, 0.0) for a_ in a_mat]
        t = [jnp.where(eye, 1.0, 0.0) - a_ for a_ in a8]
        pw = [mm(a_, a_) for a_ in a8]
        yield
        t = [t_ + mm(t_, p_) for t_, p_ in zip(t, pw)]
        yield
        pw = [mm(p_, p_) for p_ in pw]
        yield
        t = [t_ + mm(t_, p_) for t_, p_ in zip(t, pw)]
        yield
        size = 8
        while size < cs:
            sel = same_block(2 * size) & jnp.logical_not(same_block(size))
            nt = [mm(jnp.where(sel, a_, 0.0), t_) for a_, t_ in zip(a_mat, t)]
            yield
            t = [t_ - mm(t_, n_) for t_, n_ in zip(t, nt)]
            yield
            size *= 2
        uw = [mm(t_, jnp.concatenate([v_ * b_, kb_ * e_], axis=1))
              for t_, v_, b_, kb_, e_ in zip(t, v, beta_col, kb, eg_col)]
        yield
        qk = [jnp.where(lower, mm_nt(q_, k_) * d_, 0.0) for q_, k_, d_ in zip(q, k, decay)]
        yield
        qkwu = [mm(qk_, uw_) for qk_, uw_ in zip(qk, uw)]
        kgt = [(k_ * e_).T for k_, e_ in zip(k, ekg_col)]
        yield
        nm = [[mm(kgt[a][:, j * cs:(j + 1) * cs], uw[a][j * cs:(j + 1) * cs, :]) for j in range(per)]
              for a in range(ways)]

        def write():
            for a, i in enumerate(ids):
                r_s[rows[a], :] = qkwu[a][:, :dv]
                p_s[rows[a], :] = (q[a] * eg_col[a] - qkwu[a][:, dv:]).astype(BF16)
                for j in range(per):
                    n_s[i * per + j] = nm[a][j][:, :dv]
                    m_s[i * per + j] = nm[a][j][:, dv:].astype(BF16)
        return write

    def scan(m, state):
        for a in range(ways):
            i = m * ways + a
            el = el_s[pl.ds(i, 1), :]
            for j in range(per):
                rows = pl.ds(pl.multiple_of(i * gs + j * cs, cs), cs)
                c = i * per + j
                sb = state[0].astype(BF16)
                o_s[rows, :] = jnp.dot(p_s[rows, :], sb, preferred_element_type=F32) + r_s[rows, :]
                state[0] = (state[0] * el[:, j * dv:(j + 1) * dv] + n_s[c]
                            - jnp.dot(m_s[c], sb, preferred_element_type=F32))
                yield

    def weave(*gens):
        gens, results = list(gens), [None] * len(gens)
        live = list(range(len(gens)))
        while live:
            for idx in list(live):
                try:
                    next(gens[idx])
                except StopIteration as stop:
                    results[idx] = stop.value
                    live.remove(idx)
        return results

    n_blocks = n_groups // ways
    weave(prep(0))[0]()

    def step(m, state):
        holder = [state]
        _, write = weave(scan(m - 1, holder), prep(m))
        write()
        return holder[0]

    state = lax.fori_loop(1, n_blocks, step, jnp.zeros((C_DK, dv), F32))

    def finish(first, last):
        for t in range(first, last):
            rows = pl.ds(t * gs, gs)
            gate = gate_ref[rows, :]
            y_ref[rows, :] = (_rms_rows(o_s[rows, :], og_ref[...])
                              * (gate * jax.nn.sigmoid(gate))).astype(y_ref.dtype)
            yield

    done = (n_blocks - 1) * ways
    weave(scan(n_blocks - 1, [state]), finish(0, done))
    weave(finish(done, n_groups))


def _mixer_c(x2, gain, w_in, conv_w, a_log, dt_bias, o_gain, batch, seq, tm):
    d = x2.shape[1]
    hw = C_HEADS * C_DK
    tn = V7X_MXU_DIM
    used = 4 * hw + 2 * C_HEADS
    kout = -(-used // tn) * tn
    w = jnp.concatenate([w_in, jnp.zeros((d, kout - used), F32)], axis=1).astype(BF16)
    const = lambda shape: pl.BlockSpec(shape, lambda b, i: (0, 0), pipeline_mode=pl.Buffered(1))
    qkv, rest = pl.pallas_call(
        functools.partial(_in_proj_c_kernel, tn=tn),
        out_shape=[jax.ShapeDtypeStruct((batch, seq, 3 * hw), F32),
                   jax.ShapeDtypeStruct((batch, seq, kout - 3 * hw), F32)],
        grid=(batch, seq // tm),
        in_specs=[pl.BlockSpec((None, tm, d), lambda b, i: (b, i, 0)), const((1, d)), const((d, kout)),
                  const((C_CONV, 3 * hw))],
        out_specs=[pl.BlockSpec((None, tm, 3 * hw), lambda b, i: (b, i, 0)),
                   pl.BlockSpec((None, tm, kout - 3 * hw), lambda b, i: (b, i, 0))],
        scratch_shapes=[pltpu.VMEM((V7X_SUBLANES, 3 * hw), F32), pltpu.VMEM((tm + V7X_SUBLANES, 3 * hw), F32)],
        compiler_params=_cparams(("parallel", "arbitrary")),
        name="in_proj_c",
    )(x2.reshape(batch, seq, d), gain.reshape(1, d), w, conv_w)

    n_groups = seq // C_GROUP
    ba = rest[:, :, hw:hw + 2 * C_HEADS].transpose(0, 2, 1).reshape(batch, 2 * C_HEADS, n_groups, C_GROUP)
    per_head = lambda v: jnp.broadcast_to(v.reshape(C_HEADS, 1, 1), (C_HEADS, 1, C_GROUP))
    head_cols = lambda off: pl.BlockSpec((None, seq, C_DK), lambda b, h: (b, 0, off + h))
    small = lambda off: pl.BlockSpec((None, None, n_groups, C_GROUP), lambda b, h: (b, off + h, 0, 0))
    scalar_row = pl.BlockSpec((None, 1, C_GROUP), lambda b, h: (h, 0, 0))
    y = pl.pallas_call(
        _delta_kernel,
        out_shape=jax.ShapeDtypeStruct((batch, seq, hw), BF16),
        grid=(batch, C_HEADS),
        in_specs=[head_cols(0), head_cols(C_HEADS), head_cols(2 * C_HEADS),
                  head_cols(0),
                  small(C_HEADS), small(0), scalar_row, scalar_row,
                  pl.BlockSpec((1, C_DK), lambda b, h: (0, 0))],
        out_specs=head_cols(0),
        scratch_shapes=[
            pltpu.VMEM((4, n_groups, C_GROUP), F32),
            pltpu.VMEM((seq, C_DK), BF16),
            pltpu.VMEM((seq, C_DK), F32),
            pltpu.VMEM((seq // C_CHUNK, C_DK, C_DK), BF16),
            pltpu.VMEM((seq // C_CHUNK, C_DK, C_DK), F32),
            pltpu.VMEM((n_groups, (C_GROUP // C_CHUNK) * C_DK), F32),
            pltpu.VMEM((seq, C_DK), F32),
        ],
        compiler_params=_cparams(("parallel", "parallel")),
        name="delta_c",
    )(qkv, qkv, qkv, rest, ba, ba, per_head(a_log), per_head(dt_bias), o_gain.reshape(1, C_DK))
    return [y.reshape(batch * seq, hw)], [pl.BlockSpec((tm, hw), lambda i: (i, 0))]


def kernel(x, rel_bias, norm_mix, norm_mlp, mlp_w1, mlp_w2, a_w_in, a_q_gain, a_k_gain, a_w_out,
           b_w_in, b_q_gain, b_k_gain, b_w_out, c_w_in, c_conv_w, c_a_log, c_dt_bias, c_o_gain, c_w_out):
    batch, seq, d = x.shape
    depth = norm_mix.shape[0]
    a_cols = len(A_GROUPS) * A_HEADS
    bias_a, bias_b = rel_bias[:, :a_cols], rel_bias[:, a_cols:]
    x2 = x.reshape(batch * seq, d)
    tm = 512
    for i in range(depth):
        kind, j = i % 3, i // 3
        if kind == 0:
            mix, specs = _mixer_a(x2, norm_mix[i], a_w_in[j], a_q_gain[j], a_k_gain[j], bias_a, batch, seq, tm)
            w_out = a_w_out[j]
        elif kind == 1:
            mix, specs = _mixer_b(x2, norm_mix[i], b_w_in[j], b_q_gain[j], b_k_gain[j], bias_b, batch, seq, tm)
            w_out = b_w_out[j]
        else:
            mix, specs = _mixer_c(x2, norm_mix[i], c_w_in[j], c_conv_w[j], c_a_log[j], c_dt_bias[j],
                                  c_o_gain[j], batch, seq, tm)
            w_out = c_w_out[j]
        x2 = _proj_mlp(mix, specs, w_out.astype(BF16), x2, norm_mlp[i], mlp_w1[i].astype(BF16),
                       mlp_w2[i].astype(BF16), groups_a=kind == 0, tm=tm)
    return x2.reshape(batch, seq, d)
```

```python
import functools
import math

import numpy as np
import jax
import jax.numpy as jnp
from jax import lax
from jax.experimental import pallas as pl
from jax.experimental.pallas import tpu as pltpu

F32 = jnp.float32
BF16 = jnp.bfloat16
I32 = jnp.int32

EPS = 1e-6
HEAD_DIM = 64
NEG = -1e30
INT_MIN = -2 ** 31
LOG2E = math.log2(math.e)

V7X_VMEM_BYTES = 64 * 1024 * 1024
VMEM_LIMIT = V7X_VMEM_BYTES - 8 * 1024 * 1024
V7X_MXU_DIM = 256
V7X_SUBLANES = 8

NUM_BUCKETS = 32
MAX_DISTANCE = 2048

A_GROUPS = ((128, 1), (512, 4), (2048, 16))
A_HEADS = 8
A_BLOCK = 128
A_GW = A_HEADS * HEAD_DIM
A_MAX_STEP = 16

B_HEADS = 16
B_KV_HEADS = 4
B_IDX_HEADS = 8
B_IDX_DIM = 64
B_TOPK = 256
B_QT = 128
B_KC = 256
B_NDELTA = 14
B_SAFE_SPREAD = 100.0

C_HEADS = 8
C_DK = 128
C_CONV = 4
C_CHUNK = 64
C_GROUP = 4 * C_CHUNK


def _cparams(sem):
    return pltpu.CompilerParams(dimension_semantics=sem, vmem_limit_bytes=VMEM_LIMIT)


def _t5_bucket(dist):
    max_exact = NUM_BUCKETS // 2
    d = jnp.maximum(dist, 1).astype(F32)
    log_part = jnp.log(d / max_exact) / math.log(MAX_DISTANCE / max_exact) * (NUM_BUCKETS - max_exact)
    large = jnp.minimum(max_exact + log_part.astype(I32), NUM_BUCKETS - 1)
    return jnp.where(dist < max_exact, dist, large)


def _rms_rows(x, gain_row):
    ms = jnp.mean(x * x, axis=-1, keepdims=True)
    return x * lax.rsqrt(ms + EPS) * gain_row


def _in_proj_b_kernel(x_ref, g_ref, w_ref, hg_ref, bd_ref, qkv_ref, idx_ref, *, tn, n_norm):
    n_qkv = qkv_ref.shape[1] // tn
    hn = _rms_rows(x_ref[...], g_ref[...]).astype(BF16)
    for j in range(w_ref.shape[1] // tn):
        cols = slice(j * tn, (j + 1) * tn)
        y = jnp.dot(hn, w_ref[:, cols], preferred_element_type=F32)
        if j < n_norm:
            ms = jnp.dot((y * y).astype(BF16), bd_ref[...], preferred_element_type=F32) * (1.0 / HEAD_DIM)
            y = y * lax.rsqrt(ms + EPS) * hg_ref[:, cols]
        if j < n_qkv:
            qkv_ref[:, cols] = y.astype(BF16)
        else:
            idx_ref[:, (j - n_qkv) * tn:(j - n_qkv + 1) * tn] = y


def _mix_groups_a(tm, o0, o1, o2, l0, l1, l2, pt4_ref, pt16_ref):

    def token_order(ref, pt_ref):
        v = ref[...]
        if pt_ref is None:
            return v.reshape(tm, A_GW)
        pb = pt_ref.shape[0]
        per = pb // v.shape[0]
        pieces = []
        for h in range(tm // pb):
            hi, lo = _split_bf16(v[:, h * per:(h + 1) * per, :].reshape(pb, A_GW))
            pieces.append(jnp.dot(pt_ref[...], hi, preferred_element_type=F32)
                          + jnp.dot(pt_ref[...], lo, preferred_element_type=F32))
        return jnp.concatenate(pieces, axis=0)

    pts = (None, pt4_ref, pt16_ref)
    a, b, c = (token_order(r, pt) for r, pt in zip((l0, l1, l2), pts))
    m = jnp.maximum(jnp.maximum(a, b), c)
    ea, eb, ec = jnp.exp2(a - m), jnp.exp2(b - m), jnp.exp2(c - m)
    oa, ob, oc = (token_order(r, pt) for r, pt in zip((o0, o1, o2), pts))
    return (ea * oa + eb * ob + ec * oc) / (ea + eb + ec)


def _proj_mlp_kernel(*refs, tf, groups_a):
    wo_ref, x_ref, g_ref, w1_ref, w2_ref, o_ref = refs[-6:]
    y = _mix_groups_a(x_ref.shape[0], *refs[:-6]) if groups_a else refs[0][...]
    x = x_ref[...] + jnp.dot(y.astype(BF16), wo_ref[...], preferred_element_type=F32)
    hn = _rms_rows(x, g_ref[...]).astype(BF16)
    acc = x
    for f in range(w1_ref.shape[1] // tf):
        cols = slice(f * tf, (f + 1) * tf)
        h = jnp.maximum(jnp.dot(hn, w1_ref[:, cols], preferred_element_type=F32), 0.0)
        acc = acc + jnp.dot((h * h).astype(BF16), w2_ref[cols, :], preferred_element_type=F32)
    o_ref[...] = acc


def _proj_mlp(mix, mix_specs, w_out, x2, gain, w1, w2, *, groups_a, tm, tf=1024):
    n, d = x2.shape
    dff = w1.shape[1]
    const = lambda shape: pl.BlockSpec(shape, lambda i: (0, 0), pipeline_mode=pl.Buffered(1))
    return pl.pallas_call(
        functools.partial(_proj_mlp_kernel, tf=tf, groups_a=groups_a),
        out_shape=jax.ShapeDtypeStruct((n, d), F32),
        grid=(n // tm,),
        in_specs=list(mix_specs) + [const(w_out.shape), pl.BlockSpec((tm, d), lambda i: (i, 0)), const((1, d)),
                                    const((d, dff)), const((dff, d))],
        out_specs=pl.BlockSpec((tm, d), lambda i: (i, 0)),
        compiler_params=_cparams(("parallel",)),
        name="proj_mlp_a" if groups_a else "proj_mlp",
    )(*mix, w_out, x2, gain.reshape(1, d), w1, w2)


def _residue_major(tm, dilation):
    p = np.zeros((tm, tm), np.float32)
    j, r = np.meshgrid(np.arange(tm // dilation), np.arange(dilation), indexing="ij")
    p[(r * (tm // dilation) + j).ravel(), (j * dilation + r).ravel()] = 1.0
    return p


def _in_proj_a_kernel(x_ref, g_ref, w_ref, hg_ref, bd_ref, *rest):
    n_groups = len(A_GROUPS)
    perms, outs = rest[:n_groups - 1], rest[n_groups - 1:]
    tm = x_ref.shape[0]
    hn = _rms_rows(x_ref[...], g_ref[...]).astype(BF16)
    for which in range(3):
        for g, (_, dilation) in enumerate(A_GROUPS):
            j = which * n_groups + g
            cols = slice(j * A_GW, (j + 1) * A_GW)
            y = jnp.dot(hn, w_ref[:, cols], preferred_element_type=F32)
            if which < 2:
                nb = bd_ref.shape[0]
                y2 = (y * y).astype(BF16)
                ms = jnp.concatenate([jnp.dot(y2[:, c:c + nb], bd_ref[...], preferred_element_type=F32)
                                      for c in range(0, A_GW, nb)], axis=1) * (1.0 / HEAD_DIM)
                y = y * lax.rsqrt(ms + EPS) * hg_ref[:, cols]
            y = y.astype(BF16)
            if dilation == 1:
                y = y.reshape(1, tm, A_GW)
            else:
                perm = perms[g - 1]
                pb = perm.shape[0]
                y = jnp.concatenate(
                    [jnp.dot(perm[...], y[h:h + pb], preferred_element_type=F32).astype(BF16)
                     .reshape(dilation, pb // dilation, A_GW) for h in range(0, tm, pb)], axis=1)
            outs[g][:, :, which * A_GW:(which + 1) * A_GW] = y


def _attn_a_kernel(q_ref, kp_ref, kc_ref, vp_ref, vc_ref, bias_ref, o_ref, lse_ref):
    n = pl.program_id(2)
    blk = A_BLOCK
    pair = 2 * HEAD_DIM
    lane = lax.broadcasted_iota(I32, (2 * blk, pair), 1)
    row = lax.broadcasted_iota(I32, (2 * blk, blk), 0)
    pen = jnp.where((row < blk) & (n == 0), NEG, 0.0)
    nt = (((1,), (1,)), ((), ()))
    tn = (((0,), (0,)), ((), ()))
    n_res, rows = q_ref.shape[0], q_ref.shape[1]
    k_all = [jnp.concatenate([kp_ref[r], kc_ref[r]], axis=0) for r in range(n_res)]
    v_all = [jnp.concatenate([vp_ref[r], vc_ref[r]], axis=0) for r in range(n_res)]
    chains = [(r, sb, hp, a) for r in range(n_res) for sb in range(rows // blk)
              for hp in range(A_HEADS // 2) for a in range(2)]

    def window(x, sb, hp):
        return x[sb * blk:(sb + 2) * blk, hp * pair:(hp + 1) * pair]

    keep = [lane < HEAD_DIM, lane >= HEAD_DIM]
    ka = [jnp.where(keep[a], window(k_all[r], sb, hp), jnp.zeros((2 * blk, pair), BF16))
          for r, sb, hp, a in chains]
    s = [lax.dot_general(k_, q_ref[r, sb * blk:(sb + 1) * blk, hp * pair:(hp + 1) * pair], nt,
                         preferred_element_type=F32) + bias_ref[2 * hp + a]
         for k_, (r, sb, hp, a) in zip(ka, chains)]
    s = [s_ + pen if sb == 0 else s_ for s_, (_, sb, _, _) in zip(s, chains)]
    m = [jnp.max(s_, axis=0, keepdims=True) for s_ in s]
    p = [jnp.exp2(s_ - m_) for s_, m_ in zip(s, m)]
    l = [jnp.sum(p_, axis=0, keepdims=True) for p_ in p]
    pv = [lax.dot_general(window(v_all[r], sb, hp), p_.astype(BF16), tn, preferred_element_type=F32)
          for p_, (r, sb, hp, _) in zip(p, chains)]
    for idx in range(0, len(chains), 2):
        r, sb, hp, _ = chains[idx]
        o_t = jnp.concatenate([(pv[idx + a] / l[idx + a])[a * HEAD_DIM:(a + 1) * HEAD_DIM] for a in range(2)],
                              axis=0)
        l_t = jnp.concatenate([jnp.broadcast_to(m[idx + a] + jnp.log2(l[idx + a]), (HEAD_DIM, blk))
                               for a in range(2)], axis=0)
        o_ref[r, sb * blk:(sb + 1) * blk, hp * pair:(hp + 1) * pair] = o_t.T
        lse_ref[r, sb * blk:(sb + 1) * blk, hp * pair:(hp + 1) * pair] = l_t.T


def _attn_a_group(arr, bias_t, dilation, batch, seq):
    sub = seq // dilation
    step = min(A_MAX_STEP, sub // A_BLOCK)
    rows = step * A_BLOCK
    nb = sub // rows
    n_res = min(dilation, A_MAX_STEP // step)

    def spec(which, prev):
        if prev:
            return pl.BlockSpec((None, n_res, A_BLOCK, A_GW),
                                lambda b, r, n: (b, r, jnp.maximum(n * step - 1, 0), which))
        return pl.BlockSpec((None, n_res, rows, A_GW), lambda b, r, n: (b, r, n, which))

    out_spec = pl.BlockSpec((None, n_res, rows, A_GW), lambda b, r, n: (b, r, n, 0))
    out_sds = jax.ShapeDtypeStruct((batch, dilation, sub, A_GW), F32)
    return pl.pallas_call(
        _attn_a_kernel,
        out_shape=[out_sds, out_sds],
        grid=(batch, dilation // n_res, nb),
        in_specs=[spec(0, False), spec(1, True), spec(1, False), spec(2, True), spec(2, False),
                  pl.BlockSpec((A_HEADS, 2 * A_BLOCK, A_BLOCK), lambda b, r, n: (0, 0, 0))],
        out_specs=[out_spec, out_spec],
        compiler_params=_cparams(("parallel", "parallel", "arbitrary")),
        name=f"attn_a_d{dilation}",
    )(arr, arr, arr, arr, arr, bias_t)


def _bias_a(tab_g, dilation):
    step = np.arange(3 * A_BLOCK - 1) - (A_BLOCK - 1)
    vec = tab_g[_t5_bucket(jnp.asarray(np.maximum(step, 0) * dilation, I32))].astype(F32) * LOG2E
    vec = jnp.where(jnp.asarray((step >= 0) & (step <= A_BLOCK))[:, None], vec, NEG)
    return _toeplitz(vec.T, 2 * A_BLOCK, A_BLOCK)


def _mixer_a(x2, gain, w_in, q_gain, k_gain, bias_a, batch, seq, tm):
    n, d = x2.shape
    n_groups = len(A_GROUPS)
    reps = n_groups * A_HEADS
    hg = jnp.concatenate([jnp.tile(q_gain * (HEAD_DIM ** -0.5 * LOG2E), reps), jnp.tile(k_gain, reps),
                          jnp.ones((n_groups * A_GW,), F32)]).reshape(1, -1)
    kout = 3 * n_groups * A_GW
    pb = V7X_MXU_DIM
    bd = np.kron(np.eye(pb // HEAD_DIM), np.ones((HEAD_DIM, HEAD_DIM))).astype(np.float32)
    perms = [_residue_major(pb, dil) for _, dil in A_GROUPS[1:]]
    tiles = seq // tm
    const = lambda shape: pl.BlockSpec(shape, lambda i: (0, 0), pipeline_mode=pl.Buffered(1))
    grouped = lambda dil, width: pl.BlockSpec((None, dil, tm // dil, width),
                                              lambda i: (i // tiles, 0, i % tiles, 0))
    arrs = pl.pallas_call(
        _in_proj_a_kernel,
        out_shape=[jax.ShapeDtypeStruct((batch, dil, seq // dil, 3 * A_GW), BF16) for _, dil in A_GROUPS],
        grid=(n // tm,),
        in_specs=[pl.BlockSpec((tm, d), lambda i: (i, 0)), const((1, d)), const((d, kout)),
                  const((1, kout)), const((pb, pb))] + [const((pb, pb)) for _ in perms],
        out_specs=[grouped(dil, 3 * A_GW) for _, dil in A_GROUPS],
        compiler_params=_cparams(("parallel",)),
        name="in_proj_a",
    )(x2, gain.reshape(1, d), w_in.astype(BF16), hg, jnp.asarray(bd, BF16),
      *[jnp.asarray(p, BF16) for p in perms])

    os_, ls_ = [], []
    for g, (window, dilation) in enumerate(A_GROUPS):
        assert window // dilation == A_BLOCK and (seq // dilation) % (min(A_MAX_STEP, seq // dilation // A_BLOCK) * A_BLOCK) == 0 and dilation <= pb // 16
        tab_g = bias_a[:, g * A_HEADS:(g + 1) * A_HEADS]
        o, lse = _attn_a_group(arrs[g], _bias_a(tab_g, dilation), dilation, batch, seq)
        os_.append(o)
        ls_.append(lse)

    mix = [*os_, *ls_, *[jnp.asarray(p.T, BF16) for p in perms]]
    specs = [grouped(dil, A_GW) for _, dil in A_GROUPS] * 2 + [const((pb, pb)) for _ in perms]
    return mix, specs


def _float_key(s):
    i = lax.bitcast_convert_type(s, I32)
    k = jnp.where(i < 0, i ^ jnp.int32(0x7FFFFFFF), i)
    return jnp.where(s == 0.0, 0, k)


def _split_bf16(x):
    hi = x.astype(BF16)
    return hi, (x - hi.astype(F32)).astype(BF16)


def _attn_b_kernel(q_ref, k_ref, vt_ref, qi_ref, kw_ref, bias_ref, shift_ref, o_ref, key_ref, msk_ref, *,
                   bounded):
    qb = pl.program_id(1)
    kc, qt = B_KC, B_QT
    nch = qb // 2 + 1
    t_q = qb * qt + lax.broadcasted_iota(I32, (kc, qt), 1)
    sub = lax.broadcasted_iota(I32, (kc, qt), 0)
    nt = (((1,), (1,)), ((), ()))

    qh, ql = _split_bf16(qi_ref[...])
    qi3 = jnp.concatenate(
        [jnp.concatenate([x[:, h * B_IDX_DIM:(h + 1) * B_IDX_DIM] for x in (qh, qh, ql)], axis=1)
         for h in range(B_IDX_HEADS)], axis=0)
    wt = kw_ref[pl.ds(pl.multiple_of(qb * qt, qt), qt), 0:qt].T
    wt = wt * (B_IDX_HEADS ** -0.5 * B_IDX_DIM ** -0.5)

    npair = (nch + 1) // 2

    def score_pair(c2, carry):
        offs = [pl.multiple_of((2 * c2 + a) * kc, kc) for a in range(2)]
        ks = [_split_bf16(kw_ref[pl.ds(off, kc), 0:B_IDX_DIM]) for off in offs]
        sc = [lax.dot_general(jnp.concatenate([kh, kl, kh], axis=1), qi3, nt, preferred_element_type=F32)
              for kh, kl in ks]
        for off, s in zip(offs, sc):
            acc = jnp.zeros((kc, qt), F32)
            for h in range(B_IDX_HEADS):
                acc = acc + wt[B_IDX_DIM + h:B_IDX_DIM + h + 1, :] * jnp.maximum(s[:, h * qt:(h + 1) * qt], 0.0)
            key_ref[pl.ds(off, kc), :] = jnp.where(off + sub <= t_q, _float_key(acc), INT_MIN)
        return carry

    lax.fori_loop(0, npair, score_pair, 0)

    def count(pred_fn):
        def body(c2, acc):
            for a in range(2):
                off = pl.multiple_of((2 * c2 + a) * kc, kc)
                hit = jnp.where(pred_fn(key_ref[pl.ds(off, kc), :], off + sub), 1, 0)
                acc = acc + jnp.sum(hit.reshape(kc // 8, 8, qt), axis=0)
            return acc
        acc = lax.fori_loop(0, npair, body, jnp.zeros((8, qt), I32))
        return jnp.sum(acc, axis=0, keepdims=True)

    def thr_bit(it, lo):
        cand = lo + jnp.left_shift(jnp.int32(1), 31 - it)
        cnt = count(lambda key, _: key >= cand)
        return jnp.where(cnt >= B_TOPK, cand, lo)

    thr = lax.fori_loop(0, 32, thr_bit, jnp.full((1, qt), INT_MIN, I32))
    need = B_TOPK - count(lambda key, _: key > thr)
    n_eq = count(lambda key, _: key == thr)

    def tie_search(_):
        def bit(it, j):
            cand = j + jnp.left_shift(jnp.int32(1), 12 - it)
            cnt = count(lambda key, idx: (key == thr) & (idx < cand))
            return jnp.where(cnt <= need, cand, j)
        return lax.fori_loop(0, 13, bit, jnp.zeros((1, qt), I32))

    any_tie = jnp.max(jnp.where(n_eq > need, 1, 0)) > 0
    j_max = lax.cond(any_tie, tie_search, lambda _: jnp.full((1, qt), 2 ** 13, I32), 0)

    def mask_chunk(c, carry):
        off = pl.multiple_of(c * kc, kc)
        key = key_ref[pl.ds(off, kc), :]
        idx = off + sub
        sel = ((key > thr) | ((key == thr) & (idx < j_max))) & (idx <= t_q)
        msk_ref[pl.ds(off, kc), :] = jnp.where(sel, -shift_ref[...], NEG)
        return carry

    lax.fori_loop(0, 2 * npair, mask_chunk, 0)

    grp = B_HEADS // B_KV_HEADS
    q = q_ref[...]
    groups = range(B_KV_HEADS)
    qg = [jnp.concatenate([q[:, (g * grp + j) * HEAD_DIM:(g * grp + j + 1) * HEAD_DIM]
                           for j in range(grp)], axis=0) for g in groups]
    ksl = [slice(g * HEAD_DIM, (g + 1) * HEAD_DIM) for g in groups]

    fk = 2 * kc

    def flash(c2, carry):
        m, l, acc = carry
        off = pl.multiple_of(c2 * fk, fk)
        tiles = [jnp.clip(qb - (fk // qt) * c2 - a, 0, B_NDELTA - 1) for a in range(fk // qt)]
        mk = msk_ref[pl.ds(off, fk), :]
        mk = jnp.concatenate([mk] * grp, axis=1)
        s = [lax.dot_general(k_ref[pl.ds(off, fk), ksl[g]], qg[g], nt, preferred_element_type=F32)
             for g in groups]
        s = [s[g] + jnp.concatenate([bias_ref[d, g] for d in tiles], axis=0) + mk for g in groups]
        if bounded:
            p = [jnp.exp2(s[g]) for g in groups]
            l = [l[g] + jnp.sum(p[g], axis=0, keepdims=True) for g in groups]
            p = [p[g].astype(BF16) for g in groups]
            pv = [jnp.dot(vt_ref[ksl[g], pl.ds(off, fk)], p[g], preferred_element_type=F32) for g in groups]
            return m, l, [acc[g] + pv[g] for g in groups]
        else:
            m_new = [jnp.maximum(m[g], jnp.max(s[g], axis=0, keepdims=True)) for g in groups]
            p = [jnp.exp2(s[g] - m_new[g]) for g in groups]
        pv = [jnp.dot(vt_ref[ksl[g], pl.ds(off, fk)], p[g].astype(BF16), preferred_element_type=F32)
              for g in groups]
        alpha = [jnp.exp2(m[g] - m_new[g]) for g in groups]
        l = [alpha[g] * l[g] + jnp.sum(p[g], axis=0, keepdims=True) for g in groups]
        acc = [alpha[g] * acc[g] + pv[g] for g in groups]
        return m_new, l, acc

    m0 = [jnp.full((1, grp * qt), NEG, F32) for _ in groups]
    l0 = [jnp.zeros((1, grp * qt), F32) for _ in groups]
    a0 = [jnp.zeros((HEAD_DIM, grp * qt), F32) for _ in groups]
    _, l, acc = lax.fori_loop(0, npair, flash, (m0, l0, a0))
    for g in groups:
        o = acc[g] / l[g]
        for j in range(0, grp, 2):
            h = g * grp + j
            pair = jnp.concatenate([o[:, j * qt:(j + 1) * qt], o[:, (j + 1) * qt:(j + 2) * qt]], axis=0)
            o_ref[:, h * HEAD_DIM:(h + 2) * HEAD_DIM] = pair.T.astype(o_ref.dtype)


def _toeplitz(vec, n_rows, n_cols):
    span = n_rows + n_cols - 1
    assert vec.shape[-1] == span
    lead = vec.shape[:-1]
    padded = jnp.concatenate([vec, jnp.zeros(lead + (1,), vec.dtype)], axis=-1)
    flat = jnp.tile(padded, n_rows)[..., :n_rows * span]
    return flat.reshape(lead + (n_rows, span))[..., n_rows - 1:]


def _bias_b(tab):
    grp = B_HEADS // B_KV_HEADS
    span = 2 * B_QT - 1
    dist = np.arange(B_NDELTA)[:, None] * B_QT + np.arange(span)[None, :] - (B_QT - 1)
    assert (B_NDELTA - 1) * B_QT - (B_QT - 1) >= 16 * 128 ** (15.0 / 16.0) + 1
    vec = tab[_t5_bucket(jnp.asarray(np.maximum(dist, 0), I32))].astype(F32) * LOG2E
    t = _toeplitz(vec.transpose(0, 2, 1), B_QT, B_QT)
    t = t.reshape(B_NDELTA, B_KV_HEADS, grp, B_QT, B_QT).transpose(0, 1, 3, 2, 4)
    return t.reshape(B_NDELTA, B_KV_HEADS, B_QT, grp * B_QT)


def _mixer_b(x2, gain, w_in, q_gain, k_gain, bias_b, batch, seq, tm):
    d = x2.shape[1]
    nq, nkv = B_HEADS * HEAD_DIM, B_KV_HEADS * HEAD_DIM
    nidx = B_IDX_HEADS * B_IDX_DIM
    tn = V7X_MXU_DIM
    used = 2 * nkv + nq + nidx + B_IDX_DIM + B_IDX_HEADS
    kout = -(-used // tn) * tn
    w = jnp.concatenate([w_in, jnp.zeros((d, kout - used), F32)], axis=1).astype(BF16)
    hg = jnp.concatenate([jnp.tile(q_gain * (HEAD_DIM ** -0.5 * LOG2E), B_HEADS), jnp.tile(k_gain, B_KV_HEADS),
                          jnp.ones((kout - nq - nkv,), F32)]).reshape(1, -1)
    n = batch * seq
    nrest = kout - nq - 2 * nkv
    bd = np.kron(np.eye(tn // HEAD_DIM), np.ones((HEAD_DIM, HEAD_DIM))).astype(np.float32)
    const = lambda shape: pl.BlockSpec(shape, lambda i: (0, 0), pipeline_mode=pl.Buffered(1))
    qk, idx = pl.pallas_call(
        functools.partial(_in_proj_b_kernel, tn=tn, n_norm=(nq + nkv) // tn),
        out_shape=[jax.ShapeDtypeStruct((n, nq + 2 * nkv), BF16), jax.ShapeDtypeStruct((n, nrest), F32)],
        grid=(n // tm,),
        in_specs=[pl.BlockSpec((tm, d), lambda i: (i, 0)), const((1, d)), const((d, kout)),
                  const((1, kout)), const((tn, tn))],
        out_specs=[pl.BlockSpec((tm, nq + 2 * nkv), lambda i: (i, 0)),
                   pl.BlockSpec((tm, nrest), lambda i: (i, 0))],
        compiler_params=_cparams(("parallel",)),
        name="in_proj_b",
    )(x2, gain.reshape(1, d), w, hg, jnp.asarray(bd, BF16))
    qk = qk.reshape(batch, seq, nq + 2 * nkv)
    idx = idx.reshape(batch, seq, nrest)
    vt = qk[:, :, nq + nkv:].transpose(0, 2, 1)
    assert seq % (2 * B_QT) == 0 and seq <= 2 ** 12 and nkv == tn and nrest == nidx + tn

    def attend(shift, bounded):
        return pl.pallas_call(
            functools.partial(_attn_b_kernel, bounded=bounded),
            out_shape=jax.ShapeDtypeStruct((batch, seq, nq), BF16),
            grid=(batch, seq // B_QT),
            in_specs=[
                pl.BlockSpec((None, B_QT, nq), lambda b, i: (b, i, 0)),
                pl.BlockSpec((None, seq, nkv), lambda b, i: (b, 0, nq // nkv)),
                pl.BlockSpec((None, nkv, seq), lambda b, i: (b, 0, 0)),
                pl.BlockSpec((None, B_QT, nidx), lambda b, i: (b, i, 0)),
                pl.BlockSpec((None, seq, tn), lambda b, i: (b, 0, nidx // tn)),
                pl.BlockSpec((B_NDELTA, B_KV_HEADS, B_QT, (B_HEADS // B_KV_HEADS) * B_QT),
                             lambda b, i: (0, 0, 0, 0), pipeline_mode=pl.Buffered(1)),
                pl.BlockSpec((1, B_QT), lambda b, i: (0, 0)),
            ],
            out_specs=pl.BlockSpec((None, B_QT, nq), lambda b, i: (b, i, 0)),
            scratch_shapes=[pltpu.VMEM((seq, B_QT), I32), pltpu.VMEM((seq, B_QT), F32)],
            compiler_params=_cparams(("parallel", "arbitrary")),
            name="attn_b_bounded" if bounded else "attn_b",
        )(qk, qk, vt, idx, idx, bias_t, shift)

    bias_t = _bias_b(bias_b)
    qk_bound = 1.02 * HEAD_DIM * (HEAD_DIM ** -0.5 * LOG2E) * jnp.max(jnp.abs(q_gain)) * jnp.max(jnp.abs(k_gain))
    bound = qk_bound + jnp.max(bias_b) * LOG2E
    spread = bound + qk_bound - jnp.min(bias_b) * LOG2E
    y = lax.cond(spread <= B_SAFE_SPREAD,
                 lambda: attend(jnp.full((1, B_QT), bound, F32), True),
                 lambda: attend(jnp.zeros((1, B_QT), F32), False))
    return [y.reshape(batch * seq, nq)], [pl.BlockSpec((tm, nq), lambda i: (i, 0))]


def _in_proj_c_kernel(x_ref, g_ref, w_ref, cw_ref, qkv_ref, rest_ref, tail_ref, xs_ref, *, tn):
    i = pl.program_id(1)
    tm = x_ref.shape[0]
    hw = C_HEADS * C_DK
    hn = _rms_rows(x_ref[...], g_ref[...]).astype(BF16)
    for j in range(w_ref.shape[1] // tn):
        cols = slice(j * tn, (j + 1) * tn)
        y = jnp.dot(hn, w_ref[:, cols], preferred_element_type=F32)
        if j * tn >= 3 * hw:
            rest_ref[:, j * tn - 3 * hw:(j + 1) * tn - 3 * hw] = y
            continue
        hist = tail_ref.shape[0]
        xs_ref[0:hist, cols] = jnp.where(i > 0, tail_ref[:, cols], 0.0)
        xs_ref[hist:, cols] = y
        tail_ref[:, cols] = y[tm - hist:, :]
        cw = cw_ref[:, cols]
        conv = sum(cw[t:t + 1, :] * xs_ref[hist - (C_CONV - 1) + t:hist - (C_CONV - 1) + t + tm, cols]
                   for t in range(C_CONV))
        z = conv * jax.nn.sigmoid(conv)
        if j * tn >= 2 * hw:
            qkv_ref[:, cols] = z
            continue
        scale = C_DK ** -0.5 if j * tn < hw else 1.0
        for h in range(tn // C_DK):
            zh = z[:, h * C_DK:(h + 1) * C_DK]
            ss = jnp.sum(zh * zh, axis=-1, keepdims=True)
            qkv_ref[:, j * tn + h * C_DK:j * tn + (h + 1) * C_DK] = zh * (lax.rsqrt(ss + EPS) * scale)


def _delta_kernel(q_ref, k_ref, v_ref, gate_ref, a_ref, b_ref, alog_ref, dtb_ref, og_ref, y_ref,
                  rows_s, p_s, r_s, m_s, n_s, el_s, o_s):
    cs, gs = C_CHUNK, C_GROUP
    per = gs // cs
    seq, dv = v_ref.shape
    n_groups = seq // gs
    hi = lax.Precision.HIGHEST
    r = lax.broadcasted_iota(I32, (gs, gs), 0)
    c_ = lax.broadcasted_iota(I32, (gs, gs), 1)

    def same_block(size):
        sh = int(math.log2(size))
        return (r >> sh) == (c_ >> sh)

    chunk = same_block(cs)
    lower, strict, eye = chunk & (r >= c_), chunk & (r > c_), r == c_

    z = a_ref[...] + dtb_ref[...]
    softplus = jnp.maximum(z, 0.0) + jnp.log(1.0 + jnp.exp(-jnp.abs(z)))
    g_all = -jnp.exp(alog_ref[...]) * softplus
    gc_all = jnp.dot(g_all, jnp.where(chunk & (r <= c_), 1.0, 0.0), preferred_element_type=F32, precision=hi)
    gl_all = jnp.dot(g_all, jnp.where(chunk, 1.0, 0.0), preferred_element_type=F32, precision=hi)
    rows_s[0] = jax.nn.sigmoid(b_ref[...])
    rows_s[1] = gc_all
    rows_s[2] = jnp.exp(gc_all)
    rows_s[3] = jnp.exp(gl_all - gc_all)
    first = (lax.broadcasted_iota(I32, (gs, per * dv), 0)
             == (lax.broadcasted_iota(I32, (gs, per * dv), 1) // dv) * cs)
    el_s[...] = jnp.exp(jnp.dot(gl_all, jnp.where(first, 1.0, 0.0), preferred_element_type=F32, precision=hi))

    def mm(a, b):
        return jnp.dot(a.astype(BF16), b.astype(BF16), preferred_element_type=F32)

    def mm_nt(a, b):
        return lax.dot_general(a.astype(BF16), b.astype(BF16), (((1,), (1,)), ((), ())),
                               preferred_element_type=F32)

    ways = 4

    def prep(m):
        ids = [m * ways + a for a in range(ways)]
        rows = [pl.ds(pl.multiple_of(i * gs, gs), gs) for i in ids]
        rw = [jnp.concatenate([rows_s[j, pl.ds(i, 1), :] for j in range(4)] + [jnp.zeros((4, gs), F32)], axis=0)
              for i in ids]
        cl = [x.T for x in rw]
        beta_col, gc_col, eg_col, ekg_col = ([x[:, j:j + 1] for x in cl] for j in range(4))
        q, k, v = ([ref[rw_, :] for rw_ in rows] for ref in (q_ref, k_ref, v_ref))
        decay = [jnp.where(lower, jnp.exp(jnp.where(lower, gcc - x[1:2, :], 0.0)), 0.0)
                 for gcc, x in zip(gc_col, rw)]
        kb = [k_ * b_ for k_, b_ in zip(k, beta_col)]
        yield
        a_mat = [jnp.where(strict, mm_nt(kb_, k_) * d_, 0.0) for kb_, k_, d_ in zip(kb, k, decay)]
        yield
        a8 = [jnp.where(same_block(8), a_, 0.0) for a_ in a_mat]
        t = [jnp.where(eye, 1.0, 0.0) - a_ for a_ in a8]
        pw = [mm(a_, a_) for a_ in a8]
        yield
        t = [t_ + mm(t_, p_) for t_, p_ in zip(t, pw)]
        yield
        pw = [mm(p_, p_) for p_ in pw]
        yield
        t = [t_ + mm(t_, p_) for t_, p_ in zip(t, pw)]
        yield
        size = 8
        while size < cs:
            sel = same_block(2 * size) & jnp.logical_not(same_block(size))
            nt = [mm(jnp.where(sel, a_, 0.0), t_) for a_, t_ in zip(a_mat, t)]
            yield
            t = [t_ - mm(t_, n_) for t_, n_ in zip(t, nt)]
            yield
            size *= 2
        uw = [mm(t_, jnp.concatenate([v_ * b_, kb_ * e_], axis=1))
              for t_, v_, b_, kb_, e_ in zip(t, v, beta_col, kb, eg_col)]
        yield
        qk = [jnp.where(lower, mm_nt(q_, k_) * d_, 0.0) for q_, k_, d_ in zip(q, k, decay)]
        yield
        qkwu = [mm(qk_, uw_) for qk_, uw_ in zip(qk, uw)]
        kgt = [(k_ * e_).T for k_, e_ in zip(k, ekg_col)]
        yield
        nm = [[mm(kgt[a][:, j * cs:(j + 1) * cs], uw[a][j * cs:(j + 1) * cs, :]) for j in range(per)]
              for a in range(ways)]

        def write():
            for a, i in enumerate(ids):
                r_s[rows[a], :] = qkwu[a][:, :dv]
                p_s[rows[a], :] = (q[a] * eg_col[a] - qkwu[a][:, dv:]).astype(BF16)
                for j in range(per):
                    n_s[i * per + j] = nm[a][j][:, :dv]
                    m_s[i * per + j] = nm[a][j][:, dv:].astype(BF16)
        return write

    def scan(m, state):
        for a in range(ways):
            i = m * ways + a
            el = el_s[pl.ds(i, 1), :]
            for j in range(per):
                rows = pl.ds(pl.multiple_of(i * gs + j * cs, cs), cs)
                c = i * per + j
                sb = state[0].astype(BF16)
                o_s[rows, :] = jnp.dot(p_s[rows, :], sb, preferred_element_type=F32) + r_s[rows, :]
                state[0] = (state[0] * el[:, j * dv:(j + 1) * dv] + n_s[c]
                            - jnp.dot(m_s[c], sb, preferred_element_type=F32))
                yield

    def weave(*gens):
        gens, results = list(gens), [None] * len(gens)
        live = list(range(len(gens)))
        while live:
            for idx in list(live):
                try:
                    next(gens[idx])
                except StopIteration as stop:
                    results[idx] = stop.value
                    live.remove(idx)
        return results

    n_blocks = n_groups // ways
    weave(prep(0))[0]()

    def step(m, state):
        holder = [state]
        _, write = weave(scan(m - 1, holder), prep(m))
        write()
        return holder[0]

    state = lax.fori_loop(1, n_blocks, step, jnp.zeros((C_DK, dv), F32))

    def finish(first, last):
        for t in range(first, last):
            rows = pl.ds(t * gs, gs)
            gate = gate_ref[rows, :]
            y_ref[rows, :] = (_rms_rows(o_s[rows, :], og_ref[...])
                              * (gate * jax.nn.sigmoid(gate))).astype(y_ref.dtype)
            yield

    done = (n_blocks - 1) * ways
    weave(scan(n_blocks - 1, [state]), finish(0, done))
    weave(finish(done, n_groups))


def _mixer_c(x2, gain, w_in, conv_w, a_log, dt_bias, o_gain, batch, seq, tm):
    d = x2.shape[1]
    hw = C_HEADS * C_DK
    tn = V7X_MXU_DIM
    used = 4 * hw + 2 * C_HEADS
    kout = -(-used // tn) * tn
    w = jnp.concatenate([w_in, jnp.zeros((d, kout - used), F32)], axis=1).astype(BF16)
    const = lambda shape: pl.BlockSpec(shape, lambda b, i: (0, 0), pipeline_mode=pl.Buffered(1))
    qkv, rest = pl.pallas_call(
        functools.partial(_in_proj_c_kernel, tn=tn),
        out_shape=[jax.ShapeDtypeStruct((batch, seq, 3 * hw), F32),
                   jax.ShapeDtypeStruct((batch, seq, kout - 3 * hw), F32)],
        grid=(batch, seq // tm),
        in_specs=[pl.BlockSpec((None, tm, d), lambda b, i: (b, i, 0)), const((1, d)), const((d, kout)),
                  const((C_CONV, 3 * hw))],
        out_specs=[pl.BlockSpec((None, tm, 3 * hw), lambda b, i: (b, i, 0)),
                   pl.BlockSpec((None, tm, kout - 3 * hw), lambda b, i: (b, i, 0))],
        scratch_shapes=[pltpu.VMEM((V7X_SUBLANES, 3 * hw), F32), pltpu.VMEM((tm + V7X_SUBLANES, 3 * hw), F32)],
        compiler_params=_cparams(("parallel", "arbitrary")),
        name="in_proj_c",
    )(x2.reshape(batch, seq, d), gain.reshape(1, d), w, conv_w)

    n_groups = seq // C_GROUP
    ba = rest[:, :, hw:hw + 2 * C_HEADS].transpose(0, 2, 1).reshape(batch, 2 * C_HEADS, n_groups, C_GROUP)
    per_head = lambda v: jnp.broadcast_to(v.reshape(C_HEADS, 1, 1), (C_HEADS, 1, C_GROUP))
    head_cols = lambda off: pl.BlockSpec((None, seq, C_DK), lambda b, h: (b, 0, off + h))
    small = lambda off: pl.BlockSpec((None, None, n_groups, C_GROUP), lambda b, h: (b, off + h, 0, 0))
    scalar_row = pl.BlockSpec((None, 1, C_GROUP), lambda b, h: (h, 0, 0))
    y = pl.pallas_call(
        _delta_kernel,
        out_shape=jax.ShapeDtypeStruct((batch, seq, hw), BF16),
        grid=(batch, C_HEADS),
        in_specs=[head_cols(0), head_cols(C_HEADS), head_cols(2 * C_HEADS),
                  head_cols(0),
                  small(C_HEADS), small(0), scalar_row, scalar_row,
                  pl.BlockSpec((1, C_DK), lambda b, h: (0, 0))],
        out_specs=head_cols(0),
        scratch_shapes=[
            pltpu.VMEM((4, n_groups, C_GROUP), F32),
            pltpu.VMEM((seq, C_DK), BF16),
            pltpu.VMEM((seq, C_DK), F32),
            pltpu.VMEM((seq // C_CHUNK, C_DK, C_DK), BF16),
            pltpu.VMEM((seq // C_CHUNK, C_DK, C_DK), F32),
            pltpu.VMEM((n_groups, (C_GROUP // C_CHUNK) * C_DK), F32),
            pltpu.VMEM((seq, C_DK), F32),
        ],
        compiler_params=_cparams(("parallel", "parallel")),
        name="delta_c",
    )(qkv, qkv, qkv, rest, ba, ba, per_head(a_log), per_head(dt_bias), o_gain.reshape(1, C_DK))
    return [y.reshape(batch * seq, hw)], [pl.BlockSpec((tm, hw), lambda i: (i, 0))]


def kernel(x, rel_bias, norm_mix, norm_mlp, mlp_w1, mlp_w2, a_w_in, a_q_gain, a_k_gain, a_w_out,
           b_w_in, b_q_gain, b_k_gain, b_w_out, c_w_in, c_conv_w, c_a_log, c_dt_bias, c_o_gain, c_w_out):
    batch, seq, d = x.shape
    depth = norm_mix.shape[0]
    a_cols = len(A_GROUPS) * A_HEADS
    bias_a, bias_b = rel_bias[:, :a_cols], rel_bias[:, a_cols:]
    x2 = x.reshape(batch * seq, d)
    tm = 512
    for i in range(depth):
        kind, j = i % 3, i // 3
        if kind == 0:
            mix, specs = _mixer_a(x2, norm_mix[i], a_w_in[j], a_q_gain[j], a_k_gain[j], bias_a, batch, seq, tm)
            w_out = a_w_out[j]
        elif kind == 1:
            mix, specs = _mixer_b(x2, norm_mix[i], b_w_in[j], b_q_gain[j], b_k_gain[j], bias_b, batch, seq, tm)
            w_out = b_w_out[j]
        else:
            mix, specs = _mixer_c(x2, norm_mix[i], c_w_in[j], c_conv_w[j], c_a_log[j], c_dt_bias[j],
                                  c_o_gain[j], batch, seq, tm)
            w_out = c_w_out[j]
        x2 = _proj_mlp(mix, specs, w_out.astype(BF16), x2, norm_mlp[i], mlp_w1[i].astype(BF16),
                       mlp_w2[i].astype(BF16), groups_a=kind == 0, tm=tm)
    return x2.reshape(batch, seq, d)
```

```python
import functools
import math

import numpy as np
import jax
import jax.numpy as jnp
from jax import lax
from jax.experimental import pallas as pl
from jax.experimental.pallas import tpu as pltpu

F32 = jnp.float32
BF16 = jnp.bfloat16
I32 = jnp.int32

EPS = 1e-6
HEAD_DIM = 64
NEG = -1e30
INT_MIN = -2 ** 31
LOG2E = math.log2(math.e)

V7X_VMEM_BYTES = 64 * 1024 * 1024
VMEM_LIMIT = V7X_VMEM_BYTES - 8 * 1024 * 1024
V7X_MXU_DIM = 256
V7X_SUBLANES = 8

NUM_BUCKETS = 32
MAX_DISTANCE = 2048

A_GROUPS = ((128, 1), (512, 4), (2048, 16))
A_HEADS = 8
A_BLOCK = 128
A_GW = A_HEADS * HEAD_DIM
A_MAX_STEP = 8

B_HEADS = 16
B_KV_HEADS = 4
B_IDX_HEADS = 8
B_IDX_DIM = 64
B_TOPK = 256
B_QT = 128
B_KC = 256
B_NDELTA = 14
B_SAFE_SPREAD = 100.0

C_HEADS = 8
C_DK = 128
C_CONV = 4
C_CHUNK = 64
C_GROUP = 4 * C_CHUNK


def _cparams(sem):
    return pltpu.CompilerParams(dimension_semantics=sem, vmem_limit_bytes=VMEM_LIMIT)


def _t5_bucket(dist):
    max_exact = NUM_BUCKETS // 2
    d = jnp.maximum(dist, 1).astype(F32)
    log_part = jnp.log(d / max_exact) / math.log(MAX_DISTANCE / max_exact) * (NUM_BUCKETS - max_exact)
    large = jnp.minimum(max_exact + log_part.astype(I32), NUM_BUCKETS - 1)
    return jnp.where(dist < max_exact, dist, large)


def _rms_rows(x, gain_row):
    ms = jnp.mean(x * x, axis=-1, keepdims=True)
    return x * lax.rsqrt(ms + EPS) * gain_row


def _in_proj_b_kernel(x_ref, g_ref, w_ref, hg_ref, bd_ref, qkv_ref, idx_ref, *, tn, n_norm):
    n_qkv = qkv_ref.shape[1] // tn
    hn = _rms_rows(x_ref[...], g_ref[...]).astype(BF16)
    for j in range(w_ref.shape[1] // tn):
        cols = slice(j * tn, (j + 1) * tn)
        y = jnp.dot(hn, w_ref[:, cols], preferred_element_type=F32)
        if j < n_norm:
            ms = jnp.dot((y * y).astype(BF16), bd_ref[...], preferred_element_type=F32) * (1.0 / HEAD_DIM)
            y = y * lax.rsqrt(ms + EPS) * hg_ref[:, cols]
        if j < n_qkv:
            qkv_ref[:, cols] = y.astype(BF16)
        else:
            idx_ref[:, (j - n_qkv) * tn:(j - n_qkv + 1) * tn] = y


def _mix_groups_a(tm, o0, o1, o2, l0, l1, l2, pt4_ref, pt16_ref):

    def token_order(ref, pt_ref):
        v = ref[...]
        if pt_ref is None:
            return v.reshape(tm, A_GW)
        pb = pt_ref.shape[0]
        per = pb // v.shape[0]
        pieces = []
        for h in range(tm // pb):
            hi, lo = _split_bf16(v[:, h * per:(h + 1) * per, :].reshape(pb, A_GW))
            pieces.append(jnp.dot(pt_ref[...], hi, preferred_element_type=F32)
                          + jnp.dot(pt_ref[...], lo, preferred_element_type=F32))
        return jnp.concatenate(pieces, axis=0)

    pts = (None, pt4_ref, pt16_ref)
    a, b, c = (token_order(r, pt) for r, pt in zip((l0, l1, l2), pts))
    m = jnp.maximum(jnp.maximum(a, b), c)
    ea, eb, ec = jnp.exp2(a - m), jnp.exp2(b - m), jnp.exp2(c - m)
    oa, ob, oc = (token_order(r, pt) for r, pt in zip((o0, o1, o2), pts))
    return (ea * oa + eb * ob + ec * oc) / (ea + eb + ec)


def _proj_mlp_kernel(*refs, tf, groups_a):
    wo_ref, x_ref, g_ref, w1_ref, w2_ref, o_ref = refs[-6:]
    y = _mix_groups_a(x_ref.shape[0], *refs[:-6]) if groups_a else refs[0][...]
    x = x_ref[...] + jnp.dot(y.astype(BF16), wo_ref[...], preferred_element_type=F32)
    hn = _rms_rows(x, g_ref[...]).astype(BF16)
    acc = x
    for f in range(w1_ref.shape[1] // tf):
        cols = slice(f * tf, (f + 1) * tf)
        h = jnp.maximum(jnp.dot(hn, w1_ref[:, cols], preferred_element_type=F32), 0.0)
        acc = acc + jnp.dot((h * h).astype(BF16), w2_ref[cols, :], preferred_element_type=F32)
    o_ref[...] = acc


def _proj_mlp(mix, mix_specs, w_out, x2, gain, w1, w2, *, groups_a, tm, tf=1024):
    n, d = x2.shape
    dff = w1.shape[1]
    const = lambda shape: pl.BlockSpec(shape, lambda i: (0, 0), pipeline_mode=pl.Buffered(1))
    return pl.pallas_call(
        functools.partial(_proj_mlp_kernel, tf=tf, groups_a=groups_a),
        out_shape=jax.ShapeDtypeStruct((n, d), F32),
        grid=(n // tm,),
        in_specs=list(mix_specs) + [const(w_out.shape), pl.BlockSpec((tm, d), lambda i: (i, 0)), const((1, d)),
                                    const((d, dff)), const((dff, d))],
        out_specs=pl.BlockSpec((tm, d), lambda i: (i, 0)),
        compiler_params=_cparams(("parallel",)),
        name="proj_mlp_a" if groups_a else "proj_mlp",
    )(*mix, w_out, x2, gain.reshape(1, d), w1, w2)


def _residue_major(tm, dilation):
    p = np.zeros((tm, tm), np.float32)
    j, r = np.meshgrid(np.arange(tm // dilation), np.arange(dilation), indexing="ij")
    p[(r * (tm // dilation) + j).ravel(), (j * dilation + r).ravel()] = 1.0
    return p


def _in_proj_a_kernel(x_ref, g_ref, w_ref, hg_ref, bd_ref, *rest):
    n_groups = len(A_GROUPS)
    perms, outs = rest[:n_groups - 1], rest[n_groups - 1:]
    tm = x_ref.shape[0]
    hn = _rms_rows(x_ref[...], g_ref[...]).astype(BF16)
    for which in range(3):
        for g, (_, dilation) in enumerate(A_GROUPS):
            j = which * n_groups + g
            cols = slice(j * A_GW, (j + 1) * A_GW)
            y = jnp.dot(hn, w_ref[:, cols], preferred_element_type=F32)
            if which < 2:
                nb = bd_ref.shape[0]
                y2 = (y * y).astype(BF16)
                ms = jnp.concatenate([jnp.dot(y2[:, c:c + nb], bd_ref[...], preferred_element_type=F32)
                                      for c in range(0, A_GW, nb)], axis=1) * (1.0 / HEAD_DIM)
                y = y * lax.rsqrt(ms + EPS) * hg_ref[:, cols]
            y = y.astype(BF16)
            if dilation == 1:
                y = y.reshape(1, tm, A_GW)
            else:
                perm = perms[g - 1]
                pb = perm.shape[0]
                y = jnp.concatenate(
                    [jnp.dot(perm[...], y[h:h + pb], preferred_element_type=F32).astype(BF16)
                     .reshape(dilation, pb // dilation, A_GW) for h in range(0, tm, pb)], axis=1)
            outs[g][:, :, which * A_GW:(which + 1) * A_GW] = y


def _attn_a_kernel(q_ref, kp_ref, kc_ref, vp_ref, vc_ref, bias_ref, o_ref, lse_ref):
    n = pl.program_id(2)
    blk = A_BLOCK
    pair = 2 * HEAD_DIM
    lane = lax.broadcasted_iota(I32, (2 * blk, pair), 1)
    row = lax.broadcasted_iota(I32, (2 * blk, blk), 0)
    pen = jnp.where((row < blk) & (n == 0), NEG, 0.0)
    nt = (((1,), (1,)), ((), ()))
    tn = (((0,), (0,)), ((), ()))
    n_res, rows = q_ref.shape[0], q_ref.shape[1]
    k_all = [jnp.concatenate([kp_ref[r], kc_ref[r]], axis=0) for r in range(n_res)]
    v_all = [jnp.concatenate([vp_ref[r], vc_ref[r]], axis=0) for r in range(n_res)]
    chains = [(r, sb, hp, a) for r in range(n_res) for sb in range(rows // blk)
              for hp in range(A_HEADS // 2) for a in range(2)]

    def window(x, sb, hp):
        return x[sb * blk:(sb + 2) * blk, hp * pair:(hp + 1) * pair]

    keep = [lane < HEAD_DIM, lane >= HEAD_DIM]
    ka = [jnp.where(keep[a], window(k_all[r], sb, hp), jnp.zeros((2 * blk, pair), BF16))
          for r, sb, hp, a in chains]
    s = [lax.dot_general(k_, q_ref[r, sb * blk:(sb + 1) * blk, hp * pair:(hp + 1) * pair], nt,
                         preferred_element_type=F32) + bias_ref[2 * hp + a]
         for k_, (r, sb, hp, a) in zip(ka, chains)]
    s = [s_ + pen if sb == 0 else s_ for s_, (_, sb, _, _) in zip(s, chains)]
    m = [jnp.max(s_, axis=0, keepdims=True) for s_ in s]
    p = [jnp.exp2(s_ - m_) for s_, m_ in zip(s, m)]
    l = [jnp.sum(p_, axis=0, keepdims=True) for p_ in p]
    pv = [lax.dot_general(window(v_all[r], sb, hp), p_.astype(BF16), tn, preferred_element_type=F32)
          for p_, (r, sb, hp, _) in zip(p, chains)]
    for idx in range(0, len(chains), 2):
        r, sb, hp, _ = chains[idx]
        o_t = jnp.concatenate([(pv[idx + a] / l[idx + a])[a * HEAD_DIM:(a + 1) * HEAD_DIM] for a in range(2)],
                              axis=0)
        l_t = jnp.concatenate([jnp.broadcast_to(m[idx + a] + jnp.log2(l[idx + a]), (HEAD_DIM, blk))
                               for a in range(2)], axis=0)
        o_ref[r, sb * blk:(sb + 1) * blk, hp * pair:(hp + 1) * pair] = o_t.T
        lse_ref[r, sb * blk:(sb + 1) * blk, hp * pair:(hp + 1) * pair] = l_t.T


def _attn_a_group(arr, bias_t, dilation, batch, seq):
    sub = seq // dilation
    step = min(A_MAX_STEP, sub // A_BLOCK)
    rows = step * A_BLOCK
    nb = sub // rows
    n_res = min(dilation, A_MAX_STEP // step)

    def spec(which, prev):
        if prev:
            return pl.BlockSpec((None, n_res, A_BLOCK, A_GW),
                                lambda b, r, n: (b, r, jnp.maximum(n * step - 1, 0), which))
        return pl.BlockSpec((None, n_res, rows, A_GW), lambda b, r, n: (b, r, n, which))

    out_spec = pl.BlockSpec((None, n_res, rows, A_GW), lambda b, r, n: (b, r, n, 0))
    out_sds = jax.ShapeDtypeStruct((batch, dilation, sub, A_GW), F32)
    return pl.pallas_call(
        _attn_a_kernel,
        out_shape=[out_sds, out_sds],
        grid=(batch, dilation // n_res, nb),
        in_specs=[spec(0, False), spec(1, True), spec(1, False), spec(2, True), spec(2, False),
                  pl.BlockSpec((A_HEADS, 2 * A_BLOCK, A_BLOCK), lambda b, r, n: (0, 0, 0))],
        out_specs=[out_spec, out_spec],
        compiler_params=_cparams(("parallel", "parallel", "arbitrary")),
        name=f"attn_a_d{dilation}",
    )(arr, arr, arr, arr, arr, bias_t)


def _bias_a(tab_g, dilation):
    step = np.arange(3 * A_BLOCK - 1) - (A_BLOCK - 1)
    vec = tab_g[_t5_bucket(jnp.asarray(np.maximum(step, 0) * dilation, I32))].astype(F32) * LOG2E
    vec = jnp.where(jnp.asarray((step >= 0) & (step <= A_BLOCK))[:, None], vec, NEG)
    return _toeplitz(vec.T, 2 * A_BLOCK, A_BLOCK)


def _mixer_a(x2, gain, w_in, q_gain, k_gain, bias_a, batch, seq, tm):
    n, d = x2.shape
    n_groups = len(A_GROUPS)
    reps = n_groups * A_HEADS
    hg = jnp.concatenate([jnp.tile(q_gain * (HEAD_DIM ** -0.5 * LOG2E), reps), jnp.tile(k_gain, reps),
                          jnp.ones((n_groups * A_GW,), F32)]).reshape(1, -1)
    kout = 3 * n_groups * A_GW
    pb = V7X_MXU_DIM
    bd = np.kron(np.eye(pb // HEAD_DIM), np.ones((HEAD_DIM, HEAD_DIM))).astype(np.float32)
    perms = [_residue_major(pb, dil) for _, dil in A_GROUPS[1:]]
    tiles = seq // tm
    const = lambda shape: pl.BlockSpec(shape, lambda i: (0, 0), pipeline_mode=pl.Buffered(1))
    grouped = lambda dil, width: pl.BlockSpec((None, dil, tm // dil, width),
                                              lambda i: (i // tiles, 0, i % tiles, 0))
    arrs = pl.pallas_call(
        _in_proj_a_kernel,
        out_shape=[jax.ShapeDtypeStruct((batch, dil, seq // dil, 3 * A_GW), BF16) for _, dil in A_GROUPS],
        grid=(n // tm,),
        in_specs=[pl.BlockSpec((tm, d), lambda i: (i, 0)), const((1, d)), const((d, kout)),
                  const((1, kout)), const((pb, pb))] + [const((pb, pb)) for _ in perms],
        out_specs=[grouped(dil, 3 * A_GW) for _, dil in A_GROUPS],
        compiler_params=_cparams(("parallel",)),
        name="in_proj_a",
    )(x2, gain.reshape(1, d), w_in.astype(BF16), hg, jnp.asarray(bd, BF16),
      *[jnp.asarray(p, BF16) for p in perms])

    os_, ls_ = [], []
    for g, (window, dilation) in enumerate(A_GROUPS):
        assert window // dilation == A_BLOCK and (seq // dilation) % (min(A_MAX_STEP, seq // dilation // A_BLOCK) * A_BLOCK) == 0 and dilation <= pb // 16
        tab_g = bias_a[:, g * A_HEADS:(g + 1) * A_HEADS]
        o, lse = _attn_a_group(arrs[g], _bias_a(tab_g, dilation), dilation, batch, seq)
        os_.append(o)
        ls_.append(lse)

    mix = [*os_, *ls_, *[jnp.asarray(p.T, BF16) for p in perms]]
    specs = [grouped(dil, A_GW) for _, dil in A_GROUPS] * 2 + [const((pb, pb)) for _ in perms]
    return mix, specs


def _float_key(s):
    i = lax.bitcast_convert_type(s, I32)
    k = jnp.where(i < 0, i ^ jnp.int32(0x7FFFFFFF), i)
    return jnp.where(s == 0.0, 0, k)


def _split_bf16(x):
    hi = x.astype(BF16)
    return hi, (x - hi.astype(F32)).astype(BF16)


def _attn_b_kernel(q_ref, k_ref, vt_ref, qi_ref, kw_ref, bias_ref, shift_ref, o_ref, key_ref, msk_ref, *,
                   bounded):
    qb = pl.program_id(1)
    kc, qt = B_KC, B_QT
    nch = qb // 2 + 1
    t_q = qb * qt + lax.broadcasted_iota(I32, (kc, qt), 1)
    sub = lax.broadcasted_iota(I32, (kc, qt), 0)
    nt = (((1,), (1,)), ((), ()))

    qh, ql = _split_bf16(qi_ref[...])
    qi3 = jnp.concatenate(
        [jnp.concatenate([x[:, h * B_IDX_DIM:(h + 1) * B_IDX_DIM] for x in (qh, qh, ql)], axis=1)
         for h in range(B_IDX_HEADS)], axis=0)
    wt = kw_ref[pl.ds(pl.multiple_of(qb * qt, qt), qt), 0:qt].T
    wt = wt * (B_IDX_HEADS ** -0.5 * B_IDX_DIM ** -0.5)

    npair = (nch + 1) // 2

    def score_pair(c2, carry):
        offs = [pl.multiple_of((2 * c2 + a) * kc, kc) for a in range(2)]
        ks = [_split_bf16(kw_ref[pl.ds(off, kc), 0:B_IDX_DIM]) for off in offs]
        sc = [lax.dot_general(jnp.concatenate([kh, kl, kh], axis=1), qi3, nt, preferred_element_type=F32)
              for kh, kl in ks]
        for off, s in zip(offs, sc):
            acc = jnp.zeros((kc, qt), F32)
            for h in range(B_IDX_HEADS):
                acc = acc + wt[B_IDX_DIM + h:B_IDX_DIM + h + 1, :] * jnp.maximum(s[:, h * qt:(h + 1) * qt], 0.0)
            key_ref[pl.ds(off, kc), :] = jnp.where(off + sub <= t_q, _float_key(acc), INT_MIN)
        return carry

    lax.fori_loop(0, npair, score_pair, 0)

    def count(pred_fn, pairs=None):
        def body(c2, acc):
            for a in range(2):
                off = (2 * c2 + a) * kc
                off = off if pairs else pl.multiple_of(off, kc)
                hit = jnp.where(pred_fn(key_ref[pl.ds(off, kc), :], off + sub), 1, 0)
                acc = acc + jnp.sum(hit.reshape(kc // 8, 8, qt), axis=0)
            return acc
        acc = jnp.zeros((8, qt), I32)
        if pairs:
            for c2 in range(pairs):
                acc = body(c2, acc)
        else:
            acc = lax.fori_loop(0, npair, body, acc)
        return jnp.sum(acc, axis=0, keepdims=True)

    def search(pairs):
        def thr_bit(it, lo):
            cand = lo + jnp.left_shift(jnp.int32(1), 31 - it)
            cnt = count(lambda key, _: key >= cand, pairs)
            return jnp.where(cnt >= B_TOPK, cand, lo)
        return lax.fori_loop(0, 32, thr_bit, jnp.full((1, qt), INT_MIN, I32))

    max_pairs = key_ref.shape[0] // (2 * kc)
    thr = lax.switch(npair - 1, [functools.partial(search, n) for n in range(1, max_pairs + 1)])
    need = B_TOPK - count(lambda key, _: key > thr)
    n_eq = count(lambda key, _: key == thr)

    def tie_search(_):
        def bit(it, j):
            cand = j + jnp.left_shift(jnp.int32(1), 12 - it)
            cnt = count(lambda key, idx: (key == thr) & (idx < cand))
            return jnp.where(cnt <= need, cand, j)
        return lax.fori_loop(0, 13, bit, jnp.zeros((1, qt), I32))

    any_tie = jnp.max(jnp.where(n_eq > need, 1, 0)) > 0
    j_max = lax.cond(any_tie, tie_search, lambda _: jnp.full((1, qt), 2 ** 13, I32), 0)

    def mask_chunk(c, carry):
        off = pl.multiple_of(c * kc, kc)
        key = key_ref[pl.ds(off, kc), :]
        idx = off + sub
        sel = ((key > thr) | ((key == thr) & (idx < j_max))) & (idx <= t_q)
        msk_ref[pl.ds(off, kc), :] = jnp.where(sel, -shift_ref[...], NEG)
        return carry

    lax.fori_loop(0, 2 * npair, mask_chunk, 0)

    grp = B_HEADS // B_KV_HEADS
    q = q_ref[...]
    groups = range(B_KV_HEADS)
    qg = [jnp.concatenate([q[:, (g * grp + j) * HEAD_DIM:(g * grp + j + 1) * HEAD_DIM]
                           for j in range(grp)], axis=0) for g in groups]
    ksl = [slice(g * HEAD_DIM, (g + 1) * HEAD_DIM) for g in groups]

    fk = 2 * kc

    def flash(c2, carry):
        m, l, acc = carry
        off = pl.multiple_of(c2 * fk, fk)
        tiles = [jnp.clip(qb - (fk // qt) * c2 - a, 0, B_NDELTA - 1) for a in range(fk // qt)]
        mk = msk_ref[pl.ds(off, fk), :]
        mk = jnp.concatenate([mk] * grp, axis=1)
        s = [lax.dot_general(k_ref[pl.ds(off, fk), ksl[g]], qg[g], nt, preferred_element_type=F32)
             for g in groups]
        s = [s[g] + jnp.concatenate([bias_ref[d, g] for d in tiles], axis=0) + mk for g in groups]
        if bounded:
            p = [jnp.exp2(s[g]) for g in groups]
            l = [l[g] + jnp.sum(p[g], axis=0, keepdims=True) for g in groups]
            p = [p[g].astype(BF16) for g in groups]
            pv = [jnp.dot(vt_ref[ksl[g], pl.ds(off, fk)], p[g], preferred_element_type=F32) for g in groups]
            return m, l, [acc[g] + pv[g] for g in groups]
        else:
            m_new = [jnp.maximum(m[g], jnp.max(s[g], axis=0, keepdims=True)) for g in groups]
            p = [jnp.exp2(s[g] - m_new[g]) for g in groups]
        pv = [jnp.dot(vt_ref[ksl[g], pl.ds(off, fk)], p[g].astype(BF16), preferred_element_type=F32)
              for g in groups]
        alpha = [jnp.exp2(m[g] - m_new[g]) for g in groups]
        l = [alpha[g] * l[g] + jnp.sum(p[g], axis=0, keepdims=True) for g in groups]
        acc = [alpha[g] * acc[g] + pv[g] for g in groups]
        return m_new, l, acc

    m0 = [jnp.full((1, grp * qt), NEG, F32) for _ in groups]
    l0 = [jnp.zeros((1, grp * qt), F32) for _ in groups]
    a0 = [jnp.zeros((HEAD_DIM, grp * qt), F32) for _ in groups]
    _, l, acc = lax.fori_loop(0, npair, flash, (m0, l0, a0))
    for g in groups:
        o = acc[g] / l[g]
        for j in range(0, grp, 2):
            h = g * grp + j
            pair = jnp.concatenate([o[:, j * qt:(j + 1) * qt], o[:, (j + 1) * qt:(j + 2) * qt]], axis=0)
            o_ref[:, h * HEAD_DIM:(h + 2) * HEAD_DIM] = pair.T.astype(o_ref.dtype)


def _toeplitz(vec, n_rows, n_cols):
    span = n_rows + n_cols - 1
    assert vec.shape[-1] == span
    lead = vec.shape[:-1]
    padded = jnp.concatenate([vec, jnp.zeros(lead + (1,), vec.dtype)], axis=-1)
    flat = jnp.tile(padded, n_rows)[..., :n_rows * span]
    return flat.reshape(lead + (n_rows, span))[..., n_rows - 1:]


def _bias_b(tab):
    grp = B_HEADS // B_KV_HEADS
    span = 2 * B_QT - 1
    dist = np.arange(B_NDELTA)[:, None] * B_QT + np.arange(span)[None, :] - (B_QT - 1)
    assert (B_NDELTA - 1) * B_QT - (B_QT - 1) >= 16 * 128 ** (15.0 / 16.0) + 1
    vec = tab[_t5_bucket(jnp.asarray(np.maximum(dist, 0), I32))].astype(F32) * LOG2E
    t = _toeplitz(vec.transpose(0, 2, 1), B_QT, B_QT)
    t = t.reshape(B_NDELTA, B_KV_HEADS, grp, B_QT, B_QT).transpose(0, 1, 3, 2, 4)
    return t.reshape(B_NDELTA, B_KV_HEADS, B_QT, grp * B_QT)


def _mixer_b(x2, gain, w_in, q_gain, k_gain, bias_b, batch, seq, tm):
    d = x2.shape[1]
    nq, nkv = B_HEADS * HEAD_DIM, B_KV_HEADS * HEAD_DIM
    nidx = B_IDX_HEADS * B_IDX_DIM
    tn = V7X_MXU_DIM
    used = 2 * nkv + nq + nidx + B_IDX_DIM + B_IDX_HEADS
    kout = -(-used // tn) * tn
    w = jnp.concatenate([w_in, jnp.zeros((d, kout - used), F32)], axis=1).astype(BF16)
    hg = jnp.concatenate([jnp.tile(q_gain * (HEAD_DIM ** -0.5 * LOG2E), B_HEADS), jnp.tile(k_gain, B_KV_HEADS),
                          jnp.ones((kout - nq - nkv,), F32)]).reshape(1, -1)
    n = batch * seq
    nrest = kout - nq - 2 * nkv
    bd = np.kron(np.eye(tn // HEAD_DIM), np.ones((HEAD_DIM, HEAD_DIM))).astype(np.float32)
    const = lambda shape: pl.BlockSpec(shape, lambda i: (0, 0), pipeline_mode=pl.Buffered(1))
    qk, idx = pl.pallas_call(
        functools.partial(_in_proj_b_kernel, tn=tn, n_norm=(nq + nkv) // tn),
        out_shape=[jax.ShapeDtypeStruct((n, nq + 2 * nkv), BF16), jax.ShapeDtypeStruct((n, nrest), F32)],
        grid=(n // tm,),
        in_specs=[pl.BlockSpec((tm, d), lambda i: (i, 0)), const((1, d)), const((d, kout)),
                  const((1, kout)), const((tn, tn))],
        out_specs=[pl.BlockSpec((tm, nq + 2 * nkv), lambda i: (i, 0)),
                   pl.BlockSpec((tm, nrest), lambda i: (i, 0))],
        compiler_params=_cparams(("parallel",)),
        name="in_proj_b",
    )(x2, gain.reshape(1, d), w, hg, jnp.asarray(bd, BF16))
    qk = qk.reshape(batch, seq, nq + 2 * nkv)
    idx = idx.reshape(batch, seq, nrest)
    vt = qk[:, :, nq + nkv:].transpose(0, 2, 1)
    assert seq % (2 * B_QT) == 0 and seq <= 2 ** 12 and nkv == tn and nrest == nidx + tn

    def attend(shift, bounded):
        return pl.pallas_call(
            functools.partial(_attn_b_kernel, bounded=bounded),
            out_shape=jax.ShapeDtypeStruct((batch, seq, nq), BF16),
            grid=(batch, seq // B_QT),
            in_specs=[
                pl.BlockSpec((None, B_QT, nq), lambda b, i: (b, i, 0)),
                pl.BlockSpec((None, seq, nkv), lambda b, i: (b, 0, nq // nkv)),
                pl.BlockSpec((None, nkv, seq), lambda b, i: (b, 0, 0)),
                pl.BlockSpec((None, B_QT, nidx), lambda b, i: (b, i, 0)),
                pl.BlockSpec((None, seq, tn), lambda b, i: (b, 0, nidx // tn)),
                pl.BlockSpec((B_NDELTA, B_KV_HEADS, B_QT, (B_HEADS // B_KV_HEADS) * B_QT),
                             lambda b, i: (0, 0, 0, 0), pipeline_mode=pl.Buffered(1)),
                pl.BlockSpec((1, B_QT), lambda b, i: (0, 0)),
            ],
            out_specs=pl.BlockSpec((None, B_QT, nq), lambda b, i: (b, i, 0)),
            scratch_shapes=[pltpu.VMEM((seq, B_QT), I32), pltpu.VMEM((seq, B_QT), F32)],
            compiler_params=_cparams(("parallel", "arbitrary")),
            name="attn_b_bounded" if bounded else "attn_b",
        )(qk, qk, vt, idx, idx, bias_t, shift)

    bias_t = _bias_b(bias_b)
    qk_bound = 1.02 * HEAD_DIM * (HEAD_DIM ** -0.5 * LOG2E) * jnp.max(jnp.abs(q_gain)) * jnp.max(jnp.abs(k_gain))
    bound = qk_bound + jnp.max(bias_b) * LOG2E
    spread = bound + qk_bound - jnp.min(bias_b) * LOG2E
    y = lax.cond(spread <= B_SAFE_SPREAD,
                 lambda: attend(jnp.full((1, B_QT), bound, F32), True),
                 lambda: attend(jnp.zeros((1, B_QT), F32), False))
    return [y.reshape(batch * seq, nq)], [pl.BlockSpec((tm, nq), lambda i: (i, 0))]


def _in_proj_c_kernel(x_ref, g_ref, w_ref, cw_ref, qkv_ref, rest_ref, tail_ref, xs_ref, *, tn):
    i = pl.program_id(1)
    tm = x_ref.shape[0]
    hw = C_HEADS * C_DK
    hn = _rms_rows(x_ref[...], g_ref[...]).astype(BF16)
    for j in range(w_ref.shape[1] // tn):
        cols = slice(j * tn, (j + 1) * tn)
        y = jnp.dot(hn, w_ref[:, cols], preferred_element_type=F32)
        if j * tn >= 3 * hw:
            rest_ref[:, j * tn - 3 * hw:(j + 1) * tn - 3 * hw] = y
            continue
        hist = tail_ref.shape[0]
        xs_ref[0:hist, cols] = jnp.where(i > 0, tail_ref[:, cols], 0.0)
        xs_ref[hist:, cols] = y
        tail_ref[:, cols] = y[tm - hist:, :]
        cw = cw_ref[:, cols]
        conv = sum(cw[t:t + 1, :] * xs_ref[hist - (C_CONV - 1) + t:hist - (C_CONV - 1) + t + tm, cols]
                   for t in range(C_CONV))
        z = conv * jax.nn.sigmoid(conv)
        if j * tn >= 2 * hw:
            qkv_ref[:, cols] = z
            continue
        scale = C_DK ** -0.5 if j * tn < hw else 1.0
        for h in range(tn // C_DK):
            zh = z[:, h * C_DK:(h + 1) * C_DK]
            ss = jnp.sum(zh * zh, axis=-1, keepdims=True)
            qkv_ref[:, j * tn + h * C_DK:j * tn + (h + 1) * C_DK] = zh * (lax.rsqrt(ss + EPS) * scale)


def _delta_kernel(q_ref, k_ref, v_ref, gate_ref, a_ref, b_ref, alog_ref, dtb_ref, og_ref, y_ref,
                  rows_s, p_s, r_s, m_s, n_s, el_s, o_s):
    cs, gs = C_CHUNK, C_GROUP
    per = gs // cs
    seq, dv = v_ref.shape
    n_groups = seq // gs
    hi = lax.Precision.HIGHEST
    r = lax.broadcasted_iota(I32, (gs, gs), 0)
    c_ = lax.broadcasted_iota(I32, (gs, gs), 1)

    def same_block(size):
        sh = int(math.log2(size))
        return (r >> sh) == (c_ >> sh)

    chunk = same_block(cs)
    lower, strict, eye = chunk & (r >= c_), chunk & (r > c_), r == c_

    z = a_ref[...] + dtb_ref[...]
    softplus = jnp.maximum(z, 0.0) + jnp.log(1.0 + jnp.exp(-jnp.abs(z)))
    g_all = -jnp.exp(alog_ref[...]) * softplus
    gc_all = jnp.dot(g_all, jnp.where(chunk & (r <= c_), 1.0, 0.0), preferred_element_type=F32, precision=hi)
    gl_all = jnp.dot(g_all, jnp.where(chunk, 1.0, 0.0), preferred_element_type=F32, precision=hi)
    rows_s[0] = jax.nn.sigmoid(b_ref[...])
    rows_s[1] = gc_all
    rows_s[2] = jnp.exp(gc_all)
    rows_s[3] = jnp.exp(gl_all - gc_all)
    first = (lax.broadcasted_iota(I32, (gs, per * dv), 0)
             == (lax.broadcasted_iota(I32, (gs, per * dv), 1) // dv) * cs)
    el_s[...] = jnp.exp(jnp.dot(gl_all, jnp.where(first, 1.0, 0.0), preferred_element_type=F32, precision=hi))

    def mm(a, b):
        return jnp.dot(a.astype(BF16), b.astype(BF16), preferred_element_type=F32)

    def mm_nt(a, b):
        return lax.dot_general(a.astype(BF16), b.astype(BF16), (((1,), (1,)), ((), ())),
                               preferred_element_type=F32)

    ways = 4

    def prep(m):
        ids = [m * ways + a for a in range(ways)]
        rows = [pl.ds(pl.multiple_of(i * gs, gs), gs) for i in ids]
        rw = [jnp.concatenate([rows_s[j, pl.ds(i, 1), :] for j in range(4)] + [jnp.zeros((4, gs), F32)], axis=0)
              for i in ids]
        cl = [x.T for x in rw]
        beta_col, gc_col, eg_col, ekg_col = ([x[:, j:j + 1] for x in cl] for j in range(4))
        q, k, v = ([ref[rw_, :] for rw_ in rows] for ref in (q_ref, k_ref, v_ref))
        decay = [jnp.where(lower, jnp.exp(jnp.where(lower, gcc - x[1:2, :], 0.0)), 0.0)
                 for gcc, x in zip(gc_col, rw)]
        kb = [k_ * b_ for k_, b_ in zip(k, beta_col)]
        yield
        a_mat = [jnp.where(strict, mm_nt(kb_, k_) * d_, 0.0) for kb_, k_, d_ in zip(kb, k, decay)]
        yield
        a8 = [jnp.where(same_block(8), a_, 0.0) for a_ in a_mat]
        t = [jnp.where(eye, 1.0, 0.0) - a_ for a_ in a8]
        pw = [mm(a_, a_) for a_ in a8]
        yield
        t = [t_ + mm(t_, p_) for t_, p_ in zip(t, pw)]
        yield
        pw = [mm(p_, p_) for p_ in pw]
        yield
        t = [t_ + mm(t_, p_) for t_, p_ in zip(t, pw)]
        yield
        size = 8
        while size < cs:
            sel = same_block(2 * size) & jnp.logical_not(same_block(size))
            nt = [mm(jnp.where(sel, a_, 0.0), t_) for a_, t_ in zip(a_mat, t)]
            yield
            t = [t_ - mm(t_, n_) for t_, n_ in zip(t, nt)]
            yield
            size *= 2
        uw = [mm(t_, jnp.concatenate([v_ * b_, kb_ * e_], axis=1))
              for t_, v_, b_, kb_, e_ in zip(t, v, beta_col, kb, eg_col)]
        yield
        qk = [jnp.where(lower, mm_nt(q_, k_) * d_, 0.0) for q_, k_, d_ in zip(q, k, decay)]
        yield
        qkwu = [mm(qk_, uw_) for qk_, uw_ in zip(qk, uw)]
        kgt = [(k_ * e_).T for k_, e_ in zip(k, ekg_col)]
        yield
        nm = [[mm(kgt[a][:, j * cs:(j + 1) * cs], uw[a][j * cs:(j + 1) * cs, :]) for j in range(per)]
              for a in range(ways)]

        def write():
            for a, i in enumerate(ids):
                r_s[rows[a], :] = qkwu[a][:, :dv]
                p_s[rows[a], :] = (q[a] * eg_col[a] - qkwu[a][:, dv:]).astype(BF16)
                for j in range(per):
                    n_s[i * per + j] = nm[a][j][:, :dv]
                    m_s[i * per + j] = nm[a][j][:, dv:].astype(BF16)
        return write

    def scan(m, state):
        for a in range(ways):
            i = m * ways + a
            el = el_s[pl.ds(i, 1), :]
            for j in range(per):
                rows = pl.ds(pl.multiple_of(i * gs + j * cs, cs), cs)
                c = i * per + j
                sb = state[0].astype(BF16)
                o_s[rows, :] = jnp.dot(p_s[rows, :], sb, preferred_element_type=F32) + r_s[rows, :]
                state[0] = (state[0] * el[:, j * dv:(j + 1) * dv] + n_s[c]
                            - jnp.dot(m_s[c], sb, preferred_element_type=F32))
                yield

    def weave(*gens):
        gens, results = list(gens), [None] * len(gens)
        live = list(range(len(gens)))
        while live:
            for idx in list(live):
                try:
                    next(gens[idx])
                except StopIteration as stop:
                    results[idx] = stop.value
                    live.remove(idx)
        return results

    n_blocks = n_groups // ways
    weave(prep(0))[0]()

    def step(m, state):
        holder = [state]
        _, write = weave(scan(m - 1, holder), prep(m))
        write()
        return holder[0]

    state = lax.fori_loop(1, n_blocks, step, jnp.zeros((C_DK, dv), F32))

    def finish(first, last):
        for t in range(first, last):
            rows = pl.ds(t * gs, gs)
            gate = gate_ref[rows, :]
            y_ref[rows, :] = (_rms_rows(o_s[rows, :], og_ref[...])
                              * (gate * jax.nn.sigmoid(gate))).astype(y_ref.dtype)
            yield

    done = (n_blocks - 1) * ways
    weave(scan(n_blocks - 1, [state]), finish(0, done))
    weave(finish(done, n_groups))


def _mixer_c(x2, gain, w_in, conv_w, a_log, dt_bias, o_gain, batch, seq, tm):
    d = x2.shape[1]
    hw = C_HEADS * C_DK
    tn = V7X_MXU_DIM
    used = 4 * hw + 2 * C_HEADS
    kout = -(-used // tn) * tn
    w = jnp.concatenate([w_in, jnp.zeros((d, kout - used), F32)], axis=1).astype(BF16)
    const = lambda shape: pl.BlockSpec(shape, lambda b, i: (0, 0), pipeline_mode=pl.Buffered(1))
    qkv, rest = pl.pallas_call(
        functools.partial(_in_proj_c_kernel, tn=tn),
        out_shape=[jax.ShapeDtypeStruct((batch, seq, 3 * hw), F32),
                   jax.ShapeDtypeStruct((batch, seq, kout - 3 * hw), F32)],
        grid=(batch, seq // tm),
        in_specs=[pl.BlockSpec((None, tm, d), lambda b, i: (b, i, 0)), const((1, d)), const((d, kout)),
                  const((C_CONV, 3 * hw))],
        out_specs=[pl.BlockSpec((None, tm, 3 * hw), lambda b, i: (b, i, 0)),
                   pl.BlockSpec((None, tm, kout - 3 * hw), lambda b, i: (b, i, 0))],
        scratch_shapes=[pltpu.VMEM((V7X_SUBLANES, 3 * hw), F32), pltpu.VMEM((tm + V7X_SUBLANES, 3 * hw), F32)],
        compiler_params=_cparams(("parallel", "arbitrary")),
        name="in_proj_c",
    )(x2.reshape(batch, seq, d), gain.reshape(1, d), w, conv_w)

    n_groups = seq // C_GROUP
    ba = rest[:, :, hw:hw + 2 * C_HEADS].transpose(0, 2, 1).reshape(batch, 2 * C_HEADS, n_groups, C_GROUP)
    per_head = lambda v: jnp.broadcast_to(v.reshape(C_HEADS, 1, 1), (C_HEADS, 1, C_GROUP))
    head_cols = lambda off: pl.BlockSpec((None, seq, C_DK), lambda b, h: (b, 0, off + h))
    small = lambda off: pl.BlockSpec((None, None, n_groups, C_GROUP), lambda b, h: (b, off + h, 0, 0))
    scalar_row = pl.BlockSpec((None, 1, C_GROUP), lambda b, h: (h, 0, 0))
    y = pl.pallas_call(
        _delta_kernel,
        out_shape=jax.ShapeDtypeStruct((batch, seq, hw), BF16),
        grid=(batch, C_HEADS),
        in_specs=[head_cols(0), head_cols(C_HEADS), head_cols(2 * C_HEADS),
                  head_cols(0),
                  small(C_HEADS), small(0), scalar_row, scalar_row,
                  pl.BlockSpec((1, C_DK), lambda b, h: (0, 0))],
        out_specs=head_cols(0),
        scratch_shapes=[
            pltpu.VMEM((4, n_groups, C_GROUP), F32),
            pltpu.VMEM((seq, C_DK), BF16),
            pltpu.VMEM((seq, C_DK), F32),
            pltpu.VMEM((seq // C_CHUNK, C_DK, C_DK), BF16),
            pltpu.VMEM((seq // C_CHUNK, C_DK, C_DK), F32),
            pltpu.VMEM((n_groups, (C_GROUP // C_CHUNK) * C_DK), F32),
            pltpu.VMEM((seq, C_DK), F32),
        ],
        compiler_params=_cparams(("parallel", "parallel")),
        name="delta_c",
    )(qkv, qkv, qkv, rest, ba, ba, per_head(a_log), per_head(dt_bias), o_gain.reshape(1, C_DK))
    return [y.reshape(batch * seq, hw)], [pl.BlockSpec((tm, hw), lambda i: (i, 0))]


def kernel(x, rel_bias, norm_mix, norm_mlp, mlp_w1, mlp_w2, a_w_in, a_q_gain, a_k_gain, a_w_out,
           b_w_in, b_q_gain, b_k_gain, b_w_out, c_w_in, c_conv_w, c_a_log, c_dt_bias, c_o_gain, c_w_out):
    batch, seq, d = x.shape
    depth = norm_mix.shape[0]
    a_cols = len(A_GROUPS) * A_HEADS
    bias_a, bias_b = rel_bias[:, :a_cols], rel_bias[:, a_cols:]
    x2 = x.reshape(batch * seq, d)
    tm = 512
    for i in range(depth):
        kind, j = i % 3, i // 3
        if kind == 0:
            mix, specs = _mixer_a(x2, norm_mix[i], a_w_in[j], a_q_gain[j], a_k_gain[j], bias_a, batch, seq, tm)
            w_out = a_w_out[j]
        elif kind == 1:
            mix, specs = _mixer_b(x2, norm_mix[i], b_w_in[j], b_q_gain[j], b_k_gain[j], bias_b, batch, seq, tm)
            w_out = b_w_out[j]
        else:
            mix, specs = _mixer_c(x2, norm_mix[i], c_w_in[j], c_conv_w[j], c_a_log[j], c_dt_bias[j],
                                  c_o_gain[j], batch, seq, tm)
            w_out = c_w_out[j]
        x2 = _proj_mlp(mix, specs, w_out.astype(BF16), x2, norm_mlp[i], mlp_w1[i].astype(BF16),
                       mlp_w2[i].astype(BF16), groups_a=kind == 0, tm=tm)
    return x2.reshape(batch, seq, d)
```

```python
import functools
import math

import numpy as np
import jax
import jax.numpy as jnp
from jax import lax
from jax.experimental import pallas as pl
from jax.experimental.pallas import tpu as pltpu

F32 = jnp.float32
BF16 = jnp.bfloat16
I32 = jnp.int32

EPS = 1e-6
HEAD_DIM = 64
NEG = -1e30
INT_MIN = -2 ** 31
LOG2E = math.log2(math.e)

V7X_VMEM_BYTES = 64 * 1024 * 1024
VMEM_LIMIT = V7X_VMEM_BYTES - 8 * 1024 * 1024
V7X_MXU_DIM = 256
V7X_SUBLANES = 8

NUM_BUCKETS = 32
MAX_DISTANCE = 2048

A_GROUPS = ((128, 1), (512, 4), (2048, 16))
A_HEADS = 8
A_BLOCK = 128
A_GW = A_HEADS * HEAD_DIM
A_MAX_STEP = 8

B_HEADS = 16
B_KV_HEADS = 4
B_IDX_HEADS = 8
B_IDX_DIM = 64
B_TOPK = 256
B_QT = 128
B_KC = 256
B_NDELTA = 14
B_SAFE_SPREAD = 100.0

C_HEADS = 8
C_DK = 128
C_CONV = 4
C_CHUNK = 64
C_GROUP = 4 * C_CHUNK


def _cparams(sem):
    return pltpu.CompilerParams(dimension_semantics=sem, vmem_limit_bytes=VMEM_LIMIT)


def _t5_bucket(dist):
    max_exact = NUM_BUCKETS // 2
    d = jnp.maximum(dist, 1).astype(F32)
    log_part = jnp.log(d / max_exact) / math.log(MAX_DISTANCE / max_exact) * (NUM_BUCKETS - max_exact)
    large = jnp.minimum(max_exact + log_part.astype(I32), NUM_BUCKETS - 1)
    return jnp.where(dist < max_exact, dist, large)


def _rms_rows(x, gain_row):
    ms = jnp.mean(x * x, axis=-1, keepdims=True)
    return x * lax.rsqrt(ms + EPS) * gain_row


def _in_proj_b_kernel(x_ref, g_ref, w_ref, hg_ref, bd_ref, qkv_ref, idx_ref, *, tn, n_norm):
    n_qkv = qkv_ref.shape[1] // tn
    hn = _rms_rows(x_ref[...], g_ref[...]).astype(BF16)
    for j in range(w_ref.shape[1] // tn):
        cols = slice(j * tn, (j + 1) * tn)
        y = jnp.dot(hn, w_ref[:, cols], preferred_element_type=F32)
        if j < n_norm:
            ms = jnp.dot((y * y).astype(BF16), bd_ref[...], preferred_element_type=F32) * (1.0 / HEAD_DIM)
            y = y * lax.rsqrt(ms + EPS) * hg_ref[:, cols]
        if j < n_qkv:
            qkv_ref[:, cols] = y.astype(BF16)
        else:
            idx_ref[:, (j - n_qkv) * tn:(j - n_qkv + 1) * tn] = y


def _mix_groups_a(tm, o0, o1, o2, l0, l1, l2, pt4_ref, pt16_ref):

    def token_order(ref, pt_ref):
        v = ref[...]
        if pt_ref is None:
            return v.reshape(tm, A_GW)
        pb = pt_ref.shape[0]
        per = pb // v.shape[0]
        pieces = []
        for h in range(tm // pb):
            hi, lo = _split_bf16(v[:, h * per:(h + 1) * per, :].reshape(pb, A_GW))
            pieces.append(jnp.dot(pt_ref[...], hi, preferred_element_type=F32)
                          + jnp.dot(pt_ref[...], lo, preferred_element_type=F32))
        return jnp.concatenate(pieces, axis=0)

    pts = (None, pt4_ref, pt16_ref)
    a, b, c = (token_order(r, pt) for r, pt in zip((l0, l1, l2), pts))
    m = jnp.maximum(jnp.maximum(a, b), c)
    ea, eb, ec = jnp.exp2(a - m), jnp.exp2(b - m), jnp.exp2(c - m)
    oa, ob, oc = (token_order(r, pt) for r, pt in zip((o0, o1, o2), pts))
    return (ea * oa + eb * ob + ec * oc) / (ea + eb + ec)


def _proj_mlp_kernel(*refs, tf, groups_a):
    wo_ref, x_ref, g_ref, w1_ref, w2_ref, o_ref = refs[-6:]
    y = _mix_groups_a(x_ref.shape[0], *refs[:-6]) if groups_a else refs[0][...]
    x = x_ref[...] + jnp.dot(y.astype(BF16), wo_ref[...], preferred_element_type=F32)
    hn = _rms_rows(x, g_ref[...]).astype(BF16)
    acc = x
    for f in range(w1_ref.shape[1] // tf):
        cols = slice(f * tf, (f + 1) * tf)
        h = jnp.maximum(jnp.dot(hn, w1_ref[:, cols], preferred_element_type=F32), 0.0)
        acc = acc + jnp.dot((h * h).astype(BF16), w2_ref[cols, :], preferred_element_type=F32)
    o_ref[...] = acc


def _proj_mlp(mix, mix_specs, w_out, x2, gain, w1, w2, *, groups_a, tm, tf=1024):
    n, d = x2.shape
    dff = w1.shape[1]
    const = lambda shape: pl.BlockSpec(shape, lambda i: (0, 0), pipeline_mode=pl.Buffered(1))
    return pl.pallas_call(
        functools.partial(_proj_mlp_kernel, tf=tf, groups_a=groups_a),
        out_shape=jax.ShapeDtypeStruct((n, d), F32),
        grid=(n // tm,),
        in_specs=list(mix_specs) + [const(w_out.shape), pl.BlockSpec((tm, d), lambda i: (i, 0)), const((1, d)),
                                    const((d, dff)), const((dff, d))],
        out_specs=pl.BlockSpec((tm, d), lambda i: (i, 0)),
        compiler_params=_cparams(("parallel",)),
        name="proj_mlp_a" if groups_a else "proj_mlp",
    )(*mix, w_out, x2, gain.reshape(1, d), w1, w2)


def _residue_major(tm, dilation):
    p = np.zeros((tm, tm), np.float32)
    j, r = np.meshgrid(np.arange(tm // dilation), np.arange(dilation), indexing="ij")
    p[(r * (tm // dilation) + j).ravel(), (j * dilation + r).ravel()] = 1.0
    return p


def _in_proj_a_kernel(x_ref, g_ref, w_ref, hg_ref, bd_ref, *rest):
    n_groups = len(A_GROUPS)
    perms, outs = rest[:n_groups - 1], rest[n_groups - 1:]
    tm = x_ref.shape[0]
    hn = _rms_rows(x_ref[...], g_ref[...]).astype(BF16)
    for which in range(3):
        for g, (_, dilation) in enumerate(A_GROUPS):
            j = which * n_groups + g
            cols = slice(j * A_GW, (j + 1) * A_GW)
            y = jnp.dot(hn, w_ref[:, cols], preferred_element_type=F32)
            if which < 2:
                nb = bd_ref.shape[0]
                y2 = (y * y).astype(BF16)
                ms = jnp.concatenate([jnp.dot(y2[:, c:c + nb], bd_ref[...], preferred_element_type=F32)
                                      for c in range(0, A_GW, nb)], axis=1) * (1.0 / HEAD_DIM)
                y = y * lax.rsqrt(ms + EPS) * hg_ref[:, cols]
            y = y.astype(BF16)
            if dilation == 1:
                y = y.reshape(1, tm, A_GW)
            else:
                perm = perms[g - 1]
                pb = perm.shape[0]
                y = jnp.concatenate(
                    [jnp.dot(perm[...], y[h:h + pb], preferred_element_type=F32).astype(BF16)
                     .reshape(dilation, pb // dilation, A_GW) for h in range(0, tm, pb)], axis=1)
            outs[g][:, :, which * A_GW:(which + 1) * A_GW] = y


def _attn_a_kernel(q_ref, kp_ref, kc_ref, vp_ref, vc_ref, bias_ref, o_ref, lse_ref):
    n = pl.program_id(2)
    blk = A_BLOCK
    pair = 2 * HEAD_DIM
    lane = lax.broadcasted_iota(I32, (2 * blk, pair), 1)
    row = lax.broadcasted_iota(I32, (2 * blk, blk), 0)
    pen = jnp.where((row < blk) & (n == 0), NEG, 0.0)
    nt = (((1,), (1,)), ((), ()))
    tn = (((0,), (0,)), ((), ()))
    n_res, rows = q_ref.shape[0], q_ref.shape[1]
    k_all = [jnp.concatenate([kp_ref[r], kc_ref[r]], axis=0) for r in range(n_res)]
    v_all = [jnp.concatenate([vp_ref[r], vc_ref[r]], axis=0) for r in range(n_res)]
    chains = [(r, sb, hp, a) for r in range(n_res) for sb in range(rows // blk)
              for hp in range(A_HEADS // 2) for a in range(2)]

    def window(x, sb, hp):
        return x[sb * blk:(sb + 2) * blk, hp * pair:(hp + 1) * pair]

    keep = [lane < HEAD_DIM, lane >= HEAD_DIM]
    ka = [jnp.where(keep[a], window(k_all[r], sb, hp), jnp.zeros((2 * blk, pair), BF16))
          for r, sb, hp, a in chains]
    s = [lax.dot_general(k_, q_ref[r, sb * blk:(sb + 1) * blk, hp * pair:(hp + 1) * pair], nt,
                         preferred_element_type=F32) + bias_ref[2 * hp + a]
         for k_, (r, sb, hp, a) in zip(ka, chains)]
    s = [s_ + pen if sb == 0 else s_ for s_, (_, sb, _, _) in zip(s, chains)]
    m = [jnp.max(s_, axis=0, keepdims=True) for s_ in s]
    p = [jnp.exp2(s_ - m_) for s_, m_ in zip(s, m)]
    l = [jnp.sum(p_, axis=0, keepdims=True) for p_ in p]
    pv = [lax.dot_general(window(v_all[r], sb, hp), p_.astype(BF16), tn, preferred_element_type=F32)
          for p_, (r, sb, hp, _) in zip(p, chains)]
    for idx in range(0, len(chains), 2):
        r, sb, hp, _ = chains[idx]
        o_t = jnp.concatenate([(pv[idx + a] / l[idx + a])[a * HEAD_DIM:(a + 1) * HEAD_DIM] for a in range(2)],
                              axis=0)
        l_t = jnp.concatenate([jnp.broadcast_to(m[idx + a] + jnp.log2(l[idx + a]), (HEAD_DIM, blk))
                               for a in range(2)], axis=0)
        o_ref[r, sb * blk:(sb + 1) * blk, hp * pair:(hp + 1) * pair] = o_t.T
        lse_ref[r, sb * blk:(sb + 1) * blk, hp * pair:(hp + 1) * pair] = l_t.T


def _attn_a_group(arr, bias_t, dilation, batch, seq):
    sub = seq // dilation
    step = min(A_MAX_STEP, sub // A_BLOCK)
    rows = step * A_BLOCK
    nb = sub // rows
    n_res = min(dilation, A_MAX_STEP // step)

    def spec(which, prev):
        if prev:
            return pl.BlockSpec((None, n_res, A_BLOCK, A_GW),
                                lambda b, r, n: (b, r, jnp.maximum(n * step - 1, 0), which))
        return pl.BlockSpec((None, n_res, rows, A_GW), lambda b, r, n: (b, r, n, which))

    out_spec = pl.BlockSpec((None, n_res, rows, A_GW), lambda b, r, n: (b, r, n, 0))
    out_sds = jax.ShapeDtypeStruct((batch, dilation, sub, A_GW), F32)
    return pl.pallas_call(
        _attn_a_kernel,
        out_shape=[out_sds, out_sds],
        grid=(batch, dilation // n_res, nb),
        in_specs=[spec(0, False), spec(1, True), spec(1, False), spec(2, True), spec(2, False),
                  pl.BlockSpec((A_HEADS, 2 * A_BLOCK, A_BLOCK), lambda b, r, n: (0, 0, 0))],
        out_specs=[out_spec, out_spec],
        compiler_params=_cparams(("parallel", "parallel", "arbitrary")),
        name=f"attn_a_d{dilation}",
    )(arr, arr, arr, arr, arr, bias_t)


def _bias_a(tab_g, dilation):
    step = np.arange(3 * A_BLOCK - 1) - (A_BLOCK - 1)
    vec = tab_g[_t5_bucket(jnp.asarray(np.maximum(step, 0) * dilation, I32))].astype(F32) * LOG2E
    vec = jnp.where(jnp.asarray((step >= 0) & (step <= A_BLOCK))[:, None], vec, NEG)
    return _toeplitz(vec.T, 2 * A_BLOCK, A_BLOCK)


def _mixer_a(x2, gain, w_in, q_gain, k_gain, bias_a, batch, seq, tm):
    n, d = x2.shape
    n_groups = len(A_GROUPS)
    reps = n_groups * A_HEADS
    hg = jnp.concatenate([jnp.tile(q_gain * (HEAD_DIM ** -0.5 * LOG2E), reps), jnp.tile(k_gain, reps),
                          jnp.ones((n_groups * A_GW,), F32)]).reshape(1, -1)
    kout = 3 * n_groups * A_GW
    pb = V7X_MXU_DIM
    bd = np.kron(np.eye(pb // HEAD_DIM), np.ones((HEAD_DIM, HEAD_DIM))).astype(np.float32)
    perms = [_residue_major(pb, dil) for _, dil in A_GROUPS[1:]]
    tiles = seq // tm
    const = lambda shape: pl.BlockSpec(shape, lambda i: (0, 0), pipeline_mode=pl.Buffered(1))
    grouped = lambda dil, width: pl.BlockSpec((None, dil, tm // dil, width),
                                              lambda i: (i // tiles, 0, i % tiles, 0))
    arrs = pl.pallas_call(
        _in_proj_a_kernel,
        out_shape=[jax.ShapeDtypeStruct((batch, dil, seq // dil, 3 * A_GW), BF16) for _, dil in A_GROUPS],
        grid=(n // tm,),
        in_specs=[pl.BlockSpec((tm, d), lambda i: (i, 0)), const((1, d)), const((d, kout)),
                  const((1, kout)), const((pb, pb))] + [const((pb, pb)) for _ in perms],
        out_specs=[grouped(dil, 3 * A_GW) for _, dil in A_GROUPS],
        compiler_params=_cparams(("parallel",)),
        name="in_proj_a",
    )(x2, gain.reshape(1, d), w_in.astype(BF16), hg, jnp.asarray(bd, BF16),
      *[jnp.asarray(p, BF16) for p in perms])

    os_, ls_ = [], []
    for g, (window, dilation) in enumerate(A_GROUPS):
        assert window // dilation == A_BLOCK and (seq // dilation) % (min(A_MAX_STEP, seq // dilation // A_BLOCK) * A_BLOCK) == 0 and dilation <= pb // 16
        tab_g = bias_a[:, g * A_HEADS:(g + 1) * A_HEADS]
        o, lse = _attn_a_group(arrs[g], _bias_a(tab_g, dilation), dilation, batch, seq)
        os_.append(o)
        ls_.append(lse)

    mix = [*os_, *ls_, *[jnp.asarray(p.T, BF16) for p in perms]]
    specs = [grouped(dil, A_GW) for _, dil in A_GROUPS] * 2 + [const((pb, pb)) for _ in perms]
    return mix, specs


def _float_key(s):
    i = lax.bitcast_convert_type(s, I32)
    k = jnp.where(i < 0, i ^ jnp.int32(0x7FFFFFFF), i)
    return jnp.where(s == 0.0, 0, k)


def _split_bf16(x):
    hi = x.astype(BF16)
    return hi, (x - hi.astype(F32)).astype(BF16)


def _attn_b_kernel(q_ref, k_ref, vt_ref, qi_ref, kw_ref, bias_ref, shift_ref, o_ref, key_ref, *, bounded):
    qb = pl.program_id(1)
    kc, qt = B_KC, B_QT
    nch = qb // 2 + 1
    t_q = qb * qt + lax.broadcasted_iota(I32, (kc, qt), 1)
    sub = lax.broadcasted_iota(I32, (kc, qt), 0)
    nt = (((1,), (1,)), ((), ()))

    qh, ql = _split_bf16(qi_ref[...])
    qi3 = jnp.concatenate(
        [jnp.concatenate([x[:, h * B_IDX_DIM:(h + 1) * B_IDX_DIM] for x in (qh, qh, ql)], axis=1)
         for h in range(B_IDX_HEADS)], axis=0)
    wt = kw_ref[pl.ds(pl.multiple_of(qb * qt, qt), qt), 0:qt].T
    wt = wt * (B_IDX_HEADS ** -0.5 * B_IDX_DIM ** -0.5)

    npair = (nch + 1) // 2

    def score_pair(c2, carry):
        offs = [pl.multiple_of((2 * c2 + a) * kc, kc) for a in range(2)]
        ks = [_split_bf16(kw_ref[pl.ds(off, kc), 0:B_IDX_DIM]) for off in offs]
        sc = [lax.dot_general(jnp.concatenate([kh, kl, kh], axis=1), qi3, nt, preferred_element_type=F32)
              for kh, kl in ks]
        for off, s in zip(offs, sc):
            acc = jnp.zeros((kc, qt), F32)
            for h in range(B_IDX_HEADS):
                acc = acc + wt[B_IDX_DIM + h:B_IDX_DIM + h + 1, :] * jnp.maximum(s[:, h * qt:(h + 1) * qt], 0.0)
            key_ref[pl.ds(off, kc), :] = jnp.where(off + sub <= t_q, _float_key(acc), INT_MIN)
        return carry

    lax.fori_loop(0, npair, score_pair, 0)

    def count(pred_fn, pairs=None):
        def body(c2, acc):
            for a in range(2):
                off = (2 * c2 + a) * kc
                off = off if pairs else pl.multiple_of(off, kc)
                hit = jnp.where(pred_fn(key_ref[pl.ds(off, kc), :], off + sub), 1, 0)
                acc = acc + jnp.sum(hit.reshape(kc // 8, 8, qt), axis=0)
            return acc
        acc = jnp.zeros((8, qt), I32)
        if pairs:
            for c2 in range(pairs):
                acc = body(c2, acc)
        else:
            acc = lax.fori_loop(0, npair, body, acc)
        return jnp.sum(acc, axis=0, keepdims=True)

    def search(pairs):
        def thr_bit(it, lo):
            cand = lo + jnp.left_shift(jnp.int32(1), 31 - it)
            cnt = count(lambda key, _: key >= cand, pairs)
            return jnp.where(cnt >= B_TOPK, cand, lo)
        return lax.fori_loop(0, 32, thr_bit, jnp.full((1, qt), INT_MIN, I32))

    max_pairs = key_ref.shape[0] // (2 * kc)
    thr = lax.switch(npair - 1, [functools.partial(search, n) for n in range(1, max_pairs + 1)])
    need = B_TOPK - count(lambda key, _: key > thr)
    n_eq = count(lambda key, _: key == thr)

    def tie_search(_):
        def bit(it, j):
            cand = j + jnp.left_shift(jnp.int32(1), 12 - it)
            cnt = count(lambda key, idx: (key == thr) & (idx < cand))
            return jnp.where(cnt <= need, cand, j)
        return lax.fori_loop(0, 13, bit, jnp.zeros((1, qt), I32))

    any_tie = jnp.max(jnp.where(n_eq > need, 1, 0)) > 0
    j_max = lax.cond(any_tie, tie_search, lambda _: jnp.full((1, qt), 2 ** 13, I32), 0)

    grp = B_HEADS // B_KV_HEADS
    q = q_ref[...]
    groups = range(B_KV_HEADS)
    qg = [jnp.concatenate([q[:, (g * grp + j) * HEAD_DIM:(g * grp + j + 1) * HEAD_DIM]
                           for j in range(grp)], axis=0) for g in groups]
    ksl = [slice(g * HEAD_DIM, (g + 1) * HEAD_DIM) for g in groups]

    fk = 2 * kc

    def flash(c2, carry):
        m, l, acc = carry
        off = pl.multiple_of(c2 * fk, fk)
        tiles = [jnp.clip(qb - (fk // qt) * c2 - a, 0, B_NDELTA - 1) for a in range(fk // qt)]
        key = key_ref[pl.ds(off, fk), :]
        idx = off + lax.broadcasted_iota(I32, (fk, qt), 0)
        sel = ((key > thr) | ((key == thr) & (idx < j_max))) & (idx <= t_q[:1])
        mk = jnp.where(sel, -shift_ref[...], NEG)
        mk = jnp.concatenate([mk] * grp, axis=1)
        s = [lax.dot_general(k_ref[pl.ds(off, fk), ksl[g]], qg[g], nt, preferred_element_type=F32)
             for g in groups]
        s = [s[g] + jnp.concatenate([bias_ref[d, g] for d in tiles], axis=0) + mk for g in groups]
        if bounded:
            p = [jnp.exp2(s[g]) for g in groups]
            l = [l[g] + jnp.sum(p[g], axis=0, keepdims=True) for g in groups]
            p = [p[g].astype(BF16) for g in groups]
            pv = [jnp.dot(vt_ref[ksl[g], pl.ds(off, fk)], p[g], preferred_element_type=F32) for g in groups]
            return m, l, [acc[g] + pv[g] for g in groups]
        else:
            m_new = [jnp.maximum(m[g], jnp.max(s[g], axis=0, keepdims=True)) for g in groups]
            p = [jnp.exp2(s[g] - m_new[g]) for g in groups]
        pv = [jnp.dot(vt_ref[ksl[g], pl.ds(off, fk)], p[g].astype(BF16), preferred_element_type=F32)
              for g in groups]
        alpha = [jnp.exp2(m[g] - m_new[g]) for g in groups]
        l = [alpha[g] * l[g] + jnp.sum(p[g], axis=0, keepdims=True) for g in groups]
        acc = [alpha[g] * acc[g] + pv[g] for g in groups]
        return m_new, l, acc

    m0 = [jnp.full((1, grp * qt), NEG, F32) for _ in groups]
    l0 = [jnp.zeros((1, grp * qt), F32) for _ in groups]
    a0 = [jnp.zeros((HEAD_DIM, grp * qt), F32) for _ in groups]
    _, l, acc = lax.fori_loop(0, npair, flash, (m0, l0, a0))
    for g in groups:
        o = acc[g] / l[g]
        for j in range(0, grp, 2):
            h = g * grp + j
            pair = jnp.concatenate([o[:, j * qt:(j + 1) * qt], o[:, (j + 1) * qt:(j + 2) * qt]], axis=0)
            o_ref[:, h * HEAD_DIM:(h + 2) * HEAD_DIM] = pair.T.astype(o_ref.dtype)


def _toeplitz(vec, n_rows, n_cols):
    span = n_rows + n_cols - 1
    assert vec.shape[-1] == span
    lead = vec.shape[:-1]
    padded = jnp.concatenate([vec, jnp.zeros(lead + (1,), vec.dtype)], axis=-1)
    flat = jnp.tile(padded, n_rows)[..., :n_rows * span]
    return flat.reshape(lead + (n_rows, span))[..., n_rows - 1:]


def _bias_b(tab):
    grp = B_HEADS // B_KV_HEADS
    span = 2 * B_QT - 1
    dist = np.arange(B_NDELTA)[:, None] * B_QT + np.arange(span)[None, :] - (B_QT - 1)
    assert (B_NDELTA - 1) * B_QT - (B_QT - 1) >= 16 * 128 ** (15.0 / 16.0) + 1
    vec = tab[_t5_bucket(jnp.asarray(np.maximum(dist, 0), I32))].astype(F32) * LOG2E
    t = _toeplitz(vec.transpose(0, 2, 1), B_QT, B_QT)
    t = t.reshape(B_NDELTA, B_KV_HEADS, grp, B_QT, B_QT).transpose(0, 1, 3, 2, 4)
    return t.reshape(B_NDELTA, B_KV_HEADS, B_QT, grp * B_QT)


def _mixer_b(x2, gain, w_in, q_gain, k_gain, bias_b, batch, seq, tm):
    d = x2.shape[1]
    nq, nkv = B_HEADS * HEAD_DIM, B_KV_HEADS * HEAD_DIM
    nidx = B_IDX_HEADS * B_IDX_DIM
    tn = V7X_MXU_DIM
    used = 2 * nkv + nq + nidx + B_IDX_DIM + B_IDX_HEADS
    kout = -(-used // tn) * tn
    w = jnp.concatenate([w_in, jnp.zeros((d, kout - used), F32)], axis=1).astype(BF16)
    hg = jnp.concatenate([jnp.tile(q_gain * (HEAD_DIM ** -0.5 * LOG2E), B_HEADS), jnp.tile(k_gain, B_KV_HEADS),
                          jnp.ones((kout - nq - nkv,), F32)]).reshape(1, -1)
    n = batch * seq
    nrest = kout - nq - 2 * nkv
    bd = np.kron(np.eye(tn // HEAD_DIM), np.ones((HEAD_DIM, HEAD_DIM))).astype(np.float32)
    const = lambda shape: pl.BlockSpec(shape, lambda i: (0, 0), pipeline_mode=pl.Buffered(1))
    qk, idx = pl.pallas_call(
        functools.partial(_in_proj_b_kernel, tn=tn, n_norm=(nq + nkv) // tn),
        out_shape=[jax.ShapeDtypeStruct((n, nq + 2 * nkv), BF16), jax.ShapeDtypeStruct((n, nrest), F32)],
        grid=(n // tm,),
        in_specs=[pl.BlockSpec((tm, d), lambda i: (i, 0)), const((1, d)), const((d, kout)),
                  const((1, kout)), const((tn, tn))],
        out_specs=[pl.BlockSpec((tm, nq + 2 * nkv), lambda i: (i, 0)),
                   pl.BlockSpec((tm, nrest), lambda i: (i, 0))],
        compiler_params=_cparams(("parallel",)),
        name="in_proj_b",
    )(x2, gain.reshape(1, d), w, hg, jnp.asarray(bd, BF16))
    qk = qk.reshape(batch, seq, nq + 2 * nkv)
    idx = idx.reshape(batch, seq, nrest)
    vt = qk[:, :, nq + nkv:].transpose(0, 2, 1)
    assert seq % (2 * B_QT) == 0 and seq <= 2 ** 12 and nkv == tn and nrest == nidx + tn

    def attend(shift, bounded):
        return pl.pallas_call(
            functools.partial(_attn_b_kernel, bounded=bounded),
            out_shape=jax.ShapeDtypeStruct((batch, seq, nq), BF16),
            grid=(batch, seq // B_QT),
            in_specs=[
                pl.BlockSpec((None, B_QT, nq), lambda b, i: (b, i, 0)),
                pl.BlockSpec((None, seq, nkv), lambda b, i: (b, 0, nq // nkv)),
                pl.BlockSpec((None, nkv, seq), lambda b, i: (b, 0, 0)),
                pl.BlockSpec((None, B_QT, nidx), lambda b, i: (b, i, 0)),
                pl.BlockSpec((None, seq, tn), lambda b, i: (b, 0, nidx // tn)),
                pl.BlockSpec((B_NDELTA, B_KV_HEADS, B_QT, (B_HEADS // B_KV_HEADS) * B_QT),
                             lambda b, i: (0, 0, 0, 0), pipeline_mode=pl.Buffered(1)),
                pl.BlockSpec((1, B_QT), lambda b, i: (0, 0)),
            ],
            out_specs=pl.BlockSpec((None, B_QT, nq), lambda b, i: (b, i, 0)),
            scratch_shapes=[pltpu.VMEM((seq, B_QT), I32)],
            compiler_params=_cparams(("parallel", "arbitrary")),
            name="attn_b_bounded" if bounded else "attn_b",
        )(qk, qk, vt, idx, idx, bias_t, shift)

    bias_t = _bias_b(bias_b)
    qk_bound = 1.02 * HEAD_DIM * (HEAD_DIM ** -0.5 * LOG2E) * jnp.max(jnp.abs(q_gain)) * jnp.max(jnp.abs(k_gain))
    bound = qk_bound + jnp.max(bias_b) * LOG2E
    spread = bound + qk_bound - jnp.min(bias_b) * LOG2E
    y = lax.cond(spread <= B_SAFE_SPREAD,
                 lambda: attend(jnp.full((1, B_QT), bound, F32), True),
                 lambda: attend(jnp.zeros((1, B_QT), F32), False))
    return [y.reshape(batch * seq, nq)], [pl.BlockSpec((tm, nq), lambda i: (i, 0))]


def _in_proj_c_kernel(x_ref, g_ref, w_ref, cw_ref, qkv_ref, rest_ref, tail_ref, xs_ref, *, tn):
    i = pl.program_id(1)
    tm = x_ref.shape[0]
    hw = C_HEADS * C_DK
    hn = _rms_rows(x_ref[...], g_ref[...]).astype(BF16)
    for j in range(w_ref.shape[1] // tn):
        cols = slice(j * tn, (j + 1) * tn)
        y = jnp.dot(hn, w_ref[:, cols], preferred_element_type=F32)
        if j * tn >= 3 * hw:
            rest_ref[:, j * tn - 3 * hw:(j + 1) * tn - 3 * hw] = y
            continue
        hist = tail_ref.shape[0]
        xs_ref[0:hist, cols] = jnp.where(i > 0, tail_ref[:, cols], 0.0)
        xs_ref[hist:, cols] = y
        tail_ref[:, cols] = y[tm - hist:, :]
        cw = cw_ref[:, cols]
        conv = sum(cw[t:t + 1, :] * xs_ref[hist - (C_CONV - 1) + t:hist - (C_CONV - 1) + t + tm, cols]
                   for t in range(C_CONV))
        z = conv * jax.nn.sigmoid(conv)
        if j * tn >= 2 * hw:
            qkv_ref[:, cols] = z
            continue
        scale = C_DK ** -0.5 if j * tn < hw else 1.0
        for h in range(tn // C_DK):
            zh = z[:, h * C_DK:(h + 1) * C_DK]
            ss = jnp.sum(zh * zh, axis=-1, keepdims=True)
            qkv_ref[:, j * tn + h * C_DK:j * tn + (h + 1) * C_DK] = zh * (lax.rsqrt(ss + EPS) * scale)


def _delta_kernel(q_ref, k_ref, v_ref, gate_ref, a_ref, b_ref, alog_ref, dtb_ref, og_ref, y_ref,
                  rows_s, p_s, r_s, m_s, n_s, el_s, o_s):
    cs, gs = C_CHUNK, C_GROUP
    per = gs // cs
    seq, dv = v_ref.shape
    n_groups = seq // gs
    hi = lax.Precision.HIGHEST
    r = lax.broadcasted_iota(I32, (gs, gs), 0)
    c_ = lax.broadcasted_iota(I32, (gs, gs), 1)

    def same_block(size):
        sh = int(math.log2(size))
        return (r >> sh) == (c_ >> sh)

    chunk = same_block(cs)
    lower, strict, eye = chunk & (r >= c_), chunk & (r > c_), r == c_

    z = a_ref[...] + dtb_ref[...]
    softplus = jnp.maximum(z, 0.0) + jnp.log(1.0 + jnp.exp(-jnp.abs(z)))
    g_all = -jnp.exp(alog_ref[...]) * softplus
    gc_all = jnp.dot(g_all, jnp.where(chunk & (r <= c_), 1.0, 0.0), preferred_element_type=F32, precision=hi)
    gl_all = jnp.dot(g_all, jnp.where(chunk, 1.0, 0.0), preferred_element_type=F32, precision=hi)
    rows_s[0] = jax.nn.sigmoid(b_ref[...])
    rows_s[1] = gc_all
    rows_s[2] = jnp.exp(gc_all)
    rows_s[3] = jnp.exp(gl_all - gc_all)
    first = (lax.broadcasted_iota(I32, (gs, per * dv), 0)
             == (lax.broadcasted_iota(I32, (gs, per * dv), 1) // dv) * cs)
    el_s[...] = jnp.exp(jnp.dot(gl_all, jnp.where(first, 1.0, 0.0), preferred_element_type=F32, precision=hi))

    def mm(a, b):
        return jnp.dot(a.astype(BF16), b.astype(BF16), preferred_element_type=F32)

    def mm_nt(a, b):
        return lax.dot_general(a.astype(BF16), b.astype(BF16), (((1,), (1,)), ((), ())),
                               preferred_element_type=F32)

    ways = 4

    def prep(m):
        ids = [m * ways + a for a in range(ways)]
        rows = [pl.ds(pl.multiple_of(i * gs, gs), gs) for i in ids]
        rw = [jnp.concatenate([rows_s[j, pl.ds(i, 1), :] for j in range(4)] + [jnp.zeros((4, gs), F32)], axis=0)
              for i in ids]
        cl = [x.T for x in rw]
        beta_col, gc_col, eg_col, ekg_col = ([x[:, j:j + 1] for x in cl] for j in range(4))
        q, k, v = ([ref[rw_, :] for rw_ in rows] for ref in (q_ref, k_ref, v_ref))
        decay = [jnp.where(lower, jnp.exp(jnp.where(lower, gcc - x[1:2, :], 0.0)), 0.0)
                 for gcc, x in zip(gc_col, rw)]
        kb = [k_ * b_ for k_, b_ in zip(k, beta_col)]
        yield
        a_mat = [jnp.where(strict, mm_nt(kb_, k_) * d_, 0.0) for kb_, k_, d_ in zip(kb, k, decay)]
        yield
        a8 = [jnp.where(same_block(8), a_, 0.0) for a_ in a_mat]
        t = [jnp.where(eye, 1.0, 0.0) - a_ for a_ in a8]
        pw = [mm(a_, a_) for a_ in a8]
        yield
        t = [t_ + mm(t_, p_) for t_, p_ in zip(t, pw)]
        yield
        pw = [mm(p_, p_) for p_ in pw]
        yield
        t = [t_ + mm(t_, p_) for t_, p_ in zip(t, pw)]
        yield
        size = 8
        while size < cs:
            sel = same_block(2 * size) & jnp.logical_not(same_block(size))
            nt = [mm(jnp.where(sel, a_, 0.0), t_) for a_, t_ in zip(a_mat, t)]
            yield
            t = [t_ - mm(t_, n_) for t_, n_ in zip(t, nt)]
            yield
            size *= 2
        uw = [mm(t_, jnp.concatenate([v_ * b_, kb_ * e_], axis=1))
              for t_, v_, b_, kb_, e_ in zip(t, v, beta_col, kb, eg_col)]
        yield
        qk = [jnp.where(lower, mm_nt(q_, k_) * d_, 0.0) for q_, k_, d_ in zip(q, k, decay)]
        yield
        qkwu = [mm(qk_, uw_) for qk_, uw_ in zip(qk, uw)]
        kgt = [(k_ * e_).T for k_, e_ in zip(k, ekg_col)]
        yield
        nm = [[mm(kgt[a][:, j * cs:(j + 1) * cs], uw[a][j * cs:(j + 1) * cs, :]) for j in range(per)]
              for a in range(ways)]

        def write():
            for a, i in enumerate(ids):
                r_s[rows[a], :] = qkwu[a][:, :dv]
                p_s[rows[a], :] = (q[a] * eg_col[a] - qkwu[a][:, dv:]).astype(BF16)
                for j in range(per):
                    n_s[i * per + j] = nm[a][j][:, :dv]
                    m_s[i * per + j] = nm[a][j][:, dv:].astype(BF16)
        return write

    def scan(m, state):
        for a in range(ways):
            i = m * ways + a
            el = el_s[pl.ds(i, 1), :]
            for j in range(per):
                rows = pl.ds(pl.multiple_of(i * gs + j * cs, cs), cs)
                c = i * per + j
                sb = state[0].astype(BF16)
                o_s[rows, :] = jnp.dot(p_s[rows, :], sb, preferred_element_type=F32) + r_s[rows, :]
                state[0] = (state[0] * el[:, j * dv:(j + 1) * dv] + n_s[c]
                            - jnp.dot(m_s[c], sb, preferred_element_type=F32))
                yield

    def weave(*gens):
        gens, results = list(gens), [None] * len(gens)
        live = list(range(len(gens)))
        while live:
            for idx in list(live):
                try:
                    next(gens[idx])
                except StopIteration as stop:
                    results[idx] = stop.value
                    live.remove(idx)
        return results

    n_blocks = n_groups // ways
    weave(prep(0))[0]()

    def step(m, state):
        holder = [state]
        _, write = weave(scan(m - 1, holder), prep(m))
        write()
        return holder[0]

    state = lax.fori_loop(1, n_blocks, step, jnp.zeros((C_DK, dv), F32))

    def finish(first, last):
        for t in range(first, last):
            rows = pl.ds(t * gs, gs)
            gate = gate_ref[rows, :]
            y_ref[rows, :] = (_rms_rows(o_s[rows, :], og_ref[...])
                              * (gate * jax.nn.sigmoid(gate))).astype(y_ref.dtype)
            yield

    done = (n_blocks - 1) * ways
    weave(scan(n_blocks - 1, [state]), finish(0, done))
    weave(finish(done, n_groups))


def _mixer_c(x2, gain, w_in, conv_w, a_log, dt_bias, o_gain, batch, seq, tm):
    d = x2.shape[1]
    hw = C_HEADS * C_DK
    tn = V7X_MXU_DIM
    used = 4 * hw + 2 * C_HEADS
    kout = -(-used // tn) * tn
    w = jnp.concatenate([w_in, jnp.zeros((d, kout - used), F32)], axis=1).astype(BF16)
    const = lambda shape: pl.BlockSpec(shape, lambda b, i: (0, 0), pipeline_mode=pl.Buffered(1))
    qkv, rest = pl.pallas_call(
        functools.partial(_in_proj_c_kernel, tn=tn),
        out_shape=[jax.ShapeDtypeStruct((batch, seq, 3 * hw), F32),
                   jax.ShapeDtypeStruct((batch, seq, kout - 3 * hw), F32)],
        grid=(batch, seq // tm),
        in_specs=[pl.BlockSpec((None, tm, d), lambda b, i: (b, i, 0)), const((1, d)), const((d, kout)),
                  const((C_CONV, 3 * hw))],
        out_specs=[pl.BlockSpec((None, tm, 3 * hw), lambda b, i: (b, i, 0)),
                   pl.BlockSpec((None, tm, kout - 3 * hw), lambda b, i: (b, i, 0))],
        scratch_shapes=[pltpu.VMEM((V7X_SUBLANES, 3 * hw), F32), pltpu.VMEM((tm + V7X_SUBLANES, 3 * hw), F32)],
        compiler_params=_cparams(("parallel", "arbitrary")),
        name="in_proj_c",
    )(x2.reshape(batch, seq, d), gain.reshape(1, d), w, conv_w)

    n_groups = seq // C_GROUP
    ba = rest[:, :, hw:hw + 2 * C_HEADS].transpose(0, 2, 1).reshape(batch, 2 * C_HEADS, n_groups, C_GROUP)
    per_head = lambda v: jnp.broadcast_to(v.reshape(C_HEADS, 1, 1), (C_HEADS, 1, C_GROUP))
    head_cols = lambda off: pl.BlockSpec((None, seq, C_DK), lambda b, h: (b, 0, off + h))
    small = lambda off: pl.BlockSpec((None, None, n_groups, C_GROUP), lambda b, h: (b, off + h, 0, 0))
    scalar_row = pl.BlockSpec((None, 1, C_GROUP), lambda b, h: (h, 0, 0))
    y = pl.pallas_call(
        _delta_kernel,
        out_shape=jax.ShapeDtypeStruct((batch, seq, hw), BF16),
        grid=(batch, C_HEADS),
        in_specs=[head_cols(0), head_cols(C_HEADS), head_cols(2 * C_HEADS),
                  head_cols(0),
                  small(C_HEADS), small(0), scalar_row, scalar_row,
                  pl.BlockSpec((1, C_DK), lambda b, h: (0, 0))],
        out_specs=head_cols(0),
        scratch_shapes=[
            pltpu.VMEM((4, n_groups, C_GROUP), F32),
            pltpu.VMEM((seq, C_DK), BF16),
            pltpu.VMEM((seq, C_DK), F32),
            pltpu.VMEM((seq // C_CHUNK, C_DK, C_DK), BF16),
            pltpu.VMEM((seq // C_CHUNK, C_DK, C_DK), F32),
            pltpu.VMEM((n_groups, (C_GROUP // C_CHUNK) * C_DK), F32),
            pltpu.VMEM((seq, C_DK), F32),
        ],
        compiler_params=_cparams(("parallel", "parallel")),
        name="delta_c",
    )(qkv, qkv, qkv, rest, ba, ba, per_head(a_log), per_head(dt_bias), o_gain.reshape(1, C_DK))
    return [y.reshape(batch * seq, hw)], [pl.BlockSpec((tm, hw), lambda i: (i, 0))]


def kernel(x, rel_bias, norm_mix, norm_mlp, mlp_w1, mlp_w2, a_w_in, a_q_gain, a_k_gain, a_w_out,
           b_w_in, b_q_gain, b_k_gain, b_w_out, c_w_in, c_conv_w, c_a_log, c_dt_bias, c_o_gain, c_w_out):
    batch, seq, d = x.shape
    depth = norm_mix.shape[0]
    a_cols = len(A_GROUPS) * A_HEADS
    bias_a, bias_b = rel_bias[:, :a_cols], rel_bias[:, a_cols:]
    x2 = x.reshape(batch * seq, d)
    tm = 512
    for i in range(depth):
        kind, j = i % 3, i // 3
        if kind == 0:
            mix, specs = _mixer_a(x2, norm_mix[i], a_w_in[j], a_q_gain[j], a_k_gain[j], bias_a, batch, seq, tm)
            w_out = a_w_out[j]
        elif kind == 1:
            mix, specs = _mixer_b(x2, norm_mix[i], b_w_in[j], b_q_gain[j], b_k_gain[j], bias_b, batch, seq, tm)
            w_out = b_w_out[j]
        else:
            mix, specs = _mixer_c(x2, norm_mix[i], c_w_in[j], c_conv_w[j], c_a_log[j], c_dt_bias[j],
                                  c_o_gain[j], batch, seq, tm)
            w_out = c_w_out[j]
        x2 = _proj_mlp(mix, specs, w_out.astype(BF16), x2, norm_mlp[i], mlp_w1[i].astype(BF16),
                       mlp_w2[i].astype(BF16), groups_a=kind == 0, tm=tm)
    return x2.reshape(batch, seq, d)
```

```python
import functools
import math

import numpy as np
import jax
import jax.numpy as jnp
from jax import lax
from jax.experimental import pallas as pl
from jax.experimental.pallas import tpu as pltpu

F32 = jnp.float32
BF16 = jnp.bfloat16
I32 = jnp.int32

EPS = 1e-6
HEAD_DIM = 64
NEG = -1e30
INT_MIN = -2 ** 31
LOG2E = math.log2(math.e)

V7X_VMEM_BYTES = 64 * 1024 * 1024
VMEM_LIMIT = V7X_VMEM_BYTES - 8 * 1024 * 1024
V7X_MXU_DIM = 256
V7X_SUBLANES = 8

NUM_BUCKETS = 32
MAX_DISTANCE = 2048

A_GROUPS = ((128, 1), (512, 4), (2048, 16))
A_HEADS = 8
A_BLOCK = 128
A_GW = A_HEADS * HEAD_DIM
A_MAX_STEP = 8

B_HEADS = 16
B_KV_HEADS = 4
B_IDX_HEADS = 8
B_IDX_DIM = 64
B_TOPK = 256
B_QT = 128
B_KC = 256
B_NDELTA = 14
B_SAFE_SPREAD = 100.0

C_HEADS = 8
C_DK = 128
C_CONV = 4
C_CHUNK = 64
C_GROUP = 4 * C_CHUNK


def _cparams(sem):
    return pltpu.CompilerParams(dimension_semantics=sem, vmem_limit_bytes=VMEM_LIMIT)


def _t5_bucket(dist):
    max_exact = NUM_BUCKETS // 2
    d = jnp.maximum(dist, 1).astype(F32)
    log_part = jnp.log(d / max_exact) / math.log(MAX_DISTANCE / max_exact) * (NUM_BUCKETS - max_exact)
    large = jnp.minimum(max_exact + log_part.astype(I32), NUM_BUCKETS - 1)
    return jnp.where(dist < max_exact, dist, large)


def _rms_rows(x, gain_row):
    ms = jnp.mean(x * x, axis=-1, keepdims=True)
    return x * lax.rsqrt(ms + EPS) * gain_row


def _in_proj_b_kernel(x_ref, g_ref, w_ref, hg_ref, bd_ref, qkv_ref, idx_ref, *, tn, n_norm):
    n_qkv = qkv_ref.shape[1] // tn
    hn = _rms_rows(x_ref[...], g_ref[...]).astype(BF16)
    for j in range(w_ref.shape[1] // tn):
        cols = slice(j * tn, (j + 1) * tn)
        y = jnp.dot(hn, w_ref[:, cols], preferred_element_type=F32)
        if j < n_norm:
            ms = jnp.dot((y * y).astype(BF16), bd_ref[...], preferred_element_type=F32) * (1.0 / HEAD_DIM)
            y = y * lax.rsqrt(ms + EPS) * hg_ref[:, cols]
        if j < n_qkv:
            qkv_ref[:, cols] = y.astype(BF16)
        else:
            idx_ref[:, (j - n_qkv) * tn:(j - n_qkv + 1) * tn] = y


def _mix_groups_a(tm, o0, o1, o2, l0, l1, l2, pt4_ref, pt16_ref):

    def token_order(ref, pt_ref):
        v = ref[...]
        if pt_ref is None:
            return v.reshape(tm, A_GW)
        pb = pt_ref.shape[0]
        per = pb // v.shape[0]
        pieces = []
        for h in range(tm // pb):
            hi, lo = _split_bf16(v[:, h * per:(h + 1) * per, :].reshape(pb, A_GW))
            pieces.append(jnp.dot(pt_ref[...], hi, preferred_element_type=F32)
                          + jnp.dot(pt_ref[...], lo, preferred_element_type=F32))
        return jnp.concatenate(pieces, axis=0)

    pts = (None, pt4_ref, pt16_ref)
    a, b, c = (token_order(r, pt) for r, pt in zip((l0, l1, l2), pts))
    m = jnp.maximum(jnp.maximum(a, b), c)
    ea, eb, ec = jnp.exp2(a - m), jnp.exp2(b - m), jnp.exp2(c - m)
    oa, ob, oc = (token_order(r, pt) for r, pt in zip((o0, o1, o2), pts))
    return (ea * oa + eb * ob + ec * oc) / (ea + eb + ec)


def _proj_mlp_kernel(*refs, tf, groups_a):
    wo_ref, x_ref, g_ref, w1_ref, w2_ref, o_ref = refs[-6:]
    y = _mix_groups_a(x_ref.shape[0], *refs[:-6]) if groups_a else refs[0][...]
    x = x_ref[...] + jnp.dot(y.astype(BF16), wo_ref[...], preferred_element_type=F32)
    hn = _rms_rows(x, g_ref[...]).astype(BF16)
    acc = x
    for f in range(w1_ref.shape[1] // tf):
        cols = slice(f * tf, (f + 1) * tf)
        h = jnp.maximum(jnp.dot(hn, w1_ref[:, cols], preferred_element_type=F32), 0.0)
        acc = acc + jnp.dot((h * h).astype(BF16), w2_ref[cols, :], preferred_element_type=F32)
    o_ref[...] = acc


def _proj_mlp(mix, mix_specs, w_out, x2, gain, w1, w2, *, groups_a, tm, tf=1024):
    n, d = x2.shape
    dff = w1.shape[1]
    const = lambda shape: pl.BlockSpec(shape, lambda i: (0, 0), pipeline_mode=pl.Buffered(1))
    return pl.pallas_call(
        functools.partial(_proj_mlp_kernel, tf=tf, groups_a=groups_a),
        out_shape=jax.ShapeDtypeStruct((n, d), F32),
        grid=(n // tm,),
        in_specs=list(mix_specs) + [const(w_out.shape), pl.BlockSpec((tm, d), lambda i: (i, 0)), const((1, d)),
                                    const((d, dff)), const((dff, d))],
        out_specs=pl.BlockSpec((tm, d), lambda i: (i, 0)),
        compiler_params=_cparams(("parallel",)),
        name="proj_mlp_a" if groups_a else "proj_mlp",
    )(*mix, w_out, x2, gain.reshape(1, d), w1, w2)


def _residue_major(tm, dilation):
    p = np.zeros((tm, tm), np.float32)
    j, r = np.meshgrid(np.arange(tm // dilation), np.arange(dilation), indexing="ij")
    p[(r * (tm // dilation) + j).ravel(), (j * dilation + r).ravel()] = 1.0
    return p


def _in_proj_a_kernel(x_ref, g_ref, w_ref, hg_ref, bd_ref, *rest):
    n_groups = len(A_GROUPS)
    perms, outs = rest[:n_groups - 1], rest[n_groups - 1:]
    tm = x_ref.shape[0]
    hn = _rms_rows(x_ref[...], g_ref[...]).astype(BF16)
    for which in range(3):
        for g, (_, dilation) in enumerate(A_GROUPS):
            j = which * n_groups + g
            cols = slice(j * A_GW, (j + 1) * A_GW)
            y = jnp.dot(hn, w_ref[:, cols], preferred_element_type=F32)
            if which < 2:
                nb = bd_ref.shape[0]
                y2 = (y * y).astype(BF16)
                ms = jnp.concatenate([jnp.dot(y2[:, c:c + nb], bd_ref[...], preferred_element_type=F32)
                                      for c in range(0, A_GW, nb)], axis=1) * (1.0 / HEAD_DIM)
                y = y * lax.rsqrt(ms + EPS) * hg_ref[:, cols]
            y = y.astype(BF16)
            if dilation == 1:
                y = y.reshape(1, tm, A_GW)
            else:
                perm = perms[g - 1]
                pb = perm.shape[0]
                y = jnp.concatenate(
                    [jnp.dot(perm[...], y[h:h + pb], preferred_element_type=F32).astype(BF16)
                     .reshape(dilation, pb // dilation, A_GW) for h in range(0, tm, pb)], axis=1)
            outs[g][:, :, which * A_GW:(which + 1) * A_GW] = y


def _attn_a_kernel(q_ref, kp_ref, kc_ref, vp_ref, vc_ref, bias_ref, o_ref, lse_ref):
    n = pl.program_id(2)
    blk = A_BLOCK
    pair = 2 * HEAD_DIM
    lane = lax.broadcasted_iota(I32, (2 * blk, pair), 1)
    row = lax.broadcasted_iota(I32, (2 * blk, blk), 0)
    pen = jnp.where((row < blk) & (n == 0), NEG, 0.0)
    nt = (((1,), (1,)), ((), ()))
    tn = (((0,), (0,)), ((), ()))
    n_res, rows = q_ref.shape[0], q_ref.shape[1]
    k_all = [jnp.concatenate([kp_ref[r], kc_ref[r]], axis=0) for r in range(n_res)]
    v_all = [jnp.concatenate([vp_ref[r], vc_ref[r]], axis=0) for r in range(n_res)]
    chains = [(r, sb, hp, a) for r in range(n_res) for sb in range(rows // blk)
              for hp in range(A_HEADS // 2) for a in range(2)]

    def window(x, sb, hp):
        return x[sb * blk:(sb + 2) * blk, hp * pair:(hp + 1) * pair]

    keep = [lane < HEAD_DIM, lane >= HEAD_DIM]
    ka = [jnp.where(keep[a], window(k_all[r], sb, hp), jnp.zeros((2 * blk, pair), BF16))
          for r, sb, hp, a in chains]
    s = [lax.dot_general(k_, q_ref[r, sb * blk:(sb + 1) * blk, hp * pair:(hp + 1) * pair], nt,
                         preferred_element_type=F32) + bias_ref[2 * hp + a]
         for k_, (r, sb, hp, a) in zip(ka, chains)]
    s = [s_ + pen if sb == 0 else s_ for s_, (_, sb, _, _) in zip(s, chains)]
    m = [jnp.max(s_, axis=0, keepdims=True) for s_ in s]
    p = [jnp.exp2(s_ - m_) for s_, m_ in zip(s, m)]
    l = [jnp.sum(p_, axis=0, keepdims=True) for p_ in p]
    pv = [lax.dot_general(window(v_all[r], sb, hp), p_.astype(BF16), tn, preferred_element_type=F32)
          for p_, (r, sb, hp, _) in zip(p, chains)]
    for idx in range(0, len(chains), 2):
        r, sb, hp, _ = chains[idx]
        o_t = jnp.concatenate([(pv[idx + a] / l[idx + a])[a * HEAD_DIM:(a + 1) * HEAD_DIM] for a in range(2)],
                              axis=0)
        l_t = jnp.concatenate([jnp.broadcast_to(m[idx + a] + jnp.log2(l[idx + a]), (HEAD_DIM, blk))
                               for a in range(2)], axis=0)
        o_ref[r, sb * blk:(sb + 1) * blk, hp * pair:(hp + 1) * pair] = o_t.T
        lse_ref[r, sb * blk:(sb + 1) * blk, hp * pair:(hp + 1) * pair] = l_t.T


def _attn_a_group(arr, bias_t, dilation, batch, seq):
    sub = seq // dilation
    step = min(A_MAX_STEP, sub // A_BLOCK)
    rows = step * A_BLOCK
    nb = sub // rows
    n_res = min(dilation, A_MAX_STEP // step)

    def spec(which, prev):
        if prev:
            return pl.BlockSpec((None, n_res, A_BLOCK, A_GW),
                                lambda b, r, n: (b, r, jnp.maximum(n * step - 1, 0), which))
        return pl.BlockSpec((None, n_res, rows, A_GW), lambda b, r, n: (b, r, n, which))

    out_spec = pl.BlockSpec((None, n_res, rows, A_GW), lambda b, r, n: (b, r, n, 0))
    out_sds = jax.ShapeDtypeStruct((batch, dilation, sub, A_GW), F32)
    return pl.pallas_call(
        _attn_a_kernel,
        out_shape=[out_sds, out_sds],
        grid=(batch, dilation // n_res, nb),
        in_specs=[spec(0, False), spec(1, True), spec(1, False), spec(2, True), spec(2, False),
                  pl.BlockSpec((A_HEADS, 2 * A_BLOCK, A_BLOCK), lambda b, r, n: (0, 0, 0))],
        out_specs=[out_spec, out_spec],
        compiler_params=_cparams(("parallel", "parallel", "arbitrary")),
        name=f"attn_a_d{dilation}",
    )(arr, arr, arr, arr, arr, bias_t)


def _bias_a(tab_g, dilation):
    step = np.arange(3 * A_BLOCK - 1) - (A_BLOCK - 1)
    vec = tab_g[_t5_bucket(jnp.asarray(np.maximum(step, 0) * dilation, I32))].astype(F32) * LOG2E
    vec = jnp.where(jnp.asarray((step >= 0) & (step <= A_BLOCK))[:, None], vec, NEG)
    return _toeplitz(vec.T, 2 * A_BLOCK, A_BLOCK)


def _mixer_a(x2, gain, w_in, q_gain, k_gain, bias_a, batch, seq, tm):
    n, d = x2.shape
    n_groups = len(A_GROUPS)
    reps = n_groups * A_HEADS
    hg = jnp.concatenate([jnp.tile(q_gain * (HEAD_DIM ** -0.5 * LOG2E), reps), jnp.tile(k_gain, reps),
                          jnp.ones((n_groups * A_GW,), F32)]).reshape(1, -1)
    kout = 3 * n_groups * A_GW
    pb = V7X_MXU_DIM
    bd = np.kron(np.eye(pb // HEAD_DIM), np.ones((HEAD_DIM, HEAD_DIM))).astype(np.float32)
    perms = [_residue_major(pb, dil) for _, dil in A_GROUPS[1:]]
    tiles = seq // tm
    const = lambda shape: pl.BlockSpec(shape, lambda i: (0, 0), pipeline_mode=pl.Buffered(1))
    grouped = lambda dil, width: pl.BlockSpec((None, dil, tm // dil, width),
                                              lambda i: (i // tiles, 0, i % tiles, 0))
    arrs = pl.pallas_call(
        _in_proj_a_kernel,
        out_shape=[jax.ShapeDtypeStruct((batch, dil, seq // dil, 3 * A_GW), BF16) for _, dil in A_GROUPS],
        grid=(n // tm,),
        in_specs=[pl.BlockSpec((tm, d), lambda i: (i, 0)), const((1, d)), const((d, kout)),
                  const((1, kout)), const((pb, pb))] + [const((pb, pb)) for _ in perms],
        out_specs=[grouped(dil, 3 * A_GW) for _, dil in A_GROUPS],
        compiler_params=_cparams(("parallel",)),
        name="in_proj_a",
    )(x2, gain.reshape(1, d), w_in.astype(BF16), hg, jnp.asarray(bd, BF16),
      *[jnp.asarray(p, BF16) for p in perms])

    os_, ls_ = [], []
    for g, (window, dilation) in enumerate(A_GROUPS):
        assert window // dilation == A_BLOCK and (seq // dilation) % (min(A_MAX_STEP, seq // dilation // A_BLOCK) * A_BLOCK) == 0 and dilation <= pb // 16
        tab_g = bias_a[:, g * A_HEADS:(g + 1) * A_HEADS]
        o, lse = _attn_a_group(arrs[g], _bias_a(tab_g, dilation), dilation, batch, seq)
        os_.append(o)
        ls_.append(lse)

    mix = [*os_, *ls_, *[jnp.asarray(p.T, BF16) for p in perms]]
    specs = [grouped(dil, A_GW) for _, dil in A_GROUPS] * 2 + [const((pb, pb)) for _ in perms]
    return mix, specs


def _float_key(s):
    i = lax.bitcast_convert_type(s, I32)
    k = jnp.where(i < 0, i ^ jnp.int32(0x7FFFFFFF), i)
    return jnp.where(s == 0.0, 0, k)


def _split_bf16(x):
    hi = x.astype(BF16)
    return hi, (x - hi.astype(F32)).astype(BF16)


def _attn_b_kernel(q_ref, k_ref, vt_ref, qi_ref, kw_ref, bias_ref, shift_ref, o_ref, key_ref, *, bounded):
    qb = pl.program_id(1)
    kc, qt = B_KC, B_QT
    nch = qb // 2 + 1
    t_q = qb * qt + lax.broadcasted_iota(I32, (kc, qt), 1)
    sub = lax.broadcasted_iota(I32, (kc, qt), 0)
    nt = (((1,), (1,)), ((), ()))

    qh, ql = _split_bf16(qi_ref[...])
    qi3 = jnp.concatenate(
        [jnp.concatenate([x[:, h * B_IDX_DIM:(h + 1) * B_IDX_DIM] for x in (qh, qh, ql)], axis=1)
         for h in range(B_IDX_HEADS)], axis=0)
    wt = kw_ref[pl.ds(pl.multiple_of(qb * qt, qt), qt), 0:qt].T
    wt = wt * (B_IDX_HEADS ** -0.5 * B_IDX_DIM ** -0.5)

    npair = (nch + 1) // 2

    def score_pair(c2, carry):
        offs = [(2 * c2 + a) * kc for a in range(2)]
        ks = [_split_bf16(kw_ref[pl.ds(off, kc), 0:B_IDX_DIM]) for off in offs]
        sc = [lax.dot_general(jnp.concatenate([kh, kl, kh], axis=1), qi3, nt, preferred_element_type=F32)
              for kh, kl in ks]
        for off, s in zip(offs, sc):
            acc = jnp.zeros((kc, qt), F32)
            for h in range(B_IDX_HEADS):
                acc = acc + wt[B_IDX_DIM + h:B_IDX_DIM + h + 1, :] * jnp.maximum(s[:, h * qt:(h + 1) * qt], 0.0)
            key_ref[pl.ds(off, kc), :] = jnp.where(off + sub <= t_q, _float_key(acc), INT_MIN)
        return carry

    def score_all(n):
        def run():
            for c2 in range(n):
                score_pair(c2, 0)
            return 0
        return run

    lax.switch(npair - 1, [score_all(n) for n in range(1, key_ref.shape[0] // (2 * kc) + 1)])

    def count(pred_fn, pairs=None):
        def body(c2, acc):
            for a in range(2):
                off = (2 * c2 + a) * kc
                off = off if pairs else pl.multiple_of(off, kc)
                hit = jnp.where(pred_fn(key_ref[pl.ds(off, kc), :], off + sub), 1, 0)
                acc = acc + jnp.sum(hit.reshape(kc // 8, 8, qt), axis=0)
            return acc
        acc = jnp.zeros((8, qt), I32)
        if pairs:
            for c2 in range(pairs):
                acc = body(c2, acc)
        else:
            acc = lax.fori_loop(0, npair, body, acc)
        return jnp.sum(acc, axis=0, keepdims=True)

    def search(pairs):
        def thr_bit(it, lo):
            cand = lo + jnp.left_shift(jnp.int32(1), 31 - it)
            cnt = count(lambda key, _: key >= cand, pairs)
            return jnp.where(cnt >= B_TOPK, cand, lo)
        return lax.fori_loop(0, 32, thr_bit, jnp.full((1, qt), INT_MIN, I32))

    max_pairs = key_ref.shape[0] // (2 * kc)
    thr = lax.switch(npair - 1, [functools.partial(search, n) for n in range(1, max_pairs + 1)])
    need = B_TOPK - count(lambda key, _: key > thr)
    n_eq = count(lambda key, _: key == thr)

    def tie_search(_):
        def bit(it, j):
            cand = j + jnp.left_shift(jnp.int32(1), 12 - it)
            cnt = count(lambda key, idx: (key == thr) & (idx < cand))
            return jnp.where(cnt <= need, cand, j)
        return lax.fori_loop(0, 13, bit, jnp.zeros((1, qt), I32))

    any_tie = jnp.max(jnp.where(n_eq > need, 1, 0)) > 0
    j_max = lax.cond(any_tie, tie_search, lambda _: jnp.full((1, qt), 2 ** 13, I32), 0)

    grp = B_HEADS // B_KV_HEADS
    q = q_ref[...]
    groups = range(B_KV_HEADS)
    qg = [jnp.concatenate([q[:, (g * grp + j) * HEAD_DIM:(g * grp + j + 1) * HEAD_DIM]
                           for j in range(grp)], axis=0) for g in groups]
    ksl = [slice(g * HEAD_DIM, (g + 1) * HEAD_DIM) for g in groups]

    fk = 2 * kc

    def flash(c2, carry):
        m, l, acc = carry
        off = pl.multiple_of(c2 * fk, fk)
        tiles = [jnp.clip(qb - (fk // qt) * c2 - a, 0, B_NDELTA - 1) for a in range(fk // qt)]
        key = key_ref[pl.ds(off, fk), :]
        idx = off + lax.broadcasted_iota(I32, (fk, qt), 0)
        sel = ((key > thr) | ((key == thr) & (idx < j_max))) & (idx <= t_q[:1])
        mk = jnp.where(sel, -shift_ref[...], NEG)
        mk = jnp.concatenate([mk] * grp, axis=1)
        s = [lax.dot_general(k_ref[pl.ds(off, fk), ksl[g]], qg[g], nt, preferred_element_type=F32)
             for g in groups]
        s = [s[g] + jnp.concatenate([bias_ref[d, g] for d in tiles], axis=0) + mk for g in groups]
        if bounded:
            p = [jnp.exp2(s[g]) for g in groups]
            l = [l[g] + jnp.sum(p[g], axis=0, keepdims=True) for g in groups]
            p = [p[g].astype(BF16) for g in groups]
            pv = [jnp.dot(vt_ref[ksl[g], pl.ds(off, fk)], p[g], preferred_element_type=F32) for g in groups]
            return m, l, [acc[g] + pv[g] for g in groups]
        else:
            m_new = [jnp.maximum(m[g], jnp.max(s[g], axis=0, keepdims=True)) for g in groups]
            p = [jnp.exp2(s[g] - m_new[g]) for g in groups]
        pv = [jnp.dot(vt_ref[ksl[g], pl.ds(off, fk)], p[g].astype(BF16), preferred_element_type=F32)
              for g in groups]
        alpha = [jnp.exp2(m[g] - m_new[g]) for g in groups]
        l = [alpha[g] * l[g] + jnp.sum(p[g], axis=0, keepdims=True) for g in groups]
        acc = [alpha[g] * acc[g] + pv[g] for g in groups]
        return m_new, l, acc

    m0 = [jnp.full((1, grp * qt), NEG, F32) for _ in groups]
    l0 = [jnp.zeros((1, grp * qt), F32) for _ in groups]
    a0 = [jnp.zeros((HEAD_DIM, grp * qt), F32) for _ in groups]
    _, l, acc = lax.fori_loop(0, npair, flash, (m0, l0, a0))
    for g in groups:
        o = acc[g] / l[g]
        for j in range(0, grp, 2):
            h = g * grp + j
            pair = jnp.concatenate([o[:, j * qt:(j + 1) * qt], o[:, (j + 1) * qt:(j + 2) * qt]], axis=0)
            o_ref[:, h * HEAD_DIM:(h + 2) * HEAD_DIM] = pair.T.astype(o_ref.dtype)


def _toeplitz(vec, n_rows, n_cols):
    span = n_rows + n_cols - 1
    assert vec.shape[-1] == span
    lead = vec.shape[:-1]
    padded = jnp.concatenate([vec, jnp.zeros(lead + (1,), vec.dtype)], axis=-1)
    flat = jnp.tile(padded, n_rows)[..., :n_rows * span]
    return flat.reshape(lead + (n_rows, span))[..., n_rows - 1:]


def _bias_b(tab):
    grp = B_HEADS // B_KV_HEADS
    span = 2 * B_QT - 1
    dist = np.arange(B_NDELTA)[:, None] * B_QT + np.arange(span)[None, :] - (B_QT - 1)
    assert (B_NDELTA - 1) * B_QT - (B_QT - 1) >= 16 * 128 ** (15.0 / 16.0) + 1
    vec = tab[_t5_bucket(jnp.asarray(np.maximum(dist, 0), I32))].astype(F32) * LOG2E
    t = _toeplitz(vec.transpose(0, 2, 1), B_QT, B_QT)
    t = t.reshape(B_NDELTA, B_KV_HEADS, grp, B_QT, B_QT).transpose(0, 1, 3, 2, 4)
    return t.reshape(B_NDELTA, B_KV_HEADS, B_QT, grp * B_QT)


def _mixer_b(x2, gain, w_in, q_gain, k_gain, bias_b, batch, seq, tm):
    d = x2.shape[1]
    nq, nkv = B_HEADS * HEAD_DIM, B_KV_HEADS * HEAD_DIM
    nidx = B_IDX_HEADS * B_IDX_DIM
    tn = V7X_MXU_DIM
    used = 2 * nkv + nq + nidx + B_IDX_DIM + B_IDX_HEADS
    kout = -(-used // tn) * tn
    w = jnp.concatenate([w_in, jnp.zeros((d, kout - used), F32)], axis=1).astype(BF16)
    hg = jnp.concatenate([jnp.tile(q_gain * (HEAD_DIM ** -0.5 * LOG2E), B_HEADS), jnp.tile(k_gain, B_KV_HEADS),
                          jnp.ones((kout - nq - nkv,), F32)]).reshape(1, -1)
    n = batch * seq
    nrest = kout - nq - 2 * nkv
    bd = np.kron(np.eye(tn // HEAD_DIM), np.ones((HEAD_DIM, HEAD_DIM))).astype(np.float32)
    const = lambda shape: pl.BlockSpec(shape, lambda i: (0, 0), pipeline_mode=pl.Buffered(1))
    qk, idx = pl.pallas_call(
        functools.partial(_in_proj_b_kernel, tn=tn, n_norm=(nq + nkv) // tn),
        out_shape=[jax.ShapeDtypeStruct((n, nq + 2 * nkv), BF16), jax.ShapeDtypeStruct((n, nrest), F32)],
        grid=(n // tm,),
        in_specs=[pl.BlockSpec((tm, d), lambda i: (i, 0)), const((1, d)), const((d, kout)),
                  const((1, kout)), const((tn, tn))],
        out_specs=[pl.BlockSpec((tm, nq + 2 * nkv), lambda i: (i, 0)),
                   pl.BlockSpec((tm, nrest), lambda i: (i, 0))],
        compiler_params=_cparams(("parallel",)),
        name="in_proj_b",
    )(x2, gain.reshape(1, d), w, hg, jnp.asarray(bd, BF16))
    qk = qk.reshape(batch, seq, nq + 2 * nkv)
    idx = idx.reshape(batch, seq, nrest)
    vt = qk[:, :, nq + nkv:].transpose(0, 2, 1)
    assert seq % (2 * B_QT) == 0 and seq <= 2 ** 12 and nkv == tn and nrest == nidx + tn

    def attend(shift, bounded):
        return pl.pallas_call(
            functools.partial(_attn_b_kernel, bounded=bounded),
            out_shape=jax.ShapeDtypeStruct((batch, seq, nq), BF16),
            grid=(batch, seq // B_QT),
            in_specs=[
                pl.BlockSpec((None, B_QT, nq), lambda b, i: (b, i, 0)),
                pl.BlockSpec((None, seq, nkv), lambda b, i: (b, 0, nq // nkv)),
                pl.BlockSpec((None, nkv, seq), lambda b, i: (b, 0, 0)),
                pl.BlockSpec((None, B_QT, nidx), lambda b, i: (b, i, 0)),
                pl.BlockSpec((None, seq, tn), lambda b, i: (b, 0, nidx // tn)),
                pl.BlockSpec((B_NDELTA, B_KV_HEADS, B_QT, (B_HEADS // B_KV_HEADS) * B_QT),
                             lambda b, i: (0, 0, 0, 0), pipeline_mode=pl.Buffered(1)),
                pl.BlockSpec((1, B_QT), lambda b, i: (0, 0)),
            ],
            out_specs=pl.BlockSpec((None, B_QT, nq), lambda b, i: (b, i, 0)),
            scratch_shapes=[pltpu.VMEM((seq, B_QT), I32)],
            compiler_params=_cparams(("parallel", "arbitrary")),
            name="attn_b_bounded" if bounded else "attn_b",
        )(qk, qk, vt, idx, idx, bias_t, shift)

    bias_t = _bias_b(bias_b)
    qk_bound = 1.02 * HEAD_DIM * (HEAD_DIM ** -0.5 * LOG2E) * jnp.max(jnp.abs(q_gain)) * jnp.max(jnp.abs(k_gain))
    bound = qk_bound + jnp.max(bias_b) * LOG2E
    spread = bound + qk_bound - jnp.min(bias_b) * LOG2E
    y = lax.cond(spread <= B_SAFE_SPREAD,
                 lambda: attend(jnp.full((1, B_QT), bound, F32), True),
                 lambda: attend(jnp.zeros((1, B_QT), F32), False))
    return [y.reshape(batch * seq, nq)], [pl.BlockSpec((tm, nq), lambda i: (i, 0))]


def _in_proj_c_kernel(x_ref, g_ref, w_ref, cw_ref, qkv_ref, rest_ref, tail_ref, xs_ref, *, tn):
    i = pl.program_id(1)
    tm = x_ref.shape[0]
    hw = C_HEADS * C_DK
    hn = _rms_rows(x_ref[...], g_ref[...]).astype(BF16)
    for j in range(w_ref.shape[1] // tn):
        cols = slice(j * tn, (j + 1) * tn)
        y = jnp.dot(hn, w_ref[:, cols], preferred_element_type=F32)
        if j * tn >= 3 * hw:
            rest_ref[:, j * tn - 3 * hw:(j + 1) * tn - 3 * hw] = y
            continue
        hist = tail_ref.shape[0]
        xs_ref[0:hist, cols] = jnp.where(i > 0, tail_ref[:, cols], 0.0)
        xs_ref[hist:, cols] = y
        tail_ref[:, cols] = y[tm - hist:, :]
        cw = cw_ref[:, cols]
        conv = sum(cw[t:t + 1, :] * xs_ref[hist - (C_CONV - 1) + t:hist - (C_CONV - 1) + t + tm, cols]
                   for t in range(C_CONV))
        z = conv * jax.nn.sigmoid(conv)
        if j * tn >= 2 * hw:
            qkv_ref[:, cols] = z
            continue
        scale = C_DK ** -0.5 if j * tn < hw else 1.0
        for h in range(tn // C_DK):
            zh = z[:, h * C_DK:(h + 1) * C_DK]
            ss = jnp.sum(zh * zh, axis=-1, keepdims=True)
            qkv_ref[:, j * tn + h * C_DK:j * tn + (h + 1) * C_DK] = zh * (lax.rsqrt(ss + EPS) * scale)


def _delta_kernel(q_ref, k_ref, v_ref, gate_ref, a_ref, b_ref, alog_ref, dtb_ref, og_ref, y_ref,
                  rows_s, p_s, r_s, m_s, n_s, el_s, o_s):
    cs, gs = C_CHUNK, C_GROUP
    per = gs // cs
    seq, dv = v_ref.shape
    n_groups = seq // gs
    hi = lax.Precision.HIGHEST
    r = lax.broadcasted_iota(I32, (gs, gs), 0)
    c_ = lax.broadcasted_iota(I32, (gs, gs), 1)

    def same_block(size):
        sh = int(math.log2(size))
        return (r >> sh) == (c_ >> sh)

    chunk = same_block(cs)
    lower, strict, eye = chunk & (r >= c_), chunk & (r > c_), r == c_

    z = a_ref[...] + dtb_ref[...]
    softplus = jnp.maximum(z, 0.0) + jnp.log(1.0 + jnp.exp(-jnp.abs(z)))
    g_all = -jnp.exp(alog_ref[...]) * softplus
    gc_all = jnp.dot(g_all, jnp.where(chunk & (r <= c_), 1.0, 0.0), preferred_element_type=F32, precision=hi)
    gl_all = jnp.dot(g_all, jnp.where(chunk, 1.0, 0.0), preferred_element_type=F32, precision=hi)
    rows_s[0] = jax.nn.sigmoid(b_ref[...])
    rows_s[1] = gc_all
    rows_s[2] = jnp.exp(gc_all)
    rows_s[3] = jnp.exp(gl_all - gc_all)
    first = (lax.broadcasted_iota(I32, (gs, per * dv), 0)
             == (lax.broadcasted_iota(I32, (gs, per * dv), 1) // dv) * cs)
    el_s[...] = jnp.exp(jnp.dot(gl_all, jnp.where(first, 1.0, 0.0), preferred_element_type=F32, precision=hi))

    def mm(a, b):
        return jnp.dot(a.astype(BF16), b.astype(BF16), preferred_element_type=F32)

    def mm_nt(a, b):
        return lax.dot_general(a.astype(BF16), b.astype(BF16), (((1,), (1,)), ((), ())),
                               preferred_element_type=F32)

    ways = 4

    def prep(m):
        ids = [m * ways + a for a in range(ways)]
        rows = [pl.ds(pl.multiple_of(i * gs, gs), gs) for i in ids]
        rw = [jnp.concatenate([rows_s[j, pl.ds(i, 1), :] for j in range(4)] + [jnp.zeros((4, gs), F32)], axis=0)
              for i in ids]
        cl = [x.T for x in rw]
        beta_col, gc_col, eg_col, ekg_col = ([x[:, j:j + 1] for x in cl] for j in range(4))
        q, k, v = ([ref[rw_, :] for rw_ in rows] for ref in (q_ref, k_ref, v_ref))
        decay = [jnp.where(lower, jnp.exp(jnp.where(lower, gcc - x[1:2, :], 0.0)), 0.0)
                 for gcc, x in zip(gc_col, rw)]
        kb = [k_ * b_ for k_, b_ in zip(k, beta_col)]
        yield
        a_mat = [jnp.where(strict, mm_nt(kb_, k_) * d_, 0.0) for kb_, k_, d_ in zip(kb, k, decay)]
        yield
        a8 = [jnp.where(same_block(8), a_, 0.0) for a_ in a_mat]
        t = [jnp.where(eye, 1.0, 0.0) - a_ for a_ in a8]
        pw = [mm(a_, a_) for a_ in a8]
        yield
        t = [t_ + mm(t_, p_) for t_, p_ in zip(t, pw)]
        yield
        pw = [mm(p_, p_) for p_ in pw]
        yield
        t = [t_ + mm(t_, p_) for t_, p_ in zip(t, pw)]
        yield
        size = 8
        while size < cs:
            sel = same_block(2 * size) & jnp.logical_not(same_block(size))
            nt = [mm(jnp.where(sel, a_, 0.0), t_) for a_, t_ in zip(a_mat, t)]
            yield
            t = [t_ - mm(t_, n_) for t_, n_ in zip(t, nt)]
            yield
            size *= 2
        uw = [mm(t_, jnp.concatenate([v_ * b_, kb_ * e_], axis=1))
              for t_, v_, b_, kb_, e_ in zip(t, v, beta_col, kb, eg_col)]
        yield
        qk = [jnp.where(lower, mm_nt(q_, k_) * d_, 0.0) for q_, k_, d_ in zip(q, k, decay)]
        yield
        qkwu = [mm(qk_, uw_) for qk_, uw_ in zip(qk, uw)]
        kgt = [(k_ * e_).T for k_, e_ in zip(k, ekg_col)]
        yield
        nm = [[mm(kgt[a][:, j * cs:(j + 1) * cs], uw[a][j * cs:(j + 1) * cs, :]) for j in range(per)]
              for a in range(ways)]

        def write():
            for a, i in enumerate(ids):
                r_s[rows[a], :] = qkwu[a][:, :dv]
                p_s[rows[a], :] = (q[a] * eg_col[a] - qkwu[a][:, dv:]).astype(BF16)
                for j in range(per):
                    n_s[i * per + j] = nm[a][j][:, :dv]
                    m_s[i * per + j] = nm[a][j][:, dv:].astype(BF16)
        return write

    def scan(m, state):
        for a in range(ways):
            i = m * ways + a
            el = el_s[pl.ds(i, 1), :]
            for j in range(per):
                rows = pl.ds(pl.multiple_of(i * gs + j * cs, cs), cs)
                c = i * per + j
                sb = state[0].astype(BF16)
                o_s[rows, :] = jnp.dot(p_s[rows, :], sb, preferred_element_type=F32) + r_s[rows, :]
                state[0] = (state[0] * el[:, j * dv:(j + 1) * dv] + n_s[c]
                            - jnp.dot(m_s[c], sb, preferred_element_type=F32))
                yield

    def weave(*gens):
        gens, results = list(gens), [None] * len(gens)
        live = list(range(len(gens)))
        while live:
            for idx in list(live):
                try:
                    next(gens[idx])
                except StopIteration as stop:
                    results[idx] = stop.value
                    live.remove(idx)
        return results

    n_blocks = n_groups // ways
    weave(prep(0))[0]()

    def step(m, state):
        holder = [state]
        _, write = weave(scan(m - 1, holder), prep(m))
        write()
        return holder[0]

    state = lax.fori_loop(1, n_blocks, step, jnp.zeros((C_DK, dv), F32))

    def finish(first, last):
        for t in range(first, last):
            rows = pl.ds(t * gs, gs)
            gate = gate_ref[rows, :]
            y_ref[rows, :] = (_rms_rows(o_s[rows, :], og_ref[...])
                              * (gate * jax.nn.sigmoid(gate))).astype(y_ref.dtype)
            yield

    done = (n_blocks - 1) * ways
    weave(scan(n_blocks - 1, [state]), finish(0, done))
    weave(finish(done, n_groups))


def _mixer_c(x2, gain, w_in, conv_w, a_log, dt_bias, o_gain, batch, seq, tm):
    d = x2.shape[1]
    hw = C_HEADS * C_DK
    tn = V7X_MXU_DIM
    used = 4 * hw + 2 * C_HEADS
    kout = -(-used // tn) * tn
    w = jnp.concatenate([w_in, jnp.zeros((d, kout - used), F32)], axis=1).astype(BF16)
    const = lambda shape: pl.BlockSpec(shape, lambda b, i: (0, 0), pipeline_mode=pl.Buffered(1))
    qkv, rest = pl.pallas_call(
        functools.partial(_in_proj_c_kernel, tn=tn),
        out_shape=[jax.ShapeDtypeStruct((batch, seq, 3 * hw), F32),
                   jax.ShapeDtypeStruct((batch, seq, kout - 3 * hw), F32)],
        grid=(batch, seq // tm),
        in_specs=[pl.BlockSpec((None, tm, d), lambda b, i: (b, i, 0)), const((1, d)), const((d, kout)),
                  const((C_CONV, 3 * hw))],
        out_specs=[pl.BlockSpec((None, tm, 3 * hw), lambda b, i: (b, i, 0)),
                   pl.BlockSpec((None, tm, kout - 3 * hw), lambda b, i: (b, i, 0))],
        scratch_shapes=[pltpu.VMEM((V7X_SUBLANES, 3 * hw), F32), pltpu.VMEM((tm + V7X_SUBLANES, 3 * hw), F32)],
        compiler_params=_cparams(("parallel", "arbitrary")),
        name="in_proj_c",
    )(x2.reshape(batch, seq, d), gain.reshape(1, d), w, conv_w)

    n_groups = seq // C_GROUP
    ba = rest[:, :, hw:hw + 2 * C_HEADS].transpose(0, 2, 1).reshape(batch, 2 * C_HEADS, n_groups, C_GROUP)
    per_head = lambda v: jnp.broadcast_to(v.reshape(C_HEADS, 1, 1), (C_HEADS, 1, C_GROUP))
    head_cols = lambda off: pl.BlockSpec((None, seq, C_DK), lambda b, h: (b, 0, off + h))
    small = lambda off: pl.BlockSpec((None, None, n_groups, C_GROUP), lambda b, h: (b, off + h, 0, 0))
    scalar_row = pl.BlockSpec((None, 1, C_GROUP), lambda b, h: (h, 0, 0))
    y = pl.pallas_call(
        _delta_kernel,
        out_shape=jax.ShapeDtypeStruct((batch, seq, hw), BF16),
        grid=(batch, C_HEADS),
        in_specs=[head_cols(0), head_cols(C_HEADS), head_cols(2 * C_HEADS),
                  head_cols(0),
                  small(C_HEADS), small(0), scalar_row, scalar_row,
                  pl.BlockSpec((1, C_DK), lambda b, h: (0, 0))],
        out_specs=head_cols(0),
        scratch_shapes=[
            pltpu.VMEM((4, n_groups, C_GROUP), F32),
            pltpu.VMEM((seq, C_DK), BF16),
            pltpu.VMEM((seq, C_DK), F32),
            pltpu.VMEM((seq // C_CHUNK, C_DK, C_DK), BF16),
            pltpu.VMEM((seq // C_CHUNK, C_DK, C_DK), F32),
            pltpu.VMEM((n_groups, (C_GROUP // C_CHUNK) * C_DK), F32),
            pltpu.VMEM((seq, C_DK), F32),
        ],
        compiler_params=_cparams(("parallel", "parallel")),
        name="delta_c",
    )(qkv, qkv, qkv, rest, ba, ba, per_head(a_log), per_head(dt_bias), o_gain.reshape(1, C_DK))
    return [y.reshape(batch * seq, hw)], [pl.BlockSpec((tm, hw), lambda i: (i, 0))]


def kernel(x, rel_bias, norm_mix, norm_mlp, mlp_w1, mlp_w2, a_w_in, a_q_gain, a_k_gain, a_w_out,
           b_w_in, b_q_gain, b_k_gain, b_w_out, c_w_in, c_conv_w, c_a_log, c_dt_bias, c_o_gain, c_w_out):
    batch, seq, d = x.shape
    depth = norm_mix.shape[0]
    a_cols = len(A_GROUPS) * A_HEADS
    bias_a, bias_b = rel_bias[:, :a_cols], rel_bias[:, a_cols:]
    x2 = x.reshape(batch * seq, d)
    tm = 512
    for i in range(depth):
        kind, j = i % 3, i // 3
        if kind == 0:
            mix, specs = _mixer_a(x2, norm_mix[i], a_w_in[j], a_q_gain[j], a_k_gain[j], bias_a, batch, seq, tm)
            w_out = a_w_out[j]
        elif kind == 1:
            mix, specs = _mixer_b(x2, norm_mix[i], b_w_in[j], b_q_gain[j], b_k_gain[j], bias_b, batch, seq, tm)
            w_out = b_w_out[j]
        else:
            mix, specs = _mixer_c(x2, norm_mix[i], c_w_in[j], c_conv_w[j], c_a_log[j], c_dt_bias[j],
                                  c_o_gain[j], batch, seq, tm)
            w_out = c_w_out[j]
        x2 = _proj_mlp(mix, specs, w_out.astype(BF16), x2, norm_mlp[i], mlp_w1[i].astype(BF16),
                       mlp_w2[i].astype(BF16), groups_a=kind == 0, tm=tm)
    return x2.reshape(batch, seq, d)
```
